```python
import math
import jax
import jax.numpy as jnp
from jax import lax
import numpy as np

D_MODEL = 1024
BATCH = 4
SEQ = 4096
DEPTH = 4
DEC_BATCH = 32
DEC_SEQ = 8
PAST_LEN = 8192
PAGE_SIZE = 128

HEAD_DIM = 64
NSA_HEADS = 8
NSA_KV = 2
NSA_HPG = NSA_HEADS // NSA_KV
CMP_STRIDE = 16
CMP_BLK = 2 * CMP_STRIDE
SEL_BLK = 64
TOP_N = 16
WINDOW = 512
FORCE_SCORE = 1.0e4
HGRN_HEADS = 4
HGRN_DK = 128
HGRN_DV = 128
HGRN_CHUNK = 64
DSA_HEADS = 16
DSA_KV = 4
DSA_HPG = DSA_HEADS // DSA_KV
IDX_HEADS = 8
IDX_DIM = 64
IDX_TOPK = 256
D_FF = 2816
N_EXPERTS = 8
TOP_K = 2
Q_BLK = 128
ROPE_THETA = 10000.0
EPS = 1e-6
NEG = -1.0e30
N_EVEN = (DEPTH + 1) // 2
N_ODD = DEPTH // 2
NSA_QW = NSA_HEADS * HEAD_DIM
NSA_KVW = NSA_KV * HEAD_DIM
NSA_GW = NSA_HEADS * 3
HGRN_KW = HGRN_HEADS * HGRN_DK
HGRN_VW = HGRN_HEADS * HGRN_DV
EVEN_IN = NSA_QW + 6 * NSA_KVW + NSA_GW + 2 * HGRN_KW + 2 * HGRN_VW
EVEN_MIX = NSA_QW + HGRN_VW
DSA_QW = DSA_HEADS * HEAD_DIM
DSA_KVW = DSA_KV * HEAD_DIM
IDX_QW = IDX_HEADS * IDX_DIM
ODD_IN = DSA_QW + 2 * DSA_KVW + IDX_QW + IDX_DIM + IDX_HEADS
ODD_MIX = DSA_QW

kernel_name = 'hybrid_nsa_hgrn2_dsa_adaln_step'


def rms_norm(x, g):
    x32 = x.astype(jnp.float32)
    y = x32 * lax.rsqrt(jnp.mean(x32 * x32, axis=-1, keepdims=True) + EPS)
    return (y * g.astype(jnp.float32)).astype(x.dtype)


def rope(x, pos):
    half = x.shape[-1] // 2
    inv = ROPE_THETA ** (-jnp.arange(half, dtype=jnp.float32) / half)
    ang = pos.astype(jnp.float32)[:, None] * inv[None, :]
    cos = jnp.cos(ang)[:, None, :]
    sin = jnp.sin(ang)[:, None, :]
    x32 = x.astype(jnp.float32)
    x1, x2 = x32[..., :half], x32[..., half:]
    return jnp.concatenate([x1 * cos - x2 * sin, x2 * cos + x1 * sin], axis=-1).astype(x.dtype)


def masked_softmax(s, mask):
    s = jnp.where(mask, s.astype(jnp.float32), NEG)
    m = jnp.max(s, axis=-1, keepdims=True)
    e = jnp.where(mask, jnp.exp(s - m), 0.0)
    return e / jnp.maximum(jnp.sum(e, axis=-1, keepdims=True), 1e-30)


def split_cols(a, sizes):
    cuts = np.cumsum(np.array(sizes))[:-1]
    return jnp.split(a, [int(c) for c in cuts], axis=-1)


def paged_rows(pool, page_table):
    g = pool[page_table]
    return g.reshape(g.shape[0], g.shape[1] * g.shape[2], *g.shape[3:])


def over_query_blocks(fn, arrays):
    B, T = arrays[0].shape[:2]
    qb = math.gcd(T, Q_BLK)
    nb = T // qb
    blocks = tuple(a.reshape(B, nb, qb, *a.shape[2:]).swapaxes(0, 1) for a in arrays)
    out = lax.map(lambda xs: fn(xs[0] * qb, qb, *xs[1:]), (jnp.arange(nb),) + blocks)
    return out.swapaxes(0, 1).reshape(B, T, *out.shape[3:])


def swiglu(h, w1, w3, w2):
    return (jax.nn.silu(h @ w1) * (h @ w3)) @ w2


def moe_ffn(h, w_router, b_router, w1, w3, w2):
    logits = jnp.einsum('btd,de->bte', h, w_router, preferred_element_type=jnp.float32) + b_router.astype(jnp.float32)
    top_v, top_i = lax.top_k(logits, TOP_K)
    weights = jax.nn.softmax(top_v, axis=-1)
    combine = jnp.sum(jax.nn.one_hot(top_i, N_EXPERTS, dtype=jnp.float32) * weights[..., None], axis=-2).astype(h.dtype)
    y = jnp.zeros_like(h)
    for e in range(N_EXPERTS):
        y = y + combine[..., e:e + 1] * swiglu(h, w1[e], w3[e], w2[e])
    return y


def compress_blocks(x, pe, w):
    B, Lp, G, Dh = x.shape
    halves = x.reshape(B, Lp // CMP_STRIDE, CMP_STRIDE, G, Dh)
    blk = jnp.concatenate([halves[:, :-1], halves[:, 1:]], axis=2) + pe[None, None, :, None, :]
    return jnp.einsum('bnlgd,lde->bnge', blk, w.reshape(CMP_BLK, Dh, Dh))


def nsa_attention(q, rows, win_keys, gates, q_off, w_off, pe_k, pe_v, w_ck, w_cv):
    B, L = rows.shape[:2]
    Lp = -(-L // SEL_BLK) * SEL_BLK
    rows = jnp.pad(rows, ((0, 0), (0, Lp - L), (0, 0), (0, 0), (0, 0)))
    k_cmp = compress_blocks(rows[:, :, 0], pe_k, w_ck)
    v_cmp = compress_blocks(rows[:, :, 1], pe_v, w_cv).astype(jnp.float32)
    n_cmp = k_cmp.shape[1]
    n_sel = Lp // SEL_BLK
    k_sel = min(TOP_N, n_sel)
    ks_blk = rows[:, :, 2].reshape(B, n_sel, SEL_BLK, NSA_KV, HEAD_DIM).transpose(0, 3, 1, 2, 4)
    vs_blk = rows[:, :, 3].reshape(B, n_sel, SEL_BLK, NSA_KV, HEAD_DIM).transpose(0, 3, 1, 2, 4)
    kw_pad = jnp.pad(win_keys, ((0, 0), (WINDOW, 0), (0, 0), (0, 0), (0, 0)))
    cmp_end = CMP_STRIDE * jnp.arange(n_cmp) + CMP_BLK
    sel_j = jnp.arange(n_sel)
    scale = HEAD_DIM ** -0.5
    take = jax.vmap(jax.vmap(lambda a, i: a[i]))

    def block(i0, qb, q_b, g_b):
        t = q_off + i0 + jnp.arange(qb)
        qg = q_b.reshape(B, qb, NSA_KV, NSA_HPG, HEAD_DIM) * scale
        s_c = jnp.einsum('btghd,bngd->btghn', qg, k_cmp, preferred_element_type=jnp.float32)
        p_c = masked_softmax(s_c, (cmp_end[None, :] <= t[:, None] + 1)[None, :, None, None, :])
        o_c = jnp.einsum('btghn,bngd->btghd', p_c, v_cmp)
        imp = jnp.pad(p_c.sum(3), ((0, 0), (0, 0), (0, 0), (1, 0)))
        imp = imp.reshape(B, qb, NSA_KV, n_sel, SEL_BLK // CMP_STRIDE).sum(-1)
        cur = t // SEL_BLK
        forced = (sel_j[None] == 0) | (sel_j[None] == cur[:, None]) | (sel_j[None] == cur[:, None] - 1)
        visible = sel_j[None] <= cur[:, None]
        score = jnp.where(visible[None, :, None, :], imp + FORCE_SCORE * forced[None, :, None, :], NEG)
        top_s, top_i = lax.top_k(score, k_sel)
        top_s = top_s.transpose(0, 2, 1, 3)
        top_i = top_i.transpose(0, 2, 1, 3)
        k_g = take(ks_blk, top_i)
        v_g = take(vs_blk, top_i)
        kpos = top_i[..., None] * SEL_BLK + jnp.arange(SEL_BLK)
        ok = (top_s[..., None] > 0.5 * NEG) & (kpos <= t[None, None, :, None, None])
        q2 = qg.transpose(0, 2, 1, 3, 4)
        s_s = jnp.einsum('bgthd,bgtkrd->bgthkr', q2, k_g, preferred_element_type=jnp.float32)
        s_s = s_s.reshape(B, NSA_KV, qb, NSA_HPG, k_sel * SEL_BLK)
        p_s = masked_softmax(s_s, ok.reshape(B, NSA_KV, qb, 1, k_sel * SEL_BLK))
        o_s = jnp.einsum('bgthm,bgtmd->bgthd', p_s,
                         v_g.reshape(B, NSA_KV, qb, k_sel * SEL_BLK, HEAD_DIM).astype(jnp.float32))
        o_s = o_s.transpose(0, 2, 1, 3, 4)
        start = q_off - w_off + i0 + 1
        span = WINDOW + qb - 1
        kw = lax.dynamic_slice_in_dim(kw_pad, start, span, axis=1)
        kidx = start + jnp.arange(span)
        kabs = kidx - WINDOW + w_off
        okw = (kidx[None] >= WINDOW) & (kabs[None] <= t[:, None]) & (kabs[None] > t[:, None] - WINDOW)
        s_w = jnp.einsum('btghd,bsgd->btghs', qg, kw[:, :, 0], preferred_element_type=jnp.float32)
        p_w = masked_softmax(s_w, okw[None, :, None, None, :])
        o_w = jnp.einsum('btghs,bsgd->btghd', p_w, kw[:, :, 1].astype(jnp.float32))
        g = jax.nn.sigmoid(g_b.astype(jnp.float32)).reshape(B, qb, NSA_KV, NSA_HPG, 3)
        o = g[..., 0:1] * o_c + g[..., 1:2] * o_s + g[..., 2:3] * o_w
        return o.reshape(B, qb, NSA_QW).astype(q_b.dtype)

    return over_query_blocks(block, (q, gates))


def hgrn2_scan(q, k, v, logf, s0):
    q, k, v, logf = [a.astype(jnp.float32) for a in (q, k, v, logf)]
    B, T, H, DK = q.shape
    C = math.gcd(T, HGRN_CHUNK)
    nc = T // C
    to_chunks = lambda a: a.reshape(B, nc, C, *a.shape[2:]).swapaxes(0, 1)
    tri = jnp.tril(jnp.ones((C, C), dtype=bool))

    def step(S, inp):
        qc, kc, vc, gc = inp
        b = jnp.cumsum(gc, axis=1)
        o = jnp.einsum('bthk,bhkv->bthv', qc * jnp.exp(b), S)
        diff = b[:, :, None] - b[:, None, :]
        decay = jnp.exp(jnp.where(tri[None, :, :, None, None], diff, -jnp.inf))
        A = jnp.einsum('bthk,btshk,bshk->bths', qc, decay, kc)
        o = o + jnp.einsum('bths,bshv->bthv', A, vc)
        bl = b[:, -1]
        S = jnp.exp(bl)[..., None] * S + jnp.einsum('bshk,bshv->bhkv', kc * jnp.exp(bl[:, None] - b), vc)
        return S, o

    S, o = lax.scan(step, s0.astype(jnp.float32), (to_chunks(q), to_chunks(k), to_chunks(v), to_chunks(logf)))
    return o.swapaxes(0, 1).reshape(B, T, H, v.shape[-1]), S


def even_mixer(h, pos0, past, win_len, w_in, w_out, pe_k, pe_v, w_ck, w_cv, lb, norm_g):
    B, T, _ = h.shape
    pos = pos0 + jnp.arange(T)
    q, kc, vc, ks, vs, kw, vw, gl, hq, hf, hi, hg = split_cols(
        h @ w_in, [NSA_QW] + [NSA_KVW] * 6 + [NSA_GW, HGRN_KW, HGRN_KW, HGRN_VW, HGRN_VW])
    heads = lambda a, n, d: a.reshape(B, T, n, d)
    q = rope(heads(q, NSA_HEADS, HEAD_DIM), pos)
    kc, ks, kw = [rope(heads(a, NSA_KV, HEAD_DIM), pos) for a in (kc, ks, kw)]
    vc, vs, vw = [heads(a, NSA_KV, HEAD_DIM) for a in (vc, vs, vw)]
    new_rows = jnp.stack([kc, vc, ks, vs], axis=2)
    new_win = jnp.stack([kw, vw], axis=2)
    if past is None:
        rows, win_keys, w_off = new_rows, new_win, pos0
        s0 = jnp.zeros((B, HGRN_HEADS, HGRN_DK, HGRN_DV), jnp.float32)
        win_state = jnp.concatenate(
            [jnp.zeros((B, win_len) + new_win.shape[2:], new_win.dtype), new_win], axis=1)[:, -win_len:]
    else:
        past_rows, past_win, s0 = past
        rows = jnp.concatenate([past_rows, new_rows], axis=1)
        win_keys = jnp.concatenate([past_win, new_win], axis=1)
        w_off = pos0 - past_win.shape[1]
        win_state = win_keys[:, -win_len:]
    o_a = nsa_attention(q, rows, win_keys, heads(gl, NSA_HEADS, 3), pos0, w_off, pe_k, pe_v, w_ck, w_cv)
    lbh = lb.reshape(HGRN_HEADS, HGRN_DK)
    f = lbh + (1.0 - lbh) * jax.nn.sigmoid(heads(hf, HGRN_HEADS, HGRN_DK).astype(jnp.float32))
    o_b, s_new = hgrn2_scan(jax.nn.silu(heads(hq, HGRN_HEADS, HGRN_DK)), 1.0 - f,
                            heads(hi, HGRN_HEADS, HGRN_DV), jnp.log(f), s0)
    o_b = rms_norm(o_b, norm_g) * jax.nn.silu(heads(hg, HGRN_HEADS, HGRN_DV).astype(jnp.float32))
    mixed = jnp.concatenate([o_a, o_b.reshape(B, T, HGRN_VW).astype(h.dtype)], axis=-1)
    return mixed @ w_out, new_rows, win_state, s_new.astype(h.dtype)


def odd_mixer(h, pos0, past, w_in, w_out):
    B, T, _ = h.shape
    pos = pos0 + jnp.arange(T)
    q, k, v, qi, ki, wi = split_cols(h @ w_in, [DSA_QW, DSA_KVW, DSA_KVW, IDX_QW, IDX_DIM, IDX_HEADS])
    q = rope(q.reshape(B, T, DSA_HEADS, HEAD_DIM), pos)
    k = rope(k.reshape(B, T, DSA_KV, HEAD_DIM), pos)
    v = v.reshape(B, T, DSA_KV, HEAD_DIM)
    qi = rope(qi.reshape(B, T, IDX_HEADS, IDX_DIM), pos)
    ki = rope(ki[:, :, None, :], pos)[:, :, 0]
    new_kv = jnp.stack([k, v], axis=2)
    if past is None:
        kv_all, ki_all = new_kv, ki
    else:
        kv_all = jnp.concatenate([past[0], new_kv], axis=1)
        ki_all = jnp.concatenate([past[1], ki], axis=1)
    L = kv_all.shape[1]
    n_keep = min(IDX_TOPK, L // 4)
    s_pos = jnp.arange(L)
    scale = HEAD_DIM ** -0.5

    def block(i0, qb, q_b, qi_b, wi_b):
        t = pos0 + i0 + jnp.arange(qb)
        rel = jax.nn.relu(jnp.einsum('bthd,bsd->bths', qi_b, ki_all, preferred_element_type=jnp.float32) * IDX_DIM ** -0.5)
        score = jnp.einsum('bths,bth->bts', rel, wi_b.astype(jnp.float32) * IDX_HEADS ** -0.5)
        score = jnp.where(s_pos[None, None, :] <= t[None, :, None], score, NEG)
        _, sel = lax.top_k(score, n_keep)
        kv_g = jax.vmap(lambda a, i: a[i])(kv_all, sel)
        qg = q_b.reshape(B, qb, DSA_KV, DSA_HPG, HEAD_DIM) * scale
        s = jnp.einsum('btghd,btkgd->btghk', qg, kv_g[:, :, :, 0], preferred_element_type=jnp.float32)
        p = masked_softmax(s, (sel <= t[None, :, None])[:, :, None, None, :])
        o = jnp.einsum('btghk,btkgd->btghd', p, kv_g[:, :, :, 1].astype(jnp.float32))
        return o.reshape(B, qb, DSA_QW).astype(q_b.dtype)

    o = over_query_blocks(block, (q, qi, wi))
    return o @ w_out, new_kv, ki


def setup_inputs(seed: int = 0) -> dict:
    key = jax.random.key(seed)
    ks = jax.random.split(key, 40)

    def nrm(i, shape, scale=1.0):
        return jax.random.normal(ks[i], shape, jnp.float32) * scale

    def gain(i, shape):
        return 1.0 + nrm(i, shape, 0.05)

    d = D_MODEL
    n_pages = PAST_LEN // PAGE_SIZE
    n_used = DEC_BATCH * n_pages
    n_phys = n_used + max(1, n_used // 4)
    win_len = min(WINDOW, PAST_LEN)
    page_table = jax.random.permutation(ks[9], n_phys)[:n_used].reshape(DEC_BATCH, n_pages).astype(jnp.int32)
    return {
        'x_prompt': nrm(0, (BATCH, SEQ, d)),
        'x_sample': nrm(1, (DEC_BATCH, DEC_SEQ, d)),
        'cache_nsa': nrm(2, (N_EVEN, n_phys, PAGE_SIZE, 4, NSA_KV, HEAD_DIM)),
        'cache_nsa_win': nrm(3, (N_EVEN, DEC_BATCH, win_len, 2, NSA_KV, HEAD_DIM)),
        'state_hgrn': nrm(4, (N_EVEN, DEC_BATCH, HGRN_HEADS, HGRN_DK, HGRN_DV), 0.3),
        'cache_dsa_kv': nrm(5, (N_ODD, n_phys, PAGE_SIZE, 2, DSA_KV, HEAD_DIM)),
        'cache_dsa_idx': nrm(6, (N_ODD, n_phys, PAGE_SIZE, IDX_DIM)),
        'page_table': page_table,
        'c_prompt': nrm(7, (BATCH, d)),
        'c_sample': nrm(8, (DEC_BATCH, d)),
        'ada_w': nrm(10, (DEPTH, d, 6 * d), 0.5 * d ** -0.5),
        'ada_b': nrm(11, (DEPTH, 6 * d), 0.02),
        'norm1_g': gain(12, (DEPTH, d)),
        'norm2_g': gain(13, (DEPTH, d)),
        'final_g': gain(14, (d,)),
        'even_w_in': nrm(15, (N_EVEN, d, EVEN_IN), d ** -0.5),
        'even_w_out': nrm(16, (N_EVEN, EVEN_MIX, d), EVEN_MIX ** -0.5),
        'nsa_pe_k': nrm(17, (N_EVEN, CMP_BLK, HEAD_DIM), 0.1),
        'nsa_pe_v': nrm(18, (N_EVEN, CMP_BLK, HEAD_DIM), 0.1),
        'nsa_w_ck': nrm(19, (N_EVEN, CMP_BLK * HEAD_DIM, HEAD_DIM), (CMP_BLK * HEAD_DIM) ** -0.5),
        'nsa_w_cv': nrm(20, (N_EVEN, CMP_BLK * HEAD_DIM, HEAD_DIM), (CMP_BLK * HEAD_DIM) ** -0.5),
        'hgrn_lb_raw': nrm(21, (N_EVEN, HGRN_KW), 0.5),
        'hgrn_norm_g': gain(22, (N_EVEN, HGRN_DV)),
        'ffn_w1': nrm(23, (N_EVEN, d, D_FF), d ** -0.5),
        'ffn_w3': nrm(24, (N_EVEN, d, D_FF), d ** -0.5),
        'ffn_w2': nrm(25, (N_EVEN, D_FF, d), D_FF ** -0.5),
        'odd_w_in': nrm(26, (N_ODD, d, ODD_IN), d ** -0.5),
        'odd_w_out': nrm(27, (N_ODD, ODD_MIX, d), ODD_MIX ** -0.5),
        'router_w': nrm(28, (N_ODD, d, N_EXPERTS), d ** -0.5),
        'router_b': nrm(29, (N_ODD, N_EXPERTS), 0.01),
        'moe_w1': nrm(30, (N_ODD, N_EXPERTS, d, D_FF), d ** -0.5),
        'moe_w3': nrm(31, (N_ODD, N_EXPERTS, d, D_FF), d ** -0.5),
        'moe_w2': nrm(32, (N_ODD, N_EXPERTS, D_FF, d), D_FF ** -0.5),
    }


def reference(x_prompt, x_sample, cache_nsa, cache_nsa_win, state_hgrn, cache_dsa_kv, cache_dsa_idx,
              page_table, c_prompt, c_sample, ada_w, ada_b, norm1_g, norm2_g, final_g,
              even_w_in, even_w_out, nsa_pe_k, nsa_pe_v, nsa_w_ck, nsa_w_cv, hgrn_lb_raw, hgrn_norm_g,
              ffn_w1, ffn_w3, ffn_w2, odd_w_in, odd_w_out, router_w, router_b, moe_w1, moe_w3, moe_w2):
    past_len = page_table.shape[1] * PAGE_SIZE
    win_len = cache_nsa_win.shape[2]
    lb_soft = jax.nn.softmax(hgrn_lb_raw.astype(jnp.float32), axis=0)
    lower_bounds = jnp.cumsum(lb_soft, axis=0) - lb_soft[0]

    def run(x, c, pos0, sample):
        cs = jax.nn.silu(c)
        nsa_rows, nsa_win, hgrn_state, dsa_kv, dsa_idx = [], [], [], [], []
        for l in range(DEPTH):
            mod = (cs @ ada_w[l] + ada_b[l])[:, None, :]
            sh1, sc1, g1, sh2, sc2, g2 = jnp.split(mod, 6, axis=-1)
            h = rms_norm(x, norm1_g[l]) * (1 + sc1) + sh1
            i = l // 2
            if l % 2 == 0:
                past = None
                if sample:
                    past = (paged_rows(cache_nsa[i], page_table), cache_nsa_win[i], state_hgrn[i])
                o, rows, win, s_new = even_mixer(h, pos0, past, win_len, even_w_in[i], even_w_out[i],
                                                 nsa_pe_k[i], nsa_pe_v[i], nsa_w_ck[i], nsa_w_cv[i],
                                                 lower_bounds[i], hgrn_norm_g[i])
                nsa_rows.append(rows)
                nsa_win.append(win)
                hgrn_state.append(s_new)
            else:
                past = None
                if sample:
                    past = (paged_rows(cache_dsa_kv[i], page_table), paged_rows(cache_dsa_idx[i], page_table))
                o, kv, ki = odd_mixer(h, pos0, past, odd_w_in[i], odd_w_out[i])
                dsa_kv.append(kv)
                dsa_idx.append(ki)
            x = x + g1 * o
            h = rms_norm(x, norm2_g[l]) * (1 + sc2) + sh2
            if l % 2 == 0:
                f = swiglu(h, ffn_w1[i], ffn_w3[i], ffn_w2[i])
            else:
                f = moe_ffn(h, router_w[i], router_b[i], moe_w1[i], moe_w3[i], moe_w2[i])
            x = x + g2 * f
        return (rms_norm(x, final_g), jnp.stack(nsa_rows), jnp.stack(nsa_win), jnp.stack(hgrn_state),
                jnp.stack(dsa_kv), jnp.stack(dsa_idx))

    y_prompt, nsa_rows_p, nsa_win_p, hgrn_p, dsa_kv_p, dsa_idx_p = run(x_prompt, c_prompt, 0, False)
    y_sample, nsa_rows_s, nsa_win_s, hgrn_s, dsa_kv_s, dsa_idx_s = run(x_sample, c_sample, past_len, True)
    return (y_prompt, y_sample, nsa_rows_p, nsa_rows_s, nsa_win_p, nsa_win_s, hgrn_p, hgrn_s,
            dsa_kv_p, dsa_kv_s, dsa_idx_p, dsa_idx_s)
```

```python
import functools
import math

import numpy as np
import jax
import jax.numpy as jnp
from jax import lax
from jax.experimental import pallas as pl
from jax.experimental.pallas import tpu as pltpu

D_MODEL = 1024
DEPTH = 4
PAGE_SIZE = 128
HEAD_DIM = 64
NSA_HEADS = 8
NSA_KV = 2
NSA_HPG = NSA_HEADS // NSA_KV
CMP_STRIDE = 16
CMP_BLK = 2 * CMP_STRIDE
SEL_BLK = 64
TOP_N = 16
WINDOW = 512
FORCE_SCORE = 1.0e4
HGRN_HEADS = 4
HGRN_DK = 128
HGRN_DV = 128
HGRN_CHUNK = 64
DSA_HEADS = 16
DSA_KV = 4
DSA_HPG = DSA_HEADS // DSA_KV
IDX_HEADS = 8
IDX_DIM = 64
IDX_TOPK = 256
D_FF = 2816
N_EXPERTS = 8
TOP_K = 2
Q_BLK = 128
ROPE_THETA = 10000.0
EPS = 1e-6
NEG = -1.0e30
NSA_QW = NSA_HEADS * HEAD_DIM
NSA_KVW = NSA_KV * HEAD_DIM
NSA_GW = NSA_HEADS * 3
HGRN_KW = HGRN_HEADS * HGRN_DK
HGRN_VW = HGRN_HEADS * HGRN_DV
EVEN_IN = NSA_QW + 6 * NSA_KVW + NSA_GW + 2 * HGRN_KW + 2 * HGRN_VW
EVEN_MIX = NSA_QW + HGRN_VW
DSA_QW = DSA_HEADS * HEAD_DIM
DSA_KVW = DSA_KV * HEAD_DIM
IDX_QW = IDX_HEADS * IDX_DIM
ODD_IN = DSA_QW + 2 * DSA_KVW + IDX_QW + IDX_DIM + IDX_HEADS
ODD_MIX = DSA_QW

LANE = 128
ROW_TILE = 512
FF_CHUNK = 256
MOE_TILE = 256
VMEM_LIMIT = 56 * 1024 * 1024

F32 = jnp.float32
BF16 = jnp.bfloat16


def _round_up(n, m):
    return -(-n // m) * m


def _params(*sem):
    return pltpu.CompilerParams(dimension_semantics=sem, vmem_limit_bytes=VMEM_LIMIT)


def _norm_mod(x, scale, shift):
    ms = jnp.mean(x * x, axis=-1, keepdims=True)
    return x * lax.rsqrt(ms + EPS) * scale + shift


def _mod_spec(mod, tm):
    if mod.shape[1] == 1:
        return pl.BlockSpec((1, 1, mod.shape[2]), lambda b, i: (b, 0, 0))
    return pl.BlockSpec((1, tm, mod.shape[2]), lambda b, i: (b, i, 0))


def _ada_body(c_ref, w_ref, b_ref, o_ref):
    c = c_ref[...]
    cs = (c * jax.nn.sigmoid(c)).astype(BF16)
    o_ref[0] = jnp.dot(cs, w_ref[0].astype(BF16), preferred_element_type=F32) + b_ref[0]


def ada_modulation(c_all, ada_w, ada_b):
    R, D = c_all.shape
    N = ada_w.shape[2]
    tn = 1536
    return pl.pallas_call(
        _ada_body,
        grid=(DEPTH, N // tn),
        in_specs=[pl.BlockSpec((R, D), lambda l, j: (0, 0)),
                  pl.BlockSpec((1, D, tn), lambda l, j: (l, 0, j)),
                  pl.BlockSpec((1, 1, tn), lambda l, j: (l, 0, j))],
        out_specs=pl.BlockSpec((1, R, tn), lambda l, j: (l, 0, j)),
        out_shape=jax.ShapeDtypeStruct((DEPTH, R, N), F32),
        compiler_params=_params("arbitrary", "arbitrary"),
        name="ada_modulation",
    )(c_all, ada_w, ada_b.reshape(DEPTH, 1, N))


def _norm_proj_body(x_ref, sc_ref, sh_ref, w_ref, o_ref, h_ref=None):
    h = _norm_mod(x_ref[0], sc_ref[0], sh_ref[0]).astype(BF16)
    o_ref[0] = jnp.dot(h, w_ref[...], preferred_element_type=F32)
    if h_ref is not None:
        h_ref[0] = h


def norm_proj(x, scale, shift, w, with_h=False):
    B, T, D = x.shape
    N = w.shape[1]
    tm = min(T, ROW_TILE)
    out_shape = [jax.ShapeDtypeStruct((B, T, N), F32)]
    out_specs = [pl.BlockSpec((1, tm, N), lambda b, i: (b, i, 0))]
    if with_h:
        out_shape.append(jax.ShapeDtypeStruct((B, T, D), BF16))
        out_specs.append(pl.BlockSpec((1, tm, D), lambda b, i: (b, i, 0)))
    res = pl.pallas_call(
        _norm_proj_body,
        grid=(B, T // tm),
        in_specs=[pl.BlockSpec((1, tm, D), lambda b, i: (b, i, 0)),
                  _mod_spec(scale, tm), _mod_spec(shift, tm),
                  pl.BlockSpec((D, N), lambda b, i: (0, 0))],
        out_specs=out_specs,
        out_shape=out_shape,
        compiler_params=_params("parallel", "parallel"),
        name="norm_proj",
    )(x, scale, shift, w)
    return res if with_h else res[0]


def _out_proj_body(a_ref, w_ref, x_ref, g_ref, o_ref):
    y = jnp.dot(a_ref[0].astype(BF16), w_ref[...], preferred_element_type=F32)
    o_ref[0] = x_ref[0] + g_ref[0] * y


def out_proj_residual(a, w, x, gate):
    B, T, D = x.shape
    K = a.shape[2]
    tm = min(T, ROW_TILE)
    return pl.pallas_call(
        _out_proj_body,
        grid=(B, T // tm),
        in_specs=[pl.BlockSpec((1, tm, K), lambda b, i: (b, i, 0)),
                  pl.BlockSpec((K, D), lambda b, i: (0, 0)),
                  pl.BlockSpec((1, tm, D), lambda b, i: (b, i, 0)),
                  _mod_spec(gate, tm)],
        out_specs=pl.BlockSpec((1, tm, D), lambda b, i: (b, i, 0)),
        out_shape=jax.ShapeDtypeStruct((B, T, D), F32),
        compiler_params=_params("parallel", "parallel"),
        name="out_proj_residual",
    )(a, w, x, gate)


def _swiglu_acc(h, w1_ref, w3_ref, w2_ref, acc_ref):
    for c in range(D_FF // FF_CHUNK):
        cols = slice(c * FF_CHUNK, (c + 1) * FF_CHUNK)
        u = jnp.dot(h, w1_ref[:, cols], preferred_element_type=F32)
        v = jnp.dot(h, w3_ref[:, cols], preferred_element_type=F32)
        a = (u * jax.nn.sigmoid(u) * v).astype(BF16)
        part = jnp.dot(a, w2_ref[cols, :], preferred_element_type=F32)
        if c == 0:
            acc_ref[...] = part
        else:
            acc_ref[...] += part


def _ffn_body(x_ref, sc_ref, sh_ref, g_ref, w1_ref, w3_ref, w2_ref, o_ref, acc_ref):
    x = x_ref[0]
    h = _norm_mod(x, sc_ref[0], sh_ref[0]).astype(BF16)
    _swiglu_acc(h, w1_ref, w3_ref, w2_ref, acc_ref)
    o_ref[0] = x + g_ref[0] * acc_ref[...]


def ffn_residual(x, scale, shift, gate, w1, w3, w2):
    B, T, D = x.shape
    tm = min(T, ROW_TILE)
    wspec = lambda shape: pl.BlockSpec(shape, lambda b, i: (0, 0))
    return pl.pallas_call(
        _ffn_body,
        grid=(B, T // tm),
        in_specs=[pl.BlockSpec((1, tm, D), lambda b, i: (b, i, 0)),
                  _mod_spec(scale, tm), _mod_spec(shift, tm), _mod_spec(gate, tm),
                  wspec((D, D_FF)), wspec((D, D_FF)), wspec((D_FF, D))],
        out_specs=pl.BlockSpec((1, tm, D), lambda b, i: (b, i, 0)),
        out_shape=jax.ShapeDtypeStruct((B, T, D), F32),
        scratch_shapes=[pltpu.VMEM((tm, D), F32)],
        compiler_params=_params("parallel", "parallel"),
        name="ffn_residual",
    )(x, scale, shift, gate, w1, w3, w2)


def _moe_body(te_ref, nt_ref, h_ref, ws_ref, w1_ref, w3_ref, w2_ref, o_ref, acc_ref):
    i = pl.program_id(0)

    @pl.when(i < nt_ref[0])
    def _():
        _swiglu_acc(h_ref[...], w1_ref.at[0], w3_ref.at[0], w2_ref.at[0], acc_ref)
        o_ref[...] = ws_ref[...] * acc_ref[...]

    @pl.when(i >= nt_ref[0])
    def _():
        o_ref[...] = jnp.zeros_like(o_ref)


def moe_grouped_swiglu(h_sorted, w_slot, tile_expert, n_tiles_used, w1, w3, w2):
    S, D = h_sorted.shape
    tm = MOE_TILE
    n_tiles = S // tm
    wspec = lambda shape: pl.BlockSpec((1,) + shape, lambda i, te, nt: (te[i], 0, 0))
    grid_spec = pltpu.PrefetchScalarGridSpec(
        num_scalar_prefetch=2,
        grid=(n_tiles,),
        in_specs=[pl.BlockSpec((tm, D), lambda i, te, nt: (i, 0)),
                  pl.BlockSpec((tm, 1), lambda i, te, nt: (i, 0)),
                  wspec((D, D_FF)), wspec((D, D_FF)), wspec((D_FF, D))],
        out_specs=pl.BlockSpec((tm, D), lambda i, te, nt: (i, 0)),
        scratch_shapes=[pltpu.VMEM((tm, D), F32)],
    )
    return pl.pallas_call(
        _moe_body,
        grid_spec=grid_spec,
        out_shape=jax.ShapeDtypeStruct((S, D), F32),
        compiler_params=_params("arbitrary"),
        name="moe_grouped_swiglu",
    )(tile_expert, n_tiles_used, h_sorted, w_slot, w1, w3, w2)


def _final_norm_body(x_ref, g_ref, o_ref):
    x = x_ref[0]
    ms = jnp.mean(x * x, axis=-1, keepdims=True)
    o_ref[0] = x * lax.rsqrt(ms + EPS) * g_ref[...]


def final_norm(x, g):
    B, T, D = x.shape
    tm = min(T, ROW_TILE)
    return pl.pallas_call(
        _final_norm_body,
        grid=(B, T // tm),
        in_specs=[pl.BlockSpec((1, tm, D), lambda b, i: (b, i, 0)),
                  pl.BlockSpec((1, D), lambda b, i: (0, 0))],
        out_specs=pl.BlockSpec((1, tm, D), lambda b, i: (b, i, 0)),
        out_shape=jax.ShapeDtypeStruct((B, T, D), F32),
        compiler_params=_params("parallel", "parallel"),
        name="final_norm",
    )(x, g.reshape(1, D))


def rms_norm(x, g):
    x32 = x.astype(jnp.float32)
    y = x32 * lax.rsqrt(jnp.mean(x32 * x32, axis=-1, keepdims=True) + EPS)
    return (y * g.astype(jnp.float32)).astype(x.dtype)


def rope(x, pos):
    half = x.shape[-1] // 2
    inv = ROPE_THETA ** (-jnp.arange(half, dtype=jnp.float32) / half)
    ang = pos.astype(jnp.float32)[:, None] * inv[None, :]
    cos = jnp.cos(ang)[:, None, :]
    sin = jnp.sin(ang)[:, None, :]
    x32 = x.astype(jnp.float32)
    x1, x2 = x32[..., :half], x32[..., half:]
    return jnp.concatenate([x1 * cos - x2 * sin, x2 * cos + x1 * sin], axis=-1).astype(x.dtype)


def masked_softmax(s, mask):
    s = jnp.where(mask, s.astype(jnp.float32), NEG)
    m = jnp.max(s, axis=-1, keepdims=True)
    e = jnp.where(mask, jnp.exp(s - m), 0.0)
    return e / jnp.maximum(jnp.sum(e, axis=-1, keepdims=True), 1e-30)


def split_cols(a, sizes):
    cuts = np.cumsum(np.array(sizes))[:-1]
    return jnp.split(a, [int(c) for c in cuts], axis=-1)


def paged_rows(pool, page_table):
    g = pool[page_table]
    return g.reshape(g.shape[0], g.shape[1] * g.shape[2], *g.shape[3:])


def over_query_blocks(fn, arrays):
    B, T = arrays[0].shape[:2]
    qb = math.gcd(T, Q_BLK)
    nb = T // qb
    blocks = tuple(a.reshape(B, nb, qb, *a.shape[2:]).swapaxes(0, 1) for a in arrays)
    out = lax.map(lambda xs: fn(xs[0] * qb, qb, *xs[1:]), (jnp.arange(nb),) + blocks)
    return out.swapaxes(0, 1).reshape(B, T, *out.shape[3:])


def compress_blocks(x, pe, w):
    B, Lp, G, Dh = x.shape
    halves = x.reshape(B, Lp // CMP_STRIDE, CMP_STRIDE, G, Dh)
    blk = jnp.concatenate([halves[:, :-1], halves[:, 1:]], axis=2) + pe[None, None, :, None, :]
    return jnp.einsum('bnlgd,lde->bnge', blk, w.reshape(CMP_BLK, Dh, Dh))


def nsa_attention(q, rows, win_keys, gates, q_off, w_off, pe_k, pe_v, w_ck, w_cv):
    B, L = rows.shape[:2]
    Lp = -(-L // SEL_BLK) * SEL_BLK
    rows = jnp.pad(rows, ((0, 0), (0, Lp - L), (0, 0), (0, 0), (0, 0)))
    k_cmp = compress_blocks(rows[:, :, 0], pe_k, w_ck)
    v_cmp = compress_blocks(rows[:, :, 1], pe_v, w_cv).astype(jnp.float32)
    n_cmp = k_cmp.shape[1]
    n_sel = Lp // SEL_BLK
    k_sel = min(TOP_N, n_sel)
    ks_blk = rows[:, :, 2].reshape(B, n_sel, SEL_BLK, NSA_KV, HEAD_DIM).transpose(0, 3, 1, 2, 4)
    vs_blk = rows[:, :, 3].reshape(B, n_sel, SEL_BLK, NSA_KV, HEAD_DIM).transpose(0, 3, 1, 2, 4)
    kw_pad = jnp.pad(win_keys, ((0, 0), (WINDOW, 0), (0, 0), (0, 0), (0, 0)))
    cmp_end = CMP_STRIDE * jnp.arange(n_cmp) + CMP_BLK
    sel_j = jnp.arange(n_sel)
    scale = HEAD_DIM ** -0.5
    take = jax.vmap(jax.vmap(lambda a, i: a[i]))

    def block(i0, qb, q_b, g_b):
        t = q_off + i0 + jnp.arange(qb)
        qg = q_b.reshape(B, qb, NSA_KV, NSA_HPG, HEAD_DIM) * scale
        s_c = jnp.einsum('btghd,bngd->btghn', qg, k_cmp, preferred_element_type=jnp.float32)
        p_c = masked_softmax(s_c, (cmp_end[None, :] <= t[:, None] + 1)[None, :, None, None, :])
        o_c = jnp.einsum('btghn,bngd->btghd', p_c, v_cmp)
        imp = jnp.pad(p_c.sum(3), ((0, 0), (0, 0), (0, 0), (1, 0)))
        imp = imp.reshape(B, qb, NSA_KV, n_sel, SEL_BLK // CMP_STRIDE).sum(-1)
        cur = t // SEL_BLK
        forced = (sel_j[None] == 0) | (sel_j[None] == cur[:, None]) | (sel_j[None] == cur[:, None] - 1)
        visible = sel_j[None] <= cur[:, None]
        score = jnp.where(visible[None, :, None, :], imp + FORCE_SCORE * forced[None, :, None, :], NEG)
        top_s, top_i = lax.top_k(score, k_sel)
        top_s = top_s.transpose(0, 2, 1, 3)
        top_i = top_i.transpose(0, 2, 1, 3)
        k_g = take(ks_blk, top_i)
        v_g = take(vs_blk, top_i)
        kpos = top_i[..., None] * SEL_BLK + jnp.arange(SEL_BLK)
        ok = (top_s[..., None] > 0.5 * NEG) & (kpos <= t[None, None, :, None, None])
        q2 = qg.transpose(0, 2, 1, 3, 4)
        s_s = jnp.einsum('bgthd,bgtkrd->bgthkr', q2, k_g, preferred_element_type=jnp.float32)
        s_s = s_s.reshape(B, NSA_KV, qb, NSA_HPG, k_sel * SEL_BLK)
        p_s = masked_softmax(s_s, ok.reshape(B, NSA_KV, qb, 1, k_sel * SEL_BLK))
        o_s = jnp.einsum('bgthm,bgtmd->bgthd', p_s,
                         v_g.reshape(B, NSA_KV, qb, k_sel * SEL_BLK, HEAD_DIM).astype(jnp.float32))
        o_s = o_s.transpose(0, 2, 1, 3, 4)
        start = q_off - w_off + i0 + 1
        span = WINDOW + qb - 1
        kw = lax.dynamic_slice_in_dim(kw_pad, start, span, axis=1)
        kidx = start + jnp.arange(span)
        kabs = kidx - WINDOW + w_off
        okw = (kidx[None] >= WINDOW) & (kabs[None] <= t[:, None]) & (kabs[None] > t[:, None] - WINDOW)
        s_w = jnp.einsum('btghd,bsgd->btghs', qg, kw[:, :, 0], preferred_element_type=jnp.float32)
        p_w = masked_softmax(s_w, okw[None, :, None, None, :])
        o_w = jnp.einsum('btghs,bsgd->btghd', p_w, kw[:, :, 1].astype(jnp.float32))
        g = jax.nn.sigmoid(g_b.astype(jnp.float32)).reshape(B, qb, NSA_KV, NSA_HPG, 3)
        o = g[..., 0:1] * o_c + g[..., 1:2] * o_s + g[..., 2:3] * o_w
        return o.reshape(B, qb, NSA_QW).astype(q_b.dtype)

    return over_query_blocks(block, (q, gates))


def hgrn2_scan(q, k, v, logf, s0):
    q, k, v, logf = [a.astype(jnp.float32) for a in (q, k, v, logf)]
    B, T, H, DK = q.shape
    C = math.gcd(T, HGRN_CHUNK)
    nc = T // C
    to_chunks = lambda a: a.reshape(B, nc, C, *a.shape[2:]).swapaxes(0, 1)
    tri = jnp.tril(jnp.ones((C, C), dtype=bool))

    def step(S, inp):
        qc, kc, vc, gc = inp
        b = jnp.cumsum(gc, axis=1)
        o = jnp.einsum('bthk,bhkv->bthv', qc * jnp.exp(b), S)
        diff = b[:, :, None] - b[:, None, :]
        decay = jnp.exp(jnp.where(tri[None, :, :, None, None], diff, -jnp.inf))
        A = jnp.einsum('bthk,btshk,bshk->bths', qc, decay, kc)
        o = o + jnp.einsum('bths,bshv->bthv', A, vc)
        bl = b[:, -1]
        S = jnp.exp(bl)[..., None] * S + jnp.einsum('bshk,bshv->bhkv', kc * jnp.exp(bl[:, None] - b), vc)
        return S, o

    S, o = lax.scan(step, s0.astype(jnp.float32), (to_chunks(q), to_chunks(k), to_chunks(v), to_chunks(logf)))
    return o.swapaxes(0, 1).reshape(B, T, H, v.shape[-1]), S


def even_mixer(proj, pos0, past, win_len, pe_k, pe_v, w_ck, w_cv, lb, norm_g):
    B, T, _ = proj.shape
    pos = pos0 + jnp.arange(T)
    q, kc, vc, ks, vs, kw, vw, gl, hq, hf, hi, hg = split_cols(
        proj[..., :EVEN_IN], [NSA_QW] + [NSA_KVW] * 6 + [NSA_GW, HGRN_KW, HGRN_KW, HGRN_VW, HGRN_VW])
    heads = lambda a, n, d: a.reshape(B, T, n, d)
    q = rope(heads(q, NSA_HEADS, HEAD_DIM), pos)
    kc, ks, kw = [rope(heads(a, NSA_KV, HEAD_DIM), pos) for a in (kc, ks, kw)]
    vc, vs, vw = [heads(a, NSA_KV, HEAD_DIM) for a in (vc, vs, vw)]
    new_rows = jnp.stack([kc, vc, ks, vs], axis=2)
    new_win = jnp.stack([kw, vw], axis=2)
    if past is None:
        rows, win_keys, w_off = new_rows, new_win, pos0
        s0 = jnp.zeros((B, HGRN_HEADS, HGRN_DK, HGRN_DV), jnp.float32)
        win_state = jnp.concatenate(
            [jnp.zeros((B, win_len) + new_win.shape[2:], new_win.dtype), new_win], axis=1)[:, -win_len:]
    else:
        past_rows, past_win, s0 = past
        rows = jnp.concatenate([past_rows, new_rows], axis=1)
        win_keys = jnp.concatenate([past_win, new_win], axis=1)
        w_off = pos0 - past_win.shape[1]
        win_state = win_keys[:, -win_len:]
    o_a = nsa_attention(q, rows, win_keys, heads(gl, NSA_HEADS, 3), pos0, w_off, pe_k, pe_v, w_ck, w_cv)
    lbh = lb.reshape(HGRN_HEADS, HGRN_DK)
    f = lbh + (1.0 - lbh) * jax.nn.sigmoid(heads(hf, HGRN_HEADS, HGRN_DK).astype(jnp.float32))
    o_b, s_new = hgrn2_scan(jax.nn.silu(heads(hq, HGRN_HEADS, HGRN_DK)), 1.0 - f,
                            heads(hi, HGRN_HEADS, HGRN_DV), jnp.log(f), s0)
    o_b = rms_norm(o_b, norm_g) * jax.nn.silu(heads(hg, HGRN_HEADS, HGRN_DV).astype(jnp.float32))
    mixed = jnp.concatenate([o_a, o_b.reshape(B, T, HGRN_VW).astype(proj.dtype)], axis=-1)
    return mixed, new_rows, win_state, s_new.astype(proj.dtype)


def odd_mixer(proj, pos0, past):
    B, T, _ = proj.shape
    pos = pos0 + jnp.arange(T)
    q, k, v, qi, ki, wi = split_cols(proj[..., :ODD_IN], [DSA_QW, DSA_KVW, DSA_KVW, IDX_QW, IDX_DIM, IDX_HEADS])
    q = rope(q.reshape(B, T, DSA_HEADS, HEAD_DIM), pos)
    k = rope(k.reshape(B, T, DSA_KV, HEAD_DIM), pos)
    v = v.reshape(B, T, DSA_KV, HEAD_DIM)
    qi = rope(qi.reshape(B, T, IDX_HEADS, IDX_DIM), pos)
    ki = rope(ki[:, :, None, :], pos)[:, :, 0]
    new_kv = jnp.stack([k, v], axis=2)
    if past is None:
        kv_all, ki_all = new_kv, ki
    else:
        kv_all = jnp.concatenate([past[0], new_kv], axis=1)
        ki_all = jnp.concatenate([past[1], ki], axis=1)
    L = kv_all.shape[1]
    n_keep = min(IDX_TOPK, L // 4)
    s_pos = jnp.arange(L)
    scale = HEAD_DIM ** -0.5

    def block(i0, qb, q_b, qi_b, wi_b):
        t = pos0 + i0 + jnp.arange(qb)
        rel = jax.nn.relu(jnp.einsum('bthd,bsd->bths', qi_b, ki_all, preferred_element_type=jnp.float32) * IDX_DIM ** -0.5)
        score = jnp.einsum('bths,bth->bts', rel, wi_b.astype(jnp.float32) * IDX_HEADS ** -0.5)
        score = jnp.where(s_pos[None, None, :] <= t[None, :, None], score, NEG)
        _, sel = lax.top_k(score, n_keep)
        kv_g = jax.vmap(lambda a, i: a[i])(kv_all, sel)
        qg = q_b.reshape(B, qb, DSA_KV, DSA_HPG, HEAD_DIM) * scale
        s = jnp.einsum('btghd,btkgd->btghk', qg, kv_g[:, :, :, 0], preferred_element_type=jnp.float32)
        p = masked_softmax(s, (sel <= t[None, :, None])[:, :, None, None, :])
        o = jnp.einsum('btghk,btkgd->btghd', p, kv_g[:, :, :, 1].astype(jnp.float32))
        return o.reshape(B, qb, DSA_QW).astype(q_b.dtype)

    o = over_query_blocks(block, (q, qi, wi))
    return o, new_kv, ki


def moe_ffn_residual(xs, hs, logits_list, gates, w1, w3, w2):
    D = D_MODEL
    h_all = jnp.concatenate([h.reshape(-1, D) for h in hs], axis=0)
    logits = jnp.concatenate([lg.reshape(-1, lg.shape[-1])[:, :N_EXPERTS] for lg in logits_list], axis=0)
    n_tok = h_all.shape[0]
    top_v, top_i = lax.top_k(logits, TOP_K)
    weights = jax.nn.softmax(top_v, axis=-1)
    e_flat = top_i.reshape(-1)
    onehot = (e_flat[:, None] == jnp.arange(N_EXPERTS)[None, :]).astype(jnp.int32)
    csum = jnp.cumsum(onehot, axis=0)
    counts = csum[-1]
    rank = jnp.take_along_axis(csum, e_flat[:, None], axis=1)[:, 0] - 1
    padded = ((counts + MOE_TILE - 1) // MOE_TILE) * MOE_TILE
    group_end = jnp.cumsum(padded)
    group_start = group_end - padded
    slot = group_start[e_flat] + rank
    n_slots = _round_up(n_tok * TOP_K, MOE_TILE) + N_EXPERTS * MOE_TILE
    n_tiles = n_slots // MOE_TILE
    tok_of_slot = jnp.zeros((n_slots,), jnp.int32).at[slot].set(jnp.arange(n_tok * TOP_K, dtype=jnp.int32) // TOP_K)
    w_slot = jnp.zeros((n_slots,), F32).at[slot].set(weights.reshape(-1))
    tile_start = jnp.arange(n_tiles, dtype=jnp.int32) * MOE_TILE
    tile_expert = jnp.minimum(jnp.sum(tile_start[:, None] >= group_end[None, :], axis=1), N_EXPERTS - 1).astype(jnp.int32)
    n_used = (group_end[-1] // MOE_TILE).astype(jnp.int32).reshape(1)
    h_sorted = h_all[tok_of_slot]
    y_slot = moe_grouped_swiglu(h_sorted, w_slot.reshape(n_slots, 1), tile_expert, n_used, w1, w3, w2)
    slot2 = slot.reshape(n_tok, TOP_K)
    y = y_slot[slot2[:, 0]] + y_slot[slot2[:, 1]]
    outs = []
    off = 0
    for x, g in zip(xs, gates):
        n = x.shape[0] * x.shape[1]
        outs.append(x + g * y[off:off + n].reshape(x.shape))
        off += n
    return outs


def kernel(x_prompt, x_sample, cache_nsa, cache_nsa_win, state_hgrn, cache_dsa_kv, cache_dsa_idx, page_table, c_prompt, c_sample, ada_w, ada_b, norm1_g, norm2_g, final_g, even_w_in, even_w_out, nsa_pe_k, nsa_pe_v, nsa_w_ck, nsa_w_cv, hgrn_lb_raw, hgrn_norm_g, ffn_w1, ffn_w3, ffn_w2, odd_w_in, odd_w_out, router_w, router_b, moe_w1, moe_w3, moe_w2):
    D = D_MODEL
    past_len = page_table.shape[1] * PAGE_SIZE
    win_len = cache_nsa_win.shape[2]
    Bp, Tp = x_prompt.shape[:2]
    Bs, Ts = x_sample.shape[:2]
    lb_soft = jax.nn.softmax(hgrn_lb_raw.astype(F32), axis=0)
    lower_bounds = jnp.cumsum(lb_soft, axis=0) - lb_soft[0]

    R = _round_up(Bp + Bs, 8)
    c_all = jnp.zeros((R, D), F32).at[:Bp].set(c_prompt).at[Bp:Bp + Bs].set(c_sample)
    mods = ada_modulation(c_all, ada_w, ada_b)

    def group_mods(l, lo, n, per_token_rows):
        m = mods[l, lo:lo + n].reshape(n, 6, D)
        sh1, sc1, g1, sh2, sc2, g2 = [m[:, j] for j in range(6)]
        s1 = norm1_g[l][None] * (1.0 + sc1)
        s2 = norm2_g[l][None] * (1.0 + sc2)
        vecs = [s1, sh1, g1, s2, sh2, g2]
        if per_token_rows:
            return [jnp.repeat(v, per_token_rows, axis=0)[None] for v in vecs]
        return [v[:, None, :] for v in vecs]

    def pad_cols(w, n):
        return jnp.pad(w, ((0, 0), (0, n - w.shape[1]))).astype(BF16)

    xp = x_prompt
    xs = x_sample.reshape(1, Bs * Ts, D)
    outs_p = dict(rows=[], win=[], st=[], kv=[], idx=[])
    outs_s = dict(rows=[], win=[], st=[], kv=[], idx=[])
    for l in range(DEPTH):
        i = l // 2
        mp = group_mods(l, 0, Bp, 0)
        msm = group_mods(l, Bp, Bs, Ts)
        if l % 2 == 0:
            w_in = pad_cols(even_w_in[i], _round_up(EVEN_IN, LANE))
            w_out = even_w_out[i].astype(BF16)
            w1, w3, w2 = ffn_w1[i].astype(BF16), ffn_w3[i].astype(BF16), ffn_w2[i].astype(BF16)
            new_x = []
            for (x, m, od, sample) in ((xp, mp, outs_p, False), (xs, msm, outs_s, True)):
                proj = norm_proj(x, m[0], m[1], w_in)
                if sample:
                    proj = proj.reshape(Bs, Ts, -1)
                    past = (paged_rows(cache_nsa[i], page_table), cache_nsa_win[i], state_hgrn[i])
                    pos0 = past_len
                else:
                    past, pos0 = None, 0
                mixed, rows, win, s_new = even_mixer(proj, pos0, past, win_len, nsa_pe_k[i], nsa_pe_v[i],
                                                     nsa_w_ck[i], nsa_w_cv[i], lower_bounds[i], hgrn_norm_g[i])
                od['rows'].append(rows)
                od['win'].append(win)
                od['st'].append(s_new)
                mixed = mixed.reshape(x.shape[0], x.shape[1], EVEN_MIX)
                x = out_proj_residual(mixed, w_out, x, m[2])
                x = ffn_residual(x, m[3], m[4], m[5], w1, w3, w2)
                new_x.append(x)
            xp, xs = new_x
        else:
            w_in = pad_cols(odd_w_in[i], _round_up(ODD_IN, LANE))
            w_out = odd_w_out[i].astype(BF16)
            w_r = pad_cols(router_w[i], LANE)
            w1, w3, w2 = moe_w1[i].astype(BF16), moe_w3[i].astype(BF16), moe_w2[i].astype(BF16)
            mid_x, hs, lgs, gts = [], [], [], []
            for (x, m, od, sample) in ((xp, mp, outs_p, False), (xs, msm, outs_s, True)):
                proj = norm_proj(x, m[0], m[1], w_in)
                if sample:
                    proj = proj.reshape(Bs, Ts, -1)
                    past = (paged_rows(cache_dsa_kv[i], page_table), paged_rows(cache_dsa_idx[i], page_table))
                    pos0 = past_len
                else:
                    past, pos0 = None, 0
                o, kv, ki = odd_mixer(proj, pos0, past)
                od['kv'].append(kv)
                od['idx'].append(ki)
                o = o.reshape(x.shape[0], x.shape[1], ODD_MIX)
                x = out_proj_residual(o, w_out, x, m[2])
                logits, h = norm_proj(x, m[3], m[4], w_r, with_h=True)
                logits = logits[..., :N_EXPERTS] + router_b[i].astype(F32)
                mid_x.append(x)
                hs.append(h)
                lgs.append(logits)
                gts.append(m[5])
            xp, xs = moe_ffn_residual(mid_x, hs, lgs, gts, w1, w3, w2)
    y_prompt = final_norm(xp, final_g)
    y_sample = final_norm(xs, final_g).reshape(Bs, Ts, D)
    st = lambda od, k: jnp.stack(od[k])
    return (y_prompt, y_sample, st(outs_p, 'rows'), st(outs_s, 'rows'), st(outs_p, 'win'), st(outs_s, 'win'),
            st(outs_p, 'st'), st(outs_s, 'st'), st(outs_p, 'kv'), st(outs_s, 'kv'),
            st(outs_p, 'idx'), st(outs_s, 'idx'))
```

```python
import functools
import math

import numpy as np
import jax
import jax.numpy as jnp
from jax import lax
from jax.experimental import pallas as pl
from jax.experimental.pallas import tpu as pltpu

D_MODEL = 1024
DEPTH = 4
PAGE_SIZE = 128
HEAD_DIM = 64
NSA_HEADS = 8
NSA_KV = 2
NSA_HPG = NSA_HEADS // NSA_KV
CMP_STRIDE = 16
CMP_BLK = 2 * CMP_STRIDE
SEL_BLK = 64
TOP_N = 16
WINDOW = 512
FORCE_SCORE = 1.0e4
HGRN_HEADS = 4
HGRN_DK = 128
HGRN_DV = 128
HGRN_CHUNK = 64
DSA_HEADS = 16
DSA_KV = 4
DSA_HPG = DSA_HEADS // DSA_KV
IDX_HEADS = 8
IDX_DIM = 64
IDX_TOPK = 256
D_FF = 2816
N_EXPERTS = 8
TOP_K = 2
Q_BLK = 128
ROPE_THETA = 10000.0
EPS = 1e-6
NEG = -1.0e30
NSA_QW = NSA_HEADS * HEAD_DIM
NSA_KVW = NSA_KV * HEAD_DIM
NSA_GW = NSA_HEADS * 3
HGRN_KW = HGRN_HEADS * HGRN_DK
HGRN_VW = HGRN_HEADS * HGRN_DV
EVEN_IN = NSA_QW + 6 * NSA_KVW + NSA_GW + 2 * HGRN_KW + 2 * HGRN_VW
EVEN_MIX = NSA_QW + HGRN_VW
DSA_QW = DSA_HEADS * HEAD_DIM
DSA_KVW = DSA_KV * HEAD_DIM
IDX_QW = IDX_HEADS * IDX_DIM
ODD_IN = DSA_QW + 2 * DSA_KVW + IDX_QW + IDX_DIM + IDX_HEADS
ODD_MIX = DSA_QW

LANE = 128
ROW_TILE = 512
FF_CHUNK = 256
MOE_TILE = 256
PROJ_CHUNK = 512
ATT_TQ = 128
KEY_BLK = 256
HGRN_SUB = 16
HGRN_TILE = 256
INT_MIN = -2 ** 31
VMEM_LIMIT = 56 * 1024 * 1024

F32 = jnp.float32
BF16 = jnp.bfloat16


def _round_up(n, m):
    return -(-n // m) * m


def _params(*sem):
    return pltpu.CompilerParams(dimension_semantics=sem, vmem_limit_bytes=VMEM_LIMIT)


def _norm_mod(x, scale, shift):
    ms = jnp.mean(x * x, axis=-1, keepdims=True)
    return x * lax.rsqrt(ms + EPS) * scale + shift


def _mod_spec(mod, tm):
    if mod.shape[1] == 1:
        return pl.BlockSpec((1, 1, mod.shape[2]), lambda b, i: (b, 0, 0))
    return pl.BlockSpec((1, tm, mod.shape[2]), lambda b, i: (b, i, 0))


def _ada_body(c_ref, w_ref, b_ref, o_ref):
    c = c_ref[...]
    cs = (c * jax.nn.sigmoid(c)).astype(BF16)
    o_ref[0] = jnp.dot(cs, w_ref[0].astype(BF16), preferred_element_type=F32) + b_ref[0]


def ada_modulation(c_all, ada_w, ada_b):
    R, D = c_all.shape
    N = ada_w.shape[2]
    tn = 1536
    return pl.pallas_call(
        _ada_body,
        grid=(DEPTH, N // tn),
        in_specs=[pl.BlockSpec((R, D), lambda l, j: (0, 0)),
                  pl.BlockSpec((1, D, tn), lambda l, j: (l, 0, j)),
                  pl.BlockSpec((1, 1, tn), lambda l, j: (l, 0, j))],
        out_specs=pl.BlockSpec((1, R, tn), lambda l, j: (l, 0, j)),
        out_shape=jax.ShapeDtypeStruct((DEPTH, R, N), F32),
        compiler_params=_params("arbitrary", "arbitrary"),
        name="ada_modulation",
    )(c_all, ada_w, ada_b.reshape(DEPTH, 1, N))


def _norm_proj_body(x_ref, sc_ref, sh_ref, w_ref, o_ref, h_ref=None):
    h = _norm_mod(x_ref[0], sc_ref[0], sh_ref[0]).astype(BF16)
    o_ref[0] = jnp.dot(h, w_ref[...], preferred_element_type=F32)
    if h_ref is not None:
        h_ref[0] = h


def norm_proj(x, scale, shift, w, with_h=False):
    B, T, D = x.shape
    N = w.shape[1]
    tm = min(T, ROW_TILE)
    out_shape = [jax.ShapeDtypeStruct((B, T, N), F32)]
    out_specs = [pl.BlockSpec((1, tm, N), lambda b, i: (b, i, 0))]
    if with_h:
        out_shape.append(jax.ShapeDtypeStruct((B, T, D), BF16))
        out_specs.append(pl.BlockSpec((1, tm, D), lambda b, i: (b, i, 0)))
    res = pl.pallas_call(
        _norm_proj_body,
        grid=(B, T // tm),
        in_specs=[pl.BlockSpec((1, tm, D), lambda b, i: (b, i, 0)),
                  _mod_spec(scale, tm), _mod_spec(shift, tm),
                  pl.BlockSpec((D, N), lambda b, i: (0, 0))],
        out_specs=out_specs,
        out_shape=out_shape,
        compiler_params=_params("parallel", "parallel"),
        name="norm_proj",
    )(x, scale, shift, w)
    return res if with_h else res[0]


def _rope_chunk(y, cos, sin):
    lane = lax.broadcasted_iota(jnp.int32, y.shape, 1)
    swapped = jnp.where(lane % HEAD_DIM < HEAD_DIM // 2,
                        pltpu.roll(y, LANE - HEAD_DIM // 2, 1), pltpu.roll(y, HEAD_DIM // 2, 1))
    return y * cos + swapped * sin


def _proj_seg_body(x_ref, sc_ref, sh_ref, cos_ref, sin_ref, w_ref, *o_refs, segs):
    h = _norm_mod(x_ref[0], sc_ref[0], sh_ref[0]).astype(BF16)
    cos = cos_ref[...]
    sin = sin_ref[...]
    c0 = 0
    for o_ref, (width, rope_flags, dtype) in zip(o_refs, segs):
        for j0 in range(0, width, PROJ_CHUNK):
            wd = min(PROJ_CHUNK, width - j0)
            y = jnp.dot(h, w_ref[:, c0 + j0:c0 + j0 + wd], preferred_element_type=F32)
            for k in range(wd // LANE):
                yk = y[:, k * LANE:(k + 1) * LANE]
                if rope_flags[(j0 + k * LANE) // LANE]:
                    yk = _rope_chunk(yk, cos, sin)
                o_ref[0, :, j0 + k * LANE:j0 + (k + 1) * LANE] = yk.astype(dtype)
        c0 += width


def proj_segments(x, scale, shift, cos_t, sin_t, w, segs):
    B, T, D = x.shape
    tm = min(T, ROW_TILE)
    N = w.shape[1]
    return pl.pallas_call(
        functools.partial(_proj_seg_body, segs=segs),
        grid=(B, T // tm),
        in_specs=[pl.BlockSpec((1, tm, D), lambda b, i: (b, i, 0)),
                  _mod_spec(scale, tm), _mod_spec(shift, tm),
                  pl.BlockSpec((tm, LANE), lambda b, i: (i, 0)),
                  pl.BlockSpec((tm, LANE), lambda b, i: (i, 0)),
                  pl.BlockSpec((D, N), lambda b, i: (0, 0))],
        out_specs=[pl.BlockSpec((1, tm, s[0]), lambda b, i: (b, i, 0)) for s in segs],
        out_shape=[jax.ShapeDtypeStruct((B, T, s[0]), s[2]) for s in segs],
        compiler_params=_params("parallel", "parallel"),
        name="proj_segments",
    )(x, scale, shift, cos_t, sin_t, w)


def rope_tables(pos):
    half = HEAD_DIM // 2
    inv = ROPE_THETA ** (-jnp.arange(half, dtype=F32) / half)
    ang = pos.astype(F32)[:, None] * inv[None, :]
    cos, sin = jnp.cos(ang), jnp.sin(ang)
    return jnp.tile(cos, (1, 4)), jnp.tile(jnp.concatenate([-sin, sin], axis=1), (1, 2))


def _out_proj_body(*refs, n_in):
    a_refs, w_refs = refs[:n_in], refs[n_in:2 * n_in]
    x_ref, g_ref, o_ref = refs[2 * n_in:]
    y = None
    for a_ref, w_ref in zip(a_refs, w_refs):
        part = jnp.dot(a_ref[0].astype(BF16), w_ref[...], preferred_element_type=F32)
        y = part if y is None else y + part
    o_ref[0] = x_ref[0] + g_ref[0] * y


def out_proj_residual(a_list, w_list, x, gate):
    B, T, D = x.shape
    tm = min(T, ROW_TILE)
    n_in = len(a_list)
    return pl.pallas_call(
        functools.partial(_out_proj_body, n_in=n_in),
        grid=(B, T // tm),
        in_specs=[pl.BlockSpec((1, tm, a.shape[2]), lambda b, i: (b, i, 0)) for a in a_list]
        + [pl.BlockSpec(w.shape, lambda b, i: (0, 0)) for w in w_list]
        + [pl.BlockSpec((1, tm, D), lambda b, i: (b, i, 0)), _mod_spec(gate, tm)],
        out_specs=pl.BlockSpec((1, tm, D), lambda b, i: (b, i, 0)),
        out_shape=jax.ShapeDtypeStruct((B, T, D), F32),
        compiler_params=_params("parallel", "parallel"),
        name="out_proj_residual",
    )(*a_list, *w_list, x, gate)


def _swiglu_acc(h, w1_ref, w3_ref, w2_ref, acc_ref):
    for c in range(D_FF // FF_CHUNK):
        cols = slice(c * FF_CHUNK, (c + 1) * FF_CHUNK)
        u = jnp.dot(h, w1_ref[:, cols], preferred_element_type=F32)
        v = jnp.dot(h, w3_ref[:, cols], preferred_element_type=F32)
        a = (u * jax.nn.sigmoid(u) * v).astype(BF16)
        part = jnp.dot(a, w2_ref[cols, :], preferred_element_type=F32)
        if c == 0:
            acc_ref[...] = part
        else:
            acc_ref[...] += part


def _ffn_body(x_ref, sc_ref, sh_ref, g_ref, w1_ref, w3_ref, w2_ref, o_ref, acc_ref):
    x = x_ref[0]
    h = _norm_mod(x, sc_ref[0], sh_ref[0]).astype(BF16)
    _swiglu_acc(h, w1_ref, w3_ref, w2_ref, acc_ref)
    o_ref[0] = x + g_ref[0] * acc_ref[...]


def ffn_residual(x, scale, shift, gate, w1, w3, w2):
    B, T, D = x.shape
    tm = min(T, ROW_TILE)
    wspec = lambda shape: pl.BlockSpec(shape, lambda b, i: (0, 0))
    return pl.pallas_call(
        _ffn_body,
        grid=(B, T // tm),
        in_specs=[pl.BlockSpec((1, tm, D), lambda b, i: (b, i, 0)),
                  _mod_spec(scale, tm), _mod_spec(shift, tm), _mod_spec(gate, tm),
                  wspec((D, D_FF)), wspec((D, D_FF)), wspec((D_FF, D))],
        out_specs=pl.BlockSpec((1, tm, D), lambda b, i: (b, i, 0)),
        out_shape=jax.ShapeDtypeStruct((B, T, D), F32),
        scratch_shapes=[pltpu.VMEM((tm, D), F32)],
        compiler_params=_params("parallel", "parallel"),
        name="ffn_residual",
    )(x, scale, shift, gate, w1, w3, w2)


def _moe_body(te_ref, nt_ref, h_ref, ws_ref, w1_ref, w3_ref, w2_ref, o_ref, acc_ref):
    i = pl.program_id(0)

    @pl.when(i < nt_ref[0])
    def _():
        _swiglu_acc(h_ref[...], w1_ref.at[0], w3_ref.at[0], w2_ref.at[0], acc_ref)
        o_ref[...] = ws_ref[...] * acc_ref[...]

    @pl.when(i >= nt_ref[0])
    def _():
        o_ref[...] = jnp.zeros_like(o_ref)


def moe_grouped_swiglu(h_sorted, w_slot, tile_expert, n_tiles_used, w1, w3, w2):
    S, D = h_sorted.shape
    tm = MOE_TILE
    n_tiles = S // tm
    wspec = lambda shape: pl.BlockSpec((1,) + shape, lambda i, te, nt: (te[i], 0, 0))
    grid_spec = pltpu.PrefetchScalarGridSpec(
        num_scalar_prefetch=2,
        grid=(n_tiles,),
        in_specs=[pl.BlockSpec((tm, D), lambda i, te, nt: (i, 0)),
                  pl.BlockSpec((tm, 1), lambda i, te, nt: (i, 0)),
                  wspec((D, D_FF)), wspec((D, D_FF)), wspec((D_FF, D))],
        out_specs=pl.BlockSpec((tm, D), lambda i, te, nt: (i, 0)),
        scratch_shapes=[pltpu.VMEM((tm, D), F32)],
    )
    return pl.pallas_call(
        _moe_body,
        grid_spec=grid_spec,
        out_shape=jax.ShapeDtypeStruct((S, D), F32),
        compiler_params=_params("arbitrary"),
        name="moe_grouped_swiglu",
    )(tile_expert, n_tiles_used, h_sorted, w_slot, w1, w3, w2)


def _final_norm_body(x_ref, g_ref, o_ref):
    x = x_ref[0]
    ms = jnp.mean(x * x, axis=-1, keepdims=True)
    o_ref[0] = x * lax.rsqrt(ms + EPS) * g_ref[...]


def final_norm(x, g):
    B, T, D = x.shape
    tm = min(T, ROW_TILE)
    return pl.pallas_call(
        _final_norm_body,
        grid=(B, T // tm),
        in_specs=[pl.BlockSpec((1, tm, D), lambda b, i: (b, i, 0)),
                  pl.BlockSpec((1, D), lambda b, i: (0, 0))],
        out_specs=pl.BlockSpec((1, tm, D), lambda b, i: (b, i, 0)),
        out_shape=jax.ShapeDtypeStruct((B, T, D), F32),
        compiler_params=_params("parallel", "parallel"),
        name="final_norm",
    )(x, g.reshape(1, D))


def _split3(x):
    hi = x.astype(BF16)
    r1 = x - hi.astype(F32)
    mid = r1.astype(BF16)
    lo = (r1 - mid.astype(F32)).astype(BF16)
    return hi, mid, lo


def _dot01(m01, x):
    hi, mid, lo = _split3(x)
    d = lambda p: jnp.dot(m01, p, preferred_element_type=F32)
    return d(hi) + d(mid) + d(lo)


def _hgrn_body(hq_ref, hf_ref, hi_ref, hg_ref, lb_ref, ng_ref, s0_ref, o_ref, sn_ref, st_ref, *, sub, n_sub):
    i = pl.program_id(1)

    @pl.when(i == 0)
    def _():
        for h in range(HGRN_HEADS):
            st_ref[h] = s0_ref[0, h].T

    lb = lb_ref[...]
    ng = ng_ref[...]
    row = lax.broadcasted_iota(jnp.int32, (sub, sub), 0)
    col = lax.broadcasted_iota(jnp.int32, (sub, sub), 1)
    tril = (row >= col).astype(BF16)
    trow = lax.broadcasted_iota(jnp.int32, (sub, HGRN_DK), 0)

    def chunk(c, carry):
        rows = pl.ds(pl.multiple_of(c * sub, sub), sub)
        hq = hq_ref[0, rows, :]
        hf = hf_ref[0, rows, :]
        hv = hi_ref[0, rows, :]
        hg = hg_ref[0, rows, :]
        f = lb + (1.0 - lb) * jax.nn.sigmoid(hf)
        logf = jnp.log(f)
        kk = 1.0 - f
        qq = hq * jax.nn.sigmoid(hq)
        b = _dot01(tril, logf)
        bl = b[sub - 1:sub, :]
        qe = qq * jnp.exp(b)
        ke = kk * jnp.exp(bl - b)
        ebl = jnp.exp(bl)
        outs = []
        for h in range(HGRN_HEADS):
            cs = slice(h * HGRN_DK, (h + 1) * HGRN_DK)
            st = st_ref[h]
            o = lax.dot_general(qe[:, cs].astype(BF16), st.astype(BF16), (((1,), (1,)), ((), ())),
                                preferred_element_type=F32)
            bh, qh, kh, vh = b[:, cs], qq[:, cs], kk[:, cs], hv[:, cs]
            for s in range(sub):
                e = jnp.exp(jnp.minimum(bh - bh[s:s + 1, :], 0.0))
                w = jnp.where(trow >= s, e * qh * kh[s:s + 1, :], 0.0)
                o = o + jnp.sum(w, axis=-1, keepdims=True) * vh[s:s + 1, :]
            upd = lax.dot_general(vh.astype(BF16), ke[:, cs].astype(BF16), (((0,), (0,)), ((), ())),
                                  preferred_element_type=F32)
            st_ref[h] = st * ebl[:, cs] + upd
            ms = jnp.mean(o * o, axis=-1, keepdims=True)
            outs.append(o * lax.rsqrt(ms + EPS))
        o_ref[0, rows, :] = jnp.concatenate(outs, axis=-1) * ng * (hg * jax.nn.sigmoid(hg))
        return carry

    lax.fori_loop(0, n_sub, chunk, 0)

    @pl.when(i == pl.num_programs(1) - 1)
    def _():
        for h in range(HGRN_HEADS):
            sn_ref[0, h] = st_ref[h].T


def hgrn_mixer(hx, lb, norm_g, s0):
    B, T, _ = hx.shape
    tc = min(T, HGRN_TILE)
    sub = math.gcd(T, HGRN_SUB)
    spec = lambda j: pl.BlockSpec((1, tc, HGRN_KW), lambda b, i, j=j: (b, i, j))
    vec = pl.BlockSpec((1, HGRN_KW), lambda b, i: (0, 0))
    state = pl.BlockSpec((1, HGRN_HEADS, HGRN_DK, HGRN_DV), lambda b, i: (b, 0, 0, 0))
    return pl.pallas_call(
        functools.partial(_hgrn_body, sub=sub, n_sub=tc // sub),
        grid=(B, T // tc),
        in_specs=[spec(0), spec(1), spec(2), spec(3), vec, vec, state],
        out_specs=[pl.BlockSpec((1, tc, HGRN_VW), lambda b, i: (b, i, 0)), state],
        out_shape=[jax.ShapeDtypeStruct((B, T, HGRN_VW), F32),
                   jax.ShapeDtypeStruct((B, HGRN_HEADS, HGRN_DK, HGRN_DV), F32)],
        scratch_shapes=[pltpu.VMEM((HGRN_HEADS, HGRN_DV, HGRN_DK), F32)],
        compiler_params=_params("parallel", "arbitrary"),
        name="hgrn_mixer",
    )(hx, hx, hx, hx, lb.reshape(1, HGRN_KW), jnp.tile(norm_g, HGRN_HEADS).reshape(1, HGRN_VW), s0)


def _nt(a, b):
    return lax.dot_general(a, b, (((1,), (1,)), ((), ())), preferred_element_type=F32)


def _flash_step(kblk, vT, qs, mask, m, l, acc_ref):
    s = jnp.where(mask, _nt(kblk, qs), NEG)
    m_new = jnp.maximum(m, jnp.max(s, axis=0, keepdims=True))
    alpha = jnp.exp(m - m_new)
    e = jnp.where(mask, jnp.exp(s - m_new), 0.0)
    l_new = alpha * l + jnp.sum(e, axis=0, keepdims=True)
    acc_ref[...] = alpha * acc_ref[...] + jnp.dot(vT, e.astype(BF16), preferred_element_type=F32)
    return m_new, l_new


def _stack_heads(qp_ref, first_head, n):
    return jnp.concatenate([qp_ref[0, :, (first_head + h) * LANE:(first_head + h + 1) * LANE] for h in range(n)],
                           axis=0)


def _store_heads(o_ref, oT, first_head, n, slot):
    tq = oT.shape[1] // n
    lane = lax.broadcasted_iota(jnp.int32, (tq, LANE), 1)
    valid = (lane >= HEAD_DIM * slot) & (lane < HEAD_DIM * (slot + 1))
    for h in range(n):
        blk = oT[:, h * tq:(h + 1) * tq].T
        o_ref[0, :, (first_head + h) * LANE:(first_head + h + 1) * LANE] = jnp.where(valid, blk, 0.0).astype(o_ref.dtype)


def _nsa_body(qp_ref, rows_ref, win_ref, gl_ref, pek_ref, pev_ref, wck_ref, wcv_ref, o_ref,
              kcmp_ref, vcmpT_ref, ks_ref, vsT_ref, kw_ref, vwT_ref, stage_ref, shift_ref, sel_ref, acc_ref, *, T):
    i = pl.program_id(1)
    tq = ATT_TQ
    nq = NSA_HPG * tq
    n_half = T // CMP_STRIDE
    n_sel = T // SEL_BLK
    k_sel = min(TOP_N, n_sel)
    per_kb = KEY_BLK // SEL_BLK

    @pl.when(i == 0)
    def _prepare():
        nrow = lax.broadcasted_iota(jnp.int32, (n_half, LANE), 0)
        for slab, pe_ref, w_ref in ((0, pek_ref, wck_ref), (1, pev_ref, wcv_ref)):
            def stage_blk(kb, carry, slab=slab):
                rs = pl.ds(pl.multiple_of(kb * KEY_BLK, KEY_BLK), KEY_BLK)
                stage_ref[rs, :] = rows_ref[0, rs, slab * LANE:(slab + 1) * LANE]
                return carry

            lax.fori_loop(0, T // KEY_BLK, stage_blk, 0)
            first = jnp.zeros((n_half, LANE), F32)
            second = jnp.zeros((n_half, LANE), F32)
            for r in range(CMP_STRIDE):
                y = stage_ref[pl.ds(r, n_half, stride=CMP_STRIDE), :]
                first = first + jnp.dot((y + pe_ref[r:r + 1, :]).astype(BF16), w_ref[r],
                                        preferred_element_type=F32)
                second = second + jnp.dot((y + pe_ref[r + CMP_STRIDE:r + CMP_STRIDE + 1, :]).astype(BF16),
                                          w_ref[r + CMP_STRIDE], preferred_element_type=F32)
            shift_ref[0:n_half, :] = second
            shift_ref[n_half:n_half + 8, :] = jnp.zeros((8, LANE), F32)
            c = jnp.where(nrow < n_half - 1, first + shift_ref[1:n_half + 1, :], 0.0)
            if slab == 0:
                kcmp_ref[...] = c.astype(BF16)
            else:
                for j in range(n_half // LANE):
                    vcmpT_ref[:, j * LANE:(j + 1) * LANE] = c[j * LANE:(j + 1) * LANE, :].T.astype(BF16)

        def copy_blk(kb, carry):
            for half in range(KEY_BLK // LANE):
                rs = pl.ds(pl.multiple_of(kb * KEY_BLK + half * LANE, LANE), LANE)
                hs = slice(half * LANE, (half + 1) * LANE)
                ks_ref[kb, hs, :] = rows_ref[0, rs, 2 * LANE:3 * LANE].astype(BF16)
                vsT_ref[kb, :, hs] = rows_ref[0, rs, 3 * LANE:4 * LANE].T.astype(BF16)
                kw_ref[kb, hs, :] = win_ref[0, rs, 0:LANE].astype(BF16)
                vwT_ref[kb, :, hs] = win_ref[0, rs, LANE:2 * LANE].T.astype(BF16)
            return carry

        lax.fori_loop(0, T // KEY_BLK, copy_blk, 0)

    t0 = i * tq
    lane_t = t0 + (lax.broadcasted_iota(jnp.int32, (1, nq), 1) & (tq - 1))
    t_row = lane_t[:, 0:tq]
    gT = jax.nn.sigmoid(gl_ref[0].T)
    kio = lax.broadcasted_iota(jnp.int32, (KEY_BLK, nq), 0)
    nio = lax.broadcasted_iota(jnp.int32, (n_half, nq), 0)
    jcol = lax.broadcasted_iota(jnp.int32, (n_sel, tq), 0)
    pj = lax.broadcasted_iota(jnp.int32, (n_sel, n_half), 0)
    pi = lax.broadcasted_iota(jnp.int32, (n_sel, n_half), 1)
    pool = ((((pi + 1) >> 2) == pj) & (pi < n_half - 1)).astype(BF16)
    n_kb = (t0 + tq + KEY_BLK - 1) // KEY_BLK
    m0 = jnp.full((1, nq), NEG, F32)
    l0 = jnp.zeros((1, nq), F32)

    for g in range(NSA_KV):
        qs = _stack_heads(qp_ref, NSA_HPG * g, NSA_HPG)
        cmask = (CMP_STRIDE * nio + CMP_BLK <= lane_t + 1) & (nio < n_half - 1)
        s = jnp.where(cmask, _nt(kcmp_ref[...], qs), NEG)
        e = jnp.where(cmask, jnp.exp(s - jnp.max(s, axis=0, keepdims=True)), 0.0)
        p = e * (1.0 / jnp.maximum(jnp.sum(e, axis=0, keepdims=True), 1e-30))
        ocT = jnp.dot(vcmpT_ref[...], p.astype(BF16), preferred_element_type=F32)
        imp = p[:, 0:tq]
        for h in range(1, NSA_HPG):
            imp = imp + p[:, h * tq:(h + 1) * tq]
        imp_sel = _dot01(pool, imp)
        cur = t_row >> 6
        forced = (jcol == 0) | (jcol == cur) | (jcol == cur - 1)
        visible = jcol <= cur
        score = jnp.where(visible, imp_sel + jnp.where(forced, FORCE_SCORE, 0.0), NEG)
        rank = jnp.zeros((n_sel, tq), jnp.int32)
        for j2 in range(n_sel):
            r = score[j2:j2 + 1, :]
            beats = (r > score) | ((r == score) & (jcol > j2))
            rank = rank + beats.astype(jnp.int32)
        sel = jnp.where((rank < k_sel) & visible, 1.0, 0.0)
        sel_ref[...] = jnp.concatenate([sel] * NSA_HPG, axis=1)

        def sel_step(kb, carry):
            rowsel = jnp.concatenate(
                [jnp.broadcast_to(sel_ref[pl.ds(kb * per_kb + k, 1), :], (SEL_BLK, nq)) for k in range(per_kb)],
                axis=0)
            mask = (rowsel > 0.5) & (kb * KEY_BLK + kio <= lane_t)
            return _flash_step(ks_ref[kb], vsT_ref[kb], qs, mask, carry[0], carry[1], acc_ref)

        acc_ref[...] = jnp.zeros_like(acc_ref)
        _, l = lax.fori_loop(0, n_kb, sel_step, (m0, l0))
        osT = acc_ref[...] * (1.0 / jnp.maximum(l, 1e-30))

        def win_step(kb, carry):
            kpos = kb * KEY_BLK + kio
            mask = (kpos <= lane_t) & (kpos > lane_t - WINDOW)
            return _flash_step(kw_ref[kb], vwT_ref[kb], qs, mask, carry[0], carry[1], acc_ref)

        acc_ref[...] = jnp.zeros_like(acc_ref)
        w_lo = jnp.maximum(t0 - (WINDOW - 1), 0) // KEY_BLK
        _, l = lax.fori_loop(w_lo, n_kb, win_step, (m0, l0))
        owT = acc_ref[...] * (1.0 / jnp.maximum(l, 1e-30))

        def gate(c):
            return jnp.concatenate([gT[3 * (NSA_HPG * g + h) + c:3 * (NSA_HPG * g + h) + c + 1, :]
                                    for h in range(NSA_HPG)], axis=1)

        oT = gate(0) * ocT + gate(1) * osT + gate(2) * owT
        _store_heads(o_ref, oT, NSA_HPG * g, NSA_HPG, g)


def _block_diag2(w):
    w3 = w.reshape(CMP_BLK, HEAD_DIM, HEAD_DIM)
    z = jnp.zeros_like(w3)
    return jnp.concatenate([jnp.concatenate([w3, z], axis=2), jnp.concatenate([z, w3], axis=2)], axis=1).astype(BF16)


def nsa_prompt(qp, rows, win, gl, pe_k, pe_v, w_ck, w_cv):
    B, T, _ = rows.shape
    assert T % (CMP_STRIDE * LANE) == 0 and T % KEY_BLK == 0
    tq = ATT_TQ
    nq = NSA_HPG * tq
    n_half = T // CMP_STRIDE
    n_kb = T // KEY_BLK
    const = lambda shape: pl.BlockSpec(shape, lambda b, i: (0,) * len(shape))
    return pl.pallas_call(
        functools.partial(_nsa_body, T=T),
        grid=(B, T // tq),
        in_specs=[pl.BlockSpec((1, tq, NSA_HEADS * LANE), lambda b, i: (b, i, 0)),
                  pl.BlockSpec((1, T, 4 * LANE), lambda b, i: (b, 0, 0)),
                  pl.BlockSpec((1, T, 2 * LANE), lambda b, i: (b, 0, 0)),
                  pl.BlockSpec((1, tq, LANE), lambda b, i: (b, i, 0)),
                  const((CMP_BLK, LANE)), const((CMP_BLK, LANE)),
                  const((CMP_BLK, LANE, LANE)), const((CMP_BLK, LANE, LANE))],
        out_specs=pl.BlockSpec((1, tq, NSA_HEADS * LANE), lambda b, i: (b, i, 0)),
        out_shape=jax.ShapeDtypeStruct((B, T, NSA_HEADS * LANE), BF16),
        scratch_shapes=[pltpu.VMEM((n_half, LANE), BF16), pltpu.VMEM((LANE, n_half), BF16),
                        pltpu.VMEM((n_kb, KEY_BLK, LANE), BF16), pltpu.VMEM((n_kb, LANE, KEY_BLK), BF16),
                        pltpu.VMEM((n_kb, KEY_BLK, LANE), BF16), pltpu.VMEM((n_kb, LANE, KEY_BLK), BF16),
                        pltpu.VMEM((T, LANE), F32),
                        pltpu.VMEM((n_half + 8, LANE), F32), pltpu.VMEM((T // SEL_BLK, nq), F32),
                        pltpu.VMEM((LANE, nq), F32)],
        compiler_params=_params("parallel", "arbitrary"),
        name="nsa_prompt",
    )(qp, rows, win, gl, jnp.tile(pe_k, (1, 2)), jnp.tile(pe_v, (1, 2)), _block_diag2(w_ck), _block_diag2(w_cv))


def _dsa_body(qp_ref, kv_ref, qi_ref, ki_ref, wi_ref, o_ref,
              k_ref, vT_ref, kilo_ref, kihi_ref, key_ref, acc_ref, *, T, n_keep):
    i = pl.program_id(1)
    tq = ATT_TQ
    nq = DSA_HPG * tq

    @pl.when(i == 0)
    def _prepare():
        def copy_blk(kb, carry):
            for half in range(KEY_BLK // LANE):
                rs = pl.ds(pl.multiple_of(kb * KEY_BLK + half * LANE, LANE), LANE)
                hs = slice(half * LANE, (half + 1) * LANE)
                for slab in range(2):
                    k_ref[slab, kb, hs, :] = kv_ref[0, rs, slab * LANE:(slab + 1) * LANE].astype(BF16)
                    vT_ref[slab, kb, :, hs] = kv_ref[0, rs, (2 + slab) * LANE:(3 + slab) * LANE].T.astype(BF16)
                kix = ki_ref[0, rs, :]
                kilo_ref[kb, hs, :] = kix.astype(BF16)
                kihi_ref[kb, hs, :] = pltpu.roll(kix, IDX_DIM, 1).astype(BF16)
            return carry

        lax.fori_loop(0, T // KEY_BLK, copy_blk, 0)

    t0 = i * tq
    n_kb = (t0 + tq + KEY_BLK - 1) // KEY_BLK
    t_row = t0 + lax.broadcasted_iota(jnp.int32, (1, tq), 1)
    lane_t = jnp.concatenate([t_row] * DSA_HPG, axis=1)
    kio_q = lax.broadcasted_iota(jnp.int32, (KEY_BLK, tq), 0)
    kio = lax.broadcasted_iota(jnp.int32, (KEY_BLK, nq), 0)

    wT = wi_ref[0].T * (IDX_HEADS ** -0.5)
    qi = _stack_heads(qi_ref, 0, IDX_HEADS // 2)

    def idx_step(kb, carry):
        s_lo = _nt(kilo_ref[kb], qi)
        s_hi = _nt(kihi_ref[kb], qi)
        sc = jnp.zeros((KEY_BLK, tq), F32)
        for p in range(IDX_HEADS // 2):
            cs = slice(p * tq, (p + 1) * tq)
            sc = sc + jnp.maximum(s_lo[:, cs], 0.0) * wT[2 * p:2 * p + 1, :]
            sc = sc + jnp.maximum(s_hi[:, cs], 0.0) * wT[2 * p + 1:2 * p + 2, :]
        sc = jnp.where(kb * KEY_BLK + kio_q <= t_row, sc, NEG)
        bits = lax.bitcast_convert_type(sc, jnp.int32)
        key_ref[kb] = jnp.where(bits < 0, bits ^ 0x7FFFFFFF, bits)
        return carry

    lax.fori_loop(0, n_kb, idx_step, 0)

    def bit_step(it, theta):
        cand = theta + lax.shift_left(jnp.int32(1), 31 - it)

        def cnt_step(kb, c):
            return c + jnp.sum((key_ref[kb] >= cand).astype(jnp.int32), axis=0, keepdims=True)

        cnt = lax.fori_loop(0, n_kb, cnt_step, jnp.zeros((1, tq), jnp.int32))
        return jnp.where(cnt >= n_keep, cand, theta)

    theta = lax.fori_loop(0, 32, bit_step, jnp.full((1, tq), INT_MIN, jnp.int32))
    theta4 = jnp.concatenate([theta] * DSA_HPG, axis=1)

    qs = [_stack_heads(qp_ref, DSA_HPG * g, DSA_HPG) for g in range(DSA_KV)]
    m0 = jnp.full((1, nq), NEG, F32)
    l0 = jnp.zeros((1, nq), F32)
    acc_ref[...] = jnp.zeros_like(acc_ref)

    def att_step(kb, carry):
        keys = jnp.concatenate([key_ref[kb]] * DSA_HPG, axis=1)
        mask = (keys >= theta4) & (kb * KEY_BLK + kio <= lane_t)
        out = []
        for g in range(DSA_KV):
            m, l = _flash_step(k_ref[g // 2, kb], vT_ref[g // 2, kb], qs[g], mask,
                               carry[2 * g], carry[2 * g + 1], acc_ref.at[g])
            out += [m, l]
        return tuple(out)

    fin = lax.fori_loop(0, n_kb, att_step, (m0, l0) * DSA_KV)
    for g in range(DSA_KV):
        oT = acc_ref[g] * (1.0 / jnp.maximum(fin[2 * g + 1], 1e-30))
        _store_heads(o_ref, oT, DSA_HPG * g, DSA_HPG, g % 2)


def dsa_prompt(qp, kv, qi, ki, wi):
    B, T, _ = kv.shape
    assert T % KEY_BLK == 0
    tq = ATT_TQ
    nq = DSA_HPG * tq
    n_kb = T // KEY_BLK
    n_keep = min(IDX_TOPK, T // 4)
    tile = lambda w: pl.BlockSpec((1, tq, w), lambda b, i: (b, i, 0))
    whole = lambda w: pl.BlockSpec((1, T, w), lambda b, i: (b, 0, 0))
    return pl.pallas_call(
        functools.partial(_dsa_body, T=T, n_keep=n_keep),
        grid=(B, T // tq),
        in_specs=[tile(DSA_HEADS * LANE), whole(4 * LANE), tile(IDX_QW), whole(LANE), tile(LANE)],
        out_specs=tile(DSA_HEADS * LANE),
        out_shape=jax.ShapeDtypeStruct((B, T, DSA_HEADS * LANE), BF16),
        scratch_shapes=[pltpu.VMEM((2, n_kb, KEY_BLK, LANE), BF16), pltpu.VMEM((2, n_kb, LANE, KEY_BLK), BF16),
                        pltpu.VMEM((n_kb, KEY_BLK, LANE), BF16), pltpu.VMEM((n_kb, KEY_BLK, LANE), BF16),
                        pltpu.VMEM((n_kb, KEY_BLK, tq), jnp.int32), pltpu.VMEM((DSA_KV, LANE, nq), F32)],
        compiler_params=_params("parallel", "arbitrary"),
        name="dsa_prompt",
    )(qp, kv, qi, ki, wi)


def _pad_head_cols(w, slots):
    D = w.shape[0]
    H = len(slots)
    w3 = w.reshape(D, H, HEAD_DIM)
    z = jnp.zeros_like(w3)
    s = jnp.asarray(slots)[None, :, None]
    return jnp.concatenate([jnp.where(s == 0, w3, z), jnp.where(s == 1, w3, z)], axis=2).reshape(D, H * LANE)


def _pad_cols(w, n):
    return jnp.pad(w, ((0, 0), (0, n - w.shape[1])))


def rms_norm(x, g):
    x32 = x.astype(jnp.float32)
    y = x32 * lax.rsqrt(jnp.mean(x32 * x32, axis=-1, keepdims=True) + EPS)
    return (y * g.astype(jnp.float32)).astype(x.dtype)


def rope(x, pos):
    half = x.shape[-1] // 2
    inv = ROPE_THETA ** (-jnp.arange(half, dtype=jnp.float32) / half)
    ang = pos.astype(jnp.float32)[:, None] * inv[None, :]
    cos = jnp.cos(ang)[:, None, :]
    sin = jnp.sin(ang)[:, None, :]
    x32 = x.astype(jnp.float32)
    x1, x2 = x32[..., :half], x32[..., half:]
    return jnp.concatenate([x1 * cos - x2 * sin, x2 * cos + x1 * sin], axis=-1).astype(x.dtype)


def masked_softmax(s, mask):
    s = jnp.where(mask, s.astype(jnp.float32), NEG)
    m = jnp.max(s, axis=-1, keepdims=True)
    e = jnp.where(mask, jnp.exp(s - m), 0.0)
    return e / jnp.maximum(jnp.sum(e, axis=-1, keepdims=True), 1e-30)


def split_cols(a, sizes):
    cuts = np.cumsum(np.array(sizes))[:-1]
    return jnp.split(a, [int(c) for c in cuts], axis=-1)


def paged_rows(pool, page_table):
    g = pool[page_table]
    return g.reshape(g.shape[0], g.shape[1] * g.shape[2], *g.shape[3:])


def over_query_blocks(fn, arrays):
    B, T = arrays[0].shape[:2]
    qb = math.gcd(T, Q_BLK)
    nb = T // qb
    blocks = tuple(a.reshape(B, nb, qb, *a.shape[2:]).swapaxes(0, 1) for a in arrays)
    out = lax.map(lambda xs: fn(xs[0] * qb, qb, *xs[1:]), (jnp.arange(nb),) + blocks)
    return out.swapaxes(0, 1).reshape(B, T, *out.shape[3:])


def compress_blocks(x, pe, w):
    B, Lp, G, Dh = x.shape
    halves = x.reshape(B, Lp // CMP_STRIDE, CMP_STRIDE, G, Dh)
    blk = jnp.concatenate([halves[:, :-1], halves[:, 1:]], axis=2) + pe[None, None, :, None, :]
    return jnp.einsum('bnlgd,lde->bnge', blk, w.reshape(CMP_BLK, Dh, Dh))


def nsa_attention(q, rows, win_keys, gates, q_off, w_off, pe_k, pe_v, w_ck, w_cv):
    B, L = rows.shape[:2]
    Lp = -(-L // SEL_BLK) * SEL_BLK
    rows = jnp.pad(rows, ((0, 0), (0, Lp - L), (0, 0), (0, 0), (0, 0)))
    k_cmp = compress_blocks(rows[:, :, 0], pe_k, w_ck)
    v_cmp = compress_blocks(rows[:, :, 1], pe_v, w_cv).astype(jnp.float32)
    n_cmp = k_cmp.shape[1]
    n_sel = Lp // SEL_BLK
    k_sel = min(TOP_N, n_sel)
    ks_blk = rows[:, :, 2].reshape(B, n_sel, SEL_BLK, NSA_KV, HEAD_DIM).transpose(0, 3, 1, 2, 4)
    vs_blk = rows[:, :, 3].reshape(B, n_sel, SEL_BLK, NSA_KV, HEAD_DIM).transpose(0, 3, 1, 2, 4)
    kw_pad = jnp.pad(win_keys, ((0, 0), (WINDOW, 0), (0, 0), (0, 0), (0, 0)))
    cmp_end = CMP_STRIDE * jnp.arange(n_cmp) + CMP_BLK
    sel_j = jnp.arange(n_sel)
    scale = HEAD_DIM ** -0.5
    take = jax.vmap(jax.vmap(lambda a, i: a[i]))

    def block(i0, qb, q_b, g_b):
        t = q_off + i0 + jnp.arange(qb)
        qg = q_b.reshape(B, qb, NSA_KV, NSA_HPG, HEAD_DIM) * scale
        s_c = jnp.einsum('btghd,bngd->btghn', qg, k_cmp, preferred_element_type=jnp.float32)
        p_c = masked_softmax(s_c, (cmp_end[None, :] <= t[:, None] + 1)[None, :, None, None, :])
        o_c = jnp.einsum('btghn,bngd->btghd', p_c, v_cmp)
        imp = jnp.pad(p_c.sum(3), ((0, 0), (0, 0), (0, 0), (1, 0)))
        imp = imp.reshape(B, qb, NSA_KV, n_sel, SEL_BLK // CMP_STRIDE).sum(-1)
        cur = t // SEL_BLK
        forced = (sel_j[None] == 0) | (sel_j[None] == cur[:, None]) | (sel_j[None] == cur[:, None] - 1)
        visible = sel_j[None] <= cur[:, None]
        score = jnp.where(visible[None, :, None, :], imp + FORCE_SCORE * forced[None, :, None, :], NEG)
        top_s, top_i = lax.top_k(score, k_sel)
        top_s = top_s.transpose(0, 2, 1, 3)
        top_i = top_i.transpose(0, 2, 1, 3)
        k_g = take(ks_blk, top_i)
        v_g = take(vs_blk, top_i)
        kpos = top_i[..., None] * SEL_BLK + jnp.arange(SEL_BLK)
        ok = (top_s[..., None] > 0.5 * NEG) & (kpos <= t[None, None, :, None, None])
        q2 = qg.transpose(0, 2, 1, 3, 4)
        s_s = jnp.einsum('bgthd,bgtkrd->bgthkr', q2, k_g, preferred_element_type=jnp.float32)
        s_s = s_s.reshape(B, NSA_KV, qb, NSA_HPG, k_sel * SEL_BLK)
        p_s = masked_softmax(s_s, ok.reshape(B, NSA_KV, qb, 1, k_sel * SEL_BLK))
        o_s = jnp.einsum('bgthm,bgtmd->bgthd', p_s,
                         v_g.reshape(B, NSA_KV, qb, k_sel * SEL_BLK, HEAD_DIM).astype(jnp.float32))
        o_s = o_s.transpose(0, 2, 1, 3, 4)
        start = q_off - w_off + i0 + 1
        span = WINDOW + qb - 1
        kw = lax.dynamic_slice_in_dim(kw_pad, start, span, axis=1)
        kidx = start + jnp.arange(span)
        kabs = kidx - WINDOW + w_off
        okw = (kidx[None] >= WINDOW) & (kabs[None] <= t[:, None]) & (kabs[None] > t[:, None] - WINDOW)
        s_w = jnp.einsum('btghd,bsgd->btghs', qg, kw[:, :, 0], preferred_element_type=jnp.float32)
        p_w = masked_softmax(s_w, okw[None, :, None, None, :])
        o_w = jnp.einsum('btghs,bsgd->btghd', p_w, kw[:, :, 1].astype(jnp.float32))
        g = jax.nn.sigmoid(g_b.astype(jnp.float32)).reshape(B, qb, NSA_KV, NSA_HPG, 3)
        o = g[..., 0:1] * o_c + g[..., 1:2] * o_s + g[..., 2:3] * o_w
        return o.reshape(B, qb, NSA_QW).astype(q_b.dtype)

    return over_query_blocks(block, (q, gates))


def hgrn2_scan(q, k, v, logf, s0):
    q, k, v, logf = [a.astype(jnp.float32) for a in (q, k, v, logf)]
    B, T, H, DK = q.shape
    C = math.gcd(T, HGRN_CHUNK)
    nc = T // C
    to_chunks = lambda a: a.reshape(B, nc, C, *a.shape[2:]).swapaxes(0, 1)
    tri = jnp.tril(jnp.ones((C, C), dtype=bool))

    def step(S, inp):
        qc, kc, vc, gc = inp
        b = jnp.cumsum(gc, axis=1)
        o = jnp.einsum('bthk,bhkv->bthv', qc * jnp.exp(b), S)
        diff = b[:, :, None] - b[:, None, :]
        decay = jnp.exp(jnp.where(tri[None, :, :, None, None], diff, -jnp.inf))
        A = jnp.einsum('bthk,btshk,bshk->bths', qc, decay, kc)
        o = o + jnp.einsum('bths,bshv->bthv', A, vc)
        bl = b[:, -1]
        S = jnp.exp(bl)[..., None] * S + jnp.einsum('bshk,bshv->bhkv', kc * jnp.exp(bl[:, None] - b), vc)
        return S, o

    S, o = lax.scan(step, s0.astype(jnp.float32), (to_chunks(q), to_chunks(k), to_chunks(v), to_chunks(logf)))
    return o.swapaxes(0, 1).reshape(B, T, H, v.shape[-1]), S


def even_mixer(proj, pos0, past, win_len, pe_k, pe_v, w_ck, w_cv, lb, norm_g):
    B, T, _ = proj.shape
    pos = pos0 + jnp.arange(T)
    q, kc, vc, ks, vs, kw, vw, gl, hq, hf, hi, hg = split_cols(
        proj[..., :EVEN_IN], [NSA_QW] + [NSA_KVW] * 6 + [NSA_GW, HGRN_KW, HGRN_KW, HGRN_VW, HGRN_VW])
    heads = lambda a, n, d: a.reshape(B, T, n, d)
    q = rope(heads(q, NSA_HEADS, HEAD_DIM), pos)
    kc, ks, kw = [rope(heads(a, NSA_KV, HEAD_DIM), pos) for a in (kc, ks, kw)]
    vc, vs, vw = [heads(a, NSA_KV, HEAD_DIM) for a in (vc, vs, vw)]
    new_rows = jnp.stack([kc, vc, ks, vs], axis=2)
    new_win = jnp.stack([kw, vw], axis=2)
    if past is None:
        rows, win_keys, w_off = new_rows, new_win, pos0
        s0 = jnp.zeros((B, HGRN_HEADS, HGRN_DK, HGRN_DV), jnp.float32)
        win_state = jnp.concatenate(
            [jnp.zeros((B, win_len) + new_win.shape[2:], new_win.dtype), new_win], axis=1)[:, -win_len:]
    else:
        past_rows, past_win, s0 = past
        rows = jnp.concatenate([past_rows, new_rows], axis=1)
        win_keys = jnp.concatenate([past_win, new_win], axis=1)
        w_off = pos0 - past_win.shape[1]
        win_state = win_keys[:, -win_len:]
    o_a = nsa_attention(q, rows, win_keys, heads(gl, NSA_HEADS, 3), pos0, w_off, pe_k, pe_v, w_ck, w_cv)
    lbh = lb.reshape(HGRN_HEADS, HGRN_DK)
    f = lbh + (1.0 - lbh) * jax.nn.sigmoid(heads(hf, HGRN_HEADS, HGRN_DK).astype(jnp.float32))
    o_b, s_new = hgrn2_scan(jax.nn.silu(heads(hq, HGRN_HEADS, HGRN_DK)), 1.0 - f,
                            heads(hi, HGRN_HEADS, HGRN_DV), jnp.log(f), s0)
    o_b = rms_norm(o_b, norm_g) * jax.nn.silu(heads(hg, HGRN_HEADS, HGRN_DV).astype(jnp.float32))
    mixed = jnp.concatenate([o_a, o_b.reshape(B, T, HGRN_VW).astype(proj.dtype)], axis=-1)
    return mixed, new_rows, win_state, s_new.astype(proj.dtype)


def odd_mixer(proj, pos0, past):
    B, T, _ = proj.shape
    pos = pos0 + jnp.arange(T)
    q, k, v, qi, ki, wi = split_cols(proj[..., :ODD_IN], [DSA_QW, DSA_KVW, DSA_KVW, IDX_QW, IDX_DIM, IDX_HEADS])
    q = rope(q.reshape(B, T, DSA_HEADS, HEAD_DIM), pos)
    k = rope(k.reshape(B, T, DSA_KV, HEAD_DIM), pos)
    v = v.reshape(B, T, DSA_KV, HEAD_DIM)
    qi = rope(qi.reshape(B, T, IDX_HEADS, IDX_DIM), pos)
    ki = rope(ki[:, :, None, :], pos)[:, :, 0]
    new_kv = jnp.stack([k, v], axis=2)
    if past is None:
        kv_all, ki_all = new_kv, ki
    else:
        kv_all = jnp.concatenate([past[0], new_kv], axis=1)
        ki_all = jnp.concatenate([past[1], ki], axis=1)
    L = kv_all.shape[1]
    n_keep = min(IDX_TOPK, L // 4)
    s_pos = jnp.arange(L)
    scale = HEAD_DIM ** -0.5

    def block(i0, qb, q_b, qi_b, wi_b):
        t = pos0 + i0 + jnp.arange(qb)
        rel = jax.nn.relu(jnp.einsum('bthd,bsd->bths', qi_b, ki_all, preferred_element_type=jnp.float32) * IDX_DIM ** -0.5)
        score = jnp.einsum('bths,bth->bts', rel, wi_b.astype(jnp.float32) * IDX_HEADS ** -0.5)
        score = jnp.where(s_pos[None, None, :] <= t[None, :, None], score, NEG)
        _, sel = lax.top_k(score, n_keep)
        kv_g = jax.vmap(lambda a, i: a[i])(kv_all, sel)
        qg = q_b.reshape(B, qb, DSA_KV, DSA_HPG, HEAD_DIM) * scale
        s = jnp.einsum('btghd,btkgd->btghk', qg, kv_g[:, :, :, 0], preferred_element_type=jnp.float32)
        p = masked_softmax(s, (sel <= t[None, :, None])[:, :, None, None, :])
        o = jnp.einsum('btghk,btkgd->btghd', p, kv_g[:, :, :, 1].astype(jnp.float32))
        return o.reshape(B, qb, DSA_QW).astype(q_b.dtype)

    o = over_query_blocks(block, (q, qi, wi))
    return o, new_kv, ki


_NSA_SLOTS = tuple(h // NSA_HPG for h in range(NSA_HEADS))
_DSA_SLOTS = tuple((h // DSA_HPG) % 2 for h in range(DSA_HEADS))
_EVEN_SEGS = ((NSA_HEADS * LANE, (1,) * NSA_HEADS, BF16), (4 * LANE, (1, 0, 1, 0), F32), (2 * LANE, (1, 0), F32),
              (LANE, (0,), F32), (4 * HGRN_KW, (0,) * (4 * HGRN_KW // LANE), F32))
_ODD_SEGS = ((DSA_HEADS * LANE, (1,) * DSA_HEADS, BF16), (4 * LANE, (1, 1, 0, 0), F32), (IDX_QW, (1,) * 4, BF16),
             (LANE, (1,), F32), (LANE, (0,), F32))


def even_mixer_prompt(x, scale, shift, cos_t, sin_t, w_in, w_out, pe_k, pe_v, w_ck, w_cv, lb, norm_g, win_len):
    B, T, _ = x.shape
    assert T >= win_len
    c = np.cumsum([0, NSA_QW] + [NSA_KVW] * 6 + [NSA_GW] + [HGRN_KW] * 4)
    w = jnp.concatenate([
        _pad_head_cols(w_in[:, :NSA_QW] * HEAD_DIM ** -0.5, _NSA_SLOTS),
        w_in[:, c[1]:c[5]], w_in[:, c[5]:c[7]], _pad_cols(w_in[:, c[7]:c[8]], LANE), w_in[:, c[8]:c[12]]],
        axis=1).astype(BF16)
    qp, rows, win, gl, hx = proj_segments(x, scale, shift, cos_t, sin_t, w, _EVEN_SEGS)
    o_a = nsa_prompt(qp, rows, win, gl, pe_k, pe_v, w_ck, w_cv)
    o_b, s_new = hgrn_mixer(hx, lb, norm_g, jnp.zeros((B, HGRN_HEADS, HGRN_DK, HGRN_DV), F32))
    w_outs = [_pad_head_cols(w_out[:NSA_QW].T, _NSA_SLOTS).T.astype(BF16), w_out[NSA_QW:].astype(BF16)]
    new_rows = rows.reshape(B, T, 4, NSA_KV, HEAD_DIM)
    win_state = win[:, T - win_len:].reshape(B, win_len, 2, NSA_KV, HEAD_DIM)
    return [o_a, o_b], w_outs, new_rows, win_state, s_new


def odd_mixer_prompt(x, scale, shift, cos_t, sin_t, w_in, w_out):
    B, T, _ = x.shape
    c = np.cumsum([0, DSA_QW, DSA_KVW, DSA_KVW, IDX_QW, IDX_DIM, IDX_HEADS])
    w = jnp.concatenate([
        _pad_head_cols(w_in[:, :DSA_QW] * HEAD_DIM ** -0.5, _DSA_SLOTS),
        w_in[:, c[1]:c[3]], w_in[:, c[3]:c[4]] * IDX_DIM ** -0.5,
        _pad_cols(w_in[:, c[4]:c[5]], LANE), _pad_cols(w_in[:, c[5]:c[6]], LANE)], axis=1).astype(BF16)
    qp, kv, qi, ki, wi = proj_segments(x, scale, shift, cos_t, sin_t, w, _ODD_SEGS)
    o = dsa_prompt(qp, kv, qi, ki, wi)
    w_outs = [_pad_head_cols(w_out.T, _DSA_SLOTS).T.astype(BF16)]
    return [o], w_outs, kv.reshape(B, T, 2, DSA_KV, HEAD_DIM), ki[..., :IDX_DIM]


def moe_ffn_residual(xs, hs, logits_list, gates, w1, w3, w2):
    D = D_MODEL
    h_all = jnp.concatenate([h.reshape(-1, D) for h in hs], axis=0)
    logits = jnp.concatenate([lg.reshape(-1, lg.shape[-1])[:, :N_EXPERTS] for lg in logits_list], axis=0)
    n_tok = h_all.shape[0]
    top_v, top_i = lax.top_k(logits, TOP_K)
    weights = jax.nn.softmax(top_v, axis=-1)
    e_flat = top_i.reshape(-1)
    onehot = (e_flat[:, None] == jnp.arange(N_EXPERTS)[None, :]).astype(jnp.int32)
    csum = jnp.cumsum(onehot, axis=0)
    counts = csum[-1]
    rank = jnp.take_along_axis(csum, e_flat[:, None], axis=1)[:, 0] - 1
    padded = ((counts + MOE_TILE - 1) // MOE_TILE) * MOE_TILE
    group_end = jnp.cumsum(padded)
    group_start = group_end - padded
    slot = group_start[e_flat] + rank
    n_slots = _round_up(n_tok * TOP_K, MOE_TILE) + N_EXPERTS * MOE_TILE
    n_tiles = n_slots // MOE_TILE
    tok_of_slot = jnp.zeros((n_slots,), jnp.int32).at[slot].set(jnp.arange(n_tok * TOP_K, dtype=jnp.int32) // TOP_K)
    w_slot = jnp.zeros((n_slots,), F32).at[slot].set(weights.reshape(-1))
    tile_start = jnp.arange(n_tiles, dtype=jnp.int32) * MOE_TILE
    tile_expert = jnp.minimum(jnp.sum(tile_start[:, None] >= group_end[None, :], axis=1), N_EXPERTS - 1).astype(jnp.int32)
    n_used = (group_end[-1] // MOE_TILE).astype(jnp.int32).reshape(1)
    h_sorted = h_all[tok_of_slot]
    y_slot = moe_grouped_swiglu(h_sorted, w_slot.reshape(n_slots, 1), tile_expert, n_used, w1, w3, w2)
    slot2 = slot.reshape(n_tok, TOP_K)
    y = y_slot[slot2[:, 0]] + y_slot[slot2[:, 1]]
    outs = []
    off = 0
    for x, g in zip(xs, gates):
        n = x.shape[0] * x.shape[1]
        outs.append(x + g * y[off:off + n].reshape(x.shape))
        off += n
    return outs


def kernel(x_prompt, x_sample, cache_nsa, cache_nsa_win, state_hgrn, cache_dsa_kv, cache_dsa_idx, page_table, c_prompt, c_sample, ada_w, ada_b, norm1_g, norm2_g, final_g, even_w_in, even_w_out, nsa_pe_k, nsa_pe_v, nsa_w_ck, nsa_w_cv, hgrn_lb_raw, hgrn_norm_g, ffn_w1, ffn_w3, ffn_w2, odd_w_in, odd_w_out, router_w, router_b, moe_w1, moe_w3, moe_w2):
    D = D_MODEL
    past_len = page_table.shape[1] * PAGE_SIZE
    win_len = cache_nsa_win.shape[2]
    Bp, Tp = x_prompt.shape[:2]
    Bs, Ts = x_sample.shape[:2]
    lb_soft = jax.nn.softmax(hgrn_lb_raw.astype(F32), axis=0)
    lower_bounds = jnp.cumsum(lb_soft, axis=0) - lb_soft[0]

    R = _round_up(Bp + Bs, 8)
    c_all = jnp.zeros((R, D), F32).at[:Bp].set(c_prompt).at[Bp:Bp + Bs].set(c_sample)
    mods = ada_modulation(c_all, ada_w, ada_b)

    def group_mods(l, lo, n, per_token_rows):
        m = mods[l, lo:lo + n].reshape(n, 6, D)
        sh1, sc1, g1, sh2, sc2, g2 = [m[:, j] for j in range(6)]
        s1 = norm1_g[l][None] * (1.0 + sc1)
        s2 = norm2_g[l][None] * (1.0 + sc2)
        vecs = [s1, sh1, g1, s2, sh2, g2]
        if per_token_rows:
            return [jnp.repeat(v, per_token_rows, axis=0)[None] for v in vecs]
        return [v[:, None, :] for v in vecs]

    def pad_cols(w, n):
        return jnp.pad(w, ((0, 0), (0, n - w.shape[1]))).astype(BF16)

    xp = x_prompt
    xs = x_sample.reshape(1, Bs * Ts, D)
    cos_p, sin_p = rope_tables(jnp.arange(Tp))
    outs_p = dict(rows=[], win=[], st=[], kv=[], idx=[])
    outs_s = dict(rows=[], win=[], st=[], kv=[], idx=[])
    for l in range(DEPTH):
        i = l // 2
        mp = group_mods(l, 0, Bp, 0)
        msm = group_mods(l, Bp, Bs, Ts)
        if l % 2 == 0:
            w1, w3, w2 = ffn_w1[i].astype(BF16), ffn_w3[i].astype(BF16), ffn_w2[i].astype(BF16)
            mixed, w_outs, rows, win, s_new = even_mixer_prompt(
                xp, mp[0], mp[1], cos_p, sin_p, even_w_in[i], even_w_out[i], nsa_pe_k[i], nsa_pe_v[i],
                nsa_w_ck[i], nsa_w_cv[i], lower_bounds[i], hgrn_norm_g[i], win_len)
            outs_p['rows'].append(rows)
            outs_p['win'].append(win)
            outs_p['st'].append(s_new)
            xp = out_proj_residual(mixed, w_outs, xp, mp[2])
            xp = ffn_residual(xp, mp[3], mp[4], mp[5], w1, w3, w2)
            w_in = pad_cols(even_w_in[i], _round_up(EVEN_IN, LANE))
            proj = norm_proj(xs, msm[0], msm[1], w_in).reshape(Bs, Ts, -1)
            past = (paged_rows(cache_nsa[i], page_table), cache_nsa_win[i], state_hgrn[i])
            mixed, rows, win, s_new = even_mixer(proj, past_len, past, win_len, nsa_pe_k[i], nsa_pe_v[i],
                                                 nsa_w_ck[i], nsa_w_cv[i], lower_bounds[i], hgrn_norm_g[i])
            outs_s['rows'].append(rows)
            outs_s['win'].append(win)
            outs_s['st'].append(s_new)
            xs = out_proj_residual([mixed.reshape(1, Bs * Ts, EVEN_MIX)], [even_w_out[i].astype(BF16)], xs, msm[2])
            xs = ffn_residual(xs, msm[3], msm[4], msm[5], w1, w3, w2)
        else:
            w_r = pad_cols(router_w[i], LANE)
            w1, w3, w2 = moe_w1[i].astype(BF16), moe_w3[i].astype(BF16), moe_w2[i].astype(BF16)
            o, w_outs, kv, ki = odd_mixer_prompt(xp, mp[0], mp[1], cos_p, sin_p, odd_w_in[i], odd_w_out[i])
            outs_p['kv'].append(kv)
            outs_p['idx'].append(ki)
            xp = out_proj_residual(o, w_outs, xp, mp[2])
            w_in = pad_cols(odd_w_in[i], _round_up(ODD_IN, LANE))
            proj = norm_proj(xs, msm[0], msm[1], w_in).reshape(Bs, Ts, -1)
            past = (paged_rows(cache_dsa_kv[i], page_table), paged_rows(cache_dsa_idx[i], page_table))
            o, kv, ki = odd_mixer(proj, past_len, past)
            outs_s['kv'].append(kv)
            outs_s['idx'].append(ki)
            xs = out_proj_residual([o.reshape(1, Bs * Ts, ODD_MIX)], [odd_w_out[i].astype(BF16)], xs, msm[2])
            hs, lgs = [], []
            for (x, m) in ((xp, mp), (xs, msm)):
                logits, h = norm_proj(x, m[3], m[4], w_r, with_h=True)
                lgs.append(logits[..., :N_EXPERTS] + router_b[i].astype(F32))
                hs.append(h)
            xp, xs = moe_ffn_residual([xp, xs], hs, lgs, [mp[5], msm[5]], w1, w3, w2)
    y_prompt = final_norm(xp, final_g)
    y_sample = final_norm(xs, final_g).reshape(Bs, Ts, D)
    st = lambda od, k: jnp.stack(od[k])
    return (y_prompt, y_sample, st(outs_p, 'rows'), st(outs_s, 'rows'), st(outs_p, 'win'), st(outs_s, 'win'),
            st(outs_p, 'st'), st(outs_s, 'st'), st(outs_p, 'kv'), st(outs_s, 'kv'),
            st(outs_p, 'idx'), st(outs_s, 'idx'))
```

```python
import functools
import math

import numpy as np
import jax
import jax.numpy as jnp
from jax import lax
from jax.experimental import pallas as pl
from jax.experimental.pallas import tpu as pltpu

D_MODEL = 1024
DEPTH = 4
PAGE_SIZE = 128
HEAD_DIM = 64
NSA_HEADS = 8
NSA_KV = 2
NSA_HPG = NSA_HEADS // NSA_KV
CMP_STRIDE = 16
CMP_BLK = 2 * CMP_STRIDE
SEL_BLK = 64
TOP_N = 16
WINDOW = 512
FORCE_SCORE = 1.0e4
HGRN_HEADS = 4
HGRN_DK = 128
HGRN_DV = 128
HGRN_CHUNK = 64
DSA_HEADS = 16
DSA_KV = 4
DSA_HPG = DSA_HEADS // DSA_KV
IDX_HEADS = 8
IDX_DIM = 64
IDX_TOPK = 256
D_FF = 2816
N_EXPERTS = 8
TOP_K = 2
Q_BLK = 128
ROPE_THETA = 10000.0
EPS = 1e-6
NEG = -1.0e30
NSA_QW = NSA_HEADS * HEAD_DIM
NSA_KVW = NSA_KV * HEAD_DIM
NSA_GW = NSA_HEADS * 3
HGRN_KW = HGRN_HEADS * HGRN_DK
HGRN_VW = HGRN_HEADS * HGRN_DV
EVEN_IN = NSA_QW + 6 * NSA_KVW + NSA_GW + 2 * HGRN_KW + 2 * HGRN_VW
EVEN_MIX = NSA_QW + HGRN_VW
DSA_QW = DSA_HEADS * HEAD_DIM
DSA_KVW = DSA_KV * HEAD_DIM
IDX_QW = IDX_HEADS * IDX_DIM
ODD_IN = DSA_QW + 2 * DSA_KVW + IDX_QW + IDX_DIM + IDX_HEADS
ODD_MIX = DSA_QW

LANE = 128
ROW_TILE = 512
FF_CHUNK = 256
MOE_TILE = 256
PROJ_CHUNK = 512
ATT_TQ = 128
KEY_BLK = 256
HGRN_SUB = 16
HGRN_TILE = 256
PG_STEP = 8
INT_MIN = -2 ** 31
VMEM_LIMIT = 56 * 1024 * 1024

F32 = jnp.float32
BF16 = jnp.bfloat16


def _round_up(n, m):
    return -(-n // m) * m


def _params(*sem):
    return pltpu.CompilerParams(dimension_semantics=sem, vmem_limit_bytes=VMEM_LIMIT)


def _norm_mod(x, scale, shift):
    ms = jnp.mean(x * x, axis=-1, keepdims=True)
    return x * lax.rsqrt(ms + EPS) * scale + shift


def _mod_spec(mod, tm):
    if mod.shape[1] == 1:
        return pl.BlockSpec((1, 1, mod.shape[2]), lambda b, i: (b, 0, 0))
    return pl.BlockSpec((1, tm, mod.shape[2]), lambda b, i: (b, i, 0))


def _ada_body(c_ref, w_ref, b_ref, o_ref):
    c = c_ref[...]
    cs = (c * jax.nn.sigmoid(c)).astype(BF16)
    o_ref[0] = jnp.dot(cs, w_ref[0].astype(BF16), preferred_element_type=F32) + b_ref[0]


def ada_modulation(c_all, ada_w, ada_b):
    R, D = c_all.shape
    N = ada_w.shape[2]
    tn = 1536
    return pl.pallas_call(
        _ada_body,
        grid=(DEPTH, N // tn),
        in_specs=[pl.BlockSpec((R, D), lambda l, j: (0, 0)),
                  pl.BlockSpec((1, D, tn), lambda l, j: (l, 0, j)),
                  pl.BlockSpec((1, 1, tn), lambda l, j: (l, 0, j))],
        out_specs=pl.BlockSpec((1, R, tn), lambda l, j: (l, 0, j)),
        out_shape=jax.ShapeDtypeStruct((DEPTH, R, N), F32),
        compiler_params=_params("arbitrary", "arbitrary"),
        name="ada_modulation",
    )(c_all, ada_w, ada_b.reshape(DEPTH, 1, N))


def _norm_proj_body(x_ref, sc_ref, sh_ref, w_ref, o_ref, h_ref=None):
    h = _norm_mod(x_ref[0], sc_ref[0], sh_ref[0]).astype(BF16)
    o_ref[0] = jnp.dot(h, w_ref[...], preferred_element_type=F32)
    if h_ref is not None:
        h_ref[0] = h


def norm_proj(x, scale, shift, w, with_h=False):
    B, T, D = x.shape
    N = w.shape[1]
    tm = min(T, ROW_TILE)
    out_shape = [jax.ShapeDtypeStruct((B, T, N), F32)]
    out_specs = [pl.BlockSpec((1, tm, N), lambda b, i: (b, i, 0))]
    if with_h:
        out_shape.append(jax.ShapeDtypeStruct((B, T, D), BF16))
        out_specs.append(pl.BlockSpec((1, tm, D), lambda b, i: (b, i, 0)))
    res = pl.pallas_call(
        _norm_proj_body,
        grid=(B, T // tm),
        in_specs=[pl.BlockSpec((1, tm, D), lambda b, i: (b, i, 0)),
                  _mod_spec(scale, tm), _mod_spec(shift, tm),
                  pl.BlockSpec((D, N), lambda b, i: (0, 0))],
        out_specs=out_specs,
        out_shape=out_shape,
        compiler_params=_params("parallel", "parallel"),
        name="norm_proj",
    )(x, scale, shift, w)
    return res if with_h else res[0]


def _rope_chunk(y, cos, sin):
    lane = lax.broadcasted_iota(jnp.int32, y.shape, 1)
    swapped = jnp.where(lane % HEAD_DIM < HEAD_DIM // 2,
                        pltpu.roll(y, LANE - HEAD_DIM // 2, 1), pltpu.roll(y, HEAD_DIM // 2, 1))
    return y * cos + swapped * sin


def _proj_seg_body(x_ref, sc_ref, sh_ref, cos_ref, sin_ref, w_ref, *o_refs, segs):
    h = _norm_mod(x_ref[0], sc_ref[0], sh_ref[0]).astype(BF16)
    cos = cos_ref[...]
    sin = sin_ref[...]
    c0 = 0
    for o_ref, (width, rope_flags, dtype) in zip(o_refs, segs):
        for j0 in range(0, width, PROJ_CHUNK):
            wd = min(PROJ_CHUNK, width - j0)
            y = jnp.dot(h, w_ref[:, c0 + j0:c0 + j0 + wd], preferred_element_type=F32)
            for k in range(wd // LANE):
                yk = y[:, k * LANE:(k + 1) * LANE]
                if rope_flags[(j0 + k * LANE) // LANE]:
                    yk = _rope_chunk(yk, cos, sin)
                o_ref[0, :, j0 + k * LANE:j0 + (k + 1) * LANE] = yk.astype(dtype)
        c0 += width


def proj_segments(x, scale, shift, cos_t, sin_t, w, segs):
    B, T, D = x.shape
    tm = min(T, ROW_TILE)
    N = w.shape[1]
    return pl.pallas_call(
        functools.partial(_proj_seg_body, segs=segs),
        grid=(B, T // tm),
        in_specs=[pl.BlockSpec((1, tm, D), lambda b, i: (b, i, 0)),
                  _mod_spec(scale, tm), _mod_spec(shift, tm),
                  pl.BlockSpec((tm, LANE), lambda b, i: (i, 0)),
                  pl.BlockSpec((tm, LANE), lambda b, i: (i, 0)),
                  pl.BlockSpec((D, N), lambda b, i: (0, 0))],
        out_specs=[pl.BlockSpec((1, tm, s[0]), lambda b, i: (b, i, 0)) for s in segs],
        out_shape=[jax.ShapeDtypeStruct((B, T, s[0]), s[2]) for s in segs],
        compiler_params=_params("parallel", "parallel"),
        name="proj_segments",
    )(x, scale, shift, cos_t, sin_t, w)


def rope_tables(pos):
    half = HEAD_DIM // 2
    inv = ROPE_THETA ** (-jnp.arange(half, dtype=F32) / half)
    ang = pos.astype(F32)[:, None] * inv[None, :]
    cos, sin = jnp.cos(ang), jnp.sin(ang)
    return jnp.tile(cos, (1, 4)), jnp.tile(jnp.concatenate([-sin, sin], axis=1), (1, 2))


def _out_proj_body(*refs, n_in):
    a_refs, w_refs = refs[:n_in], refs[n_in:2 * n_in]
    x_ref, g_ref, o_ref = refs[2 * n_in:]
    y = None
    for a_ref, w_ref in zip(a_refs, w_refs):
        part = jnp.dot(a_ref[0].astype(BF16), w_ref[...], preferred_element_type=F32)
        y = part if y is None else y + part
    o_ref[0] = x_ref[0] + g_ref[0] * y


def out_proj_residual(a_list, w_list, x, gate):
    B, T, D = x.shape
    tm = min(T, ROW_TILE)
    n_in = len(a_list)
    return pl.pallas_call(
        functools.partial(_out_proj_body, n_in=n_in),
        grid=(B, T // tm),
        in_specs=[pl.BlockSpec((1, tm, a.shape[2]), lambda b, i: (b, i, 0)) for a in a_list]
        + [pl.BlockSpec(w.shape, lambda b, i: (0, 0)) for w in w_list]
        + [pl.BlockSpec((1, tm, D), lambda b, i: (b, i, 0)), _mod_spec(gate, tm)],
        out_specs=pl.BlockSpec((1, tm, D), lambda b, i: (b, i, 0)),
        out_shape=jax.ShapeDtypeStruct((B, T, D), F32),
        compiler_params=_params("parallel", "parallel"),
        name="out_proj_residual",
    )(*a_list, *w_list, x, gate)


def _swiglu_acc(h, w1_ref, w3_ref, w2_ref, acc_ref):
    for c in range(D_FF // FF_CHUNK):
        cols = slice(c * FF_CHUNK, (c + 1) * FF_CHUNK)
        u = jnp.dot(h, w1_ref[:, cols], preferred_element_type=F32)
        v = jnp.dot(h, w3_ref[:, cols], preferred_element_type=F32)
        a = (u * jax.nn.sigmoid(u) * v).astype(BF16)
        part = jnp.dot(a, w2_ref[cols, :], preferred_element_type=F32)
        if c == 0:
            acc_ref[...] = part
        else:
            acc_ref[...] += part


def _ffn_body(x_ref, sc_ref, sh_ref, g_ref, w1_ref, w3_ref, w2_ref, o_ref, acc_ref):
    x = x_ref[0]
    h = _norm_mod(x, sc_ref[0], sh_ref[0]).astype(BF16)
    _swiglu_acc(h, w1_ref, w3_ref, w2_ref, acc_ref)
    o_ref[0] = x + g_ref[0] * acc_ref[...]


def ffn_residual(x, scale, shift, gate, w1, w3, w2):
    B, T, D = x.shape
    tm = min(T, ROW_TILE)
    wspec = lambda shape: pl.BlockSpec(shape, lambda b, i: (0, 0))
    return pl.pallas_call(
        _ffn_body,
        grid=(B, T // tm),
        in_specs=[pl.BlockSpec((1, tm, D), lambda b, i: (b, i, 0)),
                  _mod_spec(scale, tm), _mod_spec(shift, tm), _mod_spec(gate, tm),
                  wspec((D, D_FF)), wspec((D, D_FF)), wspec((D_FF, D))],
        out_specs=pl.BlockSpec((1, tm, D), lambda b, i: (b, i, 0)),
        out_shape=jax.ShapeDtypeStruct((B, T, D), F32),
        scratch_shapes=[pltpu.VMEM((tm, D), F32)],
        compiler_params=_params("parallel", "parallel"),
        name="ffn_residual",
    )(x, scale, shift, gate, w1, w3, w2)


def _moe_body(te_ref, nt_ref, h_ref, ws_ref, w1_ref, w3_ref, w2_ref, o_ref, acc_ref):
    i = pl.program_id(0)

    @pl.when(i < nt_ref[0])
    def _():
        _swiglu_acc(h_ref[...], w1_ref.at[0], w3_ref.at[0], w2_ref.at[0], acc_ref)
        o_ref[...] = ws_ref[...] * acc_ref[...]

    @pl.when(i >= nt_ref[0])
    def _():
        o_ref[...] = jnp.zeros_like(o_ref)


def moe_grouped_swiglu(h_sorted, w_slot, tile_expert, n_tiles_used, w1, w3, w2):
    S, D = h_sorted.shape
    tm = MOE_TILE
    n_tiles = S // tm
    wspec = lambda shape: pl.BlockSpec((1,) + shape, lambda i, te, nt: (te[i], 0, 0))
    grid_spec = pltpu.PrefetchScalarGridSpec(
        num_scalar_prefetch=2,
        grid=(n_tiles,),
        in_specs=[pl.BlockSpec((tm, D), lambda i, te, nt: (i, 0)),
                  pl.BlockSpec((tm, 1), lambda i, te, nt: (i, 0)),
                  wspec((D, D_FF)), wspec((D, D_FF)), wspec((D_FF, D))],
        out_specs=pl.BlockSpec((tm, D), lambda i, te, nt: (i, 0)),
        scratch_shapes=[pltpu.VMEM((tm, D), F32)],
    )
    return pl.pallas_call(
        _moe_body,
        grid_spec=grid_spec,
        out_shape=jax.ShapeDtypeStruct((S, D), F32),
        compiler_params=_params("arbitrary"),
        name="moe_grouped_swiglu",
    )(tile_expert, n_tiles_used, h_sorted, w_slot, w1, w3, w2)


def _combine_body(x_ref, g_ref, a_ref, b_ref, o_ref):
    o_ref[0] = x_ref[0] + g_ref[0] * (a_ref[0] + b_ref[0])


def moe_combine_residual(x, gate, ya, yb):
    B, T, D = x.shape
    tm = min(T, ROW_TILE)
    tile = pl.BlockSpec((1, tm, D), lambda b, i: (b, i, 0))
    return pl.pallas_call(
        _combine_body,
        grid=(B, T // tm),
        in_specs=[tile, _mod_spec(gate, tm), tile, tile],
        out_specs=tile,
        out_shape=jax.ShapeDtypeStruct((B, T, D), F32),
        compiler_params=_params("parallel", "parallel"),
        name="moe_combine_residual",
    )(x, gate, ya, yb)


def _final_norm_body(x_ref, g_ref, o_ref):
    x = x_ref[0]
    ms = jnp.mean(x * x, axis=-1, keepdims=True)
    o_ref[0] = x * lax.rsqrt(ms + EPS) * g_ref[...]


def final_norm(x, g):
    B, T, D = x.shape
    tm = min(T, ROW_TILE)
    return pl.pallas_call(
        _final_norm_body,
        grid=(B, T // tm),
        in_specs=[pl.BlockSpec((1, tm, D), lambda b, i: (b, i, 0)),
                  pl.BlockSpec((1, D), lambda b, i: (0, 0))],
        out_specs=pl.BlockSpec((1, tm, D), lambda b, i: (b, i, 0)),
        out_shape=jax.ShapeDtypeStruct((B, T, D), F32),
        compiler_params=_params("parallel", "parallel"),
        name="final_norm",
    )(x, g.reshape(1, D))


def _split3(x):
    hi = x.astype(BF16)
    r1 = x - hi.astype(F32)
    mid = r1.astype(BF16)
    lo = (r1 - mid.astype(F32)).astype(BF16)
    return hi, mid, lo


def _dot01(m01, x):
    hi, mid, lo = _split3(x)
    d = lambda p: jnp.dot(m01, p, preferred_element_type=F32)
    return d(hi) + d(mid) + d(lo)


def _hgrn_body(hq_ref, hf_ref, hi_ref, hg_ref, lb_ref, ng_ref, s0_ref, o_ref, sn_ref, st_ref, *, sub, n_sub):
    i = pl.program_id(1)

    @pl.when(i == 0)
    def _():
        for h in range(HGRN_HEADS):
            st_ref[h] = s0_ref[0, h].T

    lb = lb_ref[...]
    ng = ng_ref[...]
    row = lax.broadcasted_iota(jnp.int32, (sub, sub), 0)
    col = lax.broadcasted_iota(jnp.int32, (sub, sub), 1)
    tril = (row >= col).astype(BF16)
    trow = lax.broadcasted_iota(jnp.int32, (sub, HGRN_DK), 0)

    def chunk(c, carry):
        rows = pl.ds(pl.multiple_of(c * sub, sub), sub)
        hq = hq_ref[0, rows, :]
        hf = hf_ref[0, rows, :]
        hv = hi_ref[0, rows, :]
        hg = hg_ref[0, rows, :]
        f = lb + (1.0 - lb) * jax.nn.sigmoid(hf)
        logf = jnp.log(f)
        kk = 1.0 - f
        qq = hq * jax.nn.sigmoid(hq)
        b = _dot01(tril, logf)
        bl = b[sub - 1:sub, :]
        qe = qq * jnp.exp(b)
        ke = kk * jnp.exp(bl - b)
        ebl = jnp.exp(bl)
        outs = []
        for h in range(HGRN_HEADS):
            cs = slice(h * HGRN_DK, (h + 1) * HGRN_DK)
            st = st_ref[h]
            o = lax.dot_general(qe[:, cs].astype(BF16), st.astype(BF16), (((1,), (1,)), ((), ())),
                                preferred_element_type=F32)
            bh, qh, kh, vh = b[:, cs], qq[:, cs], kk[:, cs], hv[:, cs]
            for s in range(sub):
                e = jnp.exp(jnp.minimum(bh - bh[s:s + 1, :], 0.0))
                w = jnp.where(trow >= s, e * qh * kh[s:s + 1, :], 0.0)
                o = o + jnp.sum(w, axis=-1, keepdims=True) * vh[s:s + 1, :]
            upd = lax.dot_general(vh.astype(BF16), ke[:, cs].astype(BF16), (((0,), (0,)), ((), ())),
                                  preferred_element_type=F32)
            st_ref[h] = st * ebl[:, cs] + upd
            ms = jnp.mean(o * o, axis=-1, keepdims=True)
            outs.append(o * lax.rsqrt(ms + EPS))
        o_ref[0, rows, :] = jnp.concatenate(outs, axis=-1) * ng * (hg * jax.nn.sigmoid(hg))
        return carry

    lax.fori_loop(0, n_sub, chunk, 0)

    @pl.when(i == pl.num_programs(1) - 1)
    def _():
        for h in range(HGRN_HEADS):
            sn_ref[0, h] = st_ref[h].T


def hgrn_mixer(hx, lb, norm_g, s0):
    B, T, _ = hx.shape
    tc = min(T, HGRN_TILE)
    sub = math.gcd(T, HGRN_SUB)
    spec = lambda j: pl.BlockSpec((1, tc, HGRN_KW), lambda b, i, j=j: (b, i, j))
    vec = pl.BlockSpec((1, HGRN_KW), lambda b, i: (0, 0))
    state = pl.BlockSpec((1, HGRN_HEADS, HGRN_DK, HGRN_DV), lambda b, i: (b, 0, 0, 0))
    return pl.pallas_call(
        functools.partial(_hgrn_body, sub=sub, n_sub=tc // sub),
        grid=(B, T // tc),
        in_specs=[spec(0), spec(1), spec(2), spec(3), vec, vec, state],
        out_specs=[pl.BlockSpec((1, tc, HGRN_VW), lambda b, i: (b, i, 0)), state],
        out_shape=[jax.ShapeDtypeStruct((B, T, HGRN_VW), F32),
                   jax.ShapeDtypeStruct((B, HGRN_HEADS, HGRN_DK, HGRN_DV), F32)],
        scratch_shapes=[pltpu.VMEM((HGRN_HEADS, HGRN_DV, HGRN_DK), F32)],
        compiler_params=_params("parallel", "arbitrary"),
        name="hgrn_mixer",
    )(hx, hx, hx, hx, lb.reshape(1, HGRN_KW), jnp.tile(norm_g, HGRN_HEADS).reshape(1, HGRN_VW), s0)


def _nt(a, b):
    return lax.dot_general(a, b, (((1,), (1,)), ((), ())), preferred_element_type=F32)


def _flash_step(kblk, vT, qs, mask, m, l, acc_ref):
    s = jnp.where(mask, _nt(kblk, qs), NEG)
    m_new = jnp.maximum(m, jnp.max(s, axis=0, keepdims=True))
    alpha = jnp.exp(m - m_new)
    e = jnp.where(mask, jnp.exp(s - m_new), 0.0)
    l_new = alpha * l + jnp.sum(e, axis=0, keepdims=True)
    acc_ref[...] = alpha * acc_ref[...] + jnp.dot(vT, e.astype(BF16), preferred_element_type=F32)
    return m_new, l_new


def _stack_heads(qp_ref, first_head, n):
    return jnp.concatenate([qp_ref[0, :, (first_head + h) * LANE:(first_head + h + 1) * LANE] for h in range(n)],
                           axis=0)


def _store_heads(o_ref, oT, first_head, n, slot):
    tq = oT.shape[1] // n
    lane = lax.broadcasted_iota(jnp.int32, (tq, LANE), 1)
    valid = (lane >= HEAD_DIM * slot) & (lane < HEAD_DIM * (slot + 1))
    for h in range(n):
        blk = oT[:, h * tq:(h + 1) * tq].T
        o_ref[0, :, (first_head + h) * LANE:(first_head + h + 1) * LANE] = jnp.where(valid, blk, 0.0).astype(o_ref.dtype)


def _nsa_body(qp_ref, rows_ref, win_ref, gl_ref, pek_ref, pev_ref, wck_ref, wcv_ref, o_ref,
              kcmp_ref, vcmpT_ref, ks_ref, vsT_ref, kw_ref, vwT_ref, stage_ref, shift_ref, sel_ref, acc_ref, *, T):
    i = pl.program_id(1)
    tq = ATT_TQ
    nq = NSA_HPG * tq
    n_half = T // CMP_STRIDE
    n_sel = T // SEL_BLK
    k_sel = min(TOP_N, n_sel)
    per_kb = KEY_BLK // SEL_BLK

    @pl.when(i == 0)
    def _prepare():
        nrow = lax.broadcasted_iota(jnp.int32, (n_half, LANE), 0)
        for slab, pe_ref, w_ref in ((0, pek_ref, wck_ref), (1, pev_ref, wcv_ref)):
            def stage_blk(kb, carry, slab=slab):
                rs = pl.ds(pl.multiple_of(kb * KEY_BLK, KEY_BLK), KEY_BLK)
                stage_ref[rs, :] = rows_ref[0, rs, slab * LANE:(slab + 1) * LANE]
                return carry

            lax.fori_loop(0, T // KEY_BLK, stage_blk, 0)
            first = jnp.zeros((n_half, LANE), F32)
            second = jnp.zeros((n_half, LANE), F32)
            for r in range(CMP_STRIDE):
                y = stage_ref[pl.ds(r, n_half, stride=CMP_STRIDE), :]
                first = first + jnp.dot((y + pe_ref[r:r + 1, :]).astype(BF16), w_ref[r],
                                        preferred_element_type=F32)
                second = second + jnp.dot((y + pe_ref[r + CMP_STRIDE:r + CMP_STRIDE + 1, :]).astype(BF16),
                                          w_ref[r + CMP_STRIDE], preferred_element_type=F32)
            shift_ref[0:n_half, :] = second
            shift_ref[n_half:n_half + 8, :] = jnp.zeros((8, LANE), F32)
            c = jnp.where(nrow < n_half - 1, first + shift_ref[1:n_half + 1, :], 0.0)
            if slab == 0:
                kcmp_ref[...] = c.astype(BF16)
            else:
                for j in range(n_half // LANE):
                    vcmpT_ref[:, j * LANE:(j + 1) * LANE] = c[j * LANE:(j + 1) * LANE, :].T.astype(BF16)

        def copy_blk(kb, carry):
            for half in range(KEY_BLK // LANE):
                rs = pl.ds(pl.multiple_of(kb * KEY_BLK + half * LANE, LANE), LANE)
                hs = slice(half * LANE, (half + 1) * LANE)
                ks_ref[kb, hs, :] = rows_ref[0, rs, 2 * LANE:3 * LANE].astype(BF16)
                vsT_ref[kb, :, hs] = rows_ref[0, rs, 3 * LANE:4 * LANE].T.astype(BF16)
                kw_ref[kb, hs, :] = win_ref[0, rs, 0:LANE].astype(BF16)
                vwT_ref[kb, :, hs] = win_ref[0, rs, LANE:2 * LANE].T.astype(BF16)
            return carry

        lax.fori_loop(0, T // KEY_BLK, copy_blk, 0)

    t0 = i * tq
    lane_t = t0 + (lax.broadcasted_iota(jnp.int32, (1, nq), 1) & (tq - 1))
    t_row = lane_t[:, 0:tq]
    gT = jax.nn.sigmoid(gl_ref[0].T)
    kio = lax.broadcasted_iota(jnp.int32, (KEY_BLK, nq), 0)
    nio = lax.broadcasted_iota(jnp.int32, (n_half, nq), 0)
    jcol = lax.broadcasted_iota(jnp.int32, (n_sel, tq), 0)
    pj = lax.broadcasted_iota(jnp.int32, (n_sel, n_half), 0)
    pi = lax.broadcasted_iota(jnp.int32, (n_sel, n_half), 1)
    pool = ((((pi + 1) >> 2) == pj) & (pi < n_half - 1)).astype(BF16)
    n_kb = (t0 + tq + KEY_BLK - 1) // KEY_BLK
    n_full = t0 // KEY_BLK
    m0 = jnp.full((1, nq), NEG, F32)
    l0 = jnp.zeros((1, nq), F32)

    for g in range(NSA_KV):
        qs = _stack_heads(qp_ref, NSA_HPG * g, NSA_HPG)
        cmask = (CMP_STRIDE * nio + CMP_BLK <= lane_t + 1) & (nio < n_half - 1)
        s = jnp.where(cmask, _nt(kcmp_ref[...], qs), NEG)
        e = jnp.where(cmask, jnp.exp(s - jnp.max(s, axis=0, keepdims=True)), 0.0)
        p = e * (1.0 / jnp.maximum(jnp.sum(e, axis=0, keepdims=True), 1e-30))
        ocT = jnp.dot(vcmpT_ref[...], p.astype(BF16), preferred_element_type=F32)
        imp = p[:, 0:tq]
        for h in range(1, NSA_HPG):
            imp = imp + p[:, h * tq:(h + 1) * tq]
        imp_sel = _dot01(pool, imp)
        cur = t_row >> 6
        forced = (jcol == 0) | (jcol == cur) | (jcol == cur - 1)
        visible = jcol <= cur
        score = jnp.where(visible, imp_sel + jnp.where(forced, FORCE_SCORE, 0.0), NEG)
        rank = jnp.zeros((n_sel, tq), jnp.int32)
        for j2 in range(n_sel):
            r = score[j2:j2 + 1, :]
            beats = (r > score) | ((r == score) & (jcol > j2))
            rank = rank + beats.astype(jnp.int32)
        sel = jnp.where((rank < k_sel) & visible, 1.0, 0.0)
        sel_ref[...] = jnp.concatenate([sel] * NSA_HPG, axis=1)

        def sel_step(kb, carry, causal):
            rowsel = jnp.concatenate(
                [jnp.broadcast_to(sel_ref[pl.ds(kb * per_kb + k, 1), :], (SEL_BLK, nq)) for k in range(per_kb)],
                axis=0)
            mask = rowsel > 0.5
            if causal:
                mask = mask & (kb * KEY_BLK + kio <= lane_t)
            return _flash_step(ks_ref[kb], vsT_ref[kb], qs, mask, carry[0], carry[1], acc_ref)

        acc_ref[...] = jnp.zeros_like(acc_ref)
        ml = lax.fori_loop(0, n_full, functools.partial(sel_step, causal=False), (m0, l0))
        _, l = lax.fori_loop(n_full, n_kb, functools.partial(sel_step, causal=True), ml)
        osT = acc_ref[...] * (1.0 / jnp.maximum(l, 1e-30))

        def win_step(kb, carry):
            kpos = kb * KEY_BLK + kio
            mask = (kpos <= lane_t) & (kpos > lane_t - WINDOW)
            return _flash_step(kw_ref[kb], vwT_ref[kb], qs, mask, carry[0], carry[1], acc_ref)

        acc_ref[...] = jnp.zeros_like(acc_ref)
        w_lo = jnp.maximum(t0 - (WINDOW - 1), 0) // KEY_BLK
        _, l = lax.fori_loop(w_lo, n_kb, win_step, (m0, l0))
        owT = acc_ref[...] * (1.0 / jnp.maximum(l, 1e-30))

        def gate(c):
            return jnp.concatenate([gT[3 * (NSA_HPG * g + h) + c:3 * (NSA_HPG * g + h) + c + 1, :]
                                    for h in range(NSA_HPG)], axis=1)

        oT = gate(0) * ocT + gate(1) * osT + gate(2) * owT
        _store_heads(o_ref, oT, NSA_HPG * g, NSA_HPG, g)


def _block_diag2(w):
    w3 = w.reshape(CMP_BLK, HEAD_DIM, HEAD_DIM)
    z = jnp.zeros_like(w3)
    return jnp.concatenate([jnp.concatenate([w3, z], axis=2), jnp.concatenate([z, w3], axis=2)], axis=1).astype(BF16)


def nsa_prompt(qp, rows, win, gl, pe_k, pe_v, w_ck, w_cv):
    B, T, _ = rows.shape
    assert T % (CMP_STRIDE * LANE) == 0 and T % KEY_BLK == 0
    tq = ATT_TQ
    nq = NSA_HPG * tq
    n_half = T // CMP_STRIDE
    n_kb = T // KEY_BLK
    const = lambda shape: pl.BlockSpec(shape, lambda b, i: (0,) * len(shape))
    return pl.pallas_call(
        functools.partial(_nsa_body, T=T),
        grid=(B, T // tq),
        in_specs=[pl.BlockSpec((1, tq, NSA_HEADS * LANE), lambda b, i: (b, i, 0)),
                  pl.BlockSpec((1, T, 4 * LANE), lambda b, i: (b, 0, 0)),
                  pl.BlockSpec((1, T, 2 * LANE), lambda b, i: (b, 0, 0)),
                  pl.BlockSpec((1, tq, LANE), lambda b, i: (b, i, 0)),
                  const((CMP_BLK, LANE)), const((CMP_BLK, LANE)),
                  const((CMP_BLK, LANE, LANE)), const((CMP_BLK, LANE, LANE))],
        out_specs=pl.BlockSpec((1, tq, NSA_HEADS * LANE), lambda b, i: (b, i, 0)),
        out_shape=jax.ShapeDtypeStruct((B, T, NSA_HEADS * LANE), BF16),
        scratch_shapes=[pltpu.VMEM((n_half, LANE), BF16), pltpu.VMEM((LANE, n_half), BF16),
                        pltpu.VMEM((n_kb, KEY_BLK, LANE), BF16), pltpu.VMEM((n_kb, LANE, KEY_BLK), BF16),
                        pltpu.VMEM((n_kb, KEY_BLK, LANE), BF16), pltpu.VMEM((n_kb, LANE, KEY_BLK), BF16),
                        pltpu.VMEM((T, LANE), F32),
                        pltpu.VMEM((n_half + 8, LANE), F32), pltpu.VMEM((T // SEL_BLK, nq), F32),
                        pltpu.VMEM((LANE, nq), F32)],
        compiler_params=_params("parallel", "arbitrary"),
        name="nsa_prompt",
    )(qp, rows, win, gl, jnp.tile(pe_k, (1, 2)), jnp.tile(pe_v, (1, 2)), _block_diag2(w_ck), _block_diag2(w_cv))


def _dsa_body(qp_ref, kv_ref, qi_ref, ki_ref, wi_ref, o_ref,
              k_ref, vT_ref, kilo_ref, kihi_ref, key_ref, acc_ref, *, T, n_keep):
    i = pl.program_id(1)
    tq = ATT_TQ
    nq = DSA_HPG * tq

    @pl.when(i == 0)
    def _prepare():
        def copy_blk(kb, carry):
            for half in range(KEY_BLK // LANE):
                rs = pl.ds(pl.multiple_of(kb * KEY_BLK + half * LANE, LANE), LANE)
                hs = slice(half * LANE, (half + 1) * LANE)
                for slab in range(2):
                    k_ref[slab, kb, hs, :] = kv_ref[0, rs, slab * LANE:(slab + 1) * LANE].astype(BF16)
                    vT_ref[slab, kb, :, hs] = kv_ref[0, rs, (2 + slab) * LANE:(3 + slab) * LANE].T.astype(BF16)
                kix = ki_ref[0, rs, :]
                kilo_ref[kb, hs, :] = kix.astype(BF16)
                kihi_ref[kb, hs, :] = pltpu.roll(kix, IDX_DIM, 1).astype(BF16)
            return carry

        lax.fori_loop(0, T // KEY_BLK, copy_blk, 0)

    t0 = i * tq
    n_kb = (t0 + tq + KEY_BLK - 1) // KEY_BLK
    t_row = t0 + lax.broadcasted_iota(jnp.int32, (1, tq), 1)
    lane_t = jnp.concatenate([t_row] * DSA_HPG, axis=1)
    kio_q = lax.broadcasted_iota(jnp.int32, (KEY_BLK, tq), 0)
    kio = lax.broadcasted_iota(jnp.int32, (KEY_BLK, nq), 0)

    wT = wi_ref[0].T * (IDX_HEADS ** -0.5)
    qi = _stack_heads(qi_ref, 0, IDX_HEADS // 2)

    def idx_step(kb, carry):
        s_lo = _nt(kilo_ref[kb], qi)
        s_hi = _nt(kihi_ref[kb], qi)
        sc = jnp.zeros((KEY_BLK, tq), F32)
        for p in range(IDX_HEADS // 2):
            cs = slice(p * tq, (p + 1) * tq)
            sc = sc + jnp.maximum(s_lo[:, cs], 0.0) * wT[2 * p:2 * p + 1, :]
            sc = sc + jnp.maximum(s_hi[:, cs], 0.0) * wT[2 * p + 1:2 * p + 2, :]
        sc = jnp.where(kb * KEY_BLK + kio_q <= t_row, sc, NEG)
        bits = lax.bitcast_convert_type(sc, jnp.int32)
        key_ref[kb] = jnp.where(bits < 0, bits ^ 0x7FFFFFFF, bits)
        return carry

    lax.fori_loop(0, n_kb, idx_step, 0)

    def bit_step(it, theta):
        cand = theta + lax.shift_left(jnp.int32(1), 31 - it)

        def cnt_step(kb, c):
            return c + jnp.sum((key_ref[kb] >= cand).astype(jnp.int32), axis=0, keepdims=True)

        cnt = lax.fori_loop(0, n_kb, cnt_step, jnp.zeros((1, tq), jnp.int32))
        return jnp.where(cnt >= n_keep, cand, theta)

    theta = lax.fori_loop(0, 32, bit_step, jnp.full((1, tq), INT_MIN, jnp.int32))
    theta4 = jnp.concatenate([theta] * DSA_HPG, axis=1)

    qs = [_stack_heads(qp_ref, DSA_HPG * g, DSA_HPG) for g in range(DSA_KV)]
    m0 = jnp.full((1, nq), NEG, F32)
    l0 = jnp.zeros((1, nq), F32)
    acc_ref[...] = jnp.zeros_like(acc_ref)

    def att_step(kb, carry, causal):
        keys = jnp.concatenate([key_ref[kb]] * DSA_HPG, axis=1)
        mask = keys >= theta4
        if causal:
            mask = mask & (kb * KEY_BLK + kio <= lane_t)
        out = []
        for g in range(DSA_KV):
            m, l = _flash_step(k_ref[g // 2, kb], vT_ref[g // 2, kb], qs[g], mask,
                               carry[2 * g], carry[2 * g + 1], acc_ref.at[g])
            out += [m, l]
        return tuple(out)

    mid = lax.fori_loop(0, t0 // KEY_BLK, functools.partial(att_step, causal=False), (m0, l0) * DSA_KV)
    fin = lax.fori_loop(t0 // KEY_BLK, n_kb, functools.partial(att_step, causal=True), mid)
    for g in range(DSA_KV):
        oT = acc_ref[g] * (1.0 / jnp.maximum(fin[2 * g + 1], 1e-30))
        _store_heads(o_ref, oT, DSA_HPG * g, DSA_HPG, g % 2)


def dsa_prompt(qp, kv, qi, ki, wi):
    B, T, _ = kv.shape
    assert T % KEY_BLK == 0
    tq = ATT_TQ
    nq = DSA_HPG * tq
    n_kb = T // KEY_BLK
    n_keep = min(IDX_TOPK, T // 4)
    tile = lambda w: pl.BlockSpec((1, tq, w), lambda b, i: (b, i, 0))
    whole = lambda w: pl.BlockSpec((1, T, w), lambda b, i: (b, 0, 0))
    return pl.pallas_call(
        functools.partial(_dsa_body, T=T, n_keep=n_keep),
        grid=(B, T // tq),
        in_specs=[tile(DSA_HEADS * LANE), whole(4 * LANE), tile(IDX_QW), whole(LANE), tile(LANE)],
        out_specs=tile(DSA_HEADS * LANE),
        out_shape=jax.ShapeDtypeStruct((B, T, DSA_HEADS * LANE), BF16),
        scratch_shapes=[pltpu.VMEM((2, n_kb, KEY_BLK, LANE), BF16), pltpu.VMEM((2, n_kb, LANE, KEY_BLK), BF16),
                        pltpu.VMEM((n_kb, KEY_BLK, LANE), BF16), pltpu.VMEM((n_kb, KEY_BLK, LANE), BF16),
                        pltpu.VMEM((n_kb, KEY_BLK, tq), jnp.int32), pltpu.VMEM((DSA_KV, LANE, nq), F32)],
        compiler_params=_params("parallel", "arbitrary"),
        name="dsa_prompt",
    )(qp, kv, qi, ki, wi)


def _pad_head_cols(w, slots):
    D = w.shape[0]
    H = len(slots)
    w3 = w.reshape(D, H, HEAD_DIM)
    z = jnp.zeros_like(w3)
    s = jnp.asarray(slots)[None, :, None]
    return jnp.concatenate([jnp.where(s == 0, w3, z), jnp.where(s == 1, w3, z)], axis=2).reshape(D, H * LANE)


def _pad_cols(w, n):
    return jnp.pad(w, ((0, 0), (0, n - w.shape[1])))


def _dot01_r(x, m01):
    hi, mid, lo = _split3(x)
    d = lambda p: jnp.dot(p, m01, preferred_element_type=F32)
    return d(hi) + d(mid) + d(lo)


def _softmax_rows(s, mask):
    s = jnp.where(mask, s, NEG)
    e = jnp.where(mask, jnp.exp(s - jnp.max(s, axis=1, keepdims=True)), 0.0)
    return e * (1.0 / jnp.maximum(jnp.sum(e, axis=1, keepdims=True), 1e-30))


def _flash_rows(s, mask, m_ref, l_ref):
    s = jnp.where(mask, s, NEG)
    m = m_ref[...]
    m_new = jnp.maximum(m, jnp.max(s, axis=1, keepdims=True))
    alpha = jnp.exp(m - m_new)
    e = jnp.where(mask, jnp.exp(s - m_new), 0.0)
    l_ref[...] = alpha * l_ref[...] + jnp.sum(e, axis=1, keepdims=True)
    m_ref[...] = m_new
    return alpha, e


def _sort_key(x):
    bits = lax.bitcast_convert_type(x, jnp.int32)
    return jnp.where(bits < 0, bits ^ 0x7FFFFFFF, bits)


def _page_spec(k, width, col):
    return pl.BlockSpec((1, PAGE_SIZE, width), lambda b, s, pt: (pt[b, s * PG_STEP + k], 0, col))


def _per_request(shape):
    return pl.BlockSpec((1,) + shape, lambda b, s, pt: (b,) + (0,) * len(shape))


def _nsa_cmp_body(pt_ref, *refs):
    pages = refs[:PG_STEP]
    pek_ref, pev_ref, wck_ref, wcv_ref, o_ref, stage_ref = refs[PG_STEP:]
    n_half = PG_STEP * PAGE_SIZE // CMP_STRIDE
    for slab, pe_ref, w_ref in ((0, pek_ref, wck_ref), (1, pev_ref, wcv_ref)):
        for pg in range(PG_STEP):
            stage_ref[pg * PAGE_SIZE:(pg + 1) * PAGE_SIZE, :] = pages[pg][0, :, slab * LANE:(slab + 1) * LANE]
        first = jnp.zeros((n_half, LANE), F32)
        second = jnp.zeros((n_half, LANE), F32)
        for r in range(CMP_STRIDE):
            y = stage_ref[pl.ds(r, n_half, stride=CMP_STRIDE), :]
            first = first + jnp.dot((y + pe_ref[r:r + 1, :]).astype(BF16), w_ref[r], preferred_element_type=F32)
            second = second + jnp.dot((y + pe_ref[r + CMP_STRIDE:r + CMP_STRIDE + 1, :]).astype(BF16),
                                      w_ref[r + CMP_STRIDE], preferred_element_type=F32)
        o_ref[0, :, (2 * slab) * LANE:(2 * slab + 1) * LANE] = first
        o_ref[0, :, (2 * slab + 1) * LANE:(2 * slab + 2) * LANE] = second


def nsa_sample_compress(cache, page_table, pe_k, pe_v, w_ck, w_cv):
    B, n_pages = page_table.shape
    assert n_pages % PG_STEP == 0
    n_half_step = PG_STEP * PAGE_SIZE // CMP_STRIDE
    const = lambda shape: pl.BlockSpec(shape, lambda b, s, pt: (0,) * len(shape))
    grid_spec = pltpu.PrefetchScalarGridSpec(
        num_scalar_prefetch=1,
        grid=(B, n_pages // PG_STEP),
        in_specs=[_page_spec(k, 2 * LANE, 0) for k in range(PG_STEP)]
        + [const((CMP_BLK, LANE)), const((CMP_BLK, LANE)), const((CMP_BLK, LANE, LANE)), const((CMP_BLK, LANE, LANE))],
        out_specs=pl.BlockSpec((1, n_half_step, 4 * LANE), lambda b, s, pt: (b, s, 0)),
        scratch_shapes=[pltpu.VMEM((PG_STEP * PAGE_SIZE, LANE), F32)],
    )
    return pl.pallas_call(
        _nsa_cmp_body,
        grid_spec=grid_spec,
        out_shape=jax.ShapeDtypeStruct((B, n_pages * PAGE_SIZE // CMP_STRIDE, 4 * LANE), F32),
        compiler_params=_params("parallel", "arbitrary"),
        name="nsa_sample_compress",
    )(page_table, *([cache] * PG_STEP), jnp.tile(pe_k, (1, 2)), jnp.tile(pe_v, (1, 2)),
      _block_diag2(w_ck), _block_diag2(w_cv))


def _nsa_smp_body(pt_ref, qs_ref, gl_ref, fs_ref, *refs, n_tok):
    pages = refs[:PG_STEP]
    nrow_ref, pwin_ref, nwin_ref, o_ref, shift_ref, selc_ref, oc_ref, m_ref, l_ref, acc_ref = refs[PG_STEP:]
    s_id = pl.program_id(1)
    C = LANE
    n_half = fs_ref.shape[1]
    n_cmp = n_half - 1
    n_selp = n_half * CMP_STRIDE // SEL_BLK
    qs = qs_ref[0]
    crow = lax.broadcasted_iota(jnp.int32, (C, LANE), 0)
    lane = lax.broadcasted_iota(jnp.int32, (C, LANE), 1)
    t_of_c = crow % n_tok

    @pl.when(s_id == 0)
    def _first():
        cmp = []
        for slab in range(2):
            shift_ref[0:n_half, :] = fs_ref[0, :, (2 * slab + 1) * LANE:(2 * slab + 2) * LANE]
            shift_ref[n_half:n_half + 8, :] = jnp.zeros((8, LANE), F32)
            cmp.append((fs_ref[0, :, (2 * slab) * LANE:(2 * slab + 1) * LANE]
                        + shift_ref[1:n_half + 1, :]).astype(BF16))
        nlane = lax.broadcasted_iota(jnp.int32, (C, n_half), 1)
        p = _softmax_rows(_nt(qs, cmp[0]), nlane < n_cmp)
        oc_ref[...] = jnp.dot(p.astype(BF16), cmp[1], preferred_element_type=F32)
        imp_rows = []
        for g in range(NSA_KV):
            acc = p[(NSA_HPG * g) * n_tok:(NSA_HPG * g + 1) * n_tok, :]
            for h in range(1, NSA_HPG):
                acc = acc + p[(NSA_HPG * g + h) * n_tok:(NSA_HPG * g + h + 1) * n_tok, :]
            imp_rows.append(acc)
        imp = jnp.concatenate(imp_rows + [jnp.zeros((C - NSA_KV * n_tok, n_half), F32)], axis=0)
        pi = lax.broadcasted_iota(jnp.int32, (n_half, n_selp), 0)
        pj = lax.broadcasted_iota(jnp.int32, (n_half, n_selp), 1)
        pool = ((((pi + 1) >> 2) == pj) & (pi < n_cmp)).astype(BF16)
        scoreT = _dot01_r(imp, pool).T
        jcol = lax.broadcasted_iota(jnp.int32, (n_selp, C), 0)
        forced = (jcol == 0) | (jcol == n_selp - 1)
        scoreT = scoreT + jnp.where(forced, FORCE_SCORE, 0.0)
        rank = jnp.zeros((n_selp, C), jnp.int32)
        for j2 in range(n_selp):
            r = scoreT[j2:j2 + 1, :]
            beats = (r > scoreT) | ((r == scoreT) & (jcol > j2))
            rank = rank + beats.astype(jnp.int32)
        sel = jnp.where(rank < TOP_N - 1, 1.0, 0.0).T
        selc_ref[...] = jnp.concatenate(
            [sel[(hh // NSA_HPG) * n_tok:(hh // NSA_HPG + 1) * n_tok, :] for hh in range(NSA_HEADS)]
            + [jnp.zeros((C - NSA_HEADS * n_tok, n_selp), F32)], axis=0)
        m_ref[...] = jnp.full(m_ref.shape, NEG, F32)
        l_ref[...] = jnp.zeros(l_ref.shape, F32)
        acc_ref[...] = jnp.zeros(acc_ref.shape, F32)

    selc = selc_ref[...].astype(BF16)
    ej = lax.broadcasted_iota(jnp.int32, (n_selp, PAGE_SIZE), 0)
    ek = lax.broadcasted_iota(jnp.int32, (n_selp, PAGE_SIZE), 1)
    for pg in range(PG_STEP):
        page = s_id * PG_STEP + pg
        kblk = pages[pg][0, :, 0:LANE].astype(BF16)
        vblk = pages[pg][0, :, LANE:2 * LANE].astype(BF16)
        expand = (ej == page * (PAGE_SIZE // SEL_BLK) + ek // SEL_BLK).astype(BF16)
        mask = jnp.dot(selc, expand, preferred_element_type=F32) > 0.5
        alpha, e = _flash_rows(_nt(qs, kblk), mask, m_ref, l_ref)
        acc_ref[...] = alpha * acc_ref[...] + jnp.dot(e.astype(BF16), vblk, preferred_element_type=F32)

    @pl.when(s_id == pl.num_programs(1) - 1)
    def _last():
        pad = jnp.zeros((LANE - n_tok, LANE), F32)
        new_ok = (lane < n_tok) & (lane <= t_of_c)
        knew = jnp.concatenate([nrow_ref[0, :, 2 * LANE:3 * LANE], pad], axis=0).astype(BF16)
        vnew = jnp.concatenate([nrow_ref[0, :, 3 * LANE:4 * LANE], pad], axis=0).astype(BF16)
        alpha, e = _flash_rows(_nt(qs, knew), new_ok, m_ref, l_ref)
        acc = alpha * acc_ref[...] + jnp.dot(e.astype(BF16), vnew, preferred_element_type=F32)
        o_s = acc * (1.0 / jnp.maximum(l_ref[...], 1e-30))
        n_win = pwin_ref.shape[1]
        kw = jnp.concatenate([pwin_ref[0, :, 0:LANE], nwin_ref[0, :, 0:LANE], pad], axis=0).astype(BF16)
        vw = jnp.concatenate([pwin_ref[0, :, LANE:2 * LANE], nwin_ref[0, :, LANE:2 * LANE], pad], axis=0).astype(BF16)
        wl = lax.broadcasted_iota(jnp.int32, (C, n_win + LANE), 1)
        tw = lax.broadcasted_iota(jnp.int32, (C, n_win + LANE), 0) % n_tok
        wmask = (((wl < n_win) & ((n_win - wl) + tw < WINDOW))
                 | ((wl >= n_win) & (wl - n_win < n_tok) & (wl - n_win <= tw)))
        pw = _softmax_rows(_nt(qs, kw), wmask)
        o_w = jnp.dot(pw.astype(BF16), vw, preferred_element_type=F32)
        g = jax.nn.sigmoid(gl_ref[0])
        o = g[:, 0:1] * oc_ref[...] + g[:, 1:2] * o_s + g[:, 2:3] * o_w
        valid = (crow < NSA_HEADS * n_tok) & (lane // HEAD_DIM == crow // (n_tok * NSA_HPG))
        o_ref[0] = jnp.where(valid, o, 0.0)


def nsa_sample_attention(qs, gcol, fs, cache, page_table, new_rows, past_win, new_win):
    B, n_pages = page_table.shape
    n_tok = new_rows.shape[1]
    n_half = fs.shape[1]
    n_selp = n_half * CMP_STRIDE // SEL_BLK
    n_win = past_win.shape[1]
    assert n_selp == LANE and NSA_HEADS * n_tok <= LANE and n_tok <= min(SEL_BLK, 8) and n_pages % PG_STEP == 0
    assert n_win % LANE == 0 and n_win <= WINDOW
    grid_spec = pltpu.PrefetchScalarGridSpec(
        num_scalar_prefetch=1,
        grid=(B, n_pages // PG_STEP),
        in_specs=[_per_request((LANE, LANE)), _per_request((LANE, LANE)), _per_request((n_half, 4 * LANE))]
        + [_page_spec(k, 2 * LANE, 1) for k in range(PG_STEP)]
        + [_per_request((n_tok, 4 * LANE)), _per_request((n_win, 2 * LANE)), _per_request((n_tok, 2 * LANE))],
        out_specs=_per_request((LANE, LANE)),
        scratch_shapes=[pltpu.VMEM((n_half + 8, LANE), F32), pltpu.VMEM((LANE, n_selp), F32),
                        pltpu.VMEM((LANE, LANE), F32), pltpu.VMEM((LANE, 1), F32), pltpu.VMEM((LANE, 1), F32),
                        pltpu.VMEM((LANE, LANE), F32)],
    )
    return pl.pallas_call(
        functools.partial(_nsa_smp_body, n_tok=n_tok),
        grid_spec=grid_spec,
        out_shape=jax.ShapeDtypeStruct((B, LANE, LANE), F32),
        compiler_params=_params("parallel", "arbitrary"),
        name="nsa_sample_attention",
    )(page_table, qs, gcol, fs, *([cache] * PG_STEP), new_rows, past_win, new_win)


def _dsa_idx_body(pt_ref, qe_ref, qo_ref, w_ref, *refs, n_tok, n_keep):
    pages = refs[:PG_STEP]
    knew_ref, sc_ref, th_ref = refs[PG_STEP:]
    s_id = pl.program_id(1)
    n_blk = sc_ref.shape[1]
    qe = qe_ref[0]
    qo = qo_ref[0]
    w = w_ref[0] * (IDX_HEADS ** -0.5)
    n_rows = (IDX_HEADS // 2) * n_tok

    def scores(kblk):
        s = jnp.maximum(_nt(qe, kblk), 0.0) * w[:, 0:1] + jnp.maximum(_nt(qo, kblk), 0.0) * w[:, 1:2]
        tot = s[0:n_tok, :]
        for p in range(1, IDX_HEADS // 2):
            tot = tot + s[p * n_tok:(p + 1) * n_tok, :]
        return tot

    for pg in range(PG_STEP):
        sc_ref[0, s_id * PG_STEP + pg] = scores(pages[pg][0].astype(BF16))

    @pl.when(s_id == pl.num_programs(1) - 1)
    def _last():
        kn = jnp.concatenate([knew_ref[0], jnp.zeros((LANE - n_tok, IDX_DIM), F32)], axis=0).astype(BF16)
        a_i = lax.broadcasted_iota(jnp.int32, (n_tok, LANE), 1)
        t_i = lax.broadcasted_iota(jnp.int32, (n_tok, LANE), 0)
        sn = jnp.where(a_i <= t_i, scores(kn), NEG)
        sc_ref[0, n_blk - 1] = jnp.where(a_i < n_tok, sn, -jnp.inf)
        keys = _sort_key(sc_ref[0])

        def bit_step(it, theta):
            cand = theta + lax.shift_left(jnp.int32(1), 31 - it)
            cnt = jnp.sum(jnp.sum((keys >= cand).astype(jnp.int32), axis=0), axis=1, keepdims=True)
            return jnp.where(cnt >= n_keep, cand, theta)

        theta = lax.fori_loop(0, 32, bit_step, jnp.full((n_tok, 1), INT_MIN, jnp.int32))
        th_ref[0] = jnp.broadcast_to(theta, (n_tok, LANE))


def dsa_sample_index(qe, qo, wcol, cache_idx, page_table, ki_new):
    B, n_pages = page_table.shape
    n_tok = ki_new.shape[1]
    assert n_tok == 8 and n_pages % PG_STEP == 0
    n_keep = min(IDX_TOPK, (n_pages * PAGE_SIZE + n_tok) // 4)
    grid_spec = pltpu.PrefetchScalarGridSpec(
        num_scalar_prefetch=1,
        grid=(B, n_pages // PG_STEP),
        in_specs=[_per_request((LANE, IDX_DIM)), _per_request((LANE, IDX_DIM)), _per_request((LANE, LANE))]
        + [_page_spec(k, IDX_DIM, 0) for k in range(PG_STEP)] + [_per_request((n_tok, IDX_DIM))],
        out_specs=[_per_request((n_pages + 1, n_tok, LANE)), _per_request((n_tok, LANE))],
    )
    return pl.pallas_call(
        functools.partial(_dsa_idx_body, n_tok=n_tok, n_keep=n_keep),
        grid_spec=grid_spec,
        out_shape=[jax.ShapeDtypeStruct((B, n_pages + 1, n_tok, LANE), F32),
                   jax.ShapeDtypeStruct((B, n_tok, LANE), jnp.int32)],
        compiler_params=_params("parallel", "arbitrary"),
        name="dsa_sample_index",
    )(page_table, qe, qo, wcol, *([cache_idx] * PG_STEP), ki_new)


def _dsa_smp_body(pt_ref, qs_ref, sc_ref, th_ref, *refs, n_tok):
    pages = refs[:PG_STEP]
    kvn_ref, o_ref, m_ref, l_ref, acc_ref = refs[PG_STEP:]
    s_id = pl.program_id(1)
    C = LANE
    qs = qs_ref[0]
    crow = lax.broadcasted_iota(jnp.int32, (C, LANE), 0)
    lane = lax.broadcasted_iota(jnp.int32, (C, LANE), 1)
    low = crow < C // 2
    theta = th_ref[0]

    @pl.when(s_id == 0)
    def _():
        m_ref[...] = jnp.full(m_ref.shape, NEG, F32)
        l_ref[...] = jnp.zeros(l_ref.shape, F32)
        acc_ref[...] = jnp.zeros(acc_ref.shape, F32)

    def step(kv, blk, extra):
        kb = kv.astype(BF16)
        keep = jnp.where(_sort_key(sc_ref[0, blk]) >= theta, 1.0, 0.0)
        mask = jnp.concatenate([keep] * (C // n_tok), axis=0) > 0.5
        if extra is not None:
            mask = mask & extra
        s = jnp.where(low, _nt(qs, kb[:, 0:LANE]), _nt(qs, kb[:, LANE:2 * LANE]))
        alpha, e = _flash_rows(s, mask, m_ref, l_ref)
        eb = e.astype(BF16)
        pv = jnp.where(low, jnp.dot(eb, kb[:, 2 * LANE:3 * LANE], preferred_element_type=F32),
                       jnp.dot(eb, kb[:, 3 * LANE:4 * LANE], preferred_element_type=F32))
        acc_ref[...] = alpha * acc_ref[...] + pv

    for pg in range(PG_STEP):
        step(pages[pg][0], s_id * PG_STEP + pg, None)

    @pl.when(s_id == pl.num_programs(1) - 1)
    def _last():
        kvn = jnp.concatenate([kvn_ref[0], jnp.zeros((LANE - n_tok, 4 * LANE), F32)], axis=0)
        step(kvn, sc_ref.shape[1] - 1, (lane < n_tok) & (lane <= crow % n_tok))
        o = acc_ref[...] * (1.0 / jnp.maximum(l_ref[...], 1e-30))
        valid = lane // HEAD_DIM == (crow // (n_tok * DSA_HPG)) % 2
        o_ref[0] = jnp.where(valid, o, 0.0)


def dsa_sample_attention(qs, scores, theta, cache_kv, page_table, kv_new):
    B, n_pages = page_table.shape
    n_tok = kv_new.shape[1]
    assert DSA_HEADS * n_tok == LANE and n_pages % PG_STEP == 0
    grid_spec = pltpu.PrefetchScalarGridSpec(
        num_scalar_prefetch=1,
        grid=(B, n_pages // PG_STEP),
        in_specs=[_per_request((LANE, LANE)), _per_request((n_pages + 1, n_tok, LANE)), _per_request((n_tok, LANE))]
        + [_page_spec(k, 4 * LANE, 0) for k in range(PG_STEP)] + [_per_request((n_tok, 4 * LANE))],
        out_specs=_per_request((LANE, LANE)),
        scratch_shapes=[pltpu.VMEM((LANE, 1), F32), pltpu.VMEM((LANE, 1), F32), pltpu.VMEM((LANE, LANE), F32)],
    )
    return pl.pallas_call(
        functools.partial(_dsa_smp_body, n_tok=n_tok),
        grid_spec=grid_spec,
        out_shape=jax.ShapeDtypeStruct((B, LANE, LANE), F32),
        compiler_params=_params("parallel", "arbitrary"),
        name="dsa_sample_attention",
    )(page_table, qs, scores, theta, *([cache_kv] * PG_STEP), kv_new)


_NSA_SLOTS = tuple(h // NSA_HPG for h in range(NSA_HEADS))
_DSA_SLOTS = tuple((h // DSA_HPG) % 2 for h in range(DSA_HEADS))
def _even_segs(q_dtype):
    return ((NSA_HEADS * LANE, (1,) * NSA_HEADS, q_dtype), (4 * LANE, (1, 0, 1, 0), F32), (2 * LANE, (1, 0), F32),
            (LANE, (0,), F32), (4 * HGRN_KW, (0,) * (4 * HGRN_KW // LANE), F32))


def _odd_segs(q_dtype):
    return ((DSA_HEADS * LANE, (1,) * DSA_HEADS, q_dtype), (4 * LANE, (1, 1, 0, 0), F32), (IDX_QW, (1,) * 4, q_dtype),
            (LANE, (1,), F32), (LANE, (0,), F32))


def _even_weights(w_in, w_out):
    c = np.cumsum([0, NSA_QW] + [NSA_KVW] * 6 + [NSA_GW] + [HGRN_KW] * 4)
    w = jnp.concatenate([
        _pad_head_cols(w_in[:, :NSA_QW] * HEAD_DIM ** -0.5, _NSA_SLOTS),
        w_in[:, c[1]:c[5]], w_in[:, c[5]:c[7]], _pad_cols(w_in[:, c[7]:c[8]], LANE), w_in[:, c[8]:c[12]]],
        axis=1).astype(BF16)
    w_outs = [_pad_head_cols(w_out[:NSA_QW].T, _NSA_SLOTS).T.astype(BF16), w_out[NSA_QW:].astype(BF16)]
    return w, w_outs


def _odd_weights(w_in, w_out):
    c = np.cumsum([0, DSA_QW, DSA_KVW, DSA_KVW, IDX_QW, IDX_DIM, IDX_HEADS])
    w = jnp.concatenate([
        _pad_head_cols(w_in[:, :DSA_QW] * HEAD_DIM ** -0.5, _DSA_SLOTS),
        w_in[:, c[1]:c[3]], w_in[:, c[3]:c[4]] * IDX_DIM ** -0.5,
        _pad_cols(w_in[:, c[4]:c[5]], LANE), _pad_cols(w_in[:, c[5]:c[6]], LANE)], axis=1).astype(BF16)
    return w, [_pad_head_cols(w_out.T, _DSA_SLOTS).T.astype(BF16)]


def even_mixer_prompt(x, scale, shift, cos_t, sin_t, w_in, w_out, pe_k, pe_v, w_ck, w_cv, lb, norm_g, win_len):
    B, T, _ = x.shape
    assert T >= win_len
    w, w_outs = _even_weights(w_in, w_out)
    qp, rows, win, gl, hx = proj_segments(x, scale, shift, cos_t, sin_t, w, _even_segs(BF16))
    o_a = nsa_prompt(qp, rows, win, gl, pe_k, pe_v, w_ck, w_cv)
    o_b, s_new = hgrn_mixer(hx, lb, norm_g, jnp.zeros((B, HGRN_HEADS, HGRN_DK, HGRN_DV), F32))
    new_rows = rows.reshape(B, T, 4, NSA_KV, HEAD_DIM)
    win_state = win[:, T - win_len:].reshape(B, win_len, 2, NSA_KV, HEAD_DIM)
    return [o_a, o_b], w_outs, new_rows, win_state, s_new


def odd_mixer_prompt(x, scale, shift, cos_t, sin_t, w_in, w_out):
    B, T, _ = x.shape
    w, w_outs = _odd_weights(w_in, w_out)
    qp, kv, qi, ki, wi = proj_segments(x, scale, shift, cos_t, sin_t, w, _odd_segs(BF16))
    o = dsa_prompt(qp, kv, qi, ki, wi)
    return [o], w_outs, kv.reshape(B, T, 2, DSA_KV, HEAD_DIM), ki[..., :IDX_DIM]


def _stack_rows(a, B, n, heads, width):
    s = a.reshape(B, n, heads, width).transpose(0, 2, 1, 3).reshape(B, heads * n, width)
    return jnp.pad(s, ((0, 0), (0, LANE - heads * n), (0, 0)))


def _unstack_rows(o, B, n, heads):
    return o[:, :heads * n].reshape(B, heads, n, LANE).transpose(0, 2, 1, 3).reshape(1, B * n, heads * LANE)


def even_mixer_sample(x, scale, shift, past_len, B, w_in, w_out, pe_k, pe_v, w_ck, w_cv, lb, norm_g,
                      cache, cache_win, state, page_table):
    n = x.shape[1] // B
    win_len = cache_win.shape[1]
    cos_t, sin_t = [jnp.tile(a, (B, 1)) for a in rope_tables(past_len + jnp.arange(n))]
    w, w_outs = _even_weights(w_in, w_out)
    qp, rows, win, gl, hx = proj_segments(x, scale, shift, cos_t, sin_t, w, _even_segs(F32))
    rows, win, hx = [a.reshape(B, n, a.shape[-1]) for a in (rows, win, hx)]
    qs = _stack_rows(qp, B, n, NSA_HEADS, LANE).astype(BF16)
    gcol = jnp.pad(_stack_rows(gl[..., :NSA_GW], B, n, NSA_HEADS, 3), ((0, 0), (0, 0), (0, LANE - 3)))
    cache2 = cache.reshape(cache.shape[0], PAGE_SIZE, 4 * LANE)
    fs = nsa_sample_compress(cache2, page_table, pe_k, pe_v, w_ck, w_cv)
    o = nsa_sample_attention(qs, gcol, fs, cache2, page_table, rows, cache_win.reshape(B, win_len, 2 * LANE), win)
    o_a = _unstack_rows(o, B, n, NSA_HEADS)
    o_b, s_new = hgrn_mixer(hx, lb, norm_g, state)
    new_rows = rows.reshape(B, n, 4, NSA_KV, HEAD_DIM)
    win_state = jnp.concatenate([cache_win, win.reshape(B, n, 2, NSA_KV, HEAD_DIM)], axis=1)[:, -win_len:]
    return [o_a, o_b.reshape(1, B * n, HGRN_VW)], w_outs, new_rows, win_state, s_new


def odd_mixer_sample(x, scale, shift, past_len, B, w_in, w_out, cache_kv, cache_idx, page_table):
    n = x.shape[1] // B
    cos_t, sin_t = [jnp.tile(a, (B, 1)) for a in rope_tables(past_len + jnp.arange(n))]
    w, w_outs = _odd_weights(w_in, w_out)
    qp, kv, qi, ki, wi = proj_segments(x, scale, shift, cos_t, sin_t, w, _odd_segs(F32))
    kv, ki = kv.reshape(B, n, 4 * LANE), ki.reshape(B, n, LANE)[..., :IDX_DIM]
    qs = _stack_rows(qp, B, n, DSA_HEADS, LANE).astype(BF16)
    qi4 = qi.reshape(1, B * n, IDX_HEADS // 2, 2, IDX_DIM)
    qe = _stack_rows(qi4[:, :, :, 0], B, n, IDX_HEADS // 2, IDX_DIM).astype(BF16)
    qo = _stack_rows(qi4[:, :, :, 1], B, n, IDX_HEADS // 2, IDX_DIM).astype(BF16)
    wcol = jnp.pad(_stack_rows(wi[..., :IDX_HEADS], B, n, IDX_HEADS // 2, 2), ((0, 0), (0, 0), (0, LANE - 2)))
    scores, theta = dsa_sample_index(qe, qo, wcol, cache_idx, page_table, ki)
    o = dsa_sample_attention(qs, scores, theta, cache_kv.reshape(cache_kv.shape[0], PAGE_SIZE, 4 * LANE),
                             page_table, kv)
    return [_unstack_rows(o, B, n, DSA_HEADS)], w_outs, kv.reshape(B, n, 2, DSA_KV, HEAD_DIM), ki


def moe_ffn_residual(xs, hs, logits_list, gates, w1, w3, w2):
    D = D_MODEL
    h_all = jnp.concatenate([h.reshape(-1, D) for h in hs], axis=0)
    logits = jnp.concatenate([lg.reshape(-1, lg.shape[-1])[:, :N_EXPERTS] for lg in logits_list], axis=0)
    n_tok = h_all.shape[0]
    top_v, top_i = lax.top_k(logits, TOP_K)
    weights = jax.nn.softmax(top_v, axis=-1)
    e_flat = top_i.reshape(-1)
    onehot = (e_flat[:, None] == jnp.arange(N_EXPERTS)[None, :]).astype(jnp.int32)
    csum = jnp.cumsum(onehot, axis=0)
    counts = csum[-1]
    rank = jnp.take_along_axis(csum, e_flat[:, None], axis=1)[:, 0] - 1
    padded = ((counts + MOE_TILE - 1) // MOE_TILE) * MOE_TILE
    group_end = jnp.cumsum(padded)
    group_start = group_end - padded
    slot = group_start[e_flat] + rank
    n_slots = _round_up(n_tok * TOP_K, MOE_TILE) + N_EXPERTS * MOE_TILE
    n_tiles = n_slots // MOE_TILE
    tok_of_slot = jnp.zeros((n_slots,), jnp.int32).at[slot].set(jnp.arange(n_tok * TOP_K, dtype=jnp.int32) // TOP_K)
    w_slot = jnp.zeros((n_slots,), F32).at[slot].set(weights.reshape(-1))
    tile_start = jnp.arange(n_tiles, dtype=jnp.int32) * MOE_TILE
    tile_expert = jnp.minimum(jnp.sum(tile_start[:, None] >= group_end[None, :], axis=1), N_EXPERTS - 1).astype(jnp.int32)
    n_used = (group_end[-1] // MOE_TILE).astype(jnp.int32).reshape(1)
    h_sorted = h_all[tok_of_slot]
    y_slot = moe_grouped_swiglu(h_sorted, w_slot.reshape(n_slots, 1), tile_expert, n_used, w1, w3, w2)
    slot2 = slot.reshape(n_tok, TOP_K)
    outs = []
    off = 0
    for x, g in zip(xs, gates):
        n = x.shape[0] * x.shape[1]
        ya = y_slot[slot2[off:off + n, 0]].reshape(x.shape)
        yb = y_slot[slot2[off:off + n, 1]].reshape(x.shape)
        outs.append(moe_combine_residual(x, g, ya, yb))
        off += n
    return outs


def kernel(x_prompt, x_sample, cache_nsa, cache_nsa_win, state_hgrn, cache_dsa_kv, cache_dsa_idx, page_table, c_prompt, c_sample, ada_w, ada_b, norm1_g, norm2_g, final_g, even_w_in, even_w_out, nsa_pe_k, nsa_pe_v, nsa_w_ck, nsa_w_cv, hgrn_lb_raw, hgrn_norm_g, ffn_w1, ffn_w3, ffn_w2, odd_w_in, odd_w_out, router_w, router_b, moe_w1, moe_w3, moe_w2):
    D = D_MODEL
    past_len = page_table.shape[1] * PAGE_SIZE
    win_len = cache_nsa_win.shape[2]
    Bp, Tp = x_prompt.shape[:2]
    Bs, Ts = x_sample.shape[:2]
    lb_soft = jax.nn.softmax(hgrn_lb_raw.astype(F32), axis=0)
    lower_bounds = jnp.cumsum(lb_soft, axis=0) - lb_soft[0]

    R = _round_up(Bp + Bs, 8)
    c_all = jnp.zeros((R, D), F32).at[:Bp].set(c_prompt).at[Bp:Bp + Bs].set(c_sample)
    mods = ada_modulation(c_all, ada_w, ada_b)

    def group_mods(l, lo, n, per_token_rows):
        m = mods[l, lo:lo + n].reshape(n, 6, D)
        sh1, sc1, g1, sh2, sc2, g2 = [m[:, j] for j in range(6)]
        s1 = norm1_g[l][None] * (1.0 + sc1)
        s2 = norm2_g[l][None] * (1.0 + sc2)
        vecs = [s1, sh1, g1, s2, sh2, g2]
        if per_token_rows:
            return [jnp.repeat(v, per_token_rows, axis=0)[None] for v in vecs]
        return [v[:, None, :] for v in vecs]

    def pad_cols(w, n):
        return jnp.pad(w, ((0, 0), (0, n - w.shape[1]))).astype(BF16)

    xp = x_prompt
    xs = x_sample.reshape(1, Bs * Ts, D)
    cos_p, sin_p = rope_tables(jnp.arange(Tp))
    outs_p = dict(rows=[], win=[], st=[], kv=[], idx=[])
    outs_s = dict(rows=[], win=[], st=[], kv=[], idx=[])
    for l in range(DEPTH):
        i = l // 2
        mp = group_mods(l, 0, Bp, 0)
        msm = group_mods(l, Bp, Bs, Ts)
        if l % 2 == 0:
            w1, w3, w2 = ffn_w1[i].astype(BF16), ffn_w3[i].astype(BF16), ffn_w2[i].astype(BF16)
            mixed, w_outs, rows, win, s_new = even_mixer_prompt(
                xp, mp[0], mp[1], cos_p, sin_p, even_w_in[i], even_w_out[i], nsa_pe_k[i], nsa_pe_v[i],
                nsa_w_ck[i], nsa_w_cv[i], lower_bounds[i], hgrn_norm_g[i], win_len)
            outs_p['rows'].append(rows)
            outs_p['win'].append(win)
            outs_p['st'].append(s_new)
            xp = out_proj_residual(mixed, w_outs, xp, mp[2])
            xp = ffn_residual(xp, mp[3], mp[4], mp[5], w1, w3, w2)
            mixed, w_outs, rows, win, s_new = even_mixer_sample(
                xs, msm[0], msm[1], past_len, Bs, even_w_in[i], even_w_out[i], nsa_pe_k[i], nsa_pe_v[i],
                nsa_w_ck[i], nsa_w_cv[i], lower_bounds[i], hgrn_norm_g[i],
                cache_nsa[i], cache_nsa_win[i], state_hgrn[i], page_table)
            outs_s['rows'].append(rows)
            outs_s['win'].append(win)
            outs_s['st'].append(s_new)
            xs = out_proj_residual(mixed, w_outs, xs, msm[2])
            xs = ffn_residual(xs, msm[3], msm[4], msm[5], w1, w3, w2)
        else:
            w_r = pad_cols(router_w[i], LANE)
            w1, w3, w2 = moe_w1[i].astype(BF16), moe_w3[i].astype(BF16), moe_w2[i].astype(BF16)
            o, w_outs, kv, ki = odd_mixer_prompt(xp, mp[0], mp[1], cos_p, sin_p, odd_w_in[i], odd_w_out[i])
            outs_p['kv'].append(kv)
            outs_p['idx'].append(ki)
            xp = out_proj_residual(o, w_outs, xp, mp[2])
            o, w_outs, kv, ki = odd_mixer_sample(xs, msm[0], msm[1], past_len, Bs, odd_w_in[i], odd_w_out[i],
                                                 cache_dsa_kv[i], cache_dsa_idx[i], page_table)
            outs_s['kv'].append(kv)
            outs_s['idx'].append(ki)
            xs = out_proj_residual(o, w_outs, xs, msm[2])
            hs, lgs = [], []
            for (x, m) in ((xp, mp), (xs, msm)):
                logits, h = norm_proj(x, m[3], m[4], w_r, with_h=True)
                lgs.append(logits[..., :N_EXPERTS] + router_b[i].astype(F32))
                hs.append(h)
            xp, xs = moe_ffn_residual([xp, xs], hs, lgs, [mp[5], msm[5]], w1, w3, w2)
    y_prompt = final_norm(xp, final_g)
    y_sample = final_norm(xs, final_g).reshape(Bs, Ts, D)
    st = lambda od, k: jnp.stack(od[k])
    return (y_prompt, y_sample, st(outs_p, 'rows'), st(outs_s, 'rows'), st(outs_p, 'win'), st(outs_s, 'win'),
            st(outs_p, 'st'), st(outs_s, 'st'), st(outs_p, 'kv'), st(outs_s, 'kv'),
            st(outs_p, 'idx'), st(outs_s, 'idx'))
```

```python
import functools
import math

import numpy as np
import jax
import jax.numpy as jnp
from jax import lax
from jax.experimental import pallas as pl
from jax.experimental.pallas import tpu as pltpu

D_MODEL = 1024
DEPTH = 4
PAGE_SIZE = 128
HEAD_DIM = 64
NSA_HEADS = 8
NSA_KV = 2
NSA_HPG = NSA_HEADS // NSA_KV
CMP_STRIDE = 16
CMP_BLK = 2 * CMP_STRIDE
SEL_BLK = 64
TOP_N = 16
WINDOW = 512
FORCE_SCORE = 1.0e4
HGRN_HEADS = 4
HGRN_DK = 128
HGRN_DV = 128
HGRN_CHUNK = 64
DSA_HEADS = 16
DSA_KV = 4
DSA_HPG = DSA_HEADS // DSA_KV
IDX_HEADS = 8
IDX_DIM = 64
IDX_TOPK = 256
D_FF = 2816
N_EXPERTS = 8
TOP_K = 2
Q_BLK = 128
ROPE_THETA = 10000.0
EPS = 1e-6
NEG = -1.0e30
NSA_QW = NSA_HEADS * HEAD_DIM
NSA_KVW = NSA_KV * HEAD_DIM
NSA_GW = NSA_HEADS * 3
HGRN_KW = HGRN_HEADS * HGRN_DK
HGRN_VW = HGRN_HEADS * HGRN_DV
EVEN_IN = NSA_QW + 6 * NSA_KVW + NSA_GW + 2 * HGRN_KW + 2 * HGRN_VW
EVEN_MIX = NSA_QW + HGRN_VW
DSA_QW = DSA_HEADS * HEAD_DIM
DSA_KVW = DSA_KV * HEAD_DIM
IDX_QW = IDX_HEADS * IDX_DIM
ODD_IN = DSA_QW + 2 * DSA_KVW + IDX_QW + IDX_DIM + IDX_HEADS
ODD_MIX = DSA_QW

LANE = 128
ROW_TILE = 512
FF_CHUNK = 256
MOE_TILE = 256
PROJ_CHUNK = 512
ATT_TQ = 128
KEY_BLK = 256
HGRN_SUB = 16
HGRN_TILE = 256
PG_STEP = 8
INT_MIN = -2 ** 31
VMEM_LIMIT = 56 * 1024 * 1024

F32 = jnp.float32
BF16 = jnp.bfloat16


def _round_up(n, m):
    return -(-n // m) * m


def _params(*sem):
    return pltpu.CompilerParams(dimension_semantics=sem, vmem_limit_bytes=VMEM_LIMIT)


def _norm_mod(x, scale, shift):
    ms = jnp.mean(x * x, axis=-1, keepdims=True)
    return x * lax.rsqrt(ms + EPS) * scale + shift


def _mod_spec(mod, tm):
    if mod.shape[1] == 1:
        return pl.BlockSpec((1, 1, mod.shape[2]), lambda b, i: (b, 0, 0))
    return pl.BlockSpec((1, tm, mod.shape[2]), lambda b, i: (b, i, 0))


def _ada_body(c_ref, w_ref, b_ref, o_ref):
    c = c_ref[...]
    cs = (c * jax.nn.sigmoid(c)).astype(BF16)
    o_ref[0] = jnp.dot(cs, w_ref[0].astype(BF16), preferred_element_type=F32) + b_ref[0]


def ada_modulation(c_all, ada_w, ada_b):
    R, D = c_all.shape
    N = ada_w.shape[2]
    tn = 1536
    return pl.pallas_call(
        _ada_body,
        grid=(DEPTH, N // tn),
        in_specs=[pl.BlockSpec((R, D), lambda l, j: (0, 0)),
                  pl.BlockSpec((1, D, tn), lambda l, j: (l, 0, j)),
                  pl.BlockSpec((1, 1, tn), lambda l, j: (l, 0, j))],
        out_specs=pl.BlockSpec((1, R, tn), lambda l, j: (l, 0, j)),
        out_shape=jax.ShapeDtypeStruct((DEPTH, R, N), F32),
        compiler_params=_params("arbitrary", "arbitrary"),
        name="ada_modulation",
    )(c_all, ada_w, ada_b.reshape(DEPTH, 1, N))


def _norm_proj_body(x_ref, sc_ref, sh_ref, w_ref, o_ref, h_ref=None):
    h = _norm_mod(x_ref[0], sc_ref[0], sh_ref[0]).astype(BF16)
    o_ref[0] = jnp.dot(h, w_ref[...], preferred_element_type=F32)
    if h_ref is not None:
        h_ref[0] = h


def norm_proj(x, scale, shift, w, with_h=False):
    B, T, D = x.shape
    N = w.shape[1]
    tm = min(T, ROW_TILE)
    out_shape = [jax.ShapeDtypeStruct((B, T, N), F32)]
    out_specs = [pl.BlockSpec((1, tm, N), lambda b, i: (b, i, 0))]
    if with_h:
        out_shape.append(jax.ShapeDtypeStruct((B, T, D), BF16))
        out_specs.append(pl.BlockSpec((1, tm, D), lambda b, i: (b, i, 0)))
    res = pl.pallas_call(
        _norm_proj_body,
        grid=(B, T // tm),
        in_specs=[pl.BlockSpec((1, tm, D), lambda b, i: (b, i, 0)),
                  _mod_spec(scale, tm), _mod_spec(shift, tm),
                  pl.BlockSpec((D, N), lambda b, i: (0, 0))],
        out_specs=out_specs,
        out_shape=out_shape,
        compiler_params=_params("parallel", "parallel"),
        name="norm_proj",
    )(x, scale, shift, w)
    return res if with_h else res[0]


def _rope_chunk(y, cos, sin):
    lane = lax.broadcasted_iota(jnp.int32, y.shape, 1)
    swapped = jnp.where(lane % HEAD_DIM < HEAD_DIM // 2,
                        pltpu.roll(y, LANE - HEAD_DIM // 2, 1), pltpu.roll(y, HEAD_DIM // 2, 1))
    return y * cos + swapped * sin


def _proj_seg_body(x_ref, sc_ref, sh_ref, cos_ref, sin_ref, w_ref, *o_refs, segs):
    h = _norm_mod(x_ref[0], sc_ref[0], sh_ref[0]).astype(BF16)
    cos = cos_ref[...]
    sin = sin_ref[...]
    t_refs = iter(o_refs[len(segs):])
    c0 = 0
    for o_ref, (width, rope_flags, dtype, channel_major) in zip(o_refs, segs):
        t_ref = next(t_refs) if channel_major else None
        for j0 in range(0, width, PROJ_CHUNK):
            wd = min(PROJ_CHUNK, width - j0)
            y = jnp.dot(h, w_ref[:, c0 + j0:c0 + j0 + wd], preferred_element_type=F32)
            for k in range(wd // LANE):
                cols = slice(j0 + k * LANE, j0 + (k + 1) * LANE)
                yk = y[:, k * LANE:(k + 1) * LANE]
                if rope_flags[(j0 + k * LANE) // LANE]:
                    yk = _rope_chunk(yk, cos, sin)
                o_ref[0, :, cols] = yk.astype(dtype)
                if t_ref is not None:
                    t_ref[0, cols, :] = yk.T
        c0 += width


def proj_segments(x, scale, shift, cos_t, sin_t, w, segs):
    B, T, D = x.shape
    tm = min(T, ROW_TILE)
    N = w.shape[1]
    cm = [s for s in segs if s[3]]
    return pl.pallas_call(
        functools.partial(_proj_seg_body, segs=segs),
        grid=(B, T // tm),
        in_specs=[pl.BlockSpec((1, tm, D), lambda b, i: (b, i, 0)),
                  _mod_spec(scale, tm), _mod_spec(shift, tm),
                  pl.BlockSpec((tm, LANE), lambda b, i: (i, 0)),
                  pl.BlockSpec((tm, LANE), lambda b, i: (i, 0)),
                  pl.BlockSpec((D, N), lambda b, i: (0, 0))],
        out_specs=[pl.BlockSpec((1, tm, s[0]), lambda b, i: (b, i, 0)) for s in segs]
        + [pl.BlockSpec((1, s[0], tm), lambda b, i: (b, 0, i)) for s in cm],
        out_shape=[jax.ShapeDtypeStruct((B, T, s[0]), s[2]) for s in segs]
        + [jax.ShapeDtypeStruct((B, s[0], T), F32) for s in cm],
        compiler_params=_params("parallel", "parallel"),
        name="proj_segments",
    )(x, scale, shift, cos_t, sin_t, w)


def rope_tables(pos):
    half = HEAD_DIM // 2
    inv = ROPE_THETA ** (-jnp.arange(half, dtype=F32) / half)
    ang = pos.astype(F32)[:, None] * inv[None, :]
    cos, sin = jnp.cos(ang), jnp.sin(ang)
    return jnp.tile(cos, (1, 4)), jnp.tile(jnp.concatenate([-sin, sin], axis=1), (1, 2))


def _out_proj_body(*refs, n_in):
    a_refs, w_refs = refs[:n_in], refs[n_in:2 * n_in]
    x_ref, g_ref, o_ref = refs[2 * n_in:]
    y = None
    for a_ref, w_ref in zip(a_refs, w_refs):
        part = jnp.dot(a_ref[0].astype(BF16), w_ref[...], preferred_element_type=F32)
        y = part if y is None else y + part
    o_ref[0] = x_ref[0] + g_ref[0] * y


def out_proj_residual(a_list, w_list, x, gate):
    B, T, D = x.shape
    tm = min(T, ROW_TILE)
    n_in = len(a_list)
    return pl.pallas_call(
        functools.partial(_out_proj_body, n_in=n_in),
        grid=(B, T // tm),
        in_specs=[pl.BlockSpec((1, tm, a.shape[2]), lambda b, i: (b, i, 0)) for a in a_list]
        + [pl.BlockSpec(w.shape, lambda b, i: (0, 0)) for w in w_list]
        + [pl.BlockSpec((1, tm, D), lambda b, i: (b, i, 0)), _mod_spec(gate, tm)],
        out_specs=pl.BlockSpec((1, tm, D), lambda b, i: (b, i, 0)),
        out_shape=jax.ShapeDtypeStruct((B, T, D), F32),
        compiler_params=_params("parallel", "parallel"),
        name="out_proj_residual",
    )(*a_list, *w_list, x, gate)


def _swiglu_acc(h, w1_ref, w3_ref, w2_ref, acc_ref):
    for c in range(D_FF // FF_CHUNK):
        cols = slice(c * FF_CHUNK, (c + 1) * FF_CHUNK)
        u = jnp.dot(h, w1_ref[:, cols], preferred_element_type=F32)
        v = jnp.dot(h, w3_ref[:, cols], preferred_element_type=F32)
        a = (u * jax.nn.sigmoid(u) * v).astype(BF16)
        part = jnp.dot(a, w2_ref[cols, :], preferred_element_type=F32)
        if c == 0:
            acc_ref[...] = part
        else:
            acc_ref[...] += part


def _ffn_body(x_ref, sc_ref, sh_ref, g_ref, w1_ref, w3_ref, w2_ref, o_ref, acc_ref):
    x = x_ref[0]
    h = _norm_mod(x, sc_ref[0], sh_ref[0]).astype(BF16)
    _swiglu_acc(h, w1_ref, w3_ref, w2_ref, acc_ref)
    o_ref[0] = x + g_ref[0] * acc_ref[...]


def ffn_residual(x, scale, shift, gate, w1, w3, w2):
    B, T, D = x.shape
    tm = min(T, ROW_TILE)
    wspec = lambda shape: pl.BlockSpec(shape, lambda b, i: (0, 0))
    return pl.pallas_call(
        _ffn_body,
        grid=(B, T // tm),
        in_specs=[pl.BlockSpec((1, tm, D), lambda b, i: (b, i, 0)),
                  _mod_spec(scale, tm), _mod_spec(shift, tm), _mod_spec(gate, tm),
                  wspec((D, D_FF)), wspec((D, D_FF)), wspec((D_FF, D))],
        out_specs=pl.BlockSpec((1, tm, D), lambda b, i: (b, i, 0)),
        out_shape=jax.ShapeDtypeStruct((B, T, D), F32),
        scratch_shapes=[pltpu.VMEM((tm, D), F32)],
        compiler_params=_params("parallel", "parallel"),
        name="ffn_residual",
    )(x, scale, shift, gate, w1, w3, w2)


def _moe_body(te_ref, nt_ref, h_ref, ws_ref, w1_ref, w3_ref, w2_ref, o_ref, acc_ref):
    i = pl.program_id(0)

    @pl.when(i < nt_ref[0])
    def _():
        _swiglu_acc(h_ref[...], w1_ref.at[0], w3_ref.at[0], w2_ref.at[0], acc_ref)
        o_ref[...] = ws_ref[...] * acc_ref[...]

    @pl.when(i >= nt_ref[0])
    def _():
        o_ref[...] = jnp.zeros_like(o_ref)


def moe_grouped_swiglu(h_sorted, w_slot, tile_expert, n_tiles_used, w1, w3, w2):
    S, D = h_sorted.shape
    tm = MOE_TILE
    n_tiles = S // tm
    wspec = lambda shape: pl.BlockSpec((1,) + shape, lambda i, te, nt: (te[i], 0, 0))
    grid_spec = pltpu.PrefetchScalarGridSpec(
        num_scalar_prefetch=2,
        grid=(n_tiles,),
        in_specs=[pl.BlockSpec((tm, D), lambda i, te, nt: (i, 0)),
                  pl.BlockSpec((tm, 1), lambda i, te, nt: (i, 0)),
                  wspec((D, D_FF)), wspec((D, D_FF)), wspec((D_FF, D))],
        out_specs=pl.BlockSpec((tm, D), lambda i, te, nt: (i, 0)),
        scratch_shapes=[pltpu.VMEM((tm, D), F32)],
    )
    return pl.pallas_call(
        _moe_body,
        grid_spec=grid_spec,
        out_shape=jax.ShapeDtypeStruct((S, D), F32),
        compiler_params=_params("arbitrary"),
        name="moe_grouped_swiglu",
    )(tile_expert, n_tiles_used, h_sorted, w_slot, w1, w3, w2)


def _combine_body(x_ref, g_ref, a_ref, b_ref, o_ref):
    o_ref[0] = x_ref[0] + g_ref[0] * (a_ref[0] + b_ref[0])


def moe_combine_residual(x, gate, ya, yb):
    B, T, D = x.shape
    tm = min(T, ROW_TILE)
    tile = pl.BlockSpec((1, tm, D), lambda b, i: (b, i, 0))
    return pl.pallas_call(
        _combine_body,
        grid=(B, T // tm),
        in_specs=[tile, _mod_spec(gate, tm), tile, tile],
        out_specs=tile,
        out_shape=jax.ShapeDtypeStruct((B, T, D), F32),
        compiler_params=_params("parallel", "parallel"),
        name="moe_combine_residual",
    )(x, gate, ya, yb)


def _final_norm_body(x_ref, g_ref, o_ref):
    x = x_ref[0]
    ms = jnp.mean(x * x, axis=-1, keepdims=True)
    o_ref[0] = x * lax.rsqrt(ms + EPS) * g_ref[...]


def final_norm(x, g):
    B, T, D = x.shape
    tm = min(T, ROW_TILE)
    return pl.pallas_call(
        _final_norm_body,
        grid=(B, T // tm),
        in_specs=[pl.BlockSpec((1, tm, D), lambda b, i: (b, i, 0)),
                  pl.BlockSpec((1, D), lambda b, i: (0, 0))],
        out_specs=pl.BlockSpec((1, tm, D), lambda b, i: (b, i, 0)),
        out_shape=jax.ShapeDtypeStruct((B, T, D), F32),
        compiler_params=_params("parallel", "parallel"),
        name="final_norm",
    )(x, g.reshape(1, D))


def _split3(x):
    hi = x.astype(BF16)
    r1 = x - hi.astype(F32)
    mid = r1.astype(BF16)
    lo = (r1 - mid.astype(F32)).astype(BF16)
    return hi, mid, lo


def _dot01(m01, x):
    hi, mid, lo = _split3(x)
    d = lambda p: jnp.dot(m01, p, preferred_element_type=F32)
    return d(hi) + d(mid) + d(lo)


def _hgrn_body(hq_ref, hf_ref, hi_ref, hg_ref, lb_ref, ng_ref, s0_ref, o_ref, sn_ref, st_ref, *, sub, n_sub):
    i = pl.program_id(1)

    @pl.when(i == 0)
    def _():
        for h in range(HGRN_HEADS):
            st_ref[h] = s0_ref[0, h].T

    lb = lb_ref[...]
    ng = ng_ref[...]
    row = lax.broadcasted_iota(jnp.int32, (sub, sub), 0)
    col = lax.broadcasted_iota(jnp.int32, (sub, sub), 1)
    tril = (row >= col).astype(BF16)
    trow = lax.broadcasted_iota(jnp.int32, (sub, HGRN_DK), 0)

    def chunk(c, carry):
        rows = pl.ds(pl.multiple_of(c * sub, sub), sub)
        hq = hq_ref[0, rows, :]
        hf = hf_ref[0, rows, :]
        hv = hi_ref[0, rows, :]
        hg = hg_ref[0, rows, :]
        f = lb + (1.0 - lb) * jax.nn.sigmoid(hf)
        logf = jnp.log(f)
        kk = 1.0 - f
        qq = hq * jax.nn.sigmoid(hq)
        b = _dot01(tril, logf)
        bl = b[sub - 1:sub, :]
        qe = qq * jnp.exp(b)
        ke = kk * jnp.exp(bl - b)
        ebl = jnp.exp(bl)
        outs = []
        for h in range(HGRN_HEADS):
            cs = slice(h * HGRN_DK, (h + 1) * HGRN_DK)
            st = st_ref[h]
            o = lax.dot_general(qe[:, cs].astype(BF16), st.astype(BF16), (((1,), (1,)), ((), ())),
                                preferred_element_type=F32)
            bh, qh, kh, vh = b[:, cs], qq[:, cs], kk[:, cs], hv[:, cs]
            for s in range(sub):
                e = jnp.exp(jnp.minimum(bh - bh[s:s + 1, :], 0.0))
                w = jnp.where(trow >= s, e * qh * kh[s:s + 1, :], 0.0)
                o = o + jnp.sum(w, axis=-1, keepdims=True) * vh[s:s + 1, :]
            upd = lax.dot_general(vh.astype(BF16), ke[:, cs].astype(BF16), (((0,), (0,)), ((), ())),
                                  preferred_element_type=F32)
            st_ref[h] = st * ebl[:, cs] + upd
            ms = jnp.mean(o * o, axis=-1, keepdims=True)
            outs.append(o * lax.rsqrt(ms + EPS))
        o_ref[0, rows, :] = jnp.concatenate(outs, axis=-1) * ng * (hg * jax.nn.sigmoid(hg))
        return carry

    lax.fori_loop(0, n_sub, chunk, 0)

    @pl.when(i == pl.num_programs(1) - 1)
    def _():
        for h in range(HGRN_HEADS):
            sn_ref[0, h] = st_ref[h].T


def hgrn_mixer(hx, lb, norm_g, s0):
    B, T, _ = hx.shape
    tc = min(T, HGRN_TILE)
    sub = math.gcd(T, HGRN_SUB)
    spec = lambda j: pl.BlockSpec((1, tc, HGRN_KW), lambda b, i, j=j: (b, i, j))
    vec = pl.BlockSpec((1, HGRN_KW), lambda b, i: (0, 0))
    state = pl.BlockSpec((1, HGRN_HEADS, HGRN_DK, HGRN_DV), lambda b, i: (b, 0, 0, 0))
    return pl.pallas_call(
        functools.partial(_hgrn_body, sub=sub, n_sub=tc // sub),
        grid=(B, T // tc),
        in_specs=[spec(0), spec(1), spec(2), spec(3), vec, vec, state],
        out_specs=[pl.BlockSpec((1, tc, HGRN_VW), lambda b, i: (b, i, 0)), state],
        out_shape=[jax.ShapeDtypeStruct((B, T, HGRN_VW), F32),
                   jax.ShapeDtypeStruct((B, HGRN_HEADS, HGRN_DK, HGRN_DV), F32)],
        scratch_shapes=[pltpu.VMEM((HGRN_HEADS, HGRN_DV, HGRN_DK), F32)],
        compiler_params=_params("parallel", "arbitrary"),
        name="hgrn_mixer",
    )(hx, hx, hx, hx, lb.reshape(1, HGRN_KW), jnp.tile(norm_g, HGRN_HEADS).reshape(1, HGRN_VW), s0)


def _nt(a, b):
    return lax.dot_general(a, b, (((1,), (1,)), ((), ())), preferred_element_type=F32)


def _flash_step(kblk, vT, q_heads, mask, m, l, acc_ref, batched):
    tq = q_heads[0].shape[0]
    if batched:
        acc = acc_ref[...]
        scores = [_nt(kblk, qh) for qh in q_heads]
    m_out, l_out, acc_out = [], [], []
    for h, qh in enumerate(q_heads):
        cs = slice(h * tq, (h + 1) * tq)
        s = jnp.where(mask, scores[h] if batched else _nt(kblk, qh), NEG)
        m_new = jnp.maximum(m[:, cs], jnp.max(s, axis=0, keepdims=True))
        alpha = jnp.exp(m[:, cs] - m_new)
        e = jnp.where(mask, jnp.exp(s - m_new), 0.0)
        l_out.append(alpha * l[:, cs] + jnp.sum(e, axis=0, keepdims=True))
        pv = jnp.dot(vT, e.astype(BF16), preferred_element_type=F32)
        if batched:
            acc_out.append(alpha * acc[:, cs] + pv)
        else:
            acc_ref[:, cs] = alpha * acc_ref[:, cs] + pv
        m_out.append(m_new)
    if batched:
        acc_ref[...] = jnp.concatenate(acc_out, axis=1)
    return jnp.concatenate(m_out, axis=1), jnp.concatenate(l_out, axis=1)


def _head_slabs(qp_ref, first_head, n):
    return [qp_ref[0, :, (first_head + h) * LANE:(first_head + h + 1) * LANE] for h in range(n)]


def _stack_heads(qp_ref, first_head, n):
    return jnp.concatenate(_head_slabs(qp_ref, first_head, n), axis=0)


def _store_heads(o_ref, oT, first_head, n, slot):
    tq = oT.shape[1] // n
    lane = lax.broadcasted_iota(jnp.int32, (tq, LANE), 1)
    valid = (lane >= HEAD_DIM * slot) & (lane < HEAD_DIM * (slot + 1))
    for h in range(n):
        blk = oT[:, h * tq:(h + 1) * tq].T
        o_ref[0, :, (first_head + h) * LANE:(first_head + h + 1) * LANE] = jnp.where(valid, blk, 0.0).astype(o_ref.dtype)


def _nsa_body(qp_ref, rows_ref, win_ref, gl_ref, pek_ref, pev_ref, wck_ref, wcv_ref, o_ref,
              kcmp_ref, vcmpT_ref, ks_ref, vsT_ref, kw_ref, vwT_ref, stage_ref, shift_ref, sel_ref, acc_ref, *, T):
    i = pl.program_id(1)
    tq = ATT_TQ
    nq = NSA_HPG * tq
    n_half = T // CMP_STRIDE
    n_sel = T // SEL_BLK
    k_sel = min(TOP_N, n_sel)
    per_kb = KEY_BLK // SEL_BLK

    @pl.when(i == 0)
    def _prepare():
        nrow = lax.broadcasted_iota(jnp.int32, (n_half, LANE), 0)
        for slab, pe_ref, w_ref in ((0, pek_ref, wck_ref), (1, pev_ref, wcv_ref)):
            def stage_blk(kb, carry, slab=slab):
                rs = pl.ds(pl.multiple_of(kb * KEY_BLK, KEY_BLK), KEY_BLK)
                stage_ref[rs, :] = rows_ref[0, rs, slab * LANE:(slab + 1) * LANE]
                return carry

            lax.fori_loop(0, T // KEY_BLK, stage_blk, 0)
            first = jnp.zeros((n_half, LANE), F32)
            second = jnp.zeros((n_half, LANE), F32)
            for r in range(CMP_STRIDE):
                y = stage_ref[pl.ds(r, n_half, stride=CMP_STRIDE), :]
                first = first + jnp.dot((y + pe_ref[r:r + 1, :]).astype(BF16), w_ref[r],
                                        preferred_element_type=F32)
                second = second + jnp.dot((y + pe_ref[r + CMP_STRIDE:r + CMP_STRIDE + 1, :]).astype(BF16),
                                          w_ref[r + CMP_STRIDE], preferred_element_type=F32)
            shift_ref[0:n_half, :] = second
            shift_ref[n_half:n_half + 8, :] = jnp.zeros((8, LANE), F32)
            c = jnp.where(nrow < n_half - 1, first + shift_ref[1:n_half + 1, :], 0.0)
            if slab == 0:
                kcmp_ref[...] = c.astype(BF16)
            else:
                for j in range(n_half // LANE):
                    vcmpT_ref[:, j * LANE:(j + 1) * LANE] = c[j * LANE:(j + 1) * LANE, :].T.astype(BF16)

        def copy_blk(kb, carry):
            for half in range(KEY_BLK // LANE):
                rs = pl.ds(pl.multiple_of(kb * KEY_BLK + half * LANE, LANE), LANE)
                hs = slice(half * LANE, (half + 1) * LANE)
                ks_ref[kb, hs, :] = rows_ref[0, rs, 2 * LANE:3 * LANE].astype(BF16)
                vsT_ref[kb, :, hs] = rows_ref[0, rs, 3 * LANE:4 * LANE].T.astype(BF16)
                kw_ref[kb, hs, :] = win_ref[0, rs, 0:LANE].astype(BF16)
                vwT_ref[kb, :, hs] = win_ref[0, rs, LANE:2 * LANE].T.astype(BF16)
            return carry

        lax.fori_loop(0, T // KEY_BLK, copy_blk, 0)

    t0 = i * tq
    lane_t = t0 + (lax.broadcasted_iota(jnp.int32, (1, nq), 1) & (tq - 1))
    t_row = lane_t[:, 0:tq]
    gT = jax.nn.sigmoid(gl_ref[0].T)
    kio = lax.broadcasted_iota(jnp.int32, (KEY_BLK, tq), 0)
    nio = lax.broadcasted_iota(jnp.int32, (n_half, nq), 0)
    jcol = lax.broadcasted_iota(jnp.int32, (n_sel, tq), 0)
    pj = lax.broadcasted_iota(jnp.int32, (n_sel, n_half), 0)
    pi = lax.broadcasted_iota(jnp.int32, (n_sel, n_half), 1)
    pool = ((((pi + 1) >> 2) == pj) & (pi < n_half - 1)).astype(BF16)
    n_kb = (t0 + tq + KEY_BLK - 1) // KEY_BLK
    n_full = t0 // KEY_BLK
    m0 = jnp.full((1, nq), NEG, F32)
    l0 = jnp.zeros((1, nq), F32)

    for g in range(NSA_KV):
        q_heads = _head_slabs(qp_ref, NSA_HPG * g, NSA_HPG)
        qs = jnp.concatenate(q_heads, axis=0)
        cmask = (CMP_STRIDE * nio + CMP_BLK <= lane_t + 1) & (nio < n_half - 1)
        s = jnp.where(cmask, _nt(kcmp_ref[...], qs), NEG)
        e = jnp.where(cmask, jnp.exp(s - jnp.max(s, axis=0, keepdims=True)), 0.0)
        p = e * (1.0 / jnp.maximum(jnp.sum(e, axis=0, keepdims=True), 1e-30))
        ocT = jnp.dot(vcmpT_ref[...], p.astype(BF16), preferred_element_type=F32)
        imp = p[:, 0:tq]
        for h in range(1, NSA_HPG):
            imp = imp + p[:, h * tq:(h + 1) * tq]
        imp_sel = _dot01(pool, imp)
        cur = t_row >> 6
        forced = (jcol == 0) | (jcol == cur) | (jcol == cur - 1)
        visible = jcol <= cur
        score = jnp.where(visible, imp_sel + jnp.where(forced, FORCE_SCORE, 0.0), NEG)
        rank = jnp.zeros((n_sel, tq), jnp.int32)
        for j2 in range(n_sel):
            r = score[j2:j2 + 1, :]
            beats = (r > score) | ((r == score) & (jcol > j2))
            rank = rank + beats.astype(jnp.int32)
        sel_ref[...] = jnp.where((rank < k_sel) & visible, 1.0, 0.0)

        def sel_step(kb, carry, causal):
            rowsel = jnp.concatenate(
                [jnp.broadcast_to(sel_ref[pl.ds(kb * per_kb + k, 1), :], (SEL_BLK, tq)) for k in range(per_kb)],
                axis=0)
            mask = rowsel > 0.5
            if causal:
                mask = mask & (kb * KEY_BLK + kio <= t_row)
            return _flash_step(ks_ref[kb], vsT_ref[kb], q_heads, mask, carry[0], carry[1], acc_ref, True)

        acc_ref[...] = jnp.zeros_like(acc_ref)
        ml = lax.fori_loop(0, n_full, functools.partial(sel_step, causal=False), (m0, l0))
        _, l = lax.fori_loop(n_full, n_kb, functools.partial(sel_step, causal=True), ml)
        osT = acc_ref[...] * (1.0 / jnp.maximum(l, 1e-30))

        def win_step(kb, carry):
            kpos = kb * KEY_BLK + kio
            mask = (kpos <= t_row) & (kpos > t_row - WINDOW)
            return _flash_step(kw_ref[kb], vwT_ref[kb], q_heads, mask, carry[0], carry[1], acc_ref, True)

        acc_ref[...] = jnp.zeros_like(acc_ref)
        w_lo = jnp.maximum(t0 - (WINDOW - 1), 0) // KEY_BLK
        _, l = lax.fori_loop(w_lo, n_kb, win_step, (m0, l0))
        owT = acc_ref[...] * (1.0 / jnp.maximum(l, 1e-30))

        def gate(c):
            return jnp.concatenate([gT[3 * (NSA_HPG * g + h) + c:3 * (NSA_HPG * g + h) + c + 1, :]
                                    for h in range(NSA_HPG)], axis=1)

        oT = gate(0) * ocT + gate(1) * osT + gate(2) * owT
        _store_heads(o_ref, oT, NSA_HPG * g, NSA_HPG, g)


def _block_diag2(w):
    w3 = w.reshape(CMP_BLK, HEAD_DIM, HEAD_DIM)
    z = jnp.zeros_like(w3)
    return jnp.concatenate([jnp.concatenate([w3, z], axis=2), jnp.concatenate([z, w3], axis=2)], axis=1).astype(BF16)


def nsa_prompt(qp, rows, win, gl, pe_k, pe_v, w_ck, w_cv):
    B, T, _ = rows.shape
    assert T % (CMP_STRIDE * LANE) == 0 and T % KEY_BLK == 0
    tq = ATT_TQ
    nq = NSA_HPG * tq
    n_half = T // CMP_STRIDE
    n_kb = T // KEY_BLK
    const = lambda shape: pl.BlockSpec(shape, lambda b, i: (0,) * len(shape))
    return pl.pallas_call(
        functools.partial(_nsa_body, T=T),
        grid=(B, T // tq),
        in_specs=[pl.BlockSpec((1, tq, NSA_HEADS * LANE), lambda b, i: (b, i, 0)),
                  pl.BlockSpec((1, T, 4 * LANE), lambda b, i: (b, 0, 0)),
                  pl.BlockSpec((1, T, 2 * LANE), lambda b, i: (b, 0, 0)),
                  pl.BlockSpec((1, tq, LANE), lambda b, i: (b, i, 0)),
                  const((CMP_BLK, LANE)), const((CMP_BLK, LANE)),
                  const((CMP_BLK, LANE, LANE)), const((CMP_BLK, LANE, LANE))],
        out_specs=pl.BlockSpec((1, tq, NSA_HEADS * LANE), lambda b, i: (b, i, 0)),
        out_shape=jax.ShapeDtypeStruct((B, T, NSA_HEADS * LANE), BF16),
        scratch_shapes=[pltpu.VMEM((n_half, LANE), BF16), pltpu.VMEM((LANE, n_half), BF16),
                        pltpu.VMEM((n_kb, KEY_BLK, LANE), BF16), pltpu.VMEM((n_kb, LANE, KEY_BLK), BF16),
                        pltpu.VMEM((n_kb, KEY_BLK, LANE), BF16), pltpu.VMEM((n_kb, LANE, KEY_BLK), BF16),
                        pltpu.VMEM((T, LANE), F32),
                        pltpu.VMEM((n_half + 8, LANE), F32), pltpu.VMEM((T // SEL_BLK, tq), F32),
                        pltpu.VMEM((LANE, nq), F32)],
        compiler_params=_params("parallel", "arbitrary"),
        name="nsa_prompt",
    )(qp, rows, win, gl, jnp.tile(pe_k, (1, 2)), jnp.tile(pe_v, (1, 2)), _block_diag2(w_ck), _block_diag2(w_cv))


def _dsa_body(qp_ref, kv_ref, qi_ref, ki_ref, wi_ref, o_ref,
              k_ref, vT_ref, kilo_ref, kihi_ref, key_ref, acc_ref, *, T, n_keep):
    i = pl.program_id(1)
    tq = ATT_TQ
    nq = DSA_HPG * tq

    @pl.when(i == 0)
    def _prepare():
        def copy_blk(kb, carry):
            for half in range(KEY_BLK // LANE):
                rs = pl.ds(pl.multiple_of(kb * KEY_BLK + half * LANE, LANE), LANE)
                hs = slice(half * LANE, (half + 1) * LANE)
                for slab in range(2):
                    k_ref[slab, kb, hs, :] = kv_ref[0, rs, slab * LANE:(slab + 1) * LANE].astype(BF16)
                    vT_ref[slab, kb, :, hs] = kv_ref[0, rs, (2 + slab) * LANE:(3 + slab) * LANE].T.astype(BF16)
                kix = ki_ref[0, rs, :]
                kilo_ref[kb, hs, :] = kix.astype(BF16)
                kihi_ref[kb, hs, :] = pltpu.roll(kix, IDX_DIM, 1).astype(BF16)
            return carry

        lax.fori_loop(0, T // KEY_BLK, copy_blk, 0)

    t0 = i * tq
    n_kb = (t0 + tq + KEY_BLK - 1) // KEY_BLK
    t_row = t0 + lax.broadcasted_iota(jnp.int32, (1, tq), 1)
    kio_q = lax.broadcasted_iota(jnp.int32, (KEY_BLK, tq), 0)

    wT = wi_ref[0].T * (IDX_HEADS ** -0.5)
    qi = _stack_heads(qi_ref, 0, IDX_HEADS // 2)

    def idx_step(kb, carry):
        s_lo = _nt(kilo_ref[kb], qi)
        s_hi = _nt(kihi_ref[kb], qi)
        sc = jnp.zeros((KEY_BLK, tq), F32)
        for p in range(IDX_HEADS // 2):
            cs = slice(p * tq, (p + 1) * tq)
            sc = sc + jnp.maximum(s_lo[:, cs], 0.0) * wT[2 * p:2 * p + 1, :]
            sc = sc + jnp.maximum(s_hi[:, cs], 0.0) * wT[2 * p + 1:2 * p + 2, :]
        sc = jnp.where(kb * KEY_BLK + kio_q <= t_row, sc, NEG)
        bits = lax.bitcast_convert_type(sc, jnp.int32)
        key_ref[kb] = jnp.where(bits < 0, bits ^ 0x7FFFFFFF, bits)
        return carry

    lax.fori_loop(0, n_kb, idx_step, 0)

    def bit_step(it, theta):
        cand = theta + lax.shift_left(jnp.int32(1), 31 - it)

        def cnt_step(kb, c):
            return c + jnp.sum((key_ref[kb] >= cand).astype(jnp.int32), axis=0, keepdims=True)

        cnt = lax.fori_loop(0, n_kb, cnt_step, jnp.zeros((1, tq), jnp.int32))
        return jnp.where(cnt >= n_keep, cand, theta)

    theta = lax.fori_loop(0, 32, bit_step, jnp.full((1, tq), INT_MIN, jnp.int32))

    q_heads = [_head_slabs(qp_ref, DSA_HPG * g, DSA_HPG) for g in range(DSA_KV)]
    m0 = jnp.full((1, nq), NEG, F32)
    l0 = jnp.zeros((1, nq), F32)
    acc_ref[...] = jnp.zeros_like(acc_ref)

    def att_step(kb, carry, causal):
        mask = key_ref[kb] >= theta
        if causal:
            mask = mask & (kb * KEY_BLK + kio_q <= t_row)
        out = []
        for g in range(DSA_KV):
            m, l = _flash_step(k_ref[g // 2, kb], vT_ref[g // 2, kb], q_heads[g], mask,
                               carry[2 * g], carry[2 * g + 1], acc_ref.at[g], False)
            out += [m, l]
        return tuple(out)

    mid = lax.fori_loop(0, t0 // KEY_BLK, functools.partial(att_step, causal=False), (m0, l0) * DSA_KV)
    fin = lax.fori_loop(t0 // KEY_BLK, n_kb, functools.partial(att_step, causal=True), mid)
    for g in range(DSA_KV):
        oT = acc_ref[g] * (1.0 / jnp.maximum(fin[2 * g + 1], 1e-30))
        _store_heads(o_ref, oT, DSA_HPG * g, DSA_HPG, g % 2)


def dsa_prompt(qp, kv, qi, ki, wi):
    B, T, _ = kv.shape
    assert T % KEY_BLK == 0
    tq = ATT_TQ
    nq = DSA_HPG * tq
    n_kb = T // KEY_BLK
    n_keep = min(IDX_TOPK, T // 4)
    tile = lambda w: pl.BlockSpec((1, tq, w), lambda b, i: (b, i, 0))
    whole = lambda w: pl.BlockSpec((1, T, w), lambda b, i: (b, 0, 0))
    return pl.pallas_call(
        functools.partial(_dsa_body, T=T, n_keep=n_keep),
        grid=(B, T // tq),
        in_specs=[tile(DSA_HEADS * LANE), whole(4 * LANE), tile(IDX_QW), whole(LANE), tile(LANE)],
        out_specs=tile(DSA_HEADS * LANE),
        out_shape=jax.ShapeDtypeStruct((B, T, DSA_HEADS * LANE), BF16),
        scratch_shapes=[pltpu.VMEM((2, n_kb, KEY_BLK, LANE), BF16), pltpu.VMEM((2, n_kb, LANE, KEY_BLK), BF16),
                        pltpu.VMEM((n_kb, KEY_BLK, LANE), BF16), pltpu.VMEM((n_kb, KEY_BLK, LANE), BF16),
                        pltpu.VMEM((n_kb, KEY_BLK, tq), jnp.int32), pltpu.VMEM((DSA_KV, LANE, nq), F32)],
        compiler_params=_params("parallel", "arbitrary"),
        name="dsa_prompt",
    )(qp, kv, qi, ki, wi)


def _pad_head_cols(w, slots):
    D = w.shape[0]
    H = len(slots)
    w3 = w.reshape(D, H, HEAD_DIM)
    z = jnp.zeros_like(w3)
    s = jnp.asarray(slots)[None, :, None]
    return jnp.concatenate([jnp.where(s == 0, w3, z), jnp.where(s == 1, w3, z)], axis=2).reshape(D, H * LANE)


def _pad_cols(w, n):
    return jnp.pad(w, ((0, 0), (0, n - w.shape[1])))


def _dot01_r(x, m01):
    hi, mid, lo = _split3(x)
    d = lambda p: jnp.dot(p, m01, preferred_element_type=F32)
    return d(hi) + d(mid) + d(lo)


def _softmax_rows(s, mask):
    s = jnp.where(mask, s, NEG)
    e = jnp.where(mask, jnp.exp(s - jnp.max(s, axis=1, keepdims=True)), 0.0)
    return e * (1.0 / jnp.maximum(jnp.sum(e, axis=1, keepdims=True), 1e-30))


def _flash_rows(s, mask, m_ref, l_ref):
    s = jnp.where(mask, s, NEG)
    m = m_ref[...]
    m_new = jnp.maximum(m, jnp.max(s, axis=1, keepdims=True))
    alpha = jnp.exp(m - m_new)
    e = jnp.where(mask, jnp.exp(s - m_new), 0.0)
    l_ref[...] = alpha * l_ref[...] + jnp.sum(e, axis=1, keepdims=True)
    m_ref[...] = m_new
    return alpha, e


def _sort_key(x):
    bits = lax.bitcast_convert_type(x, jnp.int32)
    return jnp.where(bits < 0, bits ^ 0x7FFFFFFF, bits)


def _page_spec(k, chans, blk):
    return pl.BlockSpec((1, chans, PAGE_SIZE), lambda b, s, pt: (pt[b, s * PG_STEP + k], blk, 0))


def _channel_major(cache):
    n, rows = cache.shape[:2]
    return jnp.moveaxis(cache.reshape(n, rows, -1), 1, 2)


def _per_request(shape):
    return pl.BlockSpec((1,) + shape, lambda b, s, pt: (b,) + (0,) * len(shape))


def _nsa_cmp_body(pt_ref, *refs):
    pages = refs[:PG_STEP]
    pek_ref, pev_ref, wck_ref, wcv_ref, o_ref, stage_ref = refs[PG_STEP:]
    n_half = PG_STEP * PAGE_SIZE // CMP_STRIDE
    for slab, pe_ref, w_ref in ((0, pek_ref, wck_ref), (1, pev_ref, wcv_ref)):
        for pg in range(PG_STEP):
            stage_ref[pg * PAGE_SIZE:(pg + 1) * PAGE_SIZE, :] = pages[pg][0, slab * LANE:(slab + 1) * LANE, :].T
        first = jnp.zeros((n_half, LANE), F32)
        second = jnp.zeros((n_half, LANE), F32)
        for r in range(CMP_STRIDE):
            y = stage_ref[pl.ds(r, n_half, stride=CMP_STRIDE), :]
            first = first + jnp.dot((y + pe_ref[r:r + 1, :]).astype(BF16), w_ref[r], preferred_element_type=F32)
            second = second + jnp.dot((y + pe_ref[r + CMP_STRIDE:r + CMP_STRIDE + 1, :]).astype(BF16),
                                      w_ref[r + CMP_STRIDE], preferred_element_type=F32)
        o_ref[0, :, (2 * slab) * LANE:(2 * slab + 1) * LANE] = first
        o_ref[0, :, (2 * slab + 1) * LANE:(2 * slab + 2) * LANE] = second


def nsa_sample_compress(cache, page_table, pe_k, pe_v, w_ck, w_cv):
    B, n_pages = page_table.shape
    assert n_pages % PG_STEP == 0
    n_half_step = PG_STEP * PAGE_SIZE // CMP_STRIDE
    const = lambda shape: pl.BlockSpec(shape, lambda b, s, pt: (0,) * len(shape))
    grid_spec = pltpu.PrefetchScalarGridSpec(
        num_scalar_prefetch=1,
        grid=(B, n_pages // PG_STEP),
        in_specs=[_page_spec(k, 2 * LANE, 0) for k in range(PG_STEP)]
        + [const((CMP_BLK, LANE)), const((CMP_BLK, LANE)), const((CMP_BLK, LANE, LANE)), const((CMP_BLK, LANE, LANE))],
        out_specs=pl.BlockSpec((1, n_half_step, 4 * LANE), lambda b, s, pt: (b, s, 0)),
        scratch_shapes=[pltpu.VMEM((PG_STEP * PAGE_SIZE, LANE), F32)],
    )
    return pl.pallas_call(
        _nsa_cmp_body,
        grid_spec=grid_spec,
        out_shape=jax.ShapeDtypeStruct((B, n_pages * PAGE_SIZE // CMP_STRIDE, 4 * LANE), F32),
        compiler_params=_params("parallel", "arbitrary"),
        name="nsa_sample_compress",
    )(page_table, *([cache] * PG_STEP), jnp.tile(pe_k, (1, 2)), jnp.tile(pe_v, (1, 2)),
      _block_diag2(w_ck), _block_diag2(w_cv))


def _nsa_smp_body(pt_ref, qs_ref, gl_ref, fs_ref, *refs, n_tok):
    pages = refs[:PG_STEP]
    nrow_ref, pwin_ref, nwin_ref, o_ref, shift_ref, selc_ref, oc_ref, m_ref, l_ref, acc_ref = refs[PG_STEP:]
    s_id = pl.program_id(1)
    C = LANE
    n_half = fs_ref.shape[1]
    n_cmp = n_half - 1
    n_selp = n_half * CMP_STRIDE // SEL_BLK
    qs = qs_ref[0]
    crow = lax.broadcasted_iota(jnp.int32, (C, LANE), 0)
    lane = lax.broadcasted_iota(jnp.int32, (C, LANE), 1)
    t_of_c = crow % n_tok

    @pl.when(s_id == 0)
    def _first():
        cmp = []
        for slab in range(2):
            shift_ref[0:n_half, :] = fs_ref[0, :, (2 * slab + 1) * LANE:(2 * slab + 2) * LANE]
            shift_ref[n_half:n_half + 8, :] = jnp.zeros((8, LANE), F32)
            cmp.append((fs_ref[0, :, (2 * slab) * LANE:(2 * slab + 1) * LANE]
                        + shift_ref[1:n_half + 1, :]).astype(BF16))
        nlane = lax.broadcasted_iota(jnp.int32, (C, n_half), 1)
        p = _softmax_rows(_nt(qs, cmp[0]), nlane < n_cmp)
        oc_ref[...] = jnp.dot(p.astype(BF16), cmp[1], preferred_element_type=F32)
        imp_rows = []
        for g in range(NSA_KV):
            acc = p[(NSA_HPG * g) * n_tok:(NSA_HPG * g + 1) * n_tok, :]
            for h in range(1, NSA_HPG):
                acc = acc + p[(NSA_HPG * g + h) * n_tok:(NSA_HPG * g + h + 1) * n_tok, :]
            imp_rows.append(acc)
        imp = jnp.concatenate(imp_rows + [jnp.zeros((C - NSA_KV * n_tok, n_half), F32)], axis=0)
        pi = lax.broadcasted_iota(jnp.int32, (n_half, n_selp), 0)
        pj = lax.broadcasted_iota(jnp.int32, (n_half, n_selp), 1)
        pool = ((((pi + 1) >> 2) == pj) & (pi < n_cmp)).astype(BF16)
        scoreT = _dot01_r(imp, pool).T
        jcol = lax.broadcasted_iota(jnp.int32, (n_selp, C), 0)
        forced = (jcol == 0) | (jcol == n_selp - 1)
        scoreT = scoreT + jnp.where(forced, FORCE_SCORE, 0.0)
        rank = jnp.zeros((n_selp, C), jnp.int32)
        for j2 in range(n_selp):
            r = scoreT[j2:j2 + 1, :]
            beats = (r > scoreT) | ((r == scoreT) & (jcol > j2))
            rank = rank + beats.astype(jnp.int32)
        sel = jnp.where(rank < TOP_N - 1, 1.0, 0.0).T
        selc_ref[...] = jnp.concatenate(
            [sel[(hh // NSA_HPG) * n_tok:(hh // NSA_HPG + 1) * n_tok, :] for hh in range(NSA_HEADS)]
            + [jnp.zeros((C - NSA_HEADS * n_tok, n_selp), F32)], axis=0)
        m_ref[...] = jnp.full(m_ref.shape, NEG, F32)
        l_ref[...] = jnp.zeros(l_ref.shape, F32)
        acc_ref[...] = jnp.zeros(acc_ref.shape, F32)

    selc = selc_ref[...].astype(BF16)
    ej = lax.broadcasted_iota(jnp.int32, (n_selp, PAGE_SIZE), 0)
    ek = lax.broadcasted_iota(jnp.int32, (n_selp, PAGE_SIZE), 1)
    for pg in range(PG_STEP):
        page = s_id * PG_STEP + pg
        kT = pages[pg][0, 0:LANE, :].astype(BF16)
        vT = pages[pg][0, LANE:2 * LANE, :].astype(BF16)
        expand = (ej == page * (PAGE_SIZE // SEL_BLK) + ek // SEL_BLK).astype(BF16)
        mask = jnp.dot(selc, expand, preferred_element_type=F32) > 0.5
        alpha, e = _flash_rows(jnp.dot(qs, kT, preferred_element_type=F32), mask, m_ref, l_ref)
        acc_ref[...] = alpha * acc_ref[...] + _nt(e.astype(BF16), vT)

    @pl.when(s_id == pl.num_programs(1) - 1)
    def _last():
        pad = jnp.zeros((LANE - n_tok, LANE), F32)
        new_ok = (lane < n_tok) & (lane <= t_of_c)
        knew = jnp.concatenate([nrow_ref[0, :, 2 * LANE:3 * LANE], pad], axis=0).astype(BF16)
        vnew = jnp.concatenate([nrow_ref[0, :, 3 * LANE:4 * LANE], pad], axis=0).astype(BF16)
        alpha, e = _flash_rows(_nt(qs, knew), new_ok, m_ref, l_ref)
        acc = alpha * acc_ref[...] + jnp.dot(e.astype(BF16), vnew, preferred_element_type=F32)
        o_s = acc * (1.0 / jnp.maximum(l_ref[...], 1e-30))
        n_win = pwin_ref.shape[2]
        kwn = jnp.concatenate([nwin_ref[0, :, 0:LANE], pad], axis=0).astype(BF16)
        vwn = jnp.concatenate([nwin_ref[0, :, LANE:2 * LANE], pad], axis=0).astype(BF16)
        wl = lax.broadcasted_iota(jnp.int32, (C, n_win + LANE), 1)
        tw = lax.broadcasted_iota(jnp.int32, (C, n_win + LANE), 0) % n_tok
        wmask = (((wl < n_win) & ((n_win - wl) + tw < WINDOW))
                 | ((wl >= n_win) & (wl - n_win < n_tok) & (wl - n_win <= tw)))
        s_w = jnp.concatenate([jnp.dot(qs, pwin_ref[0, 0:LANE, :].astype(BF16), preferred_element_type=F32),
                               _nt(qs, kwn)], axis=1)
        pw = _softmax_rows(s_w, wmask).astype(BF16)
        o_w = (_nt(pw[:, 0:n_win], pwin_ref[0, LANE:2 * LANE, :].astype(BF16))
               + jnp.dot(pw[:, n_win:], vwn, preferred_element_type=F32))
        g = jax.nn.sigmoid(gl_ref[0])
        o = g[:, 0:1] * oc_ref[...] + g[:, 1:2] * o_s + g[:, 2:3] * o_w
        valid = (crow < NSA_HEADS * n_tok) & (lane // HEAD_DIM == crow // (n_tok * NSA_HPG))
        o_ref[0] = jnp.where(valid, o, 0.0)


def nsa_sample_attention(qs, gcol, fs, cache, page_table, new_rows, past_win, new_win):
    B, n_pages = page_table.shape
    n_tok = new_rows.shape[1]
    n_half = fs.shape[1]
    n_selp = n_half * CMP_STRIDE // SEL_BLK
    n_win = past_win.shape[2]
    assert n_selp == LANE and NSA_HEADS * n_tok <= LANE and n_tok <= min(SEL_BLK, 8) and n_pages % PG_STEP == 0
    assert n_win % LANE == 0 and n_win <= WINDOW
    grid_spec = pltpu.PrefetchScalarGridSpec(
        num_scalar_prefetch=1,
        grid=(B, n_pages // PG_STEP),
        in_specs=[_per_request((LANE, LANE)), _per_request((LANE, LANE)), _per_request((n_half, 4 * LANE))]
        + [_page_spec(k, 2 * LANE, 1) for k in range(PG_STEP)]
        + [_per_request((n_tok, 4 * LANE)), _per_request((2 * LANE, n_win)), _per_request((n_tok, 2 * LANE))],
        out_specs=_per_request((LANE, LANE)),
        scratch_shapes=[pltpu.VMEM((n_half + 8, LANE), F32), pltpu.VMEM((LANE, n_selp), F32),
                        pltpu.VMEM((LANE, LANE), F32), pltpu.VMEM((LANE, 1), F32), pltpu.VMEM((LANE, 1), F32),
                        pltpu.VMEM((LANE, LANE), F32)],
    )
    return pl.pallas_call(
        functools.partial(_nsa_smp_body, n_tok=n_tok),
        grid_spec=grid_spec,
        out_shape=jax.ShapeDtypeStruct((B, LANE, LANE), F32),
        compiler_params=_params("parallel", "arbitrary"),
        name="nsa_sample_attention",
    )(page_table, qs, gcol, fs, *([cache] * PG_STEP), new_rows, past_win, new_win)


def _dsa_idx_body(pt_ref, qe_ref, qo_ref, w_ref, *refs, n_tok, n_keep):
    pages = refs[:PG_STEP]
    knew_ref, sc_ref, th_ref = refs[PG_STEP:]
    s_id = pl.program_id(1)
    n_blk = sc_ref.shape[1]
    qe = qe_ref[0]
    qo = qo_ref[0]
    w = w_ref[0] * (IDX_HEADS ** -0.5)
    n_rows = (IDX_HEADS // 2) * n_tok

    def scores(s_e, s_o):
        s = jnp.maximum(s_e, 0.0) * w[:, 0:1] + jnp.maximum(s_o, 0.0) * w[:, 1:2]
        tot = s[0:n_tok, :]
        for p in range(1, IDX_HEADS // 2):
            tot = tot + s[p * n_tok:(p + 1) * n_tok, :]
        return tot

    for pg in range(PG_STEP):
        kT = pages[pg][0].astype(BF16)
        sc_ref[0, s_id * PG_STEP + pg] = scores(jnp.dot(qe, kT, preferred_element_type=F32),
                                                jnp.dot(qo, kT, preferred_element_type=F32))

    @pl.when(s_id == pl.num_programs(1) - 1)
    def _last():
        kn = jnp.concatenate([knew_ref[0], jnp.zeros((LANE - n_tok, IDX_DIM), F32)], axis=0).astype(BF16)
        a_i = lax.broadcasted_iota(jnp.int32, (n_tok, LANE), 1)
        t_i = lax.broadcasted_iota(jnp.int32, (n_tok, LANE), 0)
        sn = jnp.where(a_i <= t_i, scores(_nt(qe, kn), _nt(qo, kn)), NEG)
        sc_ref[0, n_blk - 1] = jnp.where(a_i < n_tok, sn, -jnp.inf)
        keys = _sort_key(sc_ref[0])

        def bit_step(it, theta):
            cand = theta + lax.shift_left(jnp.int32(1), 31 - it)
            cnt = jnp.sum(jnp.sum((keys >= cand).astype(jnp.int32), axis=0), axis=1, keepdims=True)
            return jnp.where(cnt >= n_keep, cand, theta)

        theta = lax.fori_loop(0, 32, bit_step, jnp.full((n_tok, 1), INT_MIN, jnp.int32))
        th_ref[0] = jnp.broadcast_to(theta, (n_tok, LANE))


def dsa_sample_index(qe, qo, wcol, cache_idx, page_table, ki_new):
    B, n_pages = page_table.shape
    n_tok = ki_new.shape[1]
    assert n_tok == 8 and n_pages % PG_STEP == 0
    n_keep = min(IDX_TOPK, (n_pages * PAGE_SIZE + n_tok) // 4)
    grid_spec = pltpu.PrefetchScalarGridSpec(
        num_scalar_prefetch=1,
        grid=(B, n_pages // PG_STEP),
        in_specs=[_per_request((LANE, IDX_DIM)), _per_request((LANE, IDX_DIM)), _per_request((LANE, LANE))]
        + [_page_spec(k, IDX_DIM, 0) for k in range(PG_STEP)] + [_per_request((n_tok, IDX_DIM))],
        out_specs=[_per_request((n_pages + 1, n_tok, LANE)), _per_request((n_tok, LANE))],
    )
    return pl.pallas_call(
        functools.partial(_dsa_idx_body, n_tok=n_tok, n_keep=n_keep),
        grid_spec=grid_spec,
        out_shape=[jax.ShapeDtypeStruct((B, n_pages + 1, n_tok, LANE), F32),
                   jax.ShapeDtypeStruct((B, n_tok, LANE), jnp.int32)],
        compiler_params=_params("parallel", "arbitrary"),
        name="dsa_sample_index",
    )(page_table, qe, qo, wcol, *([cache_idx] * PG_STEP), ki_new)


def _dsa_smp_body(pt_ref, qs_ref, sc_ref, th_ref, *refs, n_tok):
    pages = refs[:PG_STEP]
    kvn_ref, o_ref, m_ref, l_ref, acc_ref = refs[PG_STEP:]
    s_id = pl.program_id(1)
    C = LANE
    qs = qs_ref[0]
    crow = lax.broadcasted_iota(jnp.int32, (C, LANE), 0)
    lane = lax.broadcasted_iota(jnp.int32, (C, LANE), 1)
    low = crow < C // 2
    theta = th_ref[0]

    @pl.when(s_id == 0)
    def _():
        m_ref[...] = jnp.full(m_ref.shape, NEG, F32)
        l_ref[...] = jnp.zeros(l_ref.shape, F32)
        acc_ref[...] = jnp.zeros(acc_ref.shape, F32)

    def step(kvT, blk, extra):
        kb = kvT.astype(BF16)
        keep = jnp.where(_sort_key(sc_ref[0, blk]) >= theta, 1.0, 0.0)
        mask = jnp.concatenate([keep] * (C // n_tok), axis=0) > 0.5
        if extra is not None:
            mask = mask & extra
        s = jnp.where(low, jnp.dot(qs, kb[0:LANE], preferred_element_type=F32),
                      jnp.dot(qs, kb[LANE:2 * LANE], preferred_element_type=F32))
        alpha, e = _flash_rows(s, mask, m_ref, l_ref)
        eb = e.astype(BF16)
        pv = jnp.where(low, _nt(eb, kb[2 * LANE:3 * LANE]), _nt(eb, kb[3 * LANE:4 * LANE]))
        acc_ref[...] = alpha * acc_ref[...] + pv

    for pg in range(PG_STEP):
        step(pages[pg][0], s_id * PG_STEP + pg, None)

    @pl.when(s_id == pl.num_programs(1) - 1)
    def _last():
        kvn = jnp.concatenate([kvn_ref[0], jnp.zeros((LANE - n_tok, 4 * LANE), F32)], axis=0)
        kvnT = jnp.concatenate([kvn[:, j * LANE:(j + 1) * LANE].T for j in range(4)], axis=0)
        step(kvnT, sc_ref.shape[1] - 1, (lane < n_tok) & (lane <= crow % n_tok))
        o = acc_ref[...] * (1.0 / jnp.maximum(l_ref[...], 1e-30))
        valid = lane // HEAD_DIM == (crow // (n_tok * DSA_HPG)) % 2
        o_ref[0] = jnp.where(valid, o, 0.0)


def dsa_sample_attention(qs, scores, theta, cache_kv, page_table, kv_new):
    B, n_pages = page_table.shape
    n_tok = kv_new.shape[1]
    assert DSA_HEADS * n_tok == LANE and n_pages % PG_STEP == 0
    grid_spec = pltpu.PrefetchScalarGridSpec(
        num_scalar_prefetch=1,
        grid=(B, n_pages // PG_STEP),
        in_specs=[_per_request((LANE, LANE)), _per_request((n_pages + 1, n_tok, LANE)), _per_request((n_tok, LANE))]
        + [_page_spec(k, 4 * LANE, 0) for k in range(PG_STEP)] + [_per_request((n_tok, 4 * LANE))],
        out_specs=_per_request((LANE, LANE)),
        scratch_shapes=[pltpu.VMEM((LANE, 1), F32), pltpu.VMEM((LANE, 1), F32), pltpu.VMEM((LANE, LANE), F32)],
    )
    return pl.pallas_call(
        functools.partial(_dsa_smp_body, n_tok=n_tok),
        grid_spec=grid_spec,
        out_shape=jax.ShapeDtypeStruct((B, LANE, LANE), F32),
        compiler_params=_params("parallel", "arbitrary"),
        name="dsa_sample_attention",
    )(page_table, qs, scores, theta, *([cache_kv] * PG_STEP), kv_new)


_NSA_SLOTS = tuple(h // NSA_HPG for h in range(NSA_HEADS))
_DSA_SLOTS = tuple((h // DSA_HPG) % 2 for h in range(DSA_HEADS))
def _even_segs(q_dtype, cm):
    return ((NSA_HEADS * LANE, (1,) * NSA_HEADS, q_dtype, False), (4 * LANE, (1, 0, 1, 0), F32, cm),
            (2 * LANE, (1, 0), F32, False), (LANE, (0,), F32, False),
            (4 * HGRN_KW, (0,) * (4 * HGRN_KW // LANE), F32, False))


def _odd_segs(q_dtype, cm):
    return ((DSA_HEADS * LANE, (1,) * DSA_HEADS, q_dtype, False), (4 * LANE, (1, 1, 0, 0), F32, cm),
            (IDX_QW, (1,) * 4, q_dtype, False), (LANE, (1,), F32, cm), (LANE, (0,), F32, False))


def _even_weights(w_in, w_out):
    c = np.cumsum([0, NSA_QW] + [NSA_KVW] * 6 + [NSA_GW] + [HGRN_KW] * 4)
    w = jnp.concatenate([
        _pad_head_cols(w_in[:, :NSA_QW] * HEAD_DIM ** -0.5, _NSA_SLOTS),
        w_in[:, c[1]:c[5]], w_in[:, c[5]:c[7]], _pad_cols(w_in[:, c[7]:c[8]], LANE), w_in[:, c[8]:c[12]]],
        axis=1).astype(BF16)
    w_outs = [_pad_head_cols(w_out[:NSA_QW].T, _NSA_SLOTS).T.astype(BF16), w_out[NSA_QW:].astype(BF16)]
    return w, w_outs


def _odd_weights(w_in, w_out):
    c = np.cumsum([0, DSA_QW, DSA_KVW, DSA_KVW, IDX_QW, IDX_DIM, IDX_HEADS])
    w = jnp.concatenate([
        _pad_head_cols(w_in[:, :DSA_QW] * HEAD_DIM ** -0.5, _DSA_SLOTS),
        w_in[:, c[1]:c[3]], w_in[:, c[3]:c[4]] * IDX_DIM ** -0.5,
        _pad_cols(w_in[:, c[4]:c[5]], LANE), _pad_cols(w_in[:, c[5]:c[6]], LANE)], axis=1).astype(BF16)
    return w, [_pad_head_cols(w_out.T, _DSA_SLOTS).T.astype(BF16)]


def even_mixer_prompt(x, scale, shift, cos_t, sin_t, w_in, w_out, pe_k, pe_v, w_ck, w_cv, lb, norm_g, win_len):
    B, T, _ = x.shape
    assert T >= win_len
    w, w_outs = _even_weights(w_in, w_out)
    qp, rows, win, gl, hx, rows_cm = proj_segments(x, scale, shift, cos_t, sin_t, w, _even_segs(BF16, True))
    o_a = nsa_prompt(qp, rows, win, gl, pe_k, pe_v, w_ck, w_cv)
    o_b, s_new = hgrn_mixer(hx, lb, norm_g, jnp.zeros((B, HGRN_HEADS, HGRN_DK, HGRN_DV), F32))
    new_rows = jnp.moveaxis(rows_cm.reshape(B, 4, NSA_KV, HEAD_DIM, T), 4, 1)
    win_state = win[:, T - win_len:].reshape(B, win_len, 2, NSA_KV, HEAD_DIM)
    return [o_a, o_b], w_outs, new_rows, win_state, s_new


def odd_mixer_prompt(x, scale, shift, cos_t, sin_t, w_in, w_out):
    B, T, _ = x.shape
    w, w_outs = _odd_weights(w_in, w_out)
    qp, kv, qi, ki, wi, kv_cm, ki_cm = proj_segments(x, scale, shift, cos_t, sin_t, w, _odd_segs(BF16, True))
    o = dsa_prompt(qp, kv, qi, ki, wi)
    new_kv = jnp.moveaxis(kv_cm.reshape(B, 2, DSA_KV, HEAD_DIM, T), 4, 1)
    return [o], w_outs, new_kv, jnp.moveaxis(ki_cm[:, :IDX_DIM], 2, 1)


def _stack_rows(a, B, n, heads, width):
    s = a.reshape(B, n, heads, width).transpose(0, 2, 1, 3).reshape(B, heads * n, width)
    return jnp.pad(s, ((0, 0), (0, LANE - heads * n), (0, 0)))


def _unstack_rows(o, B, n, heads):
    return o[:, :heads * n].reshape(B, heads, n, LANE).transpose(0, 2, 1, 3).reshape(1, B * n, heads * LANE)


def even_mixer_sample(x, scale, shift, past_len, B, w_in, w_out, pe_k, pe_v, w_ck, w_cv, lb, norm_g,
                      cache, cache_win, state, page_table):
    n = x.shape[1] // B
    win_len = cache_win.shape[1]
    cos_t, sin_t = [jnp.tile(a, (B, 1)) for a in rope_tables(past_len + jnp.arange(n))]
    w, w_outs = _even_weights(w_in, w_out)
    qp, rows, win, gl, hx = proj_segments(x, scale, shift, cos_t, sin_t, w, _even_segs(F32, False))
    rows, win, hx = [a.reshape(B, n, a.shape[-1]) for a in (rows, win, hx)]
    qs = _stack_rows(qp, B, n, NSA_HEADS, LANE).astype(BF16)
    gcol = jnp.pad(_stack_rows(gl[..., :NSA_GW], B, n, NSA_HEADS, 3), ((0, 0), (0, 0), (0, LANE - 3)))
    cache_cm = _channel_major(cache)
    fs = nsa_sample_compress(cache_cm, page_table, pe_k, pe_v, w_ck, w_cv)
    o = nsa_sample_attention(qs, gcol, fs, cache_cm, page_table, rows, _channel_major(cache_win), win)
    o_a = _unstack_rows(o, B, n, NSA_HEADS)
    o_b, s_new = hgrn_mixer(hx, lb, norm_g, state)
    new_rows = rows.reshape(B, n, 4, NSA_KV, HEAD_DIM)
    win_state = jnp.concatenate([cache_win, win.reshape(B, n, 2, NSA_KV, HEAD_DIM)], axis=1)[:, -win_len:]
    return [o_a, o_b.reshape(1, B * n, HGRN_VW)], w_outs, new_rows, win_state, s_new


def odd_mixer_sample(x, scale, shift, past_len, B, w_in, w_out, cache_kv, cache_idx, page_table):
    n = x.shape[1] // B
    cos_t, sin_t = [jnp.tile(a, (B, 1)) for a in rope_tables(past_len + jnp.arange(n))]
    w, w_outs = _odd_weights(w_in, w_out)
    qp, kv, qi, ki, wi = proj_segments(x, scale, shift, cos_t, sin_t, w, _odd_segs(F32, False))
    kv, ki = kv.reshape(B, n, 4 * LANE), ki.reshape(B, n, LANE)[..., :IDX_DIM]
    qs = _stack_rows(qp, B, n, DSA_HEADS, LANE).astype(BF16)
    qi4 = qi.reshape(1, B * n, IDX_HEADS // 2, 2, IDX_DIM)
    qe = _stack_rows(qi4[:, :, :, 0], B, n, IDX_HEADS // 2, IDX_DIM).astype(BF16)
    qo = _stack_rows(qi4[:, :, :, 1], B, n, IDX_HEADS // 2, IDX_DIM).astype(BF16)
    wcol = jnp.pad(_stack_rows(wi[..., :IDX_HEADS], B, n, IDX_HEADS // 2, 2), ((0, 0), (0, 0), (0, LANE - 2)))
    scores, theta = dsa_sample_index(qe, qo, wcol, _channel_major(cache_idx), page_table, ki)
    o = dsa_sample_attention(qs, scores, theta, _channel_major(cache_kv), page_table, kv)
    return [_unstack_rows(o, B, n, DSA_HEADS)], w_outs, kv.reshape(B, n, 2, DSA_KV, HEAD_DIM), ki


def moe_ffn_residual(xs, hs, logits_list, gates, w1, w3, w2):
    D = D_MODEL
    h_all = jnp.concatenate([h.reshape(-1, D) for h in hs], axis=0)
    logits = jnp.concatenate([lg.reshape(-1, lg.shape[-1])[:, :N_EXPERTS] for lg in logits_list], axis=0)
    n_tok = h_all.shape[0]
    top_v, top_i = lax.top_k(logits, TOP_K)
    weights = jax.nn.softmax(top_v, axis=-1)
    e_flat = top_i.reshape(-1)
    onehot = (e_flat[:, None] == jnp.arange(N_EXPERTS)[None, :]).astype(jnp.int32)
    csum = jnp.cumsum(onehot, axis=0)
    counts = csum[-1]
    rank = jnp.take_along_axis(csum, e_flat[:, None], axis=1)[:, 0] - 1
    padded = ((counts + MOE_TILE - 1) // MOE_TILE) * MOE_TILE
    group_end = jnp.cumsum(padded)
    group_start = group_end - padded
    slot = group_start[e_flat] + rank
    n_slots = _round_up(n_tok * TOP_K, MOE_TILE) + N_EXPERTS * MOE_TILE
    n_tiles = n_slots // MOE_TILE
    tok_of_slot = jnp.zeros((n_slots,), jnp.int32).at[slot].set(jnp.arange(n_tok * TOP_K, dtype=jnp.int32) // TOP_K)
    w_slot = jnp.zeros((n_slots,), F32).at[slot].set(weights.reshape(-1))
    tile_start = jnp.arange(n_tiles, dtype=jnp.int32) * MOE_TILE
    tile_expert = jnp.minimum(jnp.sum(tile_start[:, None] >= group_end[None, :], axis=1), N_EXPERTS - 1).astype(jnp.int32)
    n_used = (group_end[-1] // MOE_TILE).astype(jnp.int32).reshape(1)
    h_sorted = h_all[tok_of_slot]
    y_slot = moe_grouped_swiglu(h_sorted, w_slot.reshape(n_slots, 1), tile_expert, n_used, w1, w3, w2)
    slot2 = slot.reshape(n_tok, TOP_K)
    outs = []
    off = 0
    for x, g in zip(xs, gates):
        n = x.shape[0] * x.shape[1]
        ya = y_slot[slot2[off:off + n, 0]].reshape(x.shape)
        yb = y_slot[slot2[off:off + n, 1]].reshape(x.shape)
        outs.append(moe_combine_residual(x, g, ya, yb))
        off += n
    return outs


def kernel(x_prompt, x_sample, cache_nsa, cache_nsa_win, state_hgrn, cache_dsa_kv, cache_dsa_idx, page_table, c_prompt, c_sample, ada_w, ada_b, norm1_g, norm2_g, final_g, even_w_in, even_w_out, nsa_pe_k, nsa_pe_v, nsa_w_ck, nsa_w_cv, hgrn_lb_raw, hgrn_norm_g, ffn_w1, ffn_w3, ffn_w2, odd_w_in, odd_w_out, router_w, router_b, moe_w1, moe_w3, moe_w2):
    D = D_MODEL
    past_len = page_table.shape[1] * PAGE_SIZE
    win_len = cache_nsa_win.shape[2]
    Bp, Tp = x_prompt.shape[:2]
    Bs, Ts = x_sample.shape[:2]
    lb_soft = jax.nn.softmax(hgrn_lb_raw.astype(F32), axis=0)
    lower_bounds = jnp.cumsum(lb_soft, axis=0) - lb_soft[0]

    R = _round_up(Bp + Bs, 8)
    c_all = jnp.zeros((R, D), F32).at[:Bp].set(c_prompt).at[Bp:Bp + Bs].set(c_sample)
    mods = ada_modulation(c_all, ada_w, ada_b)

    def group_mods(l, lo, n, per_token_rows):
        m = mods[l, lo:lo + n].reshape(n, 6, D)
        sh1, sc1, g1, sh2, sc2, g2 = [m[:, j] for j in range(6)]
        s1 = norm1_g[l][None] * (1.0 + sc1)
        s2 = norm2_g[l][None] * (1.0 + sc2)
        vecs = [s1, sh1, g1, s2, sh2, g2]
        if per_token_rows:
            return [jnp.repeat(v, per_token_rows, axis=0)[None] for v in vecs]
        return [v[:, None, :] for v in vecs]

    def pad_cols(w, n):
        return jnp.pad(w, ((0, 0), (0, n - w.shape[1]))).astype(BF16)

    xp = x_prompt
    xs = x_sample.reshape(1, Bs * Ts, D)
    cos_p, sin_p = rope_tables(jnp.arange(Tp))
    outs_p = dict(rows=[], win=[], st=[], kv=[], idx=[])
    outs_s = dict(rows=[], win=[], st=[], kv=[], idx=[])
    for l in range(DEPTH):
        i = l // 2
        mp = group_mods(l, 0, Bp, 0)
        msm = group_mods(l, Bp, Bs, Ts)
        if l % 2 == 0:
            w1, w3, w2 = ffn_w1[i].astype(BF16), ffn_w3[i].astype(BF16), ffn_w2[i].astype(BF16)
            mixed, w_outs, rows, win, s_new = even_mixer_prompt(
                xp, mp[0], mp[1], cos_p, sin_p, even_w_in[i], even_w_out[i], nsa_pe_k[i], nsa_pe_v[i],
                nsa_w_ck[i], nsa_w_cv[i], lower_bounds[i], hgrn_norm_g[i], win_len)
            outs_p['rows'].append(rows)
            outs_p['win'].append(win)
            outs_p['st'].append(s_new)
            xp = out_proj_residual(mixed, w_outs, xp, mp[2])
            xp = ffn_residual(xp, mp[3], mp[4], mp[5], w1, w3, w2)
            mixed, w_outs, rows, win, s_new = even_mixer_sample(
                xs, msm[0], msm[1], past_len, Bs, even_w_in[i], even_w_out[i], nsa_pe_k[i], nsa_pe_v[i],
                nsa_w_ck[i], nsa_w_cv[i], lower_bounds[i], hgrn_norm_g[i],
                cache_nsa[i], cache_nsa_win[i], state_hgrn[i], page_table)
            outs_s['rows'].append(rows)
            outs_s['win'].append(win)
            outs_s['st'].append(s_new)
            xs = out_proj_residual(mixed, w_outs, xs, msm[2])
            xs = ffn_residual(xs, msm[3], msm[4], msm[5], w1, w3, w2)
        else:
            w_r = pad_cols(router_w[i], LANE)
            w1, w3, w2 = moe_w1[i].astype(BF16), moe_w3[i].astype(BF16), moe_w2[i].astype(BF16)
            o, w_outs, kv, ki = odd_mixer_prompt(xp, mp[0], mp[1], cos_p, sin_p, odd_w_in[i], odd_w_out[i])
            outs_p['kv'].append(kv)
            outs_p['idx'].append(ki)
            xp = out_proj_residual(o, w_outs, xp, mp[2])
            o, w_outs, kv, ki = odd_mixer_sample(xs, msm[0], msm[1], past_len, Bs, odd_w_in[i], odd_w_out[i],
                                                 cache_dsa_kv[i], cache_dsa_idx[i], page_table)
            outs_s['kv'].append(kv)
            outs_s['idx'].append(ki)
            xs = out_proj_residual(o, w_outs, xs, msm[2])
            hs, lgs = [], []
            for (x, m) in ((xp, mp), (xs, msm)):
                logits, h = norm_proj(x, m[3], m[4], w_r, with_h=True)
                lgs.append(logits[..., :N_EXPERTS] + router_b[i].astype(F32))
                hs.append(h)
            xp, xs = moe_ffn_residual([xp, xs], hs, lgs, [mp[5], msm[5]], w1, w3, w2)
    y_prompt = final_norm(xp, final_g)
    y_sample = final_norm(xs, final_g).reshape(Bs, Ts, D)
    st = lambda od, k: jnp.stack(od[k])
    return (y_prompt, y_sample, st(outs_p, 'rows'), st(outs_s, 'rows'), st(outs_p, 'win'), st(outs_s, 'win'),
            st(outs_p, 'st'), st(outs_s, 'st'), st(outs_p, 'kv'), st(outs_s, 'kv'),
            st(outs_p, 'idx'), st(outs_s, 'idx'))
```

```python
import functools
import math

import numpy as np
import jax
import jax.numpy as jnp
from jax import lax
from jax.experimental import pallas as pl
from jax.experimental.pallas import tpu as pltpu

D_MODEL = 1024
DEPTH = 4
PAGE_SIZE = 128
HEAD_DIM = 64
NSA_HEADS = 8
NSA_KV = 2
NSA_HPG = NSA_HEADS // NSA_KV
CMP_STRIDE = 16
CMP_BLK = 2 * CMP_STRIDE
SEL_BLK = 64
TOP_N = 16
WINDOW = 512
FORCE_SCORE = 1.0e4
HGRN_HEADS = 4
HGRN_DK = 128
HGRN_DV = 128
HGRN_CHUNK = 64
DSA_HEADS = 16
DSA_KV = 4
DSA_HPG = DSA_HEADS // DSA_KV
IDX_HEADS = 8
IDX_DIM = 64
IDX_TOPK = 256
D_FF = 2816
N_EXPERTS = 8
TOP_K = 2
Q_BLK = 128
ROPE_THETA = 10000.0
EPS = 1e-6
NEG = -1.0e30
NSA_QW = NSA_HEADS * HEAD_DIM
NSA_KVW = NSA_KV * HEAD_DIM
NSA_GW = NSA_HEADS * 3
HGRN_KW = HGRN_HEADS * HGRN_DK
HGRN_VW = HGRN_HEADS * HGRN_DV
EVEN_IN = NSA_QW + 6 * NSA_KVW + NSA_GW + 2 * HGRN_KW + 2 * HGRN_VW
EVEN_MIX = NSA_QW + HGRN_VW
DSA_QW = DSA_HEADS * HEAD_DIM
DSA_KVW = DSA_KV * HEAD_DIM
IDX_QW = IDX_HEADS * IDX_DIM
ODD_IN = DSA_QW + 2 * DSA_KVW + IDX_QW + IDX_DIM + IDX_HEADS
ODD_MIX = DSA_QW

LANE = 128
ROW_TILE = 512
FF_CHUNK = 256
MOE_TILE = 256
PROJ_CHUNK = 512
ATT_TQ = 128
KEY_BLK = 256
COUNT_GROUP = 4
HGRN_SUB = 16
HGRN_TILE = 256
PG_STEP = 8
INT_MIN = -2 ** 31
VMEM_LIMIT = 56 * 1024 * 1024

F32 = jnp.float32
BF16 = jnp.bfloat16


def _round_up(n, m):
    return -(-n // m) * m


def _params(*sem):
    return pltpu.CompilerParams(dimension_semantics=sem, vmem_limit_bytes=VMEM_LIMIT)


def _norm_mod(x, scale, shift):
    ms = jnp.mean(x * x, axis=-1, keepdims=True)
    return x * lax.rsqrt(ms + EPS) * scale + shift


def _mod_spec(mod, tm):
    if mod.shape[1] == 1:
        return pl.BlockSpec((1, 1, mod.shape[2]), lambda b, i: (b, 0, 0))
    return pl.BlockSpec((1, tm, mod.shape[2]), lambda b, i: (b, i, 0))


def _ada_body(c_ref, w_ref, b_ref, o_ref):
    c = c_ref[...]
    cs = (c * jax.nn.sigmoid(c)).astype(BF16)
    o_ref[0] = jnp.dot(cs, w_ref[0].astype(BF16), preferred_element_type=F32) + b_ref[0]


def ada_modulation(c_all, ada_w, ada_b):
    R, D = c_all.shape
    N = ada_w.shape[2]
    tn = 1536
    return pl.pallas_call(
        _ada_body,
        grid=(DEPTH, N // tn),
        in_specs=[pl.BlockSpec((R, D), lambda l, j: (0, 0)),
                  pl.BlockSpec((1, D, tn), lambda l, j: (l, 0, j)),
                  pl.BlockSpec((1, 1, tn), lambda l, j: (l, 0, j))],
        out_specs=pl.BlockSpec((1, R, tn), lambda l, j: (l, 0, j)),
        out_shape=jax.ShapeDtypeStruct((DEPTH, R, N), F32),
        compiler_params=_params("arbitrary", "arbitrary"),
        name="ada_modulation",
    )(c_all, ada_w, ada_b.reshape(DEPTH, 1, N))


def _norm_proj_body(x_ref, sc_ref, sh_ref, w_ref, o_ref, h_ref=None):
    h = _norm_mod(x_ref[0], sc_ref[0], sh_ref[0]).astype(BF16)
    o_ref[0] = jnp.dot(h, w_ref[...], preferred_element_type=F32)
    if h_ref is not None:
        h_ref[0] = h


def norm_proj(x, scale, shift, w, with_h=False):
    B, T, D = x.shape
    N = w.shape[1]
    tm = min(T, ROW_TILE)
    out_shape = [jax.ShapeDtypeStruct((B, T, N), F32)]
    out_specs = [pl.BlockSpec((1, tm, N), lambda b, i: (b, i, 0))]
    if with_h:
        out_shape.append(jax.ShapeDtypeStruct((B, T, D), BF16))
        out_specs.append(pl.BlockSpec((1, tm, D), lambda b, i: (b, i, 0)))
    res = pl.pallas_call(
        _norm_proj_body,
        grid=(B, T // tm),
        in_specs=[pl.BlockSpec((1, tm, D), lambda b, i: (b, i, 0)),
                  _mod_spec(scale, tm), _mod_spec(shift, tm),
                  pl.BlockSpec((D, N), lambda b, i: (0, 0))],
        out_specs=out_specs,
        out_shape=out_shape,
        compiler_params=_params("parallel", "parallel"),
        name="norm_proj",
    )(x, scale, shift, w)
    return res if with_h else res[0]


def _rope_chunk(y, cos, sin):
    lane = lax.broadcasted_iota(jnp.int32, y.shape, 1)
    swapped = jnp.where(lane % HEAD_DIM < HEAD_DIM // 2,
                        pltpu.roll(y, LANE - HEAD_DIM // 2, 1), pltpu.roll(y, HEAD_DIM // 2, 1))
    return y * cos + swapped * sin


def _proj_seg_body(x_ref, sc_ref, sh_ref, cos_ref, sin_ref, w_ref, *o_refs, segs):
    h = _norm_mod(x_ref[0], sc_ref[0], sh_ref[0]).astype(BF16)
    cos = cos_ref[...]
    sin = sin_ref[...]
    t_refs = iter(o_refs[len(segs):])
    c0 = 0
    for o_ref, (width, rope_flags, dtype, channel_major) in zip(o_refs, segs):
        t_ref = next(t_refs) if channel_major else None
        for j0 in range(0, width, PROJ_CHUNK):
            wd = min(PROJ_CHUNK, width - j0)
            y = jnp.dot(h, w_ref[:, c0 + j0:c0 + j0 + wd], preferred_element_type=F32)
            for k in range(wd // LANE):
                cols = slice(j0 + k * LANE, j0 + (k + 1) * LANE)
                yk = y[:, k * LANE:(k + 1) * LANE]
                if rope_flags[(j0 + k * LANE) // LANE]:
                    yk = _rope_chunk(yk, cos, sin)
                o_ref[0, :, cols] = yk.astype(dtype)
                if t_ref is not None:
                    t_ref[0, cols, :] = yk.T
        c0 += width


def proj_segments(x, scale, shift, cos_t, sin_t, w, segs):
    B, T, D = x.shape
    tm = min(T, ROW_TILE)
    N = w.shape[1]
    cm = [s for s in segs if s[3]]
    return pl.pallas_call(
        functools.partial(_proj_seg_body, segs=segs),
        grid=(B, T // tm),
        in_specs=[pl.BlockSpec((1, tm, D), lambda b, i: (b, i, 0)),
                  _mod_spec(scale, tm), _mod_spec(shift, tm),
                  pl.BlockSpec((tm, LANE), lambda b, i: (i, 0)),
                  pl.BlockSpec((tm, LANE), lambda b, i: (i, 0)),
                  pl.BlockSpec((D, N), lambda b, i: (0, 0))],
        out_specs=[pl.BlockSpec((1, tm, s[0]), lambda b, i: (b, i, 0)) for s in segs]
        + [pl.BlockSpec((1, s[0], tm), lambda b, i: (b, 0, i)) for s in cm],
        out_shape=[jax.ShapeDtypeStruct((B, T, s[0]), s[2]) for s in segs]
        + [jax.ShapeDtypeStruct((B, s[0], T), F32) for s in cm],
        compiler_params=_params("parallel", "parallel"),
        name="proj_segments",
    )(x, scale, shift, cos_t, sin_t, w)


def rope_tables(pos):
    half = HEAD_DIM // 2
    inv = ROPE_THETA ** (-jnp.arange(half, dtype=F32) / half)
    ang = pos.astype(F32)[:, None] * inv[None, :]
    cos, sin = jnp.cos(ang), jnp.sin(ang)
    return jnp.tile(cos, (1, 4)), jnp.tile(jnp.concatenate([-sin, sin], axis=1), (1, 2))


def _out_proj_body(*refs, n_in):
    a_refs, w_refs = refs[:n_in], refs[n_in:2 * n_in]
    x_ref, g_ref, o_ref = refs[2 * n_in:]
    y = None
    for a_ref, w_ref in zip(a_refs, w_refs):
        part = jnp.dot(a_ref[0].astype(BF16), w_ref[...], preferred_element_type=F32)
        y = part if y is None else y + part
    o_ref[0] = x_ref[0] + g_ref[0] * y


def out_proj_residual(a_list, w_list, x, gate):
    B, T, D = x.shape
    tm = min(T, ROW_TILE)
    n_in = len(a_list)
    return pl.pallas_call(
        functools.partial(_out_proj_body, n_in=n_in),
        grid=(B, T // tm),
        in_specs=[pl.BlockSpec((1, tm, a.shape[2]), lambda b, i: (b, i, 0)) for a in a_list]
        + [pl.BlockSpec(w.shape, lambda b, i: (0, 0)) for w in w_list]
        + [pl.BlockSpec((1, tm, D), lambda b, i: (b, i, 0)), _mod_spec(gate, tm)],
        out_specs=pl.BlockSpec((1, tm, D), lambda b, i: (b, i, 0)),
        out_shape=jax.ShapeDtypeStruct((B, T, D), F32),
        compiler_params=_params("parallel", "parallel"),
        name="out_proj_residual",
    )(*a_list, *w_list, x, gate)


def _swiglu_acc(h, w1_ref, w3_ref, w2_ref, acc_ref):
    for c in range(D_FF // FF_CHUNK):
        cols = slice(c * FF_CHUNK, (c + 1) * FF_CHUNK)
        u = jnp.dot(h, w1_ref[:, cols], preferred_element_type=F32)
        v = jnp.dot(h, w3_ref[:, cols], preferred_element_type=F32)
        a = (u * jax.nn.sigmoid(u) * v).astype(BF16)
        part = jnp.dot(a, w2_ref[cols, :], preferred_element_type=F32)
        if c == 0:
            acc_ref[...] = part
        else:
            acc_ref[...] += part


def _ffn_body(x_ref, sc_ref, sh_ref, g_ref, w1_ref, w3_ref, w2_ref, o_ref, acc_ref):
    x = x_ref[0]
    h = _norm_mod(x, sc_ref[0], sh_ref[0]).astype(BF16)
    _swiglu_acc(h, w1_ref, w3_ref, w2_ref, acc_ref)
    o_ref[0] = x + g_ref[0] * acc_ref[...]


def ffn_residual(x, scale, shift, gate, w1, w3, w2):
    B, T, D = x.shape
    tm = min(T, ROW_TILE)
    wspec = lambda shape: pl.BlockSpec(shape, lambda b, i: (0, 0))
    return pl.pallas_call(
        _ffn_body,
        grid=(B, T // tm),
        in_specs=[pl.BlockSpec((1, tm, D), lambda b, i: (b, i, 0)),
                  _mod_spec(scale, tm), _mod_spec(shift, tm), _mod_spec(gate, tm),
                  wspec((D, D_FF)), wspec((D, D_FF)), wspec((D_FF, D))],
        out_specs=pl.BlockSpec((1, tm, D), lambda b, i: (b, i, 0)),
        out_shape=jax.ShapeDtypeStruct((B, T, D), F32),
        scratch_shapes=[pltpu.VMEM((tm, D), F32)],
        compiler_params=_params("parallel", "parallel"),
        name="ffn_residual",
    )(x, scale, shift, gate, w1, w3, w2)


def _moe_body(te_ref, nt_ref, h_ref, ws_ref, w1_ref, w3_ref, w2_ref, o_ref, acc_ref):
    i = pl.program_id(0)

    @pl.when(i < nt_ref[0])
    def _():
        _swiglu_acc(h_ref[...], w1_ref.at[0], w3_ref.at[0], w2_ref.at[0], acc_ref)
        o_ref[...] = ws_ref[...] * acc_ref[...]

    @pl.when(i >= nt_ref[0])
    def _():
        o_ref[...] = jnp.zeros_like(o_ref)


def moe_grouped_swiglu(h_sorted, w_slot, tile_expert, n_tiles_used, w1, w3, w2):
    S, D = h_sorted.shape
    tm = MOE_TILE
    n_tiles = S // tm
    wspec = lambda shape: pl.BlockSpec((1,) + shape, lambda i, te, nt: (te[i], 0, 0))
    grid_spec = pltpu.PrefetchScalarGridSpec(
        num_scalar_prefetch=2,
        grid=(n_tiles,),
        in_specs=[pl.BlockSpec((tm, D), lambda i, te, nt: (i, 0)),
                  pl.BlockSpec((tm, 1), lambda i, te, nt: (i, 0)),
                  wspec((D, D_FF)), wspec((D, D_FF)), wspec((D_FF, D))],
        out_specs=pl.BlockSpec((tm, D), lambda i, te, nt: (i, 0)),
        scratch_shapes=[pltpu.VMEM((tm, D), F32)],
    )
    return pl.pallas_call(
        _moe_body,
        grid_spec=grid_spec,
        out_shape=jax.ShapeDtypeStruct((S, D), F32),
        compiler_params=_params("arbitrary"),
        name="moe_grouped_swiglu",
    )(tile_expert, n_tiles_used, h_sorted, w_slot, w1, w3, w2)


def _combine_body(x_ref, g_ref, a_ref, b_ref, o_ref):
    o_ref[0] = x_ref[0] + g_ref[0] * (a_ref[0] + b_ref[0])


def moe_combine_residual(x, gate, ya, yb):
    B, T, D = x.shape
    tm = min(T, ROW_TILE)
    tile = pl.BlockSpec((1, tm, D), lambda b, i: (b, i, 0))
    return pl.pallas_call(
        _combine_body,
        grid=(B, T // tm),
        in_specs=[tile, _mod_spec(gate, tm), tile, tile],
        out_specs=tile,
        out_shape=jax.ShapeDtypeStruct((B, T, D), F32),
        compiler_params=_params("parallel", "parallel"),
        name="moe_combine_residual",
    )(x, gate, ya, yb)


def _final_norm_body(x_ref, g_ref, o_ref):
    x = x_ref[0]
    ms = jnp.mean(x * x, axis=-1, keepdims=True)
    o_ref[0] = x * lax.rsqrt(ms + EPS) * g_ref[...]


def final_norm(x, g):
    B, T, D = x.shape
    tm = min(T, ROW_TILE)
    return pl.pallas_call(
        _final_norm_body,
        grid=(B, T // tm),
        in_specs=[pl.BlockSpec((1, tm, D), lambda b, i: (b, i, 0)),
                  pl.BlockSpec((1, D), lambda b, i: (0, 0))],
        out_specs=pl.BlockSpec((1, tm, D), lambda b, i: (b, i, 0)),
        out_shape=jax.ShapeDtypeStruct((B, T, D), F32),
        compiler_params=_params("parallel", "parallel"),
        name="final_norm",
    )(x, g.reshape(1, D))


def _split3(x):
    hi = x.astype(BF16)
    r1 = x - hi.astype(F32)
    mid = r1.astype(BF16)
    lo = (r1 - mid.astype(F32)).astype(BF16)
    return hi, mid, lo


def _dot01(m01, x):
    hi, mid, lo = _split3(x)
    d = lambda p: jnp.dot(m01, p, preferred_element_type=F32)
    return d(hi) + d(mid) + d(lo)


def _hgrn_body(hq_ref, hf_ref, hi_ref, hg_ref, lb_ref, ng_ref, s0_ref, o_ref, sn_ref, st_ref, *, sub, n_sub):
    i = pl.program_id(1)

    @pl.when(i == 0)
    def _():
        for h in range(HGRN_HEADS):
            st_ref[h] = s0_ref[0, h].T

    lb = lb_ref[...]
    ng = ng_ref[...]
    row = lax.broadcasted_iota(jnp.int32, (sub, sub), 0)
    col = lax.broadcasted_iota(jnp.int32, (sub, sub), 1)
    tril = (row >= col).astype(BF16)
    trow = lax.broadcasted_iota(jnp.int32, (sub, HGRN_DK), 0)

    def chunk(c, carry):
        rows = pl.ds(pl.multiple_of(c * sub, sub), sub)
        hq = hq_ref[0, rows, :]
        hf = hf_ref[0, rows, :]
        hv = hi_ref[0, rows, :]
        hg = hg_ref[0, rows, :]
        f = lb + (1.0 - lb) * jax.nn.sigmoid(hf)
        logf = jnp.log(f)
        kk = 1.0 - f
        qq = hq * jax.nn.sigmoid(hq)
        b = _dot01(tril, logf)
        bl = b[sub - 1:sub, :]
        qe = qq * jnp.exp(b)
        ke = kk * jnp.exp(bl - b)
        ebl = jnp.exp(bl)
        outs = []
        for h in range(HGRN_HEADS):
            cs = slice(h * HGRN_DK, (h + 1) * HGRN_DK)
            st = st_ref[h]
            o = lax.dot_general(qe[:, cs].astype(BF16), st.astype(BF16), (((1,), (1,)), ((), ())),
                                preferred_element_type=F32)
            bh, qh, kh, vh = b[:, cs], qq[:, cs], kk[:, cs], hv[:, cs]
            for s in range(sub):
                e = jnp.exp(jnp.minimum(bh - bh[s:s + 1, :], 0.0))
                w = jnp.where(trow >= s, e * qh * kh[s:s + 1, :], 0.0)
                o = o + jnp.sum(w, axis=-1, keepdims=True) * vh[s:s + 1, :]
            upd = lax.dot_general(vh.astype(BF16), ke[:, cs].astype(BF16), (((0,), (0,)), ((), ())),
                                  preferred_element_type=F32)
            st_ref[h] = st * ebl[:, cs] + upd
            ms = jnp.mean(o * o, axis=-1, keepdims=True)
            outs.append(o * lax.rsqrt(ms + EPS))
        o_ref[0, rows, :] = jnp.concatenate(outs, axis=-1) * ng * (hg * jax.nn.sigmoid(hg))
        return carry

    lax.fori_loop(0, n_sub, chunk, 0)

    @pl.when(i == pl.num_programs(1) - 1)
    def _():
        for h in range(HGRN_HEADS):
            sn_ref[0, h] = st_ref[h].T


def hgrn_mixer(hx, lb, norm_g, s0):
    B, T, _ = hx.shape
    tc = min(T, HGRN_TILE)
    sub = math.gcd(T, HGRN_SUB)
    spec = lambda j: pl.BlockSpec((1, tc, HGRN_KW), lambda b, i, j=j: (b, i, j))
    vec = pl.BlockSpec((1, HGRN_KW), lambda b, i: (0, 0))
    state = pl.BlockSpec((1, HGRN_HEADS, HGRN_DK, HGRN_DV), lambda b, i: (b, 0, 0, 0))
    return pl.pallas_call(
        functools.partial(_hgrn_body, sub=sub, n_sub=tc // sub),
        grid=(B, T // tc),
        in_specs=[spec(0), spec(1), spec(2), spec(3), vec, vec, state],
        out_specs=[pl.BlockSpec((1, tc, HGRN_VW), lambda b, i: (b, i, 0)), state],
        out_shape=[jax.ShapeDtypeStruct((B, T, HGRN_VW), F32),
                   jax.ShapeDtypeStruct((B, HGRN_HEADS, HGRN_DK, HGRN_DV), F32)],
        scratch_shapes=[pltpu.VMEM((HGRN_HEADS, HGRN_DV, HGRN_DK), F32)],
        compiler_params=_params("parallel", "arbitrary"),
        name="hgrn_mixer",
    )(hx, hx, hx, hx, lb.reshape(1, HGRN_KW), jnp.tile(norm_g, HGRN_HEADS).reshape(1, HGRN_VW), s0)


def _nt(a, b):
    return lax.dot_general(a, b, (((1,), (1,)), ((), ())), preferred_element_type=F32)


def _flash_step(kblk, vT, q_heads, mask, m, l, acc_ref, batched):
    tq = q_heads[0].shape[0]
    if batched:
        acc = acc_ref[...]
        scores = [_nt(kblk, qh) for qh in q_heads]
    m_out, l_out, acc_out = [], [], []
    for h, qh in enumerate(q_heads):
        cs = slice(h * tq, (h + 1) * tq)
        s = jnp.where(mask, scores[h] if batched else _nt(kblk, qh), NEG)
        m_new = jnp.maximum(m[:, cs], jnp.max(s, axis=0, keepdims=True))
        alpha = jnp.exp(m[:, cs] - m_new)
        e = jnp.where(mask, jnp.exp(s - m_new), 0.0)
        l_out.append(alpha * l[:, cs] + jnp.sum(e, axis=0, keepdims=True))
        pv = jnp.dot(vT, e.astype(BF16), preferred_element_type=F32)
        if batched:
            acc_out.append(alpha * acc[:, cs] + pv)
        else:
            acc_ref[:, cs] = alpha * acc_ref[:, cs] + pv
        m_out.append(m_new)
    if batched:
        acc_ref[...] = jnp.concatenate(acc_out, axis=1)
    return jnp.concatenate(m_out, axis=1), jnp.concatenate(l_out, axis=1)


def _head_slabs(qp_ref, first_head, n):
    return [qp_ref[0, :, (first_head + h) * LANE:(first_head + h + 1) * LANE] for h in range(n)]


def _stack_heads(qp_ref, first_head, n):
    return jnp.concatenate(_head_slabs(qp_ref, first_head, n), axis=0)


def _store_heads(o_ref, oT, first_head, n, slot):
    tq = oT.shape[1] // n
    lane = lax.broadcasted_iota(jnp.int32, (tq, LANE), 1)
    valid = (lane >= HEAD_DIM * slot) & (lane < HEAD_DIM * (slot + 1))
    for h in range(n):
        blk = oT[:, h * tq:(h + 1) * tq].T
        o_ref[0, :, (first_head + h) * LANE:(first_head + h + 1) * LANE] = jnp.where(valid, blk, 0.0).astype(o_ref.dtype)


def _nsa_body(qp_ref, rows_ref, win_ref, gl_ref, pek_ref, pev_ref, wck_ref, wcv_ref, o_ref,
              kcmp_ref, vcmpT_ref, ks_ref, vsT_ref, kw_ref, vwT_ref, stage_ref, shift_ref, sel_ref, acc_ref, *, T):
    i = pl.program_id(1)
    tq = ATT_TQ
    nq = NSA_HPG * tq
    n_half = T // CMP_STRIDE
    n_sel = T // SEL_BLK
    k_sel = min(TOP_N, n_sel)
    per_kb = KEY_BLK // SEL_BLK

    @pl.when(i == 0)
    def _prepare():
        nrow = lax.broadcasted_iota(jnp.int32, (n_half, LANE), 0)
        for slab, pe_ref, w_ref in ((0, pek_ref, wck_ref), (1, pev_ref, wcv_ref)):
            def stage_blk(kb, carry, slab=slab):
                rs = pl.ds(pl.multiple_of(kb * KEY_BLK, KEY_BLK), KEY_BLK)
                stage_ref[rs, :] = rows_ref[0, rs, slab * LANE:(slab + 1) * LANE]
                return carry

            lax.fori_loop(0, T // KEY_BLK, stage_blk, 0)
            first = jnp.zeros((n_half, LANE), F32)
            second = jnp.zeros((n_half, LANE), F32)
            for r in range(CMP_STRIDE):
                y = stage_ref[pl.ds(r, n_half, stride=CMP_STRIDE), :]
                first = first + jnp.dot((y + pe_ref[r:r + 1, :]).astype(BF16), w_ref[r],
                                        preferred_element_type=F32)
                second = second + jnp.dot((y + pe_ref[r + CMP_STRIDE:r + CMP_STRIDE + 1, :]).astype(BF16),
                                          w_ref[r + CMP_STRIDE], preferred_element_type=F32)
            shift_ref[0:n_half, :] = second
            shift_ref[n_half:n_half + 8, :] = jnp.zeros((8, LANE), F32)
            c = jnp.where(nrow < n_half - 1, first + shift_ref[1:n_half + 1, :], 0.0)
            if slab == 0:
                kcmp_ref[...] = c.astype(BF16)
            else:
                for j in range(n_half // LANE):
                    vcmpT_ref[:, j * LANE:(j + 1) * LANE] = c[j * LANE:(j + 1) * LANE, :].T.astype(BF16)

        def copy_blk(kb, carry):
            for half in range(KEY_BLK // LANE):
                rs = pl.ds(pl.multiple_of(kb * KEY_BLK + half * LANE, LANE), LANE)
                hs = slice(half * LANE, (half + 1) * LANE)
                ks_ref[kb, hs, :] = rows_ref[0, rs, 2 * LANE:3 * LANE].astype(BF16)
                vsT_ref[kb, :, hs] = rows_ref[0, rs, 3 * LANE:4 * LANE].T.astype(BF16)
                kw_ref[kb, hs, :] = win_ref[0, rs, 0:LANE].astype(BF16)
                vwT_ref[kb, :, hs] = win_ref[0, rs, LANE:2 * LANE].T.astype(BF16)
            return carry

        lax.fori_loop(0, T // KEY_BLK, copy_blk, 0)

    t0 = i * tq
    lane_t = t0 + (lax.broadcasted_iota(jnp.int32, (1, nq), 1) & (tq - 1))
    t_row = lane_t[:, 0:tq]
    gT = jax.nn.sigmoid(gl_ref[0].T)
    kio = lax.broadcasted_iota(jnp.int32, (KEY_BLK, tq), 0)
    nio = lax.broadcasted_iota(jnp.int32, (n_half, nq), 0)
    jcol = lax.broadcasted_iota(jnp.int32, (n_sel, tq), 0)
    pj = lax.broadcasted_iota(jnp.int32, (n_sel, n_half), 0)
    pi = lax.broadcasted_iota(jnp.int32, (n_sel, n_half), 1)
    pool = ((((pi + 1) >> 2) == pj) & (pi < n_half - 1)).astype(BF16)
    n_kb = (t0 + tq + KEY_BLK - 1) // KEY_BLK
    n_full = t0 // KEY_BLK
    m0 = jnp.full((1, nq), NEG, F32)
    l0 = jnp.zeros((1, nq), F32)

    for g in range(NSA_KV):
        q_heads = _head_slabs(qp_ref, NSA_HPG * g, NSA_HPG)
        qs = jnp.concatenate(q_heads, axis=0)
        cmask = (CMP_STRIDE * nio + CMP_BLK <= lane_t + 1) & (nio < n_half - 1)
        s = jnp.where(cmask, _nt(kcmp_ref[...], qs), NEG)
        e = jnp.where(cmask, jnp.exp(s - jnp.max(s, axis=0, keepdims=True)), 0.0)
        p = e * (1.0 / jnp.maximum(jnp.sum(e, axis=0, keepdims=True), 1e-30))
        ocT = jnp.dot(vcmpT_ref[...], p.astype(BF16), preferred_element_type=F32)
        imp = p[:, 0:tq]
        for h in range(1, NSA_HPG):
            imp = imp + p[:, h * tq:(h + 1) * tq]
        imp_sel = _dot01(pool, imp)
        cur = t_row >> 6
        forced = (jcol == 0) | (jcol == cur) | (jcol == cur - 1)
        visible = jcol <= cur
        score = jnp.where(visible, imp_sel + jnp.where(forced, FORCE_SCORE, 0.0), NEG)
        rank = jnp.zeros((n_sel, tq), jnp.int32)
        for j2 in range(n_sel):
            r = score[j2:j2 + 1, :]
            beats = (r > score) | ((r == score) & (jcol > j2))
            rank = rank + beats.astype(jnp.int32)
        sel_ref[...] = jnp.where((rank < k_sel) & visible, 1.0, 0.0)

        def sel_step(kb, carry, causal):
            rowsel = jnp.concatenate(
                [jnp.broadcast_to(sel_ref[pl.ds(kb * per_kb + k, 1), :], (SEL_BLK, tq)) for k in range(per_kb)],
                axis=0)
            mask = rowsel > 0.5
            if causal:
                mask = mask & (kb * KEY_BLK + kio <= t_row)
            return _flash_step(ks_ref[kb], vsT_ref[kb], q_heads, mask, carry[0], carry[1], acc_ref, True)

        acc_ref[...] = jnp.zeros_like(acc_ref)
        ml = lax.fori_loop(0, n_full, functools.partial(sel_step, causal=False), (m0, l0))
        _, l = lax.fori_loop(n_full, n_kb, functools.partial(sel_step, causal=True), ml)
        osT = acc_ref[...] * (1.0 / jnp.maximum(l, 1e-30))

        def win_step(kb, carry):
            kpos = kb * KEY_BLK + kio
            mask = (kpos <= t_row) & (kpos > t_row - WINDOW)
            return _flash_step(kw_ref[kb], vwT_ref[kb], q_heads, mask, carry[0], carry[1], acc_ref, True)

        acc_ref[...] = jnp.zeros_like(acc_ref)
        w_lo = jnp.maximum(t0 - (WINDOW - 1), 0) // KEY_BLK
        _, l = lax.fori_loop(w_lo, n_kb, win_step, (m0, l0))
        owT = acc_ref[...] * (1.0 / jnp.maximum(l, 1e-30))

        def gate(c):
            return jnp.concatenate([gT[3 * (NSA_HPG * g + h) + c:3 * (NSA_HPG * g + h) + c + 1, :]
                                    for h in range(NSA_HPG)], axis=1)

        oT = gate(0) * ocT + gate(1) * osT + gate(2) * owT
        _store_heads(o_ref, oT, NSA_HPG * g, NSA_HPG, g)


def _block_diag2(w):
    w3 = w.reshape(CMP_BLK, HEAD_DIM, HEAD_DIM)
    z = jnp.zeros_like(w3)
    return jnp.concatenate([jnp.concatenate([w3, z], axis=2), jnp.concatenate([z, w3], axis=2)], axis=1).astype(BF16)


def nsa_prompt(qp, rows, win, gl, pe_k, pe_v, w_ck, w_cv):
    B, T, _ = rows.shape
    assert T % (CMP_STRIDE * LANE) == 0 and T % KEY_BLK == 0
    tq = ATT_TQ
    nq = NSA_HPG * tq
    n_half = T // CMP_STRIDE
    n_kb = T // KEY_BLK
    const = lambda shape: pl.BlockSpec(shape, lambda b, i: (0,) * len(shape))
    return pl.pallas_call(
        functools.partial(_nsa_body, T=T),
        grid=(B, T // tq),
        in_specs=[pl.BlockSpec((1, tq, NSA_HEADS * LANE), lambda b, i: (b, i, 0)),
                  pl.BlockSpec((1, T, 4 * LANE), lambda b, i: (b, 0, 0)),
                  pl.BlockSpec((1, T, 2 * LANE), lambda b, i: (b, 0, 0)),
                  pl.BlockSpec((1, tq, LANE), lambda b, i: (b, i, 0)),
                  const((CMP_BLK, LANE)), const((CMP_BLK, LANE)),
                  const((CMP_BLK, LANE, LANE)), const((CMP_BLK, LANE, LANE))],
        out_specs=pl.BlockSpec((1, tq, NSA_HEADS * LANE), lambda b, i: (b, i, 0)),
        out_shape=jax.ShapeDtypeStruct((B, T, NSA_HEADS * LANE), BF16),
        scratch_shapes=[pltpu.VMEM((n_half, LANE), BF16), pltpu.VMEM((LANE, n_half), BF16),
                        pltpu.VMEM((n_kb, KEY_BLK, LANE), BF16), pltpu.VMEM((n_kb, LANE, KEY_BLK), BF16),
                        pltpu.VMEM((n_kb, KEY_BLK, LANE), BF16), pltpu.VMEM((n_kb, LANE, KEY_BLK), BF16),
                        pltpu.VMEM((T, LANE), F32),
                        pltpu.VMEM((n_half + 8, LANE), F32), pltpu.VMEM((T // SEL_BLK, tq), F32),
                        pltpu.VMEM((LANE, nq), F32)],
        compiler_params=_params("parallel", "arbitrary"),
        name="nsa_prompt",
    )(qp, rows, win, gl, jnp.tile(pe_k, (1, 2)), jnp.tile(pe_v, (1, 2)), _block_diag2(w_ck), _block_diag2(w_cv))


def _dsa_body(qp_ref, kv_ref, qi_ref, ki_ref, wi_ref, o_ref,
              k_ref, vT_ref, kilo_ref, kihi_ref, key_ref, acc_ref, *, T, n_keep):
    i = pl.program_id(1)
    tq = ATT_TQ
    nq = DSA_HPG * tq

    @pl.when(i == 0)
    def _prepare():
        def copy_blk(kb, carry):
            for half in range(KEY_BLK // LANE):
                rs = pl.ds(pl.multiple_of(kb * KEY_BLK + half * LANE, LANE), LANE)
                hs = slice(half * LANE, (half + 1) * LANE)
                for slab in range(2):
                    k_ref[slab, kb, hs, :] = kv_ref[0, rs, slab * LANE:(slab + 1) * LANE].astype(BF16)
                    vT_ref[slab, kb, :, hs] = kv_ref[0, rs, (2 + slab) * LANE:(3 + slab) * LANE].T.astype(BF16)
                kix = ki_ref[0, rs, :]
                kilo_ref[kb, hs, :] = kix.astype(BF16)
                kihi_ref[kb, hs, :] = pltpu.roll(kix, IDX_DIM, 1).astype(BF16)
            return carry

        lax.fori_loop(0, T // KEY_BLK, copy_blk, 0)
        key_ref[...] = jnp.full(key_ref.shape, INT_MIN, jnp.int32)

    grp = key_ref.shape[1] // KEY_BLK
    key_blk = lambda kb: (kb // grp, pl.ds(pl.multiple_of((kb % grp) * KEY_BLK, KEY_BLK), KEY_BLK), slice(None))
    t0 = i * tq
    n_kb = (t0 + tq + KEY_BLK - 1) // KEY_BLK
    t_row = t0 + lax.broadcasted_iota(jnp.int32, (1, tq), 1)
    kio_q = lax.broadcasted_iota(jnp.int32, (KEY_BLK, tq), 0)

    wT = wi_ref[0].T * (IDX_HEADS ** -0.5)
    qi = _stack_heads(qi_ref, 0, IDX_HEADS // 2)

    def idx_step(kb, carry):
        s_lo = _nt(kilo_ref[kb], qi)
        s_hi = _nt(kihi_ref[kb], qi)
        sc = jnp.zeros((KEY_BLK, tq), F32)
        for p in range(IDX_HEADS // 2):
            cs = slice(p * tq, (p + 1) * tq)
            sc = sc + jnp.maximum(s_lo[:, cs], 0.0) * wT[2 * p:2 * p + 1, :]
            sc = sc + jnp.maximum(s_hi[:, cs], 0.0) * wT[2 * p + 1:2 * p + 2, :]
        sc = jnp.where(kb * KEY_BLK + kio_q <= t_row, sc, NEG)
        bits = lax.bitcast_convert_type(sc, jnp.int32)
        key_ref[key_blk(kb)] = jnp.where(bits < 0, bits ^ 0x7FFFFFFF, bits)
        return carry

    lax.fori_loop(0, n_kb, idx_step, 0)

    def bit_step(it, theta):
        cand = theta + lax.shift_left(jnp.int32(1), 31 - it)

        def cnt_step(g, c):
            return c + jnp.sum((key_ref[g] >= cand).astype(jnp.int32), axis=0, keepdims=True)

        cnt = lax.fori_loop(0, (n_kb + grp - 1) // grp, cnt_step, jnp.zeros((1, tq), jnp.int32))
        return jnp.where(cnt >= n_keep, cand, theta)

    theta = lax.fori_loop(0, 32, bit_step, jnp.full((1, tq), INT_MIN, jnp.int32))

    q_heads = [_head_slabs(qp_ref, DSA_HPG * g, DSA_HPG) for g in range(DSA_KV)]
    m0 = jnp.full((1, nq), NEG, F32)
    l0 = jnp.zeros((1, nq), F32)
    acc_ref[...] = jnp.zeros_like(acc_ref)

    def att_step(kb, carry, causal):
        mask = key_ref[key_blk(kb)] >= theta
        if causal:
            mask = mask & (kb * KEY_BLK + kio_q <= t_row)
        out = []
        for g in range(DSA_KV):
            m, l = _flash_step(k_ref[g // 2, kb], vT_ref[g // 2, kb], q_heads[g], mask,
                               carry[2 * g], carry[2 * g + 1], acc_ref.at[g], False)
            out += [m, l]
        return tuple(out)

    mid = lax.fori_loop(0, t0 // KEY_BLK, functools.partial(att_step, causal=False), (m0, l0) * DSA_KV)
    fin = lax.fori_loop(t0 // KEY_BLK, n_kb, functools.partial(att_step, causal=True), mid)
    for g in range(DSA_KV):
        oT = acc_ref[g] * (1.0 / jnp.maximum(fin[2 * g + 1], 1e-30))
        _store_heads(o_ref, oT, DSA_HPG * g, DSA_HPG, g % 2)


def dsa_prompt(qp, kv, qi, ki, wi):
    B, T, _ = kv.shape
    assert T % KEY_BLK == 0
    tq = ATT_TQ
    nq = DSA_HPG * tq
    n_kb = T // KEY_BLK
    grp = math.gcd(n_kb, COUNT_GROUP)
    n_keep = min(IDX_TOPK, T // 4)
    tile = lambda w: pl.BlockSpec((1, tq, w), lambda b, i: (b, i, 0))
    whole = lambda w: pl.BlockSpec((1, T, w), lambda b, i: (b, 0, 0))
    return pl.pallas_call(
        functools.partial(_dsa_body, T=T, n_keep=n_keep),
        grid=(B, T // tq),
        in_specs=[tile(DSA_HEADS * LANE), whole(4 * LANE), tile(IDX_QW), whole(LANE), tile(LANE)],
        out_specs=tile(DSA_HEADS * LANE),
        out_shape=jax.ShapeDtypeStruct((B, T, DSA_HEADS * LANE), BF16),
        scratch_shapes=[pltpu.VMEM((2, n_kb, KEY_BLK, LANE), BF16), pltpu.VMEM((2, n_kb, LANE, KEY_BLK), BF16),
                        pltpu.VMEM((n_kb, KEY_BLK, LANE), BF16), pltpu.VMEM((n_kb, KEY_BLK, LANE), BF16),
                        pltpu.VMEM((n_kb // grp, grp * KEY_BLK, tq), jnp.int32),
                        pltpu.VMEM((DSA_KV, LANE, nq), F32)],
        compiler_params=_params("parallel", "arbitrary"),
        name="dsa_prompt",
    )(qp, kv, qi, ki, wi)


def _pad_head_cols(w, slots):
    D = w.shape[0]
    H = len(slots)
    w3 = w.reshape(D, H, HEAD_DIM)
    z = jnp.zeros_like(w3)
    s = jnp.asarray(slots)[None, :, None]
    return jnp.concatenate([jnp.where(s == 0, w3, z), jnp.where(s == 1, w3, z)], axis=2).reshape(D, H * LANE)


def _pad_cols(w, n):
    return jnp.pad(w, ((0, 0), (0, n - w.shape[1])))


def _dot01_r(x, m01):
    hi, mid, lo = _split3(x)
    d = lambda p: jnp.dot(p, m01, preferred_element_type=F32)
    return d(hi) + d(mid) + d(lo)


def _softmax_rows(s, mask):
    s = jnp.where(mask, s, NEG)
    e = jnp.where(mask, jnp.exp(s - jnp.max(s, axis=1, keepdims=True)), 0.0)
    return e * (1.0 / jnp.maximum(jnp.sum(e, axis=1, keepdims=True), 1e-30))


def _flash_rows(s, mask, m_ref, l_ref):
    s = jnp.where(mask, s, NEG)
    m = m_ref[...]
    m_new = jnp.maximum(m, jnp.max(s, axis=1, keepdims=True))
    alpha = jnp.exp(m - m_new)
    e = jnp.where(mask, jnp.exp(s - m_new), 0.0)
    l_ref[...] = alpha * l_ref[...] + jnp.sum(e, axis=1, keepdims=True)
    m_ref[...] = m_new
    return alpha, e


def _sort_key(x):
    bits = lax.bitcast_convert_type(x, jnp.int32)
    return jnp.where(bits < 0, bits ^ 0x7FFFFFFF, bits)


def _page_spec(layer, k, chans, blk):
    return pl.BlockSpec((None, 1, chans, PAGE_SIZE), lambda b, s, pt: (layer, pt[b, s * PG_STEP + k], blk, 0))


def _channel_major(cache):
    layers, n, rows = cache.shape[:3]
    return jnp.moveaxis(cache.reshape(layers, n, rows, -1), 2, 3)


def _per_request(shape):
    return pl.BlockSpec((1,) + shape, lambda b, s, pt: (b,) + (0,) * len(shape))


def _nsa_cmp_body(pt_ref, *refs):
    pages = refs[:PG_STEP]
    pek_ref, pev_ref, wck_ref, wcv_ref, o_ref, stage_ref = refs[PG_STEP:]
    n_half = PG_STEP * PAGE_SIZE // CMP_STRIDE
    for slab, pe_ref, w_ref in ((0, pek_ref, wck_ref), (1, pev_ref, wcv_ref)):
        for pg in range(PG_STEP):
            stage_ref[pg * PAGE_SIZE:(pg + 1) * PAGE_SIZE, :] = pages[pg][0, slab * LANE:(slab + 1) * LANE, :].T
        first = jnp.zeros((n_half, LANE), F32)
        second = jnp.zeros((n_half, LANE), F32)
        for r in range(CMP_STRIDE):
            y = stage_ref[pl.ds(r, n_half, stride=CMP_STRIDE), :]
            first = first + jnp.dot((y + pe_ref[r:r + 1, :]).astype(BF16), w_ref[r], preferred_element_type=F32)
            second = second + jnp.dot((y + pe_ref[r + CMP_STRIDE:r + CMP_STRIDE + 1, :]).astype(BF16),
                                      w_ref[r + CMP_STRIDE], preferred_element_type=F32)
        o_ref[0, :, (2 * slab) * LANE:(2 * slab + 1) * LANE] = first
        o_ref[0, :, (2 * slab + 1) * LANE:(2 * slab + 2) * LANE] = second


def nsa_sample_compress(cache, layer, page_table, pe_k, pe_v, w_ck, w_cv):
    B, n_pages = page_table.shape
    assert n_pages % PG_STEP == 0
    n_half_step = PG_STEP * PAGE_SIZE // CMP_STRIDE
    const = lambda shape: pl.BlockSpec(shape, lambda b, s, pt: (0,) * len(shape))
    grid_spec = pltpu.PrefetchScalarGridSpec(
        num_scalar_prefetch=1,
        grid=(B, n_pages // PG_STEP),
        in_specs=[_page_spec(layer, k, 2 * LANE, 0) for k in range(PG_STEP)]
        + [const((CMP_BLK, LANE)), const((CMP_BLK, LANE)), const((CMP_BLK, LANE, LANE)), const((CMP_BLK, LANE, LANE))],
        out_specs=pl.BlockSpec((1, n_half_step, 4 * LANE), lambda b, s, pt: (b, s, 0)),
        scratch_shapes=[pltpu.VMEM((PG_STEP * PAGE_SIZE, LANE), F32)],
    )
    return pl.pallas_call(
        _nsa_cmp_body,
        grid_spec=grid_spec,
        out_shape=jax.ShapeDtypeStruct((B, n_pages * PAGE_SIZE // CMP_STRIDE, 4 * LANE), F32),
        compiler_params=_params("parallel", "arbitrary"),
        name="nsa_sample_compress",
    )(page_table, *([cache] * PG_STEP), jnp.tile(pe_k, (1, 2)), jnp.tile(pe_v, (1, 2)),
      _block_diag2(w_ck), _block_diag2(w_cv))


def _nsa_smp_body(pt_ref, qs_ref, gl_ref, fs_ref, *refs, n_tok):
    pages = refs[:PG_STEP]
    nrow_ref, pwin_ref, nwin_ref, o_ref, shift_ref, selc_ref, oc_ref, m_ref, l_ref, acc_ref = refs[PG_STEP:]
    s_id = pl.program_id(1)
    C = LANE
    n_half = fs_ref.shape[1]
    n_cmp = n_half - 1
    n_selp = n_half * CMP_STRIDE // SEL_BLK
    qs = qs_ref[0]
    crow = lax.broadcasted_iota(jnp.int32, (C, LANE), 0)
    lane = lax.broadcasted_iota(jnp.int32, (C, LANE), 1)
    t_of_c = crow % n_tok

    @pl.when(s_id == 0)
    def _first():
        cmp = []
        for slab in range(2):
            shift_ref[0:n_half, :] = fs_ref[0, :, (2 * slab + 1) * LANE:(2 * slab + 2) * LANE]
            shift_ref[n_half:n_half + 8, :] = jnp.zeros((8, LANE), F32)
            cmp.append((fs_ref[0, :, (2 * slab) * LANE:(2 * slab + 1) * LANE]
                        + shift_ref[1:n_half + 1, :]).astype(BF16))
        nlane = lax.broadcasted_iota(jnp.int32, (C, n_half), 1)
        p = _softmax_rows(_nt(qs, cmp[0]), nlane < n_cmp)
        oc_ref[...] = jnp.dot(p.astype(BF16), cmp[1], preferred_element_type=F32)
        imp_rows = []
        for g in range(NSA_KV):
            acc = p[(NSA_HPG * g) * n_tok:(NSA_HPG * g + 1) * n_tok, :]
            for h in range(1, NSA_HPG):
                acc = acc + p[(NSA_HPG * g + h) * n_tok:(NSA_HPG * g + h + 1) * n_tok, :]
            imp_rows.append(acc)
        imp = jnp.concatenate(imp_rows + [jnp.zeros((C - NSA_KV * n_tok, n_half), F32)], axis=0)
        pi = lax.broadcasted_iota(jnp.int32, (n_half, n_selp), 0)
        pj = lax.broadcasted_iota(jnp.int32, (n_half, n_selp), 1)
        pool = ((((pi + 1) >> 2) == pj) & (pi < n_cmp)).astype(BF16)
        scoreT = _dot01_r(imp, pool).T
        jcol = lax.broadcasted_iota(jnp.int32, (n_selp, C), 0)
        forced = (jcol == 0) | (jcol == n_selp - 1)
        scoreT = scoreT + jnp.where(forced, FORCE_SCORE, 0.0)
        rank = jnp.zeros((n_selp, C), jnp.int32)
        for j2 in range(n_selp):
            r = scoreT[j2:j2 + 1, :]
            beats = (r > scoreT) | ((r == scoreT) & (jcol > j2))
            rank = rank + beats.astype(jnp.int32)
        sel = jnp.where(rank < TOP_N - 1, 1.0, 0.0).T
        selc_ref[...] = jnp.concatenate(
            [sel[(hh // NSA_HPG) * n_tok:(hh // NSA_HPG + 1) * n_tok, :] for hh in range(NSA_HEADS)]
            + [jnp.zeros((C - NSA_HEADS * n_tok, n_selp), F32)], axis=0)
        m_ref[...] = jnp.full(m_ref.shape, NEG, F32)
        l_ref[...] = jnp.zeros(l_ref.shape, F32)
        acc_ref[...] = jnp.zeros(acc_ref.shape, F32)

    selc = selc_ref[...].astype(BF16)
    n_keys = PG_STEP * PAGE_SIZE
    ej = lax.broadcasted_iota(jnp.int32, (n_selp, n_keys), 0)
    ek = lax.broadcasted_iota(jnp.int32, (n_selp, n_keys), 1)
    expand = (ej == s_id * (n_keys // SEL_BLK) + ek // SEL_BLK).astype(BF16)
    mask = jnp.dot(selc, expand, preferred_element_type=F32) > 0.5
    s = jnp.concatenate([jnp.dot(qs, pages[pg][0, 0:LANE, :].astype(BF16), preferred_element_type=F32)
                         for pg in range(PG_STEP)], axis=1)
    alpha, e = _flash_rows(s, mask, m_ref, l_ref)
    eb = e.astype(BF16)
    pv = _nt(eb[:, 0:PAGE_SIZE], pages[0][0, LANE:2 * LANE, :].astype(BF16))
    for pg in range(1, PG_STEP):
        pv = pv + _nt(eb[:, pg * PAGE_SIZE:(pg + 1) * PAGE_SIZE], pages[pg][0, LANE:2 * LANE, :].astype(BF16))
    acc_ref[...] = alpha * acc_ref[...] + pv

    @pl.when(s_id == pl.num_programs(1) - 1)
    def _last():
        pad = jnp.zeros((LANE - n_tok, LANE), F32)
        new_ok = (lane < n_tok) & (lane <= t_of_c)
        knew = jnp.concatenate([nrow_ref[0, :, 2 * LANE:3 * LANE], pad], axis=0).astype(BF16)
        vnew = jnp.concatenate([nrow_ref[0, :, 3 * LANE:4 * LANE], pad], axis=0).astype(BF16)
        alpha, e = _flash_rows(_nt(qs, knew), new_ok, m_ref, l_ref)
        acc = alpha * acc_ref[...] + jnp.dot(e.astype(BF16), vnew, preferred_element_type=F32)
        o_s = acc * (1.0 / jnp.maximum(l_ref[...], 1e-30))
        n_win = pwin_ref.shape[2]
        kwn = jnp.concatenate([nwin_ref[0, :, 0:LANE], pad], axis=0).astype(BF16)
        vwn = jnp.concatenate([nwin_ref[0, :, LANE:2 * LANE], pad], axis=0).astype(BF16)
        wl = lax.broadcasted_iota(jnp.int32, (C, n_win + LANE), 1)
        tw = lax.broadcasted_iota(jnp.int32, (C, n_win + LANE), 0) % n_tok
        wmask = (((wl < n_win) & ((n_win - wl) + tw < WINDOW))
                 | ((wl >= n_win) & (wl - n_win < n_tok) & (wl - n_win <= tw)))
        s_w = jnp.concatenate([jnp.dot(qs, pwin_ref[0, 0:LANE, :].astype(BF16), preferred_element_type=F32),
                               _nt(qs, kwn)], axis=1)
        pw = _softmax_rows(s_w, wmask).astype(BF16)
        o_w = (_nt(pw[:, 0:n_win], pwin_ref[0, LANE:2 * LANE, :].astype(BF16))
               + jnp.dot(pw[:, n_win:], vwn, preferred_element_type=F32))
        g = jax.nn.sigmoid(gl_ref[0])
        o = g[:, 0:1] * oc_ref[...] + g[:, 1:2] * o_s + g[:, 2:3] * o_w
        valid = (crow < NSA_HEADS * n_tok) & (lane // HEAD_DIM == crow // (n_tok * NSA_HPG))
        o_ref[0] = jnp.where(valid, o, 0.0)


def nsa_sample_attention(qs, gcol, fs, cache, layer, page_table, new_rows, past_win, new_win):
    B, n_pages = page_table.shape
    n_tok = new_rows.shape[1]
    n_half = fs.shape[1]
    n_selp = n_half * CMP_STRIDE // SEL_BLK
    n_win = past_win.shape[3]
    assert n_selp == LANE and NSA_HEADS * n_tok <= LANE and n_tok <= min(SEL_BLK, 8) and n_pages % PG_STEP == 0
    assert n_win % LANE == 0 and n_win <= WINDOW
    grid_spec = pltpu.PrefetchScalarGridSpec(
        num_scalar_prefetch=1,
        grid=(B, n_pages // PG_STEP),
        in_specs=[_per_request((LANE, LANE)), _per_request((LANE, LANE)), _per_request((n_half, 4 * LANE))]
        + [_page_spec(layer, k, 2 * LANE, 1) for k in range(PG_STEP)]
        + [_per_request((n_tok, 4 * LANE)),
           pl.BlockSpec((None, 1, 2 * LANE, n_win), lambda b, s, pt: (layer, b, 0, 0)),
           _per_request((n_tok, 2 * LANE))],
        out_specs=_per_request((LANE, LANE)),
        scratch_shapes=[pltpu.VMEM((n_half + 8, LANE), F32), pltpu.VMEM((LANE, n_selp), F32),
                        pltpu.VMEM((LANE, LANE), F32), pltpu.VMEM((LANE, 1), F32), pltpu.VMEM((LANE, 1), F32),
                        pltpu.VMEM((LANE, LANE), F32)],
    )
    return pl.pallas_call(
        functools.partial(_nsa_smp_body, n_tok=n_tok),
        grid_spec=grid_spec,
        out_shape=jax.ShapeDtypeStruct((B, LANE, LANE), F32),
        compiler_params=_params("parallel", "arbitrary"),
        name="nsa_sample_attention",
    )(page_table, qs, gcol, fs, *([cache] * PG_STEP), new_rows, past_win, new_win)


def _dsa_idx_body(pt_ref, qe_ref, qo_ref, w_ref, *refs, n_tok, n_keep):
    pages = refs[:PG_STEP]
    knew_ref, sc_ref, th_ref = refs[PG_STEP:]
    s_id = pl.program_id(1)
    n_blk = sc_ref.shape[1]
    qe = qe_ref[0]
    qo = qo_ref[0]
    w = w_ref[0] * (IDX_HEADS ** -0.5)
    n_rows = (IDX_HEADS // 2) * n_tok

    def scores(s_e, s_o):
        s = jnp.maximum(s_e, 0.0) * w[:, 0:1] + jnp.maximum(s_o, 0.0) * w[:, 1:2]
        tot = s[0:n_tok, :]
        for p in range(1, IDX_HEADS // 2):
            tot = tot + s[p * n_tok:(p + 1) * n_tok, :]
        return tot

    for pg in range(PG_STEP):
        kT = pages[pg][0].astype(BF16)
        sc_ref[0, s_id * PG_STEP + pg] = scores(jnp.dot(qe, kT, preferred_element_type=F32),
                                                jnp.dot(qo, kT, preferred_element_type=F32))

    @pl.when(s_id == pl.num_programs(1) - 1)
    def _last():
        kn = jnp.concatenate([knew_ref[0], jnp.zeros((LANE - n_tok, IDX_DIM), F32)], axis=0).astype(BF16)
        a_i = lax.broadcasted_iota(jnp.int32, (n_tok, LANE), 1)
        t_i = lax.broadcasted_iota(jnp.int32, (n_tok, LANE), 0)
        sn = jnp.where(a_i <= t_i, scores(_nt(qe, kn), _nt(qo, kn)), NEG)
        sc_ref[0, n_blk - 1] = jnp.where(a_i < n_tok, sn, -jnp.inf)
        keys = _sort_key(sc_ref[0])

        def bit_step(it, theta):
            cand = theta + lax.shift_left(jnp.int32(1), 31 - it)
            cnt = jnp.sum(jnp.sum((keys >= cand).astype(jnp.int32), axis=0), axis=1, keepdims=True)
            return jnp.where(cnt >= n_keep, cand, theta)

        theta = lax.fori_loop(0, 32, bit_step, jnp.full((n_tok, 1), INT_MIN, jnp.int32))
        th_ref[0] = jnp.broadcast_to(theta, (n_tok, LANE))


def dsa_sample_index(qe, qo, wcol, cache_idx, layer, page_table, ki_new):
    B, n_pages = page_table.shape
    n_tok = ki_new.shape[1]
    assert n_tok == 8 and n_pages % PG_STEP == 0
    n_keep = min(IDX_TOPK, (n_pages * PAGE_SIZE + n_tok) // 4)
    grid_spec = pltpu.PrefetchScalarGridSpec(
        num_scalar_prefetch=1,
        grid=(B, n_pages // PG_STEP),
        in_specs=[_per_request((LANE, IDX_DIM)), _per_request((LANE, IDX_DIM)), _per_request((LANE, LANE))]
        + [_page_spec(layer, k, IDX_DIM, 0) for k in range(PG_STEP)] + [_per_request((n_tok, IDX_DIM))],
        out_specs=[_per_request((n_pages + 1, n_tok, LANE)), _per_request((n_tok, LANE))],
    )
    return pl.pallas_call(
        functools.partial(_dsa_idx_body, n_tok=n_tok, n_keep=n_keep),
        grid_spec=grid_spec,
        out_shape=[jax.ShapeDtypeStruct((B, n_pages + 1, n_tok, LANE), F32),
                   jax.ShapeDtypeStruct((B, n_tok, LANE), jnp.int32)],
        compiler_params=_params("parallel", "arbitrary"),
        name="dsa_sample_index",
    )(page_table, qe, qo, wcol, *([cache_idx] * PG_STEP), ki_new)


def _dsa_smp_body(pt_ref, qs_ref, sc_ref, th_ref, *refs, n_tok):
    pages = refs[:PG_STEP]
    kvn_ref, o_ref, m_ref, l_ref, acc_ref = refs[PG_STEP:]
    s_id = pl.program_id(1)
    C = LANE
    qs = qs_ref[0]
    crow = lax.broadcasted_iota(jnp.int32, (C, LANE), 0)
    lane = lax.broadcasted_iota(jnp.int32, (C, LANE), 1)
    low = crow < C // 2
    theta = th_ref[0]

    @pl.when(s_id == 0)
    def _():
        m_ref[...] = jnp.full(m_ref.shape, NEG, F32)
        l_ref[...] = jnp.zeros(l_ref.shape, F32)
        acc_ref[...] = jnp.zeros(acc_ref.shape, F32)

    def step(kvTs, blk0, extra):
        kbs = [kvT.astype(BF16) for kvT in kvTs]
        keep = jnp.concatenate([jnp.where(_sort_key(sc_ref[0, blk0 + j]) >= theta, 1.0, 0.0)
                                for j in range(len(kbs))], axis=1)
        mask = jnp.concatenate([keep] * (C // n_tok), axis=0) > 0.5
        if extra is not None:
            mask = mask & extra
        s_lo = jnp.concatenate([jnp.dot(qs, kb[0:LANE], preferred_element_type=F32) for kb in kbs], axis=1)
        s_hi = jnp.concatenate([jnp.dot(qs, kb[LANE:2 * LANE], preferred_element_type=F32) for kb in kbs], axis=1)
        low_col = lax.broadcasted_iota(jnp.int32, (C, 1), 0) < C // 2
        alpha, e = _flash_rows(jnp.where(low_col, s_lo, s_hi), mask, m_ref, l_ref)
        eb = e.astype(BF16)
        pv_lo = pv_hi = None
        for j, kb in enumerate(kbs):
            ej = eb[:, j * LANE:(j + 1) * LANE]
            a, b = _nt(ej, kb[2 * LANE:3 * LANE]), _nt(ej, kb[3 * LANE:4 * LANE])
            pv_lo, pv_hi = (a, b) if pv_lo is None else (pv_lo + a, pv_hi + b)
        acc_ref[...] = alpha * acc_ref[...] + jnp.where(low, pv_lo, pv_hi)

    step([pages[pg][0] for pg in range(PG_STEP)], s_id * PG_STEP, None)

    @pl.when(s_id == pl.num_programs(1) - 1)
    def _last():
        kvn = jnp.concatenate([kvn_ref[0], jnp.zeros((LANE - n_tok, 4 * LANE), F32)], axis=0)
        kvnT = jnp.concatenate([kvn[:, j * LANE:(j + 1) * LANE].T for j in range(4)], axis=0)
        step([kvnT], sc_ref.shape[1] - 1, (lane < n_tok) & (lane <= crow % n_tok))
        o = acc_ref[...] * (1.0 / jnp.maximum(l_ref[...], 1e-30))
        valid = lane // HEAD_DIM == (crow // (n_tok * DSA_HPG)) % 2
        o_ref[0] = jnp.where(valid, o, 0.0)


def dsa_sample_attention(qs, scores, theta, cache_kv, layer, page_table, kv_new):
    B, n_pages = page_table.shape
    n_tok = kv_new.shape[1]
    assert DSA_HEADS * n_tok == LANE and n_pages % PG_STEP == 0
    grid_spec = pltpu.PrefetchScalarGridSpec(
        num_scalar_prefetch=1,
        grid=(B, n_pages // PG_STEP),
        in_specs=[_per_request((LANE, LANE)), _per_request((n_pages + 1, n_tok, LANE)), _per_request((n_tok, LANE))]
        + [_page_spec(layer, k, 4 * LANE, 0) for k in range(PG_STEP)] + [_per_request((n_tok, 4 * LANE))],
        out_specs=_per_request((LANE, LANE)),
        scratch_shapes=[pltpu.VMEM((LANE, 1), F32), pltpu.VMEM((LANE, 1), F32), pltpu.VMEM((LANE, LANE), F32)],
    )
    return pl.pallas_call(
        functools.partial(_dsa_smp_body, n_tok=n_tok),
        grid_spec=grid_spec,
        out_shape=jax.ShapeDtypeStruct((B, LANE, LANE), F32),
        compiler_params=_params("parallel", "arbitrary"),
        name="dsa_sample_attention",
    )(page_table, qs, scores, theta, *([cache_kv] * PG_STEP), kv_new)


_NSA_SLOTS = tuple(h // NSA_HPG for h in range(NSA_HEADS))
_DSA_SLOTS = tuple((h // DSA_HPG) % 2 for h in range(DSA_HEADS))
def _even_segs(q_dtype, cm):
    return ((NSA_HEADS * LANE, (1,) * NSA_HEADS, q_dtype, False), (4 * LANE, (1, 0, 1, 0), F32, cm),
            (2 * LANE, (1, 0), F32, False), (LANE, (0,), F32, False),
            (4 * HGRN_KW, (0,) * (4 * HGRN_KW // LANE), F32, False))


def _odd_segs(q_dtype, cm):
    return ((DSA_HEADS * LANE, (1,) * DSA_HEADS, q_dtype, False), (4 * LANE, (1, 1, 0, 0), F32, cm),
            (IDX_QW, (1,) * 4, q_dtype, False), (LANE, (1,), F32, cm), (LANE, (0,), F32, False))


def _even_weights(w_in, w_out):
    c = np.cumsum([0, NSA_QW] + [NSA_KVW] * 6 + [NSA_GW] + [HGRN_KW] * 4)
    w = jnp.concatenate([
        _pad_head_cols(w_in[:, :NSA_QW] * HEAD_DIM ** -0.5, _NSA_SLOTS),
        w_in[:, c[1]:c[5]], w_in[:, c[5]:c[7]], _pad_cols(w_in[:, c[7]:c[8]], LANE), w_in[:, c[8]:c[12]]],
        axis=1).astype(BF16)
    w_outs = [_pad_head_cols(w_out[:NSA_QW].T, _NSA_SLOTS).T.astype(BF16), w_out[NSA_QW:].astype(BF16)]
    return w, w_outs


def _odd_weights(w_in, w_out):
    c = np.cumsum([0, DSA_QW, DSA_KVW, DSA_KVW, IDX_QW, IDX_DIM, IDX_HEADS])
    w = jnp.concatenate([
        _pad_head_cols(w_in[:, :DSA_QW] * HEAD_DIM ** -0.5, _DSA_SLOTS),
        w_in[:, c[1]:c[3]], w_in[:, c[3]:c[4]] * IDX_DIM ** -0.5,
        _pad_cols(w_in[:, c[4]:c[5]], LANE), _pad_cols(w_in[:, c[5]:c[6]], LANE)], axis=1).astype(BF16)
    return w, [_pad_head_cols(w_out.T, _DSA_SLOTS).T.astype(BF16)]


def even_mixer_prompt(x, scale, shift, cos_t, sin_t, w_in, w_out, pe_k, pe_v, w_ck, w_cv, lb, norm_g, win_len):
    B, T, _ = x.shape
    assert T >= win_len
    w, w_outs = _even_weights(w_in, w_out)
    qp, rows, win, gl, hx, rows_cm = proj_segments(x, scale, shift, cos_t, sin_t, w, _even_segs(BF16, True))
    o_a = nsa_prompt(qp, rows, win, gl, pe_k, pe_v, w_ck, w_cv)
    o_b, s_new = hgrn_mixer(hx, lb, norm_g, jnp.zeros((B, HGRN_HEADS, HGRN_DK, HGRN_DV), F32))
    new_rows = jnp.moveaxis(rows_cm.reshape(B, 4, NSA_KV, HEAD_DIM, T), 4, 1)
    win_state = win[:, T - win_len:].reshape(B, win_len, 2, NSA_KV, HEAD_DIM)
    return [o_a, o_b], w_outs, new_rows, win_state, s_new


def odd_mixer_prompt(x, scale, shift, cos_t, sin_t, w_in, w_out):
    B, T, _ = x.shape
    w, w_outs = _odd_weights(w_in, w_out)
    qp, kv, qi, ki, wi, kv_cm, ki_cm = proj_segments(x, scale, shift, cos_t, sin_t, w, _odd_segs(BF16, True))
    o = dsa_prompt(qp, kv, qi, ki, wi)
    new_kv = jnp.moveaxis(kv_cm.reshape(B, 2, DSA_KV, HEAD_DIM, T), 4, 1)
    return [o], w_outs, new_kv, jnp.moveaxis(ki_cm[:, :IDX_DIM], 2, 1)


def _stack_rows(a, B, n, heads, width):
    s = a.reshape(B, n, heads, width).transpose(0, 2, 1, 3).reshape(B, heads * n, width)
    return jnp.pad(s, ((0, 0), (0, LANE - heads * n), (0, 0)))


def _unstack_rows(o, B, n, heads):
    return o[:, :heads * n].reshape(B, heads, n, LANE).transpose(0, 2, 1, 3).reshape(1, B * n, heads * LANE)


def even_mixer_sample(x, scale, shift, past_len, B, w_in, w_out, pe_k, pe_v, w_ck, w_cv, lb, norm_g,
                      cache, cache_win, state, page_table, layer):
    n = x.shape[1] // B
    win_len = cache_win.shape[2]
    cos_t, sin_t = [jnp.tile(a, (B, 1)) for a in rope_tables(past_len + jnp.arange(n))]
    w, w_outs = _even_weights(w_in, w_out)
    qp, rows, win, gl, hx = proj_segments(x, scale, shift, cos_t, sin_t, w, _even_segs(F32, False))
    rows, win, hx = [a.reshape(B, n, a.shape[-1]) for a in (rows, win, hx)]
    qs = _stack_rows(qp, B, n, NSA_HEADS, LANE).astype(BF16)
    gcol = jnp.pad(_stack_rows(gl[..., :NSA_GW], B, n, NSA_HEADS, 3), ((0, 0), (0, 0), (0, LANE - 3)))
    cache_cm = _channel_major(cache)
    fs = nsa_sample_compress(cache_cm, layer, page_table, pe_k, pe_v, w_ck, w_cv)
    o = nsa_sample_attention(qs, gcol, fs, cache_cm, layer, page_table, rows, _channel_major(cache_win), win)
    o_a = _unstack_rows(o, B, n, NSA_HEADS)
    o_b, s_new = hgrn_mixer(hx, lb, norm_g, state)
    new_rows = rows.reshape(B, n, 4, NSA_KV, HEAD_DIM)
    win_state = jnp.concatenate([cache_win[layer], win.reshape(B, n, 2, NSA_KV, HEAD_DIM)], axis=1)[:, -win_len:]
    return [o_a, o_b.reshape(1, B * n, HGRN_VW)], w_outs, new_rows, win_state, s_new


def odd_mixer_sample(x, scale, shift, past_len, B, w_in, w_out, cache_kv, cache_idx, page_table, layer):
    n = x.shape[1] // B
    cos_t, sin_t = [jnp.tile(a, (B, 1)) for a in rope_tables(past_len + jnp.arange(n))]
    w, w_outs = _odd_weights(w_in, w_out)
    qp, kv, qi, ki, wi = proj_segments(x, scale, shift, cos_t, sin_t, w, _odd_segs(F32, False))
    kv, ki = kv.reshape(B, n, 4 * LANE), ki.reshape(B, n, LANE)[..., :IDX_DIM]
    qs = _stack_rows(qp, B, n, DSA_HEADS, LANE).astype(BF16)
    qi4 = qi.reshape(1, B * n, IDX_HEADS // 2, 2, IDX_DIM)
    qe = _stack_rows(qi4[:, :, :, 0], B, n, IDX_HEADS // 2, IDX_DIM).astype(BF16)
    qo = _stack_rows(qi4[:, :, :, 1], B, n, IDX_HEADS // 2, IDX_DIM).astype(BF16)
    wcol = jnp.pad(_stack_rows(wi[..., :IDX_HEADS], B, n, IDX_HEADS // 2, 2), ((0, 0), (0, 0), (0, LANE - 2)))
    scores, theta = dsa_sample_index(qe, qo, wcol, _channel_major(cache_idx), layer, page_table, ki)
    o = dsa_sample_attention(qs, scores, theta, _channel_major(cache_kv), layer, page_table, kv)
    return [_unstack_rows(o, B, n, DSA_HEADS)], w_outs, kv.reshape(B, n, 2, DSA_KV, HEAD_DIM), ki


def moe_ffn_residual(xs, hs, logits_list, gates, w1, w3, w2):
    D = D_MODEL
    h_all = jnp.concatenate([h.reshape(-1, D) for h in hs], axis=0)
    logits = jnp.concatenate([lg.reshape(-1, lg.shape[-1])[:, :N_EXPERTS] for lg in logits_list], axis=0)
    n_tok = h_all.shape[0]
    top_v, top_i = lax.top_k(logits, TOP_K)
    weights = jax.nn.softmax(top_v, axis=-1)
    e_flat = top_i.reshape(-1)
    onehot = (e_flat[:, None] == jnp.arange(N_EXPERTS)[None, :]).astype(jnp.int32)
    csum = jnp.cumsum(onehot, axis=0)
    counts = csum[-1]
    rank = jnp.take_along_axis(csum, e_flat[:, None], axis=1)[:, 0] - 1
    padded = ((counts + MOE_TILE - 1) // MOE_TILE) * MOE_TILE
    group_end = jnp.cumsum(padded)
    group_start = group_end - padded
    slot = group_start[e_flat] + rank
    n_slots = _round_up(n_tok * TOP_K, MOE_TILE) + N_EXPERTS * MOE_TILE
    n_tiles = n_slots // MOE_TILE
    tok_of_slot = jnp.zeros((n_slots,), jnp.int32).at[slot].set(jnp.arange(n_tok * TOP_K, dtype=jnp.int32) // TOP_K)
    w_slot = jnp.zeros((n_slots,), F32).at[slot].set(weights.reshape(-1))
    tile_start = jnp.arange(n_tiles, dtype=jnp.int32) * MOE_TILE
    tile_expert = jnp.minimum(jnp.sum(tile_start[:, None] >= group_end[None, :], axis=1), N_EXPERTS - 1).astype(jnp.int32)
    n_used = (group_end[-1] // MOE_TILE).astype(jnp.int32).reshape(1)
    h_sorted = h_all[tok_of_slot]
    y_slot = moe_grouped_swiglu(h_sorted, w_slot.reshape(n_slots, 1), tile_expert, n_used, w1, w3, w2)
    slot2 = slot.reshape(n_tok, TOP_K)
    outs = []
    off = 0
    for x, g in zip(xs, gates):
        n = x.shape[0] * x.shape[1]
        ya = y_slot[slot2[off:off + n, 0]].reshape(x.shape)
        yb = y_slot[slot2[off:off + n, 1]].reshape(x.shape)
        outs.append(moe_combine_residual(x, g, ya, yb))
        off += n
    return outs


def kernel(x_prompt, x_sample, cache_nsa, cache_nsa_win, state_hgrn, cache_dsa_kv, cache_dsa_idx, page_table, c_prompt, c_sample, ada_w, ada_b, norm1_g, norm2_g, final_g, even_w_in, even_w_out, nsa_pe_k, nsa_pe_v, nsa_w_ck, nsa_w_cv, hgrn_lb_raw, hgrn_norm_g, ffn_w1, ffn_w3, ffn_w2, odd_w_in, odd_w_out, router_w, router_b, moe_w1, moe_w3, moe_w2):
    D = D_MODEL
    past_len = page_table.shape[1] * PAGE_SIZE
    win_len = cache_nsa_win.shape[2]
    Bp, Tp = x_prompt.shape[:2]
    Bs, Ts = x_sample.shape[:2]
    lb_soft = jax.nn.softmax(hgrn_lb_raw.astype(F32), axis=0)
    lower_bounds = jnp.cumsum(lb_soft, axis=0) - lb_soft[0]

    R = _round_up(Bp + Bs, 8)
    c_all = jnp.zeros((R, D), F32).at[:Bp].set(c_prompt).at[Bp:Bp + Bs].set(c_sample)
    mods = ada_modulation(c_all, ada_w, ada_b)

    def group_mods(l, lo, n, per_token_rows):
        m = mods[l, lo:lo + n].reshape(n, 6, D)
        sh1, sc1, g1, sh2, sc2, g2 = [m[:, j] for j in range(6)]
        s1 = norm1_g[l][None] * (1.0 + sc1)
        s2 = norm2_g[l][None] * (1.0 + sc2)
        vecs = [s1, sh1, g1, s2, sh2, g2]
        if per_token_rows:
            return [jnp.repeat(v, per_token_rows, axis=0)[None] for v in vecs]
        return [v[:, None, :] for v in vecs]

    def pad_cols(w, n):
        return jnp.pad(w, ((0, 0), (0, n - w.shape[1]))).astype(BF16)

    xp = x_prompt
    xs = x_sample.reshape(1, Bs * Ts, D)
    cos_p, sin_p = rope_tables(jnp.arange(Tp))
    outs_p = dict(rows=[], win=[], st=[], kv=[], idx=[])
    outs_s = dict(rows=[], win=[], st=[], kv=[], idx=[])
    for l in range(DEPTH):
        i = l // 2
        mp = group_mods(l, 0, Bp, 0)
        msm = group_mods(l, Bp, Bs, Ts)
        if l % 2 == 0:
            w1, w3, w2 = ffn_w1[i].astype(BF16), ffn_w3[i].astype(BF16), ffn_w2[i].astype(BF16)
            mixed, w_outs, rows, win, s_new = even_mixer_prompt(
                xp, mp[0], mp[1], cos_p, sin_p, even_w_in[i], even_w_out[i], nsa_pe_k[i], nsa_pe_v[i],
                nsa_w_ck[i], nsa_w_cv[i], lower_bounds[i], hgrn_norm_g[i], win_len)
            outs_p['rows'].append(rows)
            outs_p['win'].append(win)
            outs_p['st'].append(s_new)
            xp = out_proj_residual(mixed, w_outs, xp, mp[2])
            xp = ffn_residual(xp, mp[3], mp[4], mp[5], w1, w3, w2)
            mixed, w_outs, rows, win, s_new = even_mixer_sample(
                xs, msm[0], msm[1], past_len, Bs, even_w_in[i], even_w_out[i], nsa_pe_k[i], nsa_pe_v[i],
                nsa_w_ck[i], nsa_w_cv[i], lower_bounds[i], hgrn_norm_g[i],
                cache_nsa, cache_nsa_win, state_hgrn[i], page_table, i)
            outs_s['rows'].append(rows)
            outs_s['win'].append(win)
            outs_s['st'].append(s_new)
            xs = out_proj_residual(mixed, w_outs, xs, msm[2])
            xs = ffn_residual(xs, msm[3], msm[4], msm[5], w1, w3, w2)
        else:
            w_r = pad_cols(router_w[i], LANE)
            w1, w3, w2 = moe_w1[i].astype(BF16), moe_w3[i].astype(BF16), moe_w2[i].astype(BF16)
            o, w_outs, kv, ki = odd_mixer_prompt(xp, mp[0], mp[1], cos_p, sin_p, odd_w_in[i], odd_w_out[i])
            outs_p['kv'].append(kv)
            outs_p['idx'].append(ki)
            xp = out_proj_residual(o, w_outs, xp, mp[2])
            o, w_outs, kv, ki = odd_mixer_sample(xs, msm[0], msm[1], past_len, Bs, odd_w_in[i], odd_w_out[i],
                                                 cache_dsa_kv, cache_dsa_idx, page_table, i)
            outs_s['kv'].append(kv)
            outs_s['idx'].append(ki)
            xs = out_proj_residual(o, w_outs, xs, msm[2])
            hs, lgs = [], []
            for (x, m) in ((xp, mp), (xs, msm)):
                logits, h = norm_proj(x, m[3], m[4], w_r, with_h=True)
                lgs.append(logits[..., :N_EXPERTS] + router_b[i].astype(F32))
                hs.append(h)
            xp, xs = moe_ffn_residual([xp, xs], hs, lgs, [mp[5], msm[5]], w1, w3, w2)
    y_prompt = final_norm(xp, final_g)
    y_sample = final_norm(xs, final_g).reshape(Bs, Ts, D)
    st = lambda od, k: jnp.stack(od[k])
    return (y_prompt, y_sample, st(outs_p, 'rows'), st(outs_s, 'rows'), st(outs_p, 'win'), st(outs_s, 'win'),
            st(outs_p, 'st'), st(outs_s, 'st'), st(outs_p, 'kv'), st(outs_s, 'kv'),
            st(outs_p, 'idx'), st(outs_s, 'idx'))
```

```python
import functools
import math

import numpy as np
import jax
import jax.numpy as jnp
from jax import lax
from jax.experimental import pallas as pl
from jax.experimental.pallas import tpu as pltpu

D_MODEL = 1024
DEPTH = 4
PAGE_SIZE = 128
HEAD_DIM = 64
NSA_HEADS = 8
NSA_KV = 2
NSA_HPG = NSA_HEADS // NSA_KV
CMP_STRIDE = 16
CMP_BLK = 2 * CMP_STRIDE
SEL_BLK = 64
TOP_N = 16
WINDOW = 512
FORCE_SCORE = 1.0e4
HGRN_HEADS = 4
HGRN_DK = 128
HGRN_DV = 128
HGRN_CHUNK = 64
DSA_HEADS = 16
DSA_KV = 4
DSA_HPG = DSA_HEADS // DSA_KV
IDX_HEADS = 8
IDX_DIM = 64
IDX_TOPK = 256
D_FF = 2816
N_EXPERTS = 8
TOP_K = 2
Q_BLK = 128
ROPE_THETA = 10000.0
EPS = 1e-6
NEG = -1.0e30
NSA_QW = NSA_HEADS * HEAD_DIM
NSA_KVW = NSA_KV * HEAD_DIM
NSA_GW = NSA_HEADS * 3
HGRN_KW = HGRN_HEADS * HGRN_DK
HGRN_VW = HGRN_HEADS * HGRN_DV
EVEN_IN = NSA_QW + 6 * NSA_KVW + NSA_GW + 2 * HGRN_KW + 2 * HGRN_VW
EVEN_MIX = NSA_QW + HGRN_VW
DSA_QW = DSA_HEADS * HEAD_DIM
DSA_KVW = DSA_KV * HEAD_DIM
IDX_QW = IDX_HEADS * IDX_DIM
ODD_IN = DSA_QW + 2 * DSA_KVW + IDX_QW + IDX_DIM + IDX_HEADS
ODD_MIX = DSA_QW

LANE = 128
ROW_TILE = 512
FF_CHUNK = 256
MOE_TILE = 256
PROJ_CHUNK = 512
ATT_TQ = 128
KEY_BLK = 256
HGRN_SUB = 16
HGRN_TILE = 256
PG_STEP = 8
INT_MIN = -2 ** 31
M_FLOOR = -1.0e25
VMEM_LIMIT = 56 * 1024 * 1024

F32 = jnp.float32
BF16 = jnp.bfloat16


def _round_up(n, m):
    return -(-n // m) * m


def _params(*sem):
    return pltpu.CompilerParams(dimension_semantics=sem, vmem_limit_bytes=VMEM_LIMIT)


def _norm_mod(x, scale, shift):
    ms = jnp.mean(x * x, axis=-1, keepdims=True)
    return x * lax.rsqrt(ms + EPS) * scale + shift


def _mod_spec(mod, tm):
    if mod.shape[1] == 1:
        return pl.BlockSpec((1, 1, mod.shape[2]), lambda b, i: (b, 0, 0))
    return pl.BlockSpec((1, tm, mod.shape[2]), lambda b, i: (b, i, 0))


def _ada_body(c_ref, w_ref, b_ref, o_ref):
    c = c_ref[...]
    cs = (c * jax.nn.sigmoid(c)).astype(BF16)
    o_ref[0] = jnp.dot(cs, w_ref[0].astype(BF16), preferred_element_type=F32) + b_ref[0]


def ada_modulation(c_all, ada_w, ada_b):
    R, D = c_all.shape
    N = ada_w.shape[2]
    tn = 1536
    return pl.pallas_call(
        _ada_body,
        grid=(DEPTH, N // tn),
        in_specs=[pl.BlockSpec((R, D), lambda l, j: (0, 0)),
                  pl.BlockSpec((1, D, tn), lambda l, j: (l, 0, j)),
                  pl.BlockSpec((1, 1, tn), lambda l, j: (l, 0, j))],
        out_specs=pl.BlockSpec((1, R, tn), lambda l, j: (l, 0, j)),
        out_shape=jax.ShapeDtypeStruct((DEPTH, R, N), F32),
        compiler_params=_params("arbitrary", "arbitrary"),
        name="ada_modulation",
    )(c_all, ada_w, ada_b.reshape(DEPTH, 1, N))


def _norm_proj_body(x_ref, sc_ref, sh_ref, w_ref, o_ref, h_ref=None):
    h = _norm_mod(x_ref[0], sc_ref[0], sh_ref[0]).astype(BF16)
    o_ref[0] = jnp.dot(h, w_ref[...], preferred_element_type=F32)
    if h_ref is not None:
        h_ref[0] = h


def norm_proj(x, scale, shift, w, with_h=False):
    B, T, D = x.shape
    N = w.shape[1]
    tm = min(T, ROW_TILE)
    out_shape = [jax.ShapeDtypeStruct((B, T, N), F32)]
    out_specs = [pl.BlockSpec((1, tm, N), lambda b, i: (b, i, 0))]
    if with_h:
        out_shape.append(jax.ShapeDtypeStruct((B, T, D), BF16))
        out_specs.append(pl.BlockSpec((1, tm, D), lambda b, i: (b, i, 0)))
    res = pl.pallas_call(
        _norm_proj_body,
        grid=(B, T // tm),
        in_specs=[pl.BlockSpec((1, tm, D), lambda b, i: (b, i, 0)),
                  _mod_spec(scale, tm), _mod_spec(shift, tm),
                  pl.BlockSpec((D, N), lambda b, i: (0, 0))],
        out_specs=out_specs,
        out_shape=out_shape,
        compiler_params=_params("parallel", "parallel"),
        name="norm_proj",
    )(x, scale, shift, w)
    return res if with_h else res[0]


def _rope_chunk(y, cos, sin):
    lane = lax.broadcasted_iota(jnp.int32, y.shape, 1)
    swapped = jnp.where(lane % HEAD_DIM < HEAD_DIM // 2,
                        pltpu.roll(y, LANE - HEAD_DIM // 2, 1), pltpu.roll(y, HEAD_DIM // 2, 1))
    return y * cos + swapped * sin


def _proj_seg_body(x_ref, sc_ref, sh_ref, cos_ref, sin_ref, w_ref, *o_refs, segs):
    h = _norm_mod(x_ref[0], sc_ref[0], sh_ref[0]).astype(BF16)
    cos = cos_ref[...]
    sin = sin_ref[...]
    t_refs = iter(o_refs[len(segs):])
    c0 = 0
    for o_ref, (width, rope_flags, dtype, channel_major) in zip(o_refs, segs):
        t_ref = next(t_refs) if channel_major else None
        for j0 in range(0, width, PROJ_CHUNK):
            wd = min(PROJ_CHUNK, width - j0)
            y = jnp.dot(h, w_ref[:, c0 + j0:c0 + j0 + wd], preferred_element_type=F32)
            for k in range(wd // LANE):
                cols = slice(j0 + k * LANE, j0 + (k + 1) * LANE)
                yk = y[:, k * LANE:(k + 1) * LANE]
                if rope_flags[(j0 + k * LANE) // LANE]:
                    yk = _rope_chunk(yk, cos, sin)
                o_ref[0, :, cols] = yk.astype(dtype)
                if t_ref is not None:
                    t_ref[0, cols, :] = yk.T
        c0 += width


def proj_segments(x, scale, shift, cos_t, sin_t, w, segs):
    B, T, D = x.shape
    tm = min(T, ROW_TILE)
    N = w.shape[1]
    cm = [s for s in segs if s[3]]
    return pl.pallas_call(
        functools.partial(_proj_seg_body, segs=segs),
        grid=(B, T // tm),
        in_specs=[pl.BlockSpec((1, tm, D), lambda b, i: (b, i, 0)),
                  _mod_spec(scale, tm), _mod_spec(shift, tm),
                  pl.BlockSpec((tm, LANE), lambda b, i: (i, 0)),
                  pl.BlockSpec((tm, LANE), lambda b, i: (i, 0)),
                  pl.BlockSpec((D, N), lambda b, i: (0, 0))],
        out_specs=[pl.BlockSpec((1, tm, s[0]), lambda b, i: (b, i, 0)) for s in segs]
        + [pl.BlockSpec((1, s[0], tm), lambda b, i: (b, 0, i)) for s in cm],
        out_shape=[jax.ShapeDtypeStruct((B, T, s[0]), s[2]) for s in segs]
        + [jax.ShapeDtypeStruct((B, s[0], T), F32) for s in cm],
        compiler_params=_params("parallel", "parallel"),
        name="proj_segments",
    )(x, scale, shift, cos_t, sin_t, w)


def rope_tables(pos):
    half = HEAD_DIM // 2
    inv = ROPE_THETA ** (-jnp.arange(half, dtype=F32) / half)
    ang = pos.astype(F32)[:, None] * inv[None, :]
    cos, sin = jnp.cos(ang), jnp.sin(ang)
    return jnp.tile(cos, (1, 4)), jnp.tile(jnp.concatenate([-sin, sin], axis=1), (1, 2))


def _out_proj_body(*refs, n_in):
    a_refs, w_refs = refs[:n_in], refs[n_in:2 * n_in]
    x_ref, g_ref, o_ref = refs[2 * n_in:]
    y = None
    for a_ref, w_ref in zip(a_refs, w_refs):
        part = jnp.dot(a_ref[0].astype(BF16), w_ref[...], preferred_element_type=F32)
        y = part if y is None else y + part
    o_ref[0] = x_ref[0] + g_ref[0] * y


def out_proj_residual(a_list, w_list, x, gate):
    B, T, D = x.shape
    tm = min(T, ROW_TILE)
    n_in = len(a_list)
    return pl.pallas_call(
        functools.partial(_out_proj_body, n_in=n_in),
        grid=(B, T // tm),
        in_specs=[pl.BlockSpec((1, tm, a.shape[2]), lambda b, i: (b, i, 0)) for a in a_list]
        + [pl.BlockSpec(w.shape, lambda b, i: (0, 0)) for w in w_list]
        + [pl.BlockSpec((1, tm, D), lambda b, i: (b, i, 0)), _mod_spec(gate, tm)],
        out_specs=pl.BlockSpec((1, tm, D), lambda b, i: (b, i, 0)),
        out_shape=jax.ShapeDtypeStruct((B, T, D), F32),
        compiler_params=_params("parallel", "parallel"),
        name="out_proj_residual",
    )(*a_list, *w_list, x, gate)


def _swiglu_acc(h, w1_ref, w3_ref, w2_ref, acc_ref):
    for c in range(D_FF // FF_CHUNK):
        cols = slice(c * FF_CHUNK, (c + 1) * FF_CHUNK)
        u = jnp.dot(h, w1_ref[:, cols], preferred_element_type=F32)
        v = jnp.dot(h, w3_ref[:, cols], preferred_element_type=F32)
        a = (u * jax.nn.sigmoid(u) * v).astype(BF16)
        part = jnp.dot(a, w2_ref[cols, :], preferred_element_type=F32)
        if c == 0:
            acc_ref[...] = part
        else:
            acc_ref[...] += part


def _ffn_body(x_ref, sc_ref, sh_ref, g_ref, w1_ref, w3_ref, w2_ref, o_ref, acc_ref):
    x = x_ref[0]
    h = _norm_mod(x, sc_ref[0], sh_ref[0]).astype(BF16)
    _swiglu_acc(h, w1_ref, w3_ref, w2_ref, acc_ref)
    o_ref[0] = x + g_ref[0] * acc_ref[...]


def ffn_residual(x, scale, shift, gate, w1, w3, w2):
    B, T, D = x.shape
    tm = min(T, ROW_TILE)
    wspec = lambda shape: pl.BlockSpec(shape, lambda b, i: (0, 0))
    return pl.pallas_call(
        _ffn_body,
        grid=(B, T // tm),
        in_specs=[pl.BlockSpec((1, tm, D), lambda b, i: (b, i, 0)),
                  _mod_spec(scale, tm), _mod_spec(shift, tm), _mod_spec(gate, tm),
                  wspec((D, D_FF)), wspec((D, D_FF)), wspec((D_FF, D))],
        out_specs=pl.BlockSpec((1, tm, D), lambda b, i: (b, i, 0)),
        out_shape=jax.ShapeDtypeStruct((B, T, D), F32),
        scratch_shapes=[pltpu.VMEM((tm, D), F32)],
        compiler_params=_params("parallel", "parallel"),
        name="ffn_residual",
    )(x, scale, shift, gate, w1, w3, w2)


def _moe_body(te_ref, nt_ref, h_ref, w1_ref, w3_ref, w2_ref, o_ref, acc_ref):
    i = pl.program_id(0)

    @pl.when(i < nt_ref[0])
    def _():
        _swiglu_acc(h_ref[...], w1_ref.at[0], w3_ref.at[0], w2_ref.at[0], acc_ref)
        o_ref[...] = acc_ref[...]

    @pl.when(i >= nt_ref[0])
    def _():
        o_ref[...] = jnp.zeros_like(o_ref)


def moe_grouped_swiglu(h_sorted, tile_expert, n_tiles_used, w1, w3, w2):
    S, D = h_sorted.shape
    tm = MOE_TILE
    n_tiles = S // tm
    wspec = lambda shape: pl.BlockSpec((1,) + shape, lambda i, te, nt: (te[i], 0, 0))
    grid_spec = pltpu.PrefetchScalarGridSpec(
        num_scalar_prefetch=2,
        grid=(n_tiles,),
        in_specs=[pl.BlockSpec((tm, D), lambda i, te, nt: (i, 0)),
                  wspec((D, D_FF)), wspec((D, D_FF)), wspec((D_FF, D))],
        out_specs=pl.BlockSpec((tm, D), lambda i, te, nt: (i, 0)),
        scratch_shapes=[pltpu.VMEM((tm, D), F32)],
    )
    return pl.pallas_call(
        _moe_body,
        grid_spec=grid_spec,
        out_shape=jax.ShapeDtypeStruct((S, D), F32),
        compiler_params=_params("arbitrary"),
        name="moe_grouped_swiglu",
    )(tile_expert, n_tiles_used, h_sorted, w1, w3, w2)


def _combine_body(x_ref, g_ref, w_ref, a_ref, b_ref, o_ref):
    w = w_ref[0]
    o_ref[0] = x_ref[0] + g_ref[0] * (w[:, 0:1] * a_ref[0] + w[:, 1:2] * b_ref[0])


def moe_combine_residual(x, gate, w, ya, yb):
    B, T, D = x.shape
    tm = min(T, ROW_TILE)
    tile = pl.BlockSpec((1, tm, D), lambda b, i: (b, i, 0))
    return pl.pallas_call(
        _combine_body,
        grid=(B, T // tm),
        in_specs=[tile, _mod_spec(gate, tm), pl.BlockSpec((1, tm, TOP_K), lambda b, i: (b, i, 0)), tile, tile],
        out_specs=tile,
        out_shape=jax.ShapeDtypeStruct((B, T, D), F32),
        compiler_params=_params("parallel", "parallel"),
        name="moe_combine_residual",
    )(x, gate, w, ya, yb)


def _final_norm_body(x_ref, g_ref, o_ref):
    x = x_ref[0]
    ms = jnp.mean(x * x, axis=-1, keepdims=True)
    o_ref[0] = x * lax.rsqrt(ms + EPS) * g_ref[...]


def final_norm(x, g):
    B, T, D = x.shape
    tm = min(T, ROW_TILE)
    return pl.pallas_call(
        _final_norm_body,
        grid=(B, T // tm),
        in_specs=[pl.BlockSpec((1, tm, D), lambda b, i: (b, i, 0)),
                  pl.BlockSpec((1, D), lambda b, i: (0, 0))],
        out_specs=pl.BlockSpec((1, tm, D), lambda b, i: (b, i, 0)),
        out_shape=jax.ShapeDtypeStruct((B, T, D), F32),
        compiler_params=_params("parallel", "parallel"),
        name="final_norm",
    )(x, g.reshape(1, D))


def _split3(x):
    hi = x.astype(BF16)
    r1 = x - hi.astype(F32)
    mid = r1.astype(BF16)
    lo = (r1 - mid.astype(F32)).astype(BF16)
    return hi, mid, lo


def _dot01(m01, x):
    hi, mid, lo = _split3(x)
    d = lambda p: jnp.dot(m01, p, preferred_element_type=F32)
    return d(hi) + d(mid) + d(lo)


def _hgrn_body(hq_ref, hf_ref, hi_ref, hg_ref, lb_ref, ng_ref, s0_ref, o_ref, sn_ref, st_ref, *, sub, n_sub):
    i = pl.program_id(1)

    @pl.when(i == 0)
    def _():
        for h in range(HGRN_HEADS):
            st_ref[h] = s0_ref[0, h].T

    lb = lb_ref[...]
    ng = ng_ref[...]
    row = lax.broadcasted_iota(jnp.int32, (sub, sub), 0)
    col = lax.broadcasted_iota(jnp.int32, (sub, sub), 1)
    tril = (row >= col).astype(BF16)
    trow = lax.broadcasted_iota(jnp.int32, (sub, HGRN_DK), 0)

    def chunk(c, carry):
        rows = pl.ds(pl.multiple_of(c * sub, sub), sub)
        hq = hq_ref[0, rows, :]
        hf = hf_ref[0, rows, :]
        hv = hi_ref[0, rows, :]
        hg = hg_ref[0, rows, :]
        f = lb + (1.0 - lb) * jax.nn.sigmoid(hf)
        logf = jnp.log(f)
        kk = 1.0 - f
        qq = hq * jax.nn.sigmoid(hq)
        b = _dot01(tril, logf)
        bl = b[sub - 1:sub, :]
        qe = qq * jnp.exp(b)
        ke = kk * jnp.exp(bl - b)
        ebl = jnp.exp(bl)
        outs = []
        for h in range(HGRN_HEADS):
            cs = slice(h * HGRN_DK, (h + 1) * HGRN_DK)
            st = st_ref[h]
            o = lax.dot_general(qe[:, cs].astype(BF16), st.astype(BF16), (((1,), (1,)), ((), ())),
                                preferred_element_type=F32)
            bh, qh, kh, vh = b[:, cs], qq[:, cs], kk[:, cs], hv[:, cs]
            for s in range(sub):
                e = jnp.exp(jnp.minimum(bh - bh[s:s + 1, :], 0.0))
                w = jnp.where(trow >= s, e * qh * kh[s:s + 1, :], 0.0)
                o = o + jnp.sum(w, axis=-1, keepdims=True) * vh[s:s + 1, :]
            upd = lax.dot_general(vh.astype(BF16), ke[:, cs].astype(BF16), (((0,), (0,)), ((), ())),
                                  preferred_element_type=F32)
            st_ref[h] = st * ebl[:, cs] + upd
            ms = jnp.mean(o * o, axis=-1, keepdims=True)
            outs.append(o * lax.rsqrt(ms + EPS))
        o_ref[0, rows, :] = jnp.concatenate(outs, axis=-1) * ng * (hg * jax.nn.sigmoid(hg))
        return carry

    lax.fori_loop(0, n_sub, chunk, 0)

    @pl.when(i == pl.num_programs(1) - 1)
    def _():
        for h in range(HGRN_HEADS):
            sn_ref[0, h] = st_ref[h].T


def hgrn_mixer(hx, lb, norm_g, s0):
    B, T, _ = hx.shape
    tc = min(T, HGRN_TILE)
    sub = math.gcd(T, HGRN_SUB)
    spec = lambda j: pl.BlockSpec((1, tc, HGRN_KW), lambda b, i, j=j: (b, i, j))
    vec = pl.BlockSpec((1, HGRN_KW), lambda b, i: (0, 0))
    state = pl.BlockSpec((1, HGRN_HEADS, HGRN_DK, HGRN_DV), lambda b, i: (b, 0, 0, 0))
    return pl.pallas_call(
        functools.partial(_hgrn_body, sub=sub, n_sub=tc // sub),
        grid=(B, T // tc),
        in_specs=[spec(0), spec(1), spec(2), spec(3), vec, vec, state],
        out_specs=[pl.BlockSpec((1, tc, HGRN_VW), lambda b, i: (b, i, 0)), state],
        out_shape=[jax.ShapeDtypeStruct((B, T, HGRN_VW), F32),
                   jax.ShapeDtypeStruct((B, HGRN_HEADS, HGRN_DK, HGRN_DV), F32)],
        scratch_shapes=[pltpu.VMEM((HGRN_HEADS, HGRN_DV, HGRN_DK), F32)],
        compiler_params=_params("parallel", "arbitrary"),
        name="hgrn_mixer",
    )(hx, hx, hx, hx, lb.reshape(1, HGRN_KW), jnp.tile(norm_g, HGRN_HEADS).reshape(1, HGRN_VW), s0)


def _nt(a, b):
    return lax.dot_general(a, b, (((1,), (1,)), ((), ())), preferred_element_type=F32)


def _flash_step(kblk, vT, q_heads, mask, m, l, acc_ref, batched):
    tq = q_heads[0].shape[0]
    if batched:
        acc = acc_ref[...]
        scores = [_nt(kblk, qh) for qh in q_heads]
    m_out, l_out, acc_out = [], [], []
    for h, qh in enumerate(q_heads):
        cs = slice(h * tq, (h + 1) * tq)
        s = jnp.where(mask, scores[h] if batched else _nt(kblk, qh), NEG)
        m_new = jnp.maximum(m[:, cs], jnp.max(s, axis=0, keepdims=True))
        alpha = jnp.exp(m[:, cs] - m_new)
        e = jnp.exp(s - m_new)
        l_out.append(alpha * l[:, cs] + jnp.sum(e, axis=0, keepdims=True))
        pv = jnp.dot(vT, e.astype(BF16), preferred_element_type=F32)
        if batched:
            acc_out.append(alpha * acc[:, cs] + pv)
        else:
            acc_ref[:, cs] = alpha * acc_ref[:, cs] + pv
        m_out.append(m_new)
    if batched:
        acc_ref[...] = jnp.concatenate(acc_out, axis=1)
    return jnp.concatenate(m_out, axis=1), jnp.concatenate(l_out, axis=1)


def _head_slabs(qp_ref, first_head, n):
    return [qp_ref[0, :, (first_head + h) * LANE:(first_head + h + 1) * LANE] for h in range(n)]


def _stack_heads(qp_ref, first_head, n):
    return jnp.concatenate(_head_slabs(qp_ref, first_head, n), axis=0)


def _store_heads(o_ref, oT, first_head, n, slot):
    tq = oT.shape[1] // n
    lane = lax.broadcasted_iota(jnp.int32, (tq, LANE), 1)
    valid = (lane >= HEAD_DIM * slot) & (lane < HEAD_DIM * (slot + 1))
    for h in range(n):
        blk = oT[:, h * tq:(h + 1) * tq].T
        o_ref[0, :, (first_head + h) * LANE:(first_head + h + 1) * LANE] = jnp.where(valid, blk, 0.0).astype(o_ref.dtype)


def _nsa_body(qp_ref, rows_ref, win_ref, gl_ref, pek_ref, pev_ref, wck_ref, wcv_ref, o_ref,
              kcmp_ref, vcmpT_ref, ks_ref, vsT_ref, kw_ref, vwT_ref, stage_ref, shift_ref, sel_ref, acc_ref, *, T):
    i = pl.program_id(1)
    tq = ATT_TQ
    nq = NSA_HPG * tq
    n_half = T // CMP_STRIDE
    n_sel = T // SEL_BLK
    k_sel = min(TOP_N, n_sel)
    per_kb = KEY_BLK // SEL_BLK

    @pl.when(i == 0)
    def _prepare():
        nrow = lax.broadcasted_iota(jnp.int32, (n_half, LANE), 0)
        for slab, pe_ref, w_ref in ((0, pek_ref, wck_ref), (1, pev_ref, wcv_ref)):
            def stage_blk(kb, carry, slab=slab):
                rs = pl.ds(pl.multiple_of(kb * KEY_BLK, KEY_BLK), KEY_BLK)
                stage_ref[rs, :] = rows_ref[0, rs, slab * LANE:(slab + 1) * LANE]
                return carry

            lax.fori_loop(0, T // KEY_BLK, stage_blk, 0)
            first = jnp.zeros((n_half, LANE), F32)
            second = jnp.zeros((n_half, LANE), F32)
            for r in range(CMP_STRIDE):
                y = stage_ref[pl.ds(r, n_half, stride=CMP_STRIDE), :]
                first = first + jnp.dot((y + pe_ref[r:r + 1, :]).astype(BF16), w_ref[r],
                                        preferred_element_type=F32)
                second = second + jnp.dot((y + pe_ref[r + CMP_STRIDE:r + CMP_STRIDE + 1, :]).astype(BF16),
                                          w_ref[r + CMP_STRIDE], preferred_element_type=F32)
            shift_ref[0:n_half, :] = second
            shift_ref[n_half:n_half + 8, :] = jnp.zeros((8, LANE), F32)
            c = jnp.where(nrow < n_half - 1, first + shift_ref[1:n_half + 1, :], 0.0)
            if slab == 0:
                kcmp_ref[...] = c.astype(BF16)
            else:
                for j in range(n_half // LANE):
                    vcmpT_ref[:, j * LANE:(j + 1) * LANE] = c[j * LANE:(j + 1) * LANE, :].T.astype(BF16)

        def copy_blk(kb, carry):
            for half in range(KEY_BLK // LANE):
                rs = pl.ds(pl.multiple_of(kb * KEY_BLK + half * LANE, LANE), LANE)
                hs = slice(half * LANE, (half + 1) * LANE)
                ks_ref[kb, hs, :] = rows_ref[0, rs, 2 * LANE:3 * LANE].astype(BF16)
                vsT_ref[kb, :, hs] = rows_ref[0, rs, 3 * LANE:4 * LANE].T.astype(BF16)
                kw_ref[kb, hs, :] = win_ref[0, rs, 0:LANE].astype(BF16)
                vwT_ref[kb, :, hs] = win_ref[0, rs, LANE:2 * LANE].T.astype(BF16)
            return carry

        lax.fori_loop(0, T // KEY_BLK, copy_blk, 0)

    t0 = i * tq
    lane_t = t0 + (lax.broadcasted_iota(jnp.int32, (1, nq), 1) & (tq - 1))
    t_row = lane_t[:, 0:tq]
    gT = jax.nn.sigmoid(gl_ref[0].T)
    kio = lax.broadcasted_iota(jnp.int32, (KEY_BLK, tq), 0)
    nio = lax.broadcasted_iota(jnp.int32, (n_half, nq), 0)
    jcol = lax.broadcasted_iota(jnp.int32, (n_sel, tq), 0)
    pj = lax.broadcasted_iota(jnp.int32, (n_sel, n_half), 0)
    pi = lax.broadcasted_iota(jnp.int32, (n_sel, n_half), 1)
    pool = ((((pi + 1) >> 2) == pj) & (pi < n_half - 1)).astype(BF16)
    n_kb = (t0 + tq + KEY_BLK - 1) // KEY_BLK
    n_full = t0 // KEY_BLK
    m0 = jnp.full((1, nq), M_FLOOR, F32)
    l0 = jnp.zeros((1, nq), F32)

    for g in range(NSA_KV):
        q_heads = _head_slabs(qp_ref, NSA_HPG * g, NSA_HPG)
        qs = jnp.concatenate(q_heads, axis=0)
        cmask = (CMP_STRIDE * nio + CMP_BLK <= lane_t + 1) & (nio < n_half - 1)
        s = jnp.where(cmask, _nt(kcmp_ref[...], qs), NEG)
        e = jnp.where(cmask, jnp.exp(s - jnp.max(s, axis=0, keepdims=True)), 0.0)
        p = e * (1.0 / jnp.maximum(jnp.sum(e, axis=0, keepdims=True), 1e-30))
        ocT = jnp.dot(vcmpT_ref[...], p.astype(BF16), preferred_element_type=F32)
        imp = p[:, 0:tq]
        for h in range(1, NSA_HPG):
            imp = imp + p[:, h * tq:(h + 1) * tq]
        imp_sel = _dot01(pool, imp)
        cur = t_row >> 6
        forced = (jcol == 0) | (jcol == cur) | (jcol == cur - 1)
        visible = jcol <= cur
        score = jnp.where(visible, imp_sel + jnp.where(forced, FORCE_SCORE, 0.0), NEG)
        rank = jnp.zeros((n_sel, tq), jnp.int32)
        for j2 in range(n_sel):
            r = score[j2:j2 + 1, :]
            beats = (r > score) | ((r == score) & (jcol > j2))
            rank = rank + beats.astype(jnp.int32)
        sel_ref[...] = jnp.where((rank < k_sel) & visible, 1.0, 0.0)

        def sel_step(kb, carry, causal):
            rowsel = jnp.concatenate(
                [jnp.broadcast_to(sel_ref[pl.ds(kb * per_kb + k, 1), :], (SEL_BLK, tq)) for k in range(per_kb)],
                axis=0)
            mask = rowsel > 0.5
            if causal:
                mask = mask & (kb * KEY_BLK + kio <= t_row)
            return _flash_step(ks_ref[kb], vsT_ref[kb], q_heads, mask, carry[0], carry[1], acc_ref, True)

        acc_ref[...] = jnp.zeros_like(acc_ref)
        ml = lax.fori_loop(0, n_full, functools.partial(sel_step, causal=False), (m0, l0))
        _, l = lax.fori_loop(n_full, n_kb, functools.partial(sel_step, causal=True), ml)
        osT = acc_ref[...] * (1.0 / jnp.maximum(l, 1e-30))

        def win_step(kb, carry):
            kpos = kb * KEY_BLK + kio
            mask = (kpos <= t_row) & (kpos > t_row - WINDOW)
            return _flash_step(kw_ref[kb], vwT_ref[kb], q_heads, mask, carry[0], carry[1], acc_ref, True)

        acc_ref[...] = jnp.zeros_like(acc_ref)
        w_lo = jnp.maximum(t0 - (WINDOW - 1), 0) // KEY_BLK
        _, l = lax.fori_loop(w_lo, n_kb, win_step, (m0, l0))
        owT = acc_ref[...] * (1.0 / jnp.maximum(l, 1e-30))

        def gate(c):
            return jnp.concatenate([gT[3 * (NSA_HPG * g + h) + c:3 * (NSA_HPG * g + h) + c + 1, :]
                                    for h in range(NSA_HPG)], axis=1)

        oT = gate(0) * ocT + gate(1) * osT + gate(2) * owT
        _store_heads(o_ref, oT, NSA_HPG * g, NSA_HPG, g)


def _block_diag2(w):
    w3 = w.reshape(CMP_BLK, HEAD_DIM, HEAD_DIM)
    z = jnp.zeros_like(w3)
    return jnp.concatenate([jnp.concatenate([w3, z], axis=2), jnp.concatenate([z, w3], axis=2)], axis=1).astype(BF16)


def nsa_prompt(qp, rows, win, gl, pe_k, pe_v, w_ck, w_cv):
    B, T, _ = rows.shape
    assert T % (CMP_STRIDE * LANE) == 0 and T % KEY_BLK == 0
    tq = ATT_TQ
    nq = NSA_HPG * tq
    n_half = T // CMP_STRIDE
    n_kb = T // KEY_BLK
    const = lambda shape: pl.BlockSpec(shape, lambda b, i: (0,) * len(shape))
    return pl.pallas_call(
        functools.partial(_nsa_body, T=T),
        grid=(B, T // tq),
        in_specs=[pl.BlockSpec((1, tq, NSA_HEADS * LANE), lambda b, i: (b, i, 0)),
                  pl.BlockSpec((1, T, 4 * LANE), lambda b, i: (b, 0, 0)),
                  pl.BlockSpec((1, T, 2 * LANE), lambda b, i: (b, 0, 0)),
                  pl.BlockSpec((1, tq, LANE), lambda b, i: (b, i, 0)),
                  const((CMP_BLK, LANE)), const((CMP_BLK, LANE)),
                  const((CMP_BLK, LANE, LANE)), const((CMP_BLK, LANE, LANE))],
        out_specs=pl.BlockSpec((1, tq, NSA_HEADS * LANE), lambda b, i: (b, i, 0)),
        out_shape=jax.ShapeDtypeStruct((B, T, NSA_HEADS * LANE), BF16),
        scratch_shapes=[pltpu.VMEM((n_half, LANE), BF16), pltpu.VMEM((LANE, n_half), BF16),
                        pltpu.VMEM((n_kb, KEY_BLK, LANE), BF16), pltpu.VMEM((n_kb, LANE, KEY_BLK), BF16),
                        pltpu.VMEM((n_kb, KEY_BLK, LANE), BF16), pltpu.VMEM((n_kb, LANE, KEY_BLK), BF16),
                        pltpu.VMEM((T, LANE), F32),
                        pltpu.VMEM((n_half + 8, LANE), F32), pltpu.VMEM((T // SEL_BLK, tq), F32),
                        pltpu.VMEM((LANE, nq), F32)],
        compiler_params=_params("parallel", "arbitrary"),
        name="nsa_prompt",
    )(qp, rows, win, gl, jnp.tile(pe_k, (1, 2)), jnp.tile(pe_v, (1, 2)), _block_diag2(w_ck), _block_diag2(w_cv))


def _dsa_body(qp_ref, kv_ref, qi_ref, ki_ref, wi_ref, o_ref,
              k_ref, vT_ref, kilo_ref, kihi_ref, key_ref, acc_ref, *, T, n_keep):
    i = pl.program_id(1)
    tq = ATT_TQ
    nq = DSA_HPG * tq

    @pl.when(i == 0)
    def _prepare():
        def copy_blk(kb, carry):
            for half in range(KEY_BLK // LANE):
                rs = pl.ds(pl.multiple_of(kb * KEY_BLK + half * LANE, LANE), LANE)
                hs = slice(half * LANE, (half + 1) * LANE)
                for slab in range(2):
                    k_ref[slab, kb, hs, :] = kv_ref[0, rs, slab * LANE:(slab + 1) * LANE].astype(BF16)
                    vT_ref[slab, kb, :, hs] = kv_ref[0, rs, (2 + slab) * LANE:(3 + slab) * LANE].T.astype(BF16)
                kix = ki_ref[0, rs, :]
                kilo_ref[kb, hs, :] = kix.astype(BF16)
                kihi_ref[kb, hs, :] = pltpu.roll(kix, IDX_DIM, 1).astype(BF16)
            return carry

        lax.fori_loop(0, T // KEY_BLK, copy_blk, 0)

    t0 = i * tq
    n_kb = (t0 + tq + KEY_BLK - 1) // KEY_BLK
    t_row = t0 + lax.broadcasted_iota(jnp.int32, (1, tq), 1)
    kio_q = lax.broadcasted_iota(jnp.int32, (KEY_BLK, tq), 0)

    wT = wi_ref[0].T * (IDX_HEADS ** -0.5)
    qi = _stack_heads(qi_ref, 0, IDX_HEADS // 2)

    def idx_step(kb, carry):
        s_lo = _nt(kilo_ref[kb], qi)
        s_hi = _nt(kihi_ref[kb], qi)
        sc = jnp.zeros((KEY_BLK, tq), F32)
        for p in range(IDX_HEADS // 2):
            cs = slice(p * tq, (p + 1) * tq)
            sc = sc + jnp.maximum(s_lo[:, cs], 0.0) * wT[2 * p:2 * p + 1, :]
            sc = sc + jnp.maximum(s_hi[:, cs], 0.0) * wT[2 * p + 1:2 * p + 2, :]
        sc = jnp.where(kb * KEY_BLK + kio_q <= t_row, sc, NEG)
        bits = lax.bitcast_convert_type(sc, jnp.int32)
        key_ref[kb] = jnp.where(bits < 0, bits ^ 0x7FFFFFFF, bits)
        return carry

    lax.fori_loop(0, n_kb, idx_step, 0)

    def bit_step(it, theta):
        cand = theta + lax.shift_left(jnp.int32(1), 31 - it)

        def cnt_step(kb, c):
            return c + jnp.sum((key_ref[kb] >= cand).astype(jnp.int32), axis=0, keepdims=True)

        cnt = lax.fori_loop(0, n_kb, cnt_step, jnp.zeros((1, tq), jnp.int32))
        return jnp.where(cnt >= n_keep, cand, theta)

    theta = lax.fori_loop(0, 32, bit_step, jnp.full((1, tq), INT_MIN, jnp.int32))

    q_heads = [_head_slabs(qp_ref, DSA_HPG * g, DSA_HPG) for g in range(DSA_KV)]
    m0 = jnp.full((1, nq), M_FLOOR, F32)
    l0 = jnp.zeros((1, nq), F32)
    acc_ref[...] = jnp.zeros_like(acc_ref)

    def att_step(kb, carry, causal):
        mask = key_ref[kb] >= theta
        if causal:
            mask = mask & (kb * KEY_BLK + kio_q <= t_row)
        out = []
        for g in range(DSA_KV):
            m, l = _flash_step(k_ref[g // 2, kb], vT_ref[g // 2, kb], q_heads[g], mask,
                               carry[2 * g], carry[2 * g + 1], acc_ref.at[g], False)
            out += [m, l]
        return tuple(out)

    mid = lax.fori_loop(0, t0 // KEY_BLK, functools.partial(att_step, causal=False), (m0, l0) * DSA_KV)
    fin = lax.fori_loop(t0 // KEY_BLK, n_kb, functools.partial(att_step, causal=True), mid)
    for g in range(DSA_KV):
        oT = acc_ref[g] * (1.0 / jnp.maximum(fin[2 * g + 1], 1e-30))
        _store_heads(o_ref, oT, DSA_HPG * g, DSA_HPG, g % 2)


def dsa_prompt(qp, kv, qi, ki, wi):
    B, T, _ = kv.shape
    assert T % KEY_BLK == 0
    tq = ATT_TQ
    nq = DSA_HPG * tq
    n_kb = T // KEY_BLK
    n_keep = min(IDX_TOPK, T // 4)
    tile = lambda w: pl.BlockSpec((1, tq, w), lambda b, i: (b, i, 0))
    whole = lambda w: pl.BlockSpec((1, T, w), lambda b, i: (b, 0, 0))
    return pl.pallas_call(
        functools.partial(_dsa_body, T=T, n_keep=n_keep),
        grid=(B, T // tq),
        in_specs=[tile(DSA_HEADS * LANE), whole(4 * LANE), tile(IDX_QW), whole(LANE), tile(LANE)],
        out_specs=tile(DSA_HEADS * LANE),
        out_shape=jax.ShapeDtypeStruct((B, T, DSA_HEADS * LANE), BF16),
        scratch_shapes=[pltpu.VMEM((2, n_kb, KEY_BLK, LANE), BF16), pltpu.VMEM((2, n_kb, LANE, KEY_BLK), BF16),
                        pltpu.VMEM((n_kb, KEY_BLK, LANE), BF16), pltpu.VMEM((n_kb, KEY_BLK, LANE), BF16),
                        pltpu.VMEM((n_kb, KEY_BLK, tq), jnp.int32), pltpu.VMEM((DSA_KV, LANE, nq), F32)],
        compiler_params=_params("parallel", "arbitrary"),
        name="dsa_prompt",
    )(qp, kv, qi, ki, wi)


def _pad_head_cols(w, slots):
    D = w.shape[0]
    H = len(slots)
    w3 = w.reshape(D, H, HEAD_DIM)
    z = jnp.zeros_like(w3)
    s = jnp.asarray(slots)[None, :, None]
    return jnp.concatenate([jnp.where(s == 0, w3, z), jnp.where(s == 1, w3, z)], axis=2).reshape(D, H * LANE)


def _pad_cols(w, n):
    return jnp.pad(w, ((0, 0), (0, n - w.shape[1])))


def _dot01_r(x, m01):
    hi, mid, lo = _split3(x)
    d = lambda p: jnp.dot(p, m01, preferred_element_type=F32)
    return d(hi) + d(mid) + d(lo)


def _softmax_rows(s, mask):
    s = jnp.where(mask, s, NEG)
    e = jnp.where(mask, jnp.exp(s - jnp.max(s, axis=1, keepdims=True)), 0.0)
    return e * (1.0 / jnp.maximum(jnp.sum(e, axis=1, keepdims=True), 1e-30))


def _flash_rows(s, mask, m_ref, l_ref):
    s = jnp.where(mask, s, NEG)
    m = m_ref[...]
    m_new = jnp.maximum(m, jnp.max(s, axis=1, keepdims=True))
    alpha = jnp.exp(m - m_new)
    e = jnp.where(mask, jnp.exp(s - m_new), 0.0)
    l_ref[...] = alpha * l_ref[...] + jnp.sum(e, axis=1, keepdims=True)
    m_ref[...] = m_new
    return alpha, e


def _sort_key(x):
    bits = lax.bitcast_convert_type(x, jnp.int32)
    return jnp.where(bits < 0, bits ^ 0x7FFFFFFF, bits)


def _page_spec(layer, k, chans, blk):
    return pl.BlockSpec((None, 1, chans, PAGE_SIZE), lambda b, s, pt: (layer, pt[b, s * PG_STEP + k], blk, 0))


def _channel_major(cache):
    layers, n, rows = cache.shape[:3]
    return jnp.moveaxis(cache.reshape(layers, n, rows, -1), 2, 3)


def _per_request(shape):
    return pl.BlockSpec((1,) + shape, lambda b, s, pt: (b,) + (0,) * len(shape))


def _nsa_cmp_body(pt_ref, *refs):
    pages = refs[:PG_STEP]
    pek_ref, pev_ref, wck_ref, wcv_ref, o_ref, stage_ref = refs[PG_STEP:]
    n_half = PG_STEP * PAGE_SIZE // CMP_STRIDE
    for slab, pe_ref, w_ref in ((0, pek_ref, wck_ref), (1, pev_ref, wcv_ref)):
        for pg in range(PG_STEP):
            stage_ref[pg * PAGE_SIZE:(pg + 1) * PAGE_SIZE, :] = pages[pg][0, slab * LANE:(slab + 1) * LANE, :].T
        first = jnp.zeros((n_half, LANE), F32)
        second = jnp.zeros((n_half, LANE), F32)
        for r in range(CMP_STRIDE):
            y = stage_ref[pl.ds(r, n_half, stride=CMP_STRIDE), :]
            first = first + jnp.dot((y + pe_ref[r:r + 1, :]).astype(BF16), w_ref[r], preferred_element_type=F32)
            second = second + jnp.dot((y + pe_ref[r + CMP_STRIDE:r + CMP_STRIDE + 1, :]).astype(BF16),
                                      w_ref[r + CMP_STRIDE], preferred_element_type=F32)
        o_ref[0, :, (2 * slab) * LANE:(2 * slab + 1) * LANE] = first
        o_ref[0, :, (2 * slab + 1) * LANE:(2 * slab + 2) * LANE] = second


def nsa_sample_compress(cache, layer, page_table, pe_k, pe_v, w_ck, w_cv):
    B, n_pages = page_table.shape
    assert n_pages % PG_STEP == 0
    n_half_step = PG_STEP * PAGE_SIZE // CMP_STRIDE
    const = lambda shape: pl.BlockSpec(shape, lambda b, s, pt: (0,) * len(shape))
    grid_spec = pltpu.PrefetchScalarGridSpec(
        num_scalar_prefetch=1,
        grid=(B, n_pages // PG_STEP),
        in_specs=[_page_spec(layer, k, 2 * LANE, 0) for k in range(PG_STEP)]
        + [const((CMP_BLK, LANE)), const((CMP_BLK, LANE)), const((CMP_BLK, LANE, LANE)), const((CMP_BLK, LANE, LANE))],
        out_specs=pl.BlockSpec((1, n_half_step, 4 * LANE), lambda b, s, pt: (b, s, 0)),
        scratch_shapes=[pltpu.VMEM((PG_STEP * PAGE_SIZE, LANE), F32)],
    )
    return pl.pallas_call(
        _nsa_cmp_body,
        grid_spec=grid_spec,
        out_shape=jax.ShapeDtypeStruct((B, n_pages * PAGE_SIZE // CMP_STRIDE, 4 * LANE), F32),
        compiler_params=_params("parallel", "arbitrary"),
        name="nsa_sample_compress",
    )(page_table, *([cache] * PG_STEP), jnp.tile(pe_k, (1, 2)), jnp.tile(pe_v, (1, 2)),
      _block_diag2(w_ck), _block_diag2(w_cv))


def _nsa_smp_body(pt_ref, qs_ref, gl_ref, fs_ref, *refs, n_tok):
    pages = refs[:PG_STEP]
    nrow_ref, pwin_ref, nwin_ref, o_ref, shift_ref, selc_ref, oc_ref, m_ref, l_ref, acc_ref = refs[PG_STEP:]
    s_id = pl.program_id(1)
    C = LANE
    n_half = fs_ref.shape[1]
    n_cmp = n_half - 1
    n_selp = n_half * CMP_STRIDE // SEL_BLK
    qs = qs_ref[0]
    crow = lax.broadcasted_iota(jnp.int32, (C, LANE), 0)
    lane = lax.broadcasted_iota(jnp.int32, (C, LANE), 1)
    t_of_c = crow % n_tok

    @pl.when(s_id == 0)
    def _first():
        cmp = []
        for slab in range(2):
            shift_ref[0:n_half, :] = fs_ref[0, :, (2 * slab + 1) * LANE:(2 * slab + 2) * LANE]
            shift_ref[n_half:n_half + 8, :] = jnp.zeros((8, LANE), F32)
            cmp.append((fs_ref[0, :, (2 * slab) * LANE:(2 * slab + 1) * LANE]
                        + shift_ref[1:n_half + 1, :]).astype(BF16))
        nlane = lax.broadcasted_iota(jnp.int32, (C, n_half), 1)
        p = _softmax_rows(_nt(qs, cmp[0]), nlane < n_cmp)
        oc_ref[...] = jnp.dot(p.astype(BF16), cmp[1], preferred_element_type=F32)
        imp_rows = []
        for g in range(NSA_KV):
            acc = p[(NSA_HPG * g) * n_tok:(NSA_HPG * g + 1) * n_tok, :]
            for h in range(1, NSA_HPG):
                acc = acc + p[(NSA_HPG * g + h) * n_tok:(NSA_HPG * g + h + 1) * n_tok, :]
            imp_rows.append(acc)
        imp = jnp.concatenate(imp_rows + [jnp.zeros((C - NSA_KV * n_tok, n_half), F32)], axis=0)
        pi = lax.broadcasted_iota(jnp.int32, (n_half, n_selp), 0)
        pj = lax.broadcasted_iota(jnp.int32, (n_half, n_selp), 1)
        pool = ((((pi + 1) >> 2) == pj) & (pi < n_cmp)).astype(BF16)
        scoreT = _dot01_r(imp, pool).T
        jcol = lax.broadcasted_iota(jnp.int32, (n_selp, C), 0)
        forced = (jcol == 0) | (jcol == n_selp - 1)
        scoreT = scoreT + jnp.where(forced, FORCE_SCORE, 0.0)
        rank = jnp.zeros((n_selp, C), jnp.int32)
        for j2 in range(n_selp):
            r = scoreT[j2:j2 + 1, :]
            beats = (r > scoreT) | ((r == scoreT) & (jcol > j2))
            rank = rank + beats.astype(jnp.int32)
        sel = jnp.where(rank < TOP_N - 1, 1.0, 0.0).T
        selc_ref[...] = jnp.concatenate(
            [sel[(hh // NSA_HPG) * n_tok:(hh // NSA_HPG + 1) * n_tok, :] for hh in range(NSA_HEADS)]
            + [jnp.zeros((C - NSA_HEADS * n_tok, n_selp), F32)], axis=0)
        m_ref[...] = jnp.full(m_ref.shape, NEG, F32)
        l_ref[...] = jnp.zeros(l_ref.shape, F32)
        acc_ref[...] = jnp.zeros(acc_ref.shape, F32)

    selc = selc_ref[...].astype(BF16)
    n_keys = PG_STEP * PAGE_SIZE
    ej = lax.broadcasted_iota(jnp.int32, (n_selp, n_keys), 0)
    ek = lax.broadcasted_iota(jnp.int32, (n_selp, n_keys), 1)
    expand = (ej == s_id * (n_keys // SEL_BLK) + ek // SEL_BLK).astype(BF16)
    mask = jnp.dot(selc, expand, preferred_element_type=F32) > 0.5
    s = jnp.concatenate([jnp.dot(qs, pages[pg][0, 0:LANE, :].astype(BF16), preferred_element_type=F32)
                         for pg in range(PG_STEP)], axis=1)
    alpha, e = _flash_rows(s, mask, m_ref, l_ref)
    eb = e.astype(BF16)
    pv = _nt(eb[:, 0:PAGE_SIZE], pages[0][0, LANE:2 * LANE, :].astype(BF16))
    for pg in range(1, PG_STEP):
        pv = pv + _nt(eb[:, pg * PAGE_SIZE:(pg + 1) * PAGE_SIZE], pages[pg][0, LANE:2 * LANE, :].astype(BF16))
    acc_ref[...] = alpha * acc_ref[...] + pv

    @pl.when(s_id == pl.num_programs(1) - 1)
    def _last():
        pad = jnp.zeros((LANE - n_tok, LANE), F32)
        new_ok = (lane < n_tok) & (lane <= t_of_c)
        knew = jnp.concatenate([nrow_ref[0, :, 2 * LANE:3 * LANE], pad], axis=0).astype(BF16)
        vnew = jnp.concatenate([nrow_ref[0, :, 3 * LANE:4 * LANE], pad], axis=0).astype(BF16)
        alpha, e = _flash_rows(_nt(qs, knew), new_ok, m_ref, l_ref)
        acc = alpha * acc_ref[...] + jnp.dot(e.astype(BF16), vnew, preferred_element_type=F32)
        o_s = acc * (1.0 / jnp.maximum(l_ref[...], 1e-30))
        n_win = pwin_ref.shape[2]
        kwn = jnp.concatenate([nwin_ref[0, :, 0:LANE], pad], axis=0).astype(BF16)
        vwn = jnp.concatenate([nwin_ref[0, :, LANE:2 * LANE], pad], axis=0).astype(BF16)
        wl = lax.broadcasted_iota(jnp.int32, (C, n_win + LANE), 1)
        tw = lax.broadcasted_iota(jnp.int32, (C, n_win + LANE), 0) % n_tok
        wmask = (((wl < n_win) & ((n_win - wl) + tw < WINDOW))
                 | ((wl >= n_win) & (wl - n_win < n_tok) & (wl - n_win <= tw)))
        s_w = jnp.concatenate([jnp.dot(qs, pwin_ref[0, 0:LANE, :].astype(BF16), preferred_element_type=F32),
                               _nt(qs, kwn)], axis=1)
        pw = _softmax_rows(s_w, wmask).astype(BF16)
        o_w = (_nt(pw[:, 0:n_win], pwin_ref[0, LANE:2 * LANE, :].astype(BF16))
               + jnp.dot(pw[:, n_win:], vwn, preferred_element_type=F32))
        g = jax.nn.sigmoid(gl_ref[0])
        o = g[:, 0:1] * oc_ref[...] + g[:, 1:2] * o_s + g[:, 2:3] * o_w
        valid = (crow < NSA_HEADS * n_tok) & (lane // HEAD_DIM == crow // (n_tok * NSA_HPG))
        o_ref[0] = jnp.where(valid, o, 0.0)


def nsa_sample_attention(qs, gcol, fs, cache, layer, page_table, new_rows, past_win, new_win):
    B, n_pages = page_table.shape
    n_tok = new_rows.shape[1]
    n_half = fs.shape[1]
    n_selp = n_half * CMP_STRIDE // SEL_BLK
    n_win = past_win.shape[3]
    assert n_selp == LANE and NSA_HEADS * n_tok <= LANE and n_tok <= min(SEL_BLK, 8) and n_pages % PG_STEP == 0
    assert n_win % LANE == 0 and n_win <= WINDOW
    grid_spec = pltpu.PrefetchScalarGridSpec(
        num_scalar_prefetch=1,
        grid=(B, n_pages // PG_STEP),
        in_specs=[_per_request((LANE, LANE)), _per_request((LANE, LANE)), _per_request((n_half, 4 * LANE))]
        + [_page_spec(layer, k, 2 * LANE, 1) for k in range(PG_STEP)]
        + [_per_request((n_tok, 4 * LANE)),
           pl.BlockSpec((None, 1, 2 * LANE, n_win), lambda b, s, pt: (layer, b, 0, 0)),
           _per_request((n_tok, 2 * LANE))],
        out_specs=_per_request((LANE, LANE)),
        scratch_shapes=[pltpu.VMEM((n_half + 8, LANE), F32), pltpu.VMEM((LANE, n_selp), F32),
                        pltpu.VMEM((LANE, LANE), F32), pltpu.VMEM((LANE, 1), F32), pltpu.VMEM((LANE, 1), F32),
                        pltpu.VMEM((LANE, LANE), F32)],
    )
    return pl.pallas_call(
        functools.partial(_nsa_smp_body, n_tok=n_tok),
        grid_spec=grid_spec,
        out_shape=jax.ShapeDtypeStruct((B, LANE, LANE), F32),
        compiler_params=_params("parallel", "arbitrary"),
        name="nsa_sample_attention",
    )(page_table, qs, gcol, fs, *([cache] * PG_STEP), new_rows, past_win, new_win)


def _dsa_idx_body(pt_ref, qe_ref, qo_ref, w_ref, *refs, n_tok, n_keep):
    pages = refs[:PG_STEP]
    knew_ref, sc_ref, th_ref = refs[PG_STEP:]
    s_id = pl.program_id(1)
    n_blk = sc_ref.shape[1]
    qe = qe_ref[0]
    qo = qo_ref[0]
    w = w_ref[0] * (IDX_HEADS ** -0.5)
    n_rows = (IDX_HEADS // 2) * n_tok

    def scores(s_e, s_o):
        s = jnp.maximum(s_e, 0.0) * w[:, 0:1] + jnp.maximum(s_o, 0.0) * w[:, 1:2]
        tot = s[0:n_tok, :]
        for p in range(1, IDX_HEADS // 2):
            tot = tot + s[p * n_tok:(p + 1) * n_tok, :]
        return tot

    kT = jnp.concatenate([pages[pg][0] for pg in range(PG_STEP)], axis=1).astype(BF16)
    sc = scores(jnp.dot(qe, kT, preferred_element_type=F32), jnp.dot(qo, kT, preferred_element_type=F32))
    for pg in range(PG_STEP):
        sc_ref[0, s_id * PG_STEP + pg] = sc[:, pg * PAGE_SIZE:(pg + 1) * PAGE_SIZE]

    @pl.when(s_id == pl.num_programs(1) - 1)
    def _last():
        kn = jnp.concatenate([knew_ref[0], jnp.zeros((LANE - n_tok, IDX_DIM), F32)], axis=0).astype(BF16)
        a_i = lax.broadcasted_iota(jnp.int32, (n_tok, LANE), 1)
        t_i = lax.broadcasted_iota(jnp.int32, (n_tok, LANE), 0)
        sn = jnp.where(a_i <= t_i, scores(_nt(qe, kn), _nt(qo, kn)), NEG)
        sc_ref[0, n_blk - 1] = jnp.where(a_i < n_tok, sn, -jnp.inf)
        keys = _sort_key(sc_ref[0])

        def bit_step(it, theta):
            cand = theta + lax.shift_left(jnp.int32(1), 31 - it)
            cnt = jnp.sum(jnp.sum((keys >= cand).astype(jnp.int32), axis=0), axis=1, keepdims=True)
            return jnp.where(cnt >= n_keep, cand, theta)

        theta = lax.fori_loop(0, 32, bit_step, jnp.full((n_tok, 1), INT_MIN, jnp.int32))
        th_ref[0] = jnp.broadcast_to(theta, (n_tok, LANE))


def dsa_sample_index(qe, qo, wcol, cache_idx, layer, page_table, ki_new):
    B, n_pages = page_table.shape
    n_tok = ki_new.shape[1]
    assert n_tok == 8 and n_pages % PG_STEP == 0
    n_keep = min(IDX_TOPK, (n_pages * PAGE_SIZE + n_tok) // 4)
    grid_spec = pltpu.PrefetchScalarGridSpec(
        num_scalar_prefetch=1,
        grid=(B, n_pages // PG_STEP),
        in_specs=[_per_request((LANE, IDX_DIM)), _per_request((LANE, IDX_DIM)), _per_request((LANE, LANE))]
        + [_page_spec(layer, k, IDX_DIM, 0) for k in range(PG_STEP)] + [_per_request((n_tok, IDX_DIM))],
        out_specs=[_per_request((n_pages + 1, n_tok, LANE)), _per_request((n_tok, LANE))],
    )
    return pl.pallas_call(
        functools.partial(_dsa_idx_body, n_tok=n_tok, n_keep=n_keep),
        grid_spec=grid_spec,
        out_shape=[jax.ShapeDtypeStruct((B, n_pages + 1, n_tok, LANE), F32),
                   jax.ShapeDtypeStruct((B, n_tok, LANE), jnp.int32)],
        compiler_params=_params("parallel", "arbitrary"),
        name="dsa_sample_index",
    )(page_table, qe, qo, wcol, *([cache_idx] * PG_STEP), ki_new)


def _dsa_smp_body(pt_ref, qs_ref, sc_ref, th_ref, *refs, n_tok):
    pages = refs[:PG_STEP]
    kvn_ref, o_ref, m_ref, l_ref, acc_ref = refs[PG_STEP:]
    s_id = pl.program_id(1)
    C = LANE
    qs = qs_ref[0]
    crow = lax.broadcasted_iota(jnp.int32, (C, LANE), 0)
    lane = lax.broadcasted_iota(jnp.int32, (C, LANE), 1)
    low = crow < C // 2
    theta = th_ref[0]

    @pl.when(s_id == 0)
    def _():
        m_ref[...] = jnp.full(m_ref.shape, NEG, F32)
        l_ref[...] = jnp.zeros(l_ref.shape, F32)
        acc_ref[...] = jnp.zeros(acc_ref.shape, F32)

    def step(kvTs, blk0, extra):
        kbs = [kvT.astype(BF16) for kvT in kvTs]
        keep = jnp.concatenate([jnp.where(_sort_key(sc_ref[0, blk0 + j]) >= theta, 1.0, 0.0)
                                for j in range(len(kbs))], axis=1)
        mask = jnp.concatenate([keep] * (C // n_tok), axis=0) > 0.5
        if extra is not None:
            mask = mask & extra
        s_lo = jnp.concatenate([jnp.dot(qs, kb[0:LANE], preferred_element_type=F32) for kb in kbs], axis=1)
        s_hi = jnp.concatenate([jnp.dot(qs, kb[LANE:2 * LANE], preferred_element_type=F32) for kb in kbs], axis=1)
        low_col = lax.broadcasted_iota(jnp.int32, (C, 1), 0) < C // 2
        alpha, e = _flash_rows(jnp.where(low_col, s_lo, s_hi), mask, m_ref, l_ref)
        eb = e.astype(BF16)
        pv_lo = pv_hi = None
        for j, kb in enumerate(kbs):
            ej = eb[:, j * LANE:(j + 1) * LANE]
            a, b = _nt(ej, kb[2 * LANE:3 * LANE]), _nt(ej, kb[3 * LANE:4 * LANE])
            pv_lo, pv_hi = (a, b) if pv_lo is None else (pv_lo + a, pv_hi + b)
        acc_ref[...] = alpha * acc_ref[...] + jnp.where(low, pv_lo, pv_hi)

    step([pages[pg][0] for pg in range(PG_STEP)], s_id * PG_STEP, None)

    @pl.when(s_id == pl.num_programs(1) - 1)
    def _last():
        kvn = jnp.concatenate([kvn_ref[0], jnp.zeros((LANE - n_tok, 4 * LANE), F32)], axis=0)
        kvnT = jnp.concatenate([kvn[:, j * LANE:(j + 1) * LANE].T for j in range(4)], axis=0)
        step([kvnT], sc_ref.shape[1] - 1, (lane < n_tok) & (lane <= crow % n_tok))
        o = acc_ref[...] * (1.0 / jnp.maximum(l_ref[...], 1e-30))
        valid = lane // HEAD_DIM == (crow // (n_tok * DSA_HPG)) % 2
        o_ref[0] = jnp.where(valid, o, 0.0)


def dsa_sample_attention(qs, scores, theta, cache_kv, layer, page_table, kv_new):
    B, n_pages = page_table.shape
    n_tok = kv_new.shape[1]
    assert DSA_HEADS * n_tok == LANE and n_pages % PG_STEP == 0
    grid_spec = pltpu.PrefetchScalarGridSpec(
        num_scalar_prefetch=1,
        grid=(B, n_pages // PG_STEP),
        in_specs=[_per_request((LANE, LANE)), _per_request((n_pages + 1, n_tok, LANE)), _per_request((n_tok, LANE))]
        + [_page_spec(layer, k, 4 * LANE, 0) for k in range(PG_STEP)] + [_per_request((n_tok, 4 * LANE))],
        out_specs=_per_request((LANE, LANE)),
        scratch_shapes=[pltpu.VMEM((LANE, 1), F32), pltpu.VMEM((LANE, 1), F32), pltpu.VMEM((LANE, LANE), F32)],
    )
    return pl.pallas_call(
        functools.partial(_dsa_smp_body, n_tok=n_tok),
        grid_spec=grid_spec,
        out_shape=jax.ShapeDtypeStruct((B, LANE, LANE), F32),
        compiler_params=_params("parallel", "arbitrary"),
        name="dsa_sample_attention",
    )(page_table, qs, scores, theta, *([cache_kv] * PG_STEP), kv_new)


_NSA_SLOTS = tuple(h // NSA_HPG for h in range(NSA_HEADS))
_DSA_SLOTS = tuple((h // DSA_HPG) % 2 for h in range(DSA_HEADS))
def _even_segs(q_dtype, cm):
    return ((NSA_HEADS * LANE, (1,) * NSA_HEADS, q_dtype, False), (4 * LANE, (1, 0, 1, 0), F32, cm),
            (2 * LANE, (1, 0), F32, False), (LANE, (0,), F32, False),
            (4 * HGRN_KW, (0,) * (4 * HGRN_KW // LANE), F32, False))


def _odd_segs(q_dtype, cm):
    return ((DSA_HEADS * LANE, (1,) * DSA_HEADS, q_dtype, False), (4 * LANE, (1, 1, 0, 0), F32, cm),
            (IDX_QW, (1,) * 4, q_dtype, False), (LANE, (1,), F32, cm), (LANE, (0,), F32, False))


def _even_weights(w_in, w_out):
    c = np.cumsum([0, NSA_QW] + [NSA_KVW] * 6 + [NSA_GW] + [HGRN_KW] * 4)
    w = jnp.concatenate([
        _pad_head_cols(w_in[:, :NSA_QW] * HEAD_DIM ** -0.5, _NSA_SLOTS),
        w_in[:, c[1]:c[5]], w_in[:, c[5]:c[7]], _pad_cols(w_in[:, c[7]:c[8]], LANE), w_in[:, c[8]:c[12]]],
        axis=1).astype(BF16)
    w_outs = [_pad_head_cols(w_out[:NSA_QW].T, _NSA_SLOTS).T.astype(BF16), w_out[NSA_QW:].astype(BF16)]
    return w, w_outs


def _odd_weights(w_in, w_out):
    c = np.cumsum([0, DSA_QW, DSA_KVW, DSA_KVW, IDX_QW, IDX_DIM, IDX_HEADS])
    w = jnp.concatenate([
        _pad_head_cols(w_in[:, :DSA_QW] * HEAD_DIM ** -0.5, _DSA_SLOTS),
        w_in[:, c[1]:c[3]], w_in[:, c[3]:c[4]] * IDX_DIM ** -0.5,
        _pad_cols(w_in[:, c[4]:c[5]], LANE), _pad_cols(w_in[:, c[5]:c[6]], LANE)], axis=1).astype(BF16)
    return w, [_pad_head_cols(w_out.T, _DSA_SLOTS).T.astype(BF16)]


def even_mixer_prompt(x, scale, shift, cos_t, sin_t, w_in, w_out, pe_k, pe_v, w_ck, w_cv, lb, norm_g, win_len):
    B, T, _ = x.shape
    assert T >= win_len
    w, w_outs = _even_weights(w_in, w_out)
    qp, rows, win, gl, hx, rows_cm = proj_segments(x, scale, shift, cos_t, sin_t, w, _even_segs(BF16, True))
    o_a = nsa_prompt(qp, rows, win, gl, pe_k, pe_v, w_ck, w_cv)
    o_b, s_new = hgrn_mixer(hx, lb, norm_g, jnp.zeros((B, HGRN_HEADS, HGRN_DK, HGRN_DV), F32))
    new_rows = jnp.moveaxis(rows_cm.reshape(B, 4, NSA_KV, HEAD_DIM, T), 4, 1)
    win_state = win[:, T - win_len:].reshape(B, win_len, 2, NSA_KV, HEAD_DIM)
    return [o_a, o_b], w_outs, new_rows, win_state, s_new


def odd_mixer_prompt(x, scale, shift, cos_t, sin_t, w_in, w_out):
    B, T, _ = x.shape
    w, w_outs = _odd_weights(w_in, w_out)
    qp, kv, qi, ki, wi, kv_cm, ki_cm = proj_segments(x, scale, shift, cos_t, sin_t, w, _odd_segs(BF16, True))
    o = dsa_prompt(qp, kv, qi, ki, wi)
    new_kv = jnp.moveaxis(kv_cm.reshape(B, 2, DSA_KV, HEAD_DIM, T), 4, 1)
    return [o], w_outs, new_kv, jnp.moveaxis(ki_cm[:, :IDX_DIM], 2, 1)


def _stack_rows(a, B, n, heads, width):
    s = a.reshape(B, n, heads, width).transpose(0, 2, 1, 3).reshape(B, heads * n, width)
    return jnp.pad(s, ((0, 0), (0, LANE - heads * n), (0, 0)))


def _unstack_rows(o, B, n, heads):
    return o[:, :heads * n].reshape(B, heads, n, LANE).transpose(0, 2, 1, 3).reshape(1, B * n, heads * LANE)


def even_mixer_sample(x, scale, shift, past_len, B, w_in, w_out, pe_k, pe_v, w_ck, w_cv, lb, norm_g,
                      cache, cache_win, state, page_table, layer):
    n = x.shape[1] // B
    win_len = cache_win.shape[2]
    cos_t, sin_t = [jnp.tile(a, (B, 1)) for a in rope_tables(past_len + jnp.arange(n))]
    w, w_outs = _even_weights(w_in, w_out)
    qp, rows, win, gl, hx = proj_segments(x, scale, shift, cos_t, sin_t, w, _even_segs(F32, False))
    rows, win, hx = [a.reshape(B, n, a.shape[-1]) for a in (rows, win, hx)]
    qs = _stack_rows(qp, B, n, NSA_HEADS, LANE).astype(BF16)
    gcol = jnp.pad(_stack_rows(gl[..., :NSA_GW], B, n, NSA_HEADS, 3), ((0, 0), (0, 0), (0, LANE - 3)))
    cache_cm = _channel_major(cache)
    fs = nsa_sample_compress(cache_cm, layer, page_table, pe_k, pe_v, w_ck, w_cv)
    o = nsa_sample_attention(qs, gcol, fs, cache_cm, layer, page_table, rows, _channel_major(cache_win), win)
    o_a = _unstack_rows(o, B, n, NSA_HEADS)
    o_b, s_new = hgrn_mixer(hx, lb, norm_g, state)
    new_rows = rows.reshape(B, n, 4, NSA_KV, HEAD_DIM)
    win_state = jnp.concatenate([cache_win[layer], win.reshape(B, n, 2, NSA_KV, HEAD_DIM)], axis=1)[:, -win_len:]
    return [o_a, o_b.reshape(1, B * n, HGRN_VW)], w_outs, new_rows, win_state, s_new


def odd_mixer_sample(x, scale, shift, past_len, B, w_in, w_out, cache_kv, cache_idx, page_table, layer):
    n = x.shape[1] // B
    cos_t, sin_t = [jnp.tile(a, (B, 1)) for a in rope_tables(past_len + jnp.arange(n))]
    w, w_outs = _odd_weights(w_in, w_out)
    qp, kv, qi, ki, wi = proj_segments(x, scale, shift, cos_t, sin_t, w, _odd_segs(F32, False))
    kv, ki = kv.reshape(B, n, 4 * LANE), ki.reshape(B, n, LANE)[..., :IDX_DIM]
    qs = _stack_rows(qp, B, n, DSA_HEADS, LANE).astype(BF16)
    qi4 = qi.reshape(1, B * n, IDX_HEADS // 2, 2, IDX_DIM)
    qe = _stack_rows(qi4[:, :, :, 0], B, n, IDX_HEADS // 2, IDX_DIM).astype(BF16)
    qo = _stack_rows(qi4[:, :, :, 1], B, n, IDX_HEADS // 2, IDX_DIM).astype(BF16)
    wcol = jnp.pad(_stack_rows(wi[..., :IDX_HEADS], B, n, IDX_HEADS // 2, 2), ((0, 0), (0, 0), (0, LANE - 2)))
    scores, theta = dsa_sample_index(qe, qo, wcol, _channel_major(cache_idx), layer, page_table, ki)
    o = dsa_sample_attention(qs, scores, theta, _channel_major(cache_kv), layer, page_table, kv)
    return [_unstack_rows(o, B, n, DSA_HEADS)], w_outs, kv.reshape(B, n, 2, DSA_KV, HEAD_DIM), ki


def moe_ffn_residual(xs, hs, logits_list, gates, w1, w3, w2):
    D = D_MODEL
    h_all = jnp.concatenate([h.reshape(-1, D) for h in hs], axis=0)
    logits = jnp.concatenate([lg.reshape(-1, lg.shape[-1])[:, :N_EXPERTS] for lg in logits_list], axis=0)
    n_tok = h_all.shape[0]
    top_v, top_i = lax.top_k(logits, TOP_K)
    weights = jax.nn.softmax(top_v, axis=-1)
    e_flat = top_i.reshape(-1)
    onehot = (e_flat[:, None] == jnp.arange(N_EXPERTS)[None, :]).astype(jnp.int32)
    csum = jnp.cumsum(onehot, axis=0)
    counts = csum[-1]
    rank = jnp.take_along_axis(csum, e_flat[:, None], axis=1)[:, 0] - 1
    padded = ((counts + MOE_TILE - 1) // MOE_TILE) * MOE_TILE
    group_end = jnp.cumsum(padded)
    group_start = group_end - padded
    slot = group_start[e_flat] + rank
    n_slots = _round_up(n_tok * TOP_K, MOE_TILE) + N_EXPERTS * MOE_TILE
    n_tiles = n_slots // MOE_TILE
    tok_of_slot = jnp.zeros((n_slots,), jnp.int32).at[slot].set(jnp.arange(n_tok * TOP_K, dtype=jnp.int32) // TOP_K)
    tile_start = jnp.arange(n_tiles, dtype=jnp.int32) * MOE_TILE
    tile_expert = jnp.minimum(jnp.sum(tile_start[:, None] >= group_end[None, :], axis=1), N_EXPERTS - 1).astype(jnp.int32)
    n_used = (group_end[-1] // MOE_TILE).astype(jnp.int32).reshape(1)
    h_sorted = h_all[tok_of_slot]
    y_slot = moe_grouped_swiglu(h_sorted, tile_expert, n_used, w1, w3, w2)
    slot2 = slot.reshape(n_tok, TOP_K)
    outs = []
    off = 0
    for x, g in zip(xs, gates):
        n = x.shape[0] * x.shape[1]
        ya = y_slot[slot2[off:off + n, 0]].reshape(x.shape)
        yb = y_slot[slot2[off:off + n, 1]].reshape(x.shape)
        w = weights[off:off + n].reshape(x.shape[0], x.shape[1], TOP_K)
        outs.append(moe_combine_residual(x, g, w, ya, yb))
        off += n
    return outs


def kernel(x_prompt, x_sample, cache_nsa, cache_nsa_win, state_hgrn, cache_dsa_kv, cache_dsa_idx, page_table, c_prompt, c_sample, ada_w, ada_b, norm1_g, norm2_g, final_g, even_w_in, even_w_out, nsa_pe_k, nsa_pe_v, nsa_w_ck, nsa_w_cv, hgrn_lb_raw, hgrn_norm_g, ffn_w1, ffn_w3, ffn_w2, odd_w_in, odd_w_out, router_w, router_b, moe_w1, moe_w3, moe_w2):
    D = D_MODEL
    past_len = page_table.shape[1] * PAGE_SIZE
    win_len = cache_nsa_win.shape[2]
    Bp, Tp = x_prompt.shape[:2]
    Bs, Ts = x_sample.shape[:2]
    lb_soft = jax.nn.softmax(hgrn_lb_raw.astype(F32), axis=0)
    lower_bounds = jnp.cumsum(lb_soft, axis=0) - lb_soft[0]

    R = _round_up(Bp + Bs, 8)
    c_all = jnp.zeros((R, D), F32).at[:Bp].set(c_prompt).at[Bp:Bp + Bs].set(c_sample)
    mods = ada_modulation(c_all, ada_w, ada_b)

    def group_mods(l, lo, n, per_token_rows):
        m = mods[l, lo:lo + n].reshape(n, 6, D)
        sh1, sc1, g1, sh2, sc2, g2 = [m[:, j] for j in range(6)]
        s1 = norm1_g[l][None] * (1.0 + sc1)
        s2 = norm2_g[l][None] * (1.0 + sc2)
        vecs = [s1, sh1, g1, s2, sh2, g2]
        if per_token_rows:
            return [jnp.repeat(v, per_token_rows, axis=0)[None] for v in vecs]
        return [v[:, None, :] for v in vecs]

    def pad_cols(w, n):
        return jnp.pad(w, ((0, 0), (0, n - w.shape[1]))).astype(BF16)

    xp = x_prompt
    xs = x_sample.reshape(1, Bs * Ts, D)
    cos_p, sin_p = rope_tables(jnp.arange(Tp))
    outs_p = dict(rows=[], win=[], st=[], kv=[], idx=[])
    outs_s = dict(rows=[], win=[], st=[], kv=[], idx=[])
    for l in range(DEPTH):
        i = l // 2
        mp = group_mods(l, 0, Bp, 0)
        msm = group_mods(l, Bp, Bs, Ts)
        if l % 2 == 0:
            w1, w3, w2 = ffn_w1[i].astype(BF16), ffn_w3[i].astype(BF16), ffn_w2[i].astype(BF16)
            mixed, w_outs, rows, win, s_new = even_mixer_prompt(
                xp, mp[0], mp[1], cos_p, sin_p, even_w_in[i], even_w_out[i], nsa_pe_k[i], nsa_pe_v[i],
                nsa_w_ck[i], nsa_w_cv[i], lower_bounds[i], hgrn_norm_g[i], win_len)
            outs_p['rows'].append(rows)
            outs_p['win'].append(win)
            outs_p['st'].append(s_new)
            xp = out_proj_residual(mixed, w_outs, xp, mp[2])
            xp = ffn_residual(xp, mp[3], mp[4], mp[5], w1, w3, w2)
            mixed, w_outs, rows, win, s_new = even_mixer_sample(
                xs, msm[0], msm[1], past_len, Bs, even_w_in[i], even_w_out[i], nsa_pe_k[i], nsa_pe_v[i],
                nsa_w_ck[i], nsa_w_cv[i], lower_bounds[i], hgrn_norm_g[i],
                cache_nsa, cache_nsa_win, state_hgrn[i], page_table, i)
            outs_s['rows'].append(rows)
            outs_s['win'].append(win)
            outs_s['st'].append(s_new)
            xs = out_proj_residual(mixed, w_outs, xs, msm[2])
            xs = ffn_residual(xs, msm[3], msm[4], msm[5], w1, w3, w2)
        else:
            w_r = pad_cols(router_w[i], LANE)
            w1, w3, w2 = moe_w1[i].astype(BF16), moe_w3[i].astype(BF16), moe_w2[i].astype(BF16)
            o, w_outs, kv, ki = odd_mixer_prompt(xp, mp[0], mp[1], cos_p, sin_p, odd_w_in[i], odd_w_out[i])
            outs_p['kv'].append(kv)
            outs_p['idx'].append(ki)
            xp = out_proj_residual(o, w_outs, xp, mp[2])
            o, w_outs, kv, ki = odd_mixer_sample(xs, msm[0], msm[1], past_len, Bs, odd_w_in[i], odd_w_out[i],
                                                 cache_dsa_kv, cache_dsa_idx, page_table, i)
            outs_s['kv'].append(kv)
            outs_s['idx'].append(ki)
            xs = out_proj_residual(o, w_outs, xs, msm[2])
            hs, lgs = [], []
            for (x, m) in ((xp, mp), (xs, msm)):
                logits, h = norm_proj(x, m[3], m[4], w_r, with_h=True)
                lgs.append(logits[..., :N_EXPERTS] + router_b[i].astype(F32))
                hs.append(h)
            xp, xs = moe_ffn_residual([xp, xs], hs, lgs, [mp[5], msm[5]], w1, w3, w2)
    y_prompt = final_norm(xp, final_g)
    y_sample = final_norm(xs, final_g).reshape(Bs, Ts, D)
    st = lambda od, k: jnp.stack(od[k])
    return (y_prompt, y_sample, st(outs_p, 'rows'), st(outs_s, 'rows'), st(outs_p, 'win'), st(outs_s, 'win'),
            st(outs_p, 'st'), st(outs_s, 'st'), st(outs_p, 'kv'), st(outs_s, 'kv'),
            st(outs_p, 'idx'), st(outs_s, 'idx'))
```

```python
import functools
import math

import numpy as np
import jax
import jax.numpy as jnp
from jax import lax
from jax.experimental import pallas as pl
from jax.experimental.pallas import tpu as pltpu

D_MODEL = 1024
DEPTH = 4
PAGE_SIZE = 128
HEAD_DIM = 64
NSA_HEADS = 8
NSA_KV = 2
NSA_HPG = NSA_HEADS // NSA_KV
CMP_STRIDE = 16
CMP_BLK = 2 * CMP_STRIDE
SEL_BLK = 64
TOP_N = 16
WINDOW = 512
FORCE_SCORE = 1.0e4
HGRN_HEADS = 4
HGRN_DK = 128
HGRN_DV = 128
HGRN_CHUNK = 64
DSA_HEADS = 16
DSA_KV = 4
DSA_HPG = DSA_HEADS // DSA_KV
IDX_HEADS = 8
IDX_DIM = 64
IDX_TOPK = 256
D_FF = 2816
N_EXPERTS = 8
TOP_K = 2
Q_BLK = 128
ROPE_THETA = 10000.0
EPS = 1e-6
NEG = -1.0e30
NSA_QW = NSA_HEADS * HEAD_DIM
NSA_KVW = NSA_KV * HEAD_DIM
NSA_GW = NSA_HEADS * 3
HGRN_KW = HGRN_HEADS * HGRN_DK
HGRN_VW = HGRN_HEADS * HGRN_DV
EVEN_IN = NSA_QW + 6 * NSA_KVW + NSA_GW + 2 * HGRN_KW + 2 * HGRN_VW
EVEN_MIX = NSA_QW + HGRN_VW
DSA_QW = DSA_HEADS * HEAD_DIM
DSA_KVW = DSA_KV * HEAD_DIM
IDX_QW = IDX_HEADS * IDX_DIM
ODD_IN = DSA_QW + 2 * DSA_KVW + IDX_QW + IDX_DIM + IDX_HEADS
ODD_MIX = DSA_QW

LANE = 128
ROW_TILE = 512
FF_CHUNK = 256
MOE_TILE = 256
PROJ_CHUNK = 512
ATT_TQ = 128
KEY_BLK = 256
HGRN_SUB = 16
HGRN_TILE = 256
PG_STEP = 8
INT_MIN = -2 ** 31
M_FLOOR = -1.0e25
VMEM_LIMIT = 56 * 1024 * 1024

F32 = jnp.float32
BF16 = jnp.bfloat16


def _round_up(n, m):
    return -(-n // m) * m


def _params(*sem):
    return pltpu.CompilerParams(dimension_semantics=sem, vmem_limit_bytes=VMEM_LIMIT)


def _norm_mod(x, scale, shift):
    ms = jnp.mean(x * x, axis=-1, keepdims=True)
    return x * lax.rsqrt(ms + EPS) * scale + shift


def _mod_spec(mod, tm):
    if mod.shape[1] == 1:
        return pl.BlockSpec((1, 1, mod.shape[2]), lambda b, i: (b, 0, 0))
    return pl.BlockSpec((1, tm, mod.shape[2]), lambda b, i: (b, i, 0))


def _ada_body(c_ref, w_ref, b_ref, o_ref):
    c = c_ref[...]
    cs = (c * jax.nn.sigmoid(c)).astype(BF16)
    o_ref[0] = jnp.dot(cs, w_ref[0].astype(BF16), preferred_element_type=F32) + b_ref[0]


def ada_modulation(c_all, ada_w, ada_b):
    R, D = c_all.shape
    N = ada_w.shape[2]
    tn = 1536
    return pl.pallas_call(
        _ada_body,
        grid=(DEPTH, N // tn),
        in_specs=[pl.BlockSpec((R, D), lambda l, j: (0, 0)),
                  pl.BlockSpec((1, D, tn), lambda l, j: (l, 0, j)),
                  pl.BlockSpec((1, 1, tn), lambda l, j: (l, 0, j))],
        out_specs=pl.BlockSpec((1, R, tn), lambda l, j: (l, 0, j)),
        out_shape=jax.ShapeDtypeStruct((DEPTH, R, N), F32),
        compiler_params=_params("arbitrary", "arbitrary"),
        name="ada_modulation",
    )(c_all, ada_w, ada_b.reshape(DEPTH, 1, N))


def _norm_proj_body(x_ref, sc_ref, sh_ref, w_ref, o_ref, h_ref=None):
    h = _norm_mod(x_ref[0], sc_ref[0], sh_ref[0]).astype(BF16)
    o_ref[0] = jnp.dot(h, w_ref[...], preferred_element_type=F32)
    if h_ref is not None:
        h_ref[0] = h


def norm_proj(x, scale, shift, w, with_h=False):
    B, T, D = x.shape
    N = w.shape[1]
    tm = min(T, ROW_TILE)
    out_shape = [jax.ShapeDtypeStruct((B, T, N), F32)]
    out_specs = [pl.BlockSpec((1, tm, N), lambda b, i: (b, i, 0))]
    if with_h:
        out_shape.append(jax.ShapeDtypeStruct((B, T, D), BF16))
        out_specs.append(pl.BlockSpec((1, tm, D), lambda b, i: (b, i, 0)))
    res = pl.pallas_call(
        _norm_proj_body,
        grid=(B, T // tm),
        in_specs=[pl.BlockSpec((1, tm, D), lambda b, i: (b, i, 0)),
                  _mod_spec(scale, tm), _mod_spec(shift, tm),
                  pl.BlockSpec((D, N), lambda b, i: (0, 0))],
        out_specs=out_specs,
        out_shape=out_shape,
        compiler_params=_params("parallel", "parallel"),
        name="norm_proj",
    )(x, scale, shift, w)
    return res if with_h else res[0]


def _rope_chunk(y, cos, sin):
    lane = lax.broadcasted_iota(jnp.int32, y.shape, 1)
    swapped = jnp.where(lane % HEAD_DIM < HEAD_DIM // 2,
                        pltpu.roll(y, LANE - HEAD_DIM // 2, 1), pltpu.roll(y, HEAD_DIM // 2, 1))
    return y * cos + swapped * sin


def _proj_seg_body(x_ref, sc_ref, sh_ref, cos_ref, sin_ref, w_ref, *o_refs, segs):
    h = _norm_mod(x_ref[0], sc_ref[0], sh_ref[0]).astype(BF16)
    cos = cos_ref[...]
    sin = sin_ref[...]
    t_refs = iter(o_refs[len(segs):])
    c0 = 0
    for o_ref, (width, rope_flags, dtype, channel_major) in zip(o_refs, segs):
        t_ref = next(t_refs) if channel_major else None
        for j0 in range(0, width, PROJ_CHUNK):
            wd = min(PROJ_CHUNK, width - j0)
            y = jnp.dot(h, w_ref[:, c0 + j0:c0 + j0 + wd], preferred_element_type=F32)
            for k in range(wd // LANE):
                cols = slice(j0 + k * LANE, j0 + (k + 1) * LANE)
                yk = y[:, k * LANE:(k + 1) * LANE]
                if rope_flags[(j0 + k * LANE) // LANE]:
                    yk = _rope_chunk(yk, cos, sin)
                o_ref[0, :, cols] = yk.astype(dtype)
                if t_ref is not None:
                    t_ref[0, cols, :] = yk.T
        c0 += width


def proj_segments(x, scale, shift, cos_t, sin_t, w, segs):
    B, T, D = x.shape
    tm = min(T, ROW_TILE)
    N = w.shape[1]
    cm = [s for s in segs if s[3]]
    return pl.pallas_call(
        functools.partial(_proj_seg_body, segs=segs),
        grid=(B, T // tm),
        in_specs=[pl.BlockSpec((1, tm, D), lambda b, i: (b, i, 0)),
                  _mod_spec(scale, tm), _mod_spec(shift, tm),
                  pl.BlockSpec((tm, LANE), lambda b, i: (i, 0)),
                  pl.BlockSpec((tm, LANE), lambda b, i: (i, 0)),
                  pl.BlockSpec((D, N), lambda b, i: (0, 0))],
        out_specs=[pl.BlockSpec((1, tm, s[0]), lambda b, i: (b, i, 0)) for s in segs]
        + [pl.BlockSpec((1, s[0], tm), lambda b, i: (b, 0, i)) for s in cm],
        out_shape=[jax.ShapeDtypeStruct((B, T, s[0]), s[2]) for s in segs]
        + [jax.ShapeDtypeStruct((B, s[0], T), F32) for s in cm],
        compiler_params=_params("parallel", "parallel"),
        name="proj_segments",
    )(x, scale, shift, cos_t, sin_t, w)


def rope_tables(pos):
    half = HEAD_DIM // 2
    inv = ROPE_THETA ** (-jnp.arange(half, dtype=F32) / half)
    ang = pos.astype(F32)[:, None] * inv[None, :]
    cos, sin = jnp.cos(ang), jnp.sin(ang)
    return jnp.tile(cos, (1, 4)), jnp.tile(jnp.concatenate([-sin, sin], axis=1), (1, 2))


def _out_proj_body(*refs, n_in):
    a_refs, w_refs = refs[:n_in], refs[n_in:2 * n_in]
    x_ref, g_ref, o_ref = refs[2 * n_in:]
    y = None
    for a_ref, w_ref in zip(a_refs, w_refs):
        part = jnp.dot(a_ref[0].astype(BF16), w_ref[...], preferred_element_type=F32)
        y = part if y is None else y + part
    o_ref[0] = x_ref[0] + g_ref[0] * y


def out_proj_residual(a_list, w_list, x, gate):
    B, T, D = x.shape
    tm = min(T, ROW_TILE)
    n_in = len(a_list)
    return pl.pallas_call(
        functools.partial(_out_proj_body, n_in=n_in),
        grid=(B, T // tm),
        in_specs=[pl.BlockSpec((1, tm, a.shape[2]), lambda b, i: (b, i, 0)) for a in a_list]
        + [pl.BlockSpec(w.shape, lambda b, i: (0, 0)) for w in w_list]
        + [pl.BlockSpec((1, tm, D), lambda b, i: (b, i, 0)), _mod_spec(gate, tm)],
        out_specs=pl.BlockSpec((1, tm, D), lambda b, i: (b, i, 0)),
        out_shape=jax.ShapeDtypeStruct((B, T, D), F32),
        compiler_params=_params("parallel", "parallel"),
        name="out_proj_residual",
    )(*a_list, *w_list, x, gate)


def _swiglu_acc(h, w1_ref, w3_ref, w2_ref, acc_ref):
    for c in range(D_FF // FF_CHUNK):
        cols = slice(c * FF_CHUNK, (c + 1) * FF_CHUNK)
        u = jnp.dot(h, w1_ref[:, cols], preferred_element_type=F32)
        v = jnp.dot(h, w3_ref[:, cols], preferred_element_type=F32)
        a = (u * jax.nn.sigmoid(u) * v).astype(BF16)
        part = jnp.dot(a, w2_ref[cols, :], preferred_element_type=F32)
        if c == 0:
            acc_ref[...] = part
        else:
            acc_ref[...] += part


def _ffn_body(x_ref, sc_ref, sh_ref, g_ref, w1_ref, w3_ref, w2_ref, o_ref, acc_ref):
    x = x_ref[0]
    h = _norm_mod(x, sc_ref[0], sh_ref[0]).astype(BF16)
    _swiglu_acc(h, w1_ref, w3_ref, w2_ref, acc_ref)
    o_ref[0] = x + g_ref[0] * acc_ref[...]


def ffn_residual(x, scale, shift, gate, w1, w3, w2):
    B, T, D = x.shape
    tm = min(T, ROW_TILE)
    wspec = lambda shape: pl.BlockSpec(shape, lambda b, i: (0, 0))
    return pl.pallas_call(
        _ffn_body,
        grid=(B, T // tm),
        in_specs=[pl.BlockSpec((1, tm, D), lambda b, i: (b, i, 0)),
                  _mod_spec(scale, tm), _mod_spec(shift, tm), _mod_spec(gate, tm),
                  wspec((D, D_FF)), wspec((D, D_FF)), wspec((D_FF, D))],
        out_specs=pl.BlockSpec((1, tm, D), lambda b, i: (b, i, 0)),
        out_shape=jax.ShapeDtypeStruct((B, T, D), F32),
        scratch_shapes=[pltpu.VMEM((tm, D), F32)],
        compiler_params=_params("parallel", "parallel"),
        name="ffn_residual",
    )(x, scale, shift, gate, w1, w3, w2)


def _moe_body(te_ref, nt_ref, h_ref, w1_ref, w3_ref, w2_ref, o_ref, acc_ref):
    i = pl.program_id(0)

    @pl.when(i < nt_ref[0])
    def _():
        _swiglu_acc(h_ref[...], w1_ref.at[0], w3_ref.at[0], w2_ref.at[0], acc_ref)
        o_ref[...] = acc_ref[...]

    @pl.when(i >= nt_ref[0])
    def _():
        o_ref[...] = jnp.zeros_like(o_ref)


def moe_grouped_swiglu(h_sorted, tile_expert, n_tiles_used, w1, w3, w2):
    S, D = h_sorted.shape
    tm = MOE_TILE
    n_tiles = S // tm
    wspec = lambda shape: pl.BlockSpec((1,) + shape, lambda i, te, nt: (te[i], 0, 0))
    grid_spec = pltpu.PrefetchScalarGridSpec(
        num_scalar_prefetch=2,
        grid=(n_tiles,),
        in_specs=[pl.BlockSpec((tm, D), lambda i, te, nt: (i, 0)),
                  wspec((D, D_FF)), wspec((D, D_FF)), wspec((D_FF, D))],
        out_specs=pl.BlockSpec((tm, D), lambda i, te, nt: (i, 0)),
        scratch_shapes=[pltpu.VMEM((tm, D), F32)],
    )
    return pl.pallas_call(
        _moe_body,
        grid_spec=grid_spec,
        out_shape=jax.ShapeDtypeStruct((S, D), F32),
        compiler_params=_params("arbitrary"),
        name="moe_grouped_swiglu",
    )(tile_expert, n_tiles_used, h_sorted, w1, w3, w2)


def _combine_body(x_ref, g_ref, w_ref, a_ref, b_ref, o_ref):
    w = w_ref[0]
    o_ref[0] = x_ref[0] + g_ref[0] * (w[:, 0:1] * a_ref[0] + w[:, 1:2] * b_ref[0])


def moe_combine_residual(x, gate, w, ya, yb):
    B, T, D = x.shape
    tm = min(T, ROW_TILE)
    tile = pl.BlockSpec((1, tm, D), lambda b, i: (b, i, 0))
    return pl.pallas_call(
        _combine_body,
        grid=(B, T // tm),
        in_specs=[tile, _mod_spec(gate, tm), pl.BlockSpec((1, tm, TOP_K), lambda b, i: (b, i, 0)), tile, tile],
        out_specs=tile,
        out_shape=jax.ShapeDtypeStruct((B, T, D), F32),
        compiler_params=_params("parallel", "parallel"),
        name="moe_combine_residual",
    )(x, gate, w, ya, yb)


def _final_norm_body(x_ref, g_ref, o_ref):
    x = x_ref[0]
    ms = jnp.mean(x * x, axis=-1, keepdims=True)
    o_ref[0] = x * lax.rsqrt(ms + EPS) * g_ref[...]


def final_norm(x, g):
    B, T, D = x.shape
    tm = min(T, ROW_TILE)
    return pl.pallas_call(
        _final_norm_body,
        grid=(B, T // tm),
        in_specs=[pl.BlockSpec((1, tm, D), lambda b, i: (b, i, 0)),
                  pl.BlockSpec((1, D), lambda b, i: (0, 0))],
        out_specs=pl.BlockSpec((1, tm, D), lambda b, i: (b, i, 0)),
        out_shape=jax.ShapeDtypeStruct((B, T, D), F32),
        compiler_params=_params("parallel", "parallel"),
        name="final_norm",
    )(x, g.reshape(1, D))


def _split3(x):
    hi = x.astype(BF16)
    r1 = x - hi.astype(F32)
    mid = r1.astype(BF16)
    lo = (r1 - mid.astype(F32)).astype(BF16)
    return hi, mid, lo


def _dot01(m01, x):
    hi, mid, lo = _split3(x)
    d = lambda p: jnp.dot(m01, p, preferred_element_type=F32)
    return d(hi) + d(mid) + d(lo)


def _hgrn_body(hq_ref, hf_ref, hi_ref, hg_ref, lb_ref, ng_ref, s0_ref, o_ref, sn_ref, st_ref, *, sub, n_sub):
    i = pl.program_id(1)

    @pl.when(i == 0)
    def _():
        for h in range(HGRN_HEADS):
            st_ref[h] = s0_ref[0, h].T

    lb = lb_ref[...]
    ng = ng_ref[...]
    row = lax.broadcasted_iota(jnp.int32, (sub, sub), 0)
    col = lax.broadcasted_iota(jnp.int32, (sub, sub), 1)
    tril = (row >= col).astype(BF16)
    trow = lax.broadcasted_iota(jnp.int32, (sub, HGRN_DK), 0)

    def chunk(c, carry):
        rows = pl.ds(pl.multiple_of(c * sub, sub), sub)
        hq = hq_ref[0, rows, :]
        hf = hf_ref[0, rows, :]
        hv = hi_ref[0, rows, :]
        hg = hg_ref[0, rows, :]
        f = lb + (1.0 - lb) * jax.nn.sigmoid(hf)
        logf = jnp.log(f)
        kk = 1.0 - f
        qq = hq * jax.nn.sigmoid(hq)
        b = _dot01(tril, logf)
        bl = b[sub - 1:sub, :]
        qe = qq * jnp.exp(b)
        ke = kk * jnp.exp(bl - b)
        ebl = jnp.exp(bl)
        outs = []
        for h in range(HGRN_HEADS):
            cs = slice(h * HGRN_DK, (h + 1) * HGRN_DK)
            st = st_ref[h]
            o = lax.dot_general(qe[:, cs].astype(BF16), st.astype(BF16), (((1,), (1,)), ((), ())),
                                preferred_element_type=F32)
            bh, qh, kh, vh = b[:, cs], qq[:, cs], kk[:, cs], hv[:, cs]
            for s in range(sub):
                e = jnp.exp(jnp.minimum(bh - bh[s:s + 1, :], 0.0))
                w = jnp.where(trow >= s, e * qh * kh[s:s + 1, :], 0.0)
                o = o + jnp.sum(w, axis=-1, keepdims=True) * vh[s:s + 1, :]
            upd = lax.dot_general(vh.astype(BF16), ke[:, cs].astype(BF16), (((0,), (0,)), ((), ())),
                                  preferred_element_type=F32)
            st_ref[h] = st * ebl[:, cs] + upd
            ms = jnp.mean(o * o, axis=-1, keepdims=True)
            outs.append(o * lax.rsqrt(ms + EPS))
        o_ref[0, rows, :] = jnp.concatenate(outs, axis=-1) * ng * (hg * jax.nn.sigmoid(hg))
        return carry

    lax.fori_loop(0, n_sub, chunk, 0)

    @pl.when(i == pl.num_programs(1) - 1)
    def _():
        for h in range(HGRN_HEADS):
            sn_ref[0, h] = st_ref[h].T


def hgrn_mixer(hx, lb, norm_g, s0):
    B, T, _ = hx.shape
    tc = min(T, HGRN_TILE)
    sub = math.gcd(T, HGRN_SUB)
    spec = lambda j: pl.BlockSpec((1, tc, HGRN_KW), lambda b, i, j=j: (b, i, j))
    vec = pl.BlockSpec((1, HGRN_KW), lambda b, i: (0, 0))
    state = pl.BlockSpec((1, HGRN_HEADS, HGRN_DK, HGRN_DV), lambda b, i: (b, 0, 0, 0))
    return pl.pallas_call(
        functools.partial(_hgrn_body, sub=sub, n_sub=tc // sub),
        grid=(B, T // tc),
        in_specs=[spec(0), spec(1), spec(2), spec(3), vec, vec, state],
        out_specs=[pl.BlockSpec((1, tc, HGRN_VW), lambda b, i: (b, i, 0)), state],
        out_shape=[jax.ShapeDtypeStruct((B, T, HGRN_VW), F32),
                   jax.ShapeDtypeStruct((B, HGRN_HEADS, HGRN_DK, HGRN_DV), F32)],
        scratch_shapes=[pltpu.VMEM((HGRN_HEADS, HGRN_DV, HGRN_DK), F32)],
        compiler_params=_params("parallel", "arbitrary"),
        name="hgrn_mixer",
    )(hx, hx, hx, hx, lb.reshape(1, HGRN_KW), jnp.tile(norm_g, HGRN_HEADS).reshape(1, HGRN_VW), s0)


def _nt(a, b):
    return lax.dot_general(a, b, (((1,), (1,)), ((), ())), preferred_element_type=F32)


def _flash_step(kblk, vT, q_heads, mask, m, l, acc_ref, batched):
    tq = q_heads[0].shape[0]
    if batched:
        acc = acc_ref[...]
        scores = [_nt(kblk, qh) for qh in q_heads]
    m_out, l_out, acc_out = [], [], []
    for h, qh in enumerate(q_heads):
        cs = slice(h * tq, (h + 1) * tq)
        s = jnp.where(mask, scores[h] if batched else _nt(kblk, qh), NEG)
        m_new = jnp.maximum(m[:, cs], jnp.max(s, axis=0, keepdims=True))
        alpha = jnp.exp(m[:, cs] - m_new)
        e = jnp.exp(s - m_new)
        l_out.append(alpha * l[:, cs] + jnp.sum(e, axis=0, keepdims=True))
        pv = jnp.dot(vT, e.astype(BF16), preferred_element_type=F32)
        if batched:
            acc_out.append(alpha * acc[:, cs] + pv)
        else:
            acc_ref[:, cs] = alpha * acc_ref[:, cs] + pv
        m_out.append(m_new)
    if batched:
        acc_ref[...] = jnp.concatenate(acc_out, axis=1)
    return jnp.concatenate(m_out, axis=1), jnp.concatenate(l_out, axis=1)


def _head_slabs(qp_ref, first_head, n):
    return [qp_ref[0, :, (first_head + h) * LANE:(first_head + h + 1) * LANE] for h in range(n)]


def _stack_heads(qp_ref, first_head, n):
    return jnp.concatenate(_head_slabs(qp_ref, first_head, n), axis=0)


def _store_heads(o_ref, oT, first_head, n, slot):
    tq = oT.shape[1] // n
    lane = lax.broadcasted_iota(jnp.int32, (tq, LANE), 1)
    valid = (lane >= HEAD_DIM * slot) & (lane < HEAD_DIM * (slot + 1))
    for h in range(n):
        blk = oT[:, h * tq:(h + 1) * tq].T
        o_ref[0, :, (first_head + h) * LANE:(first_head + h + 1) * LANE] = jnp.where(valid, blk, 0.0).astype(o_ref.dtype)


def _nsa_body(qp_ref, rows_ref, win_ref, gl_ref, pek_ref, pev_ref, wck_ref, wcv_ref, o_ref,
              kcmp_ref, vcmpT_ref, ks_ref, vsT_ref, kw_ref, vwT_ref, stage_ref, shift_ref, sel_ref, acc_ref, *, T):
    i = pl.program_id(1)
    tq = ATT_TQ
    nq = NSA_HPG * tq
    n_half = T // CMP_STRIDE
    n_sel = T // SEL_BLK
    k_sel = min(TOP_N, n_sel)
    per_kb = KEY_BLK // SEL_BLK

    @pl.when(i == 0)
    def _prepare():
        nrow = lax.broadcasted_iota(jnp.int32, (n_half, LANE), 0)
        for slab, pe_ref, w_ref in ((0, pek_ref, wck_ref), (1, pev_ref, wcv_ref)):
            def stage_blk(kb, carry, slab=slab):
                rs = pl.ds(pl.multiple_of(kb * KEY_BLK, KEY_BLK), KEY_BLK)
                stage_ref[rs, :] = rows_ref[0, rs, slab * LANE:(slab + 1) * LANE]
                return carry

            lax.fori_loop(0, T // KEY_BLK, stage_blk, 0)
            first = jnp.zeros((n_half, LANE), F32)
            second = jnp.zeros((n_half, LANE), F32)
            for r in range(CMP_STRIDE):
                y = stage_ref[pl.ds(r, n_half, stride=CMP_STRIDE), :]
                first = first + jnp.dot((y + pe_ref[r:r + 1, :]).astype(BF16), w_ref[r],
                                        preferred_element_type=F32)
                second = second + jnp.dot((y + pe_ref[r + CMP_STRIDE:r + CMP_STRIDE + 1, :]).astype(BF16),
                                          w_ref[r + CMP_STRIDE], preferred_element_type=F32)
            shift_ref[0:n_half, :] = second
            shift_ref[n_half:n_half + 8, :] = jnp.zeros((8, LANE), F32)
            c = jnp.where(nrow < n_half - 1, first + shift_ref[1:n_half + 1, :], 0.0)
            if slab == 0:
                kcmp_ref[...] = c.astype(BF16)
            else:
                for j in range(n_half // LANE):
                    vcmpT_ref[:, j * LANE:(j + 1) * LANE] = c[j * LANE:(j + 1) * LANE, :].T.astype(BF16)

        def copy_blk(kb, carry):
            for half in range(KEY_BLK // LANE):
                rs = pl.ds(pl.multiple_of(kb * KEY_BLK + half * LANE, LANE), LANE)
                hs = slice(half * LANE, (half + 1) * LANE)
                ks_ref[kb, hs, :] = rows_ref[0, rs, 2 * LANE:3 * LANE].astype(BF16)
                vsT_ref[kb, :, hs] = rows_ref[0, rs, 3 * LANE:4 * LANE].T.astype(BF16)
                kw_ref[kb, hs, :] = win_ref[0, rs, 0:LANE].astype(BF16)
                vwT_ref[kb, :, hs] = win_ref[0, rs, LANE:2 * LANE].T.astype(BF16)
            return carry

        lax.fori_loop(0, T // KEY_BLK, copy_blk, 0)

    t0 = i * tq
    lane_t = t0 + (lax.broadcasted_iota(jnp.int32, (1, nq), 1) & (tq - 1))
    t_row = lane_t[:, 0:tq]
    gT = jax.nn.sigmoid(gl_ref[0].T)
    kio = lax.broadcasted_iota(jnp.int32, (KEY_BLK, tq), 0)
    nio = lax.broadcasted_iota(jnp.int32, (n_half, nq), 0)
    jcol = lax.broadcasted_iota(jnp.int32, (n_sel, tq), 0)
    pj = lax.broadcasted_iota(jnp.int32, (n_sel, n_half), 0)
    pi = lax.broadcasted_iota(jnp.int32, (n_sel, n_half), 1)
    pool = ((((pi + 1) >> 2) == pj) & (pi < n_half - 1)).astype(BF16)
    n_kb = (t0 + tq + KEY_BLK - 1) // KEY_BLK
    n_full = t0 // KEY_BLK
    m0 = jnp.full((1, nq), M_FLOOR, F32)
    l0 = jnp.zeros((1, nq), F32)

    for g in range(NSA_KV):
        q_heads = _head_slabs(qp_ref, NSA_HPG * g, NSA_HPG)
        qs = jnp.concatenate(q_heads, axis=0)
        cmask = (CMP_STRIDE * nio + CMP_BLK <= lane_t + 1) & (nio < n_half - 1)
        s = jnp.where(cmask, _nt(kcmp_ref[...], qs), NEG)
        e = jnp.where(cmask, jnp.exp(s - jnp.max(s, axis=0, keepdims=True)), 0.0)
        p = e * (1.0 / jnp.maximum(jnp.sum(e, axis=0, keepdims=True), 1e-30))
        ocT = jnp.dot(vcmpT_ref[...], p.astype(BF16), preferred_element_type=F32)
        imp = p[:, 0:tq]
        for h in range(1, NSA_HPG):
            imp = imp + p[:, h * tq:(h + 1) * tq]
        imp_sel = _dot01(pool, imp)
        cur = t_row >> 6
        forced = (jcol == 0) | (jcol == cur) | (jcol == cur - 1)
        visible = jcol <= cur
        score = jnp.where(visible, imp_sel + jnp.where(forced, FORCE_SCORE, 0.0), NEG)
        rank = jnp.zeros((n_sel, tq), jnp.int32)
        for j2 in range(n_sel):
            r = score[j2:j2 + 1, :]
            beats = (r > score) | ((r == score) & (jcol > j2))
            rank = rank + beats.astype(jnp.int32)
        sel_ref[...] = jnp.where((rank < k_sel) & visible, 1.0, 0.0)

        def sel_step(kb, carry, causal):
            rowsel = jnp.concatenate(
                [jnp.broadcast_to(sel_ref[pl.ds(kb * per_kb + k, 1), :], (SEL_BLK, tq)) for k in range(per_kb)],
                axis=0)
            mask = rowsel > 0.5
            if causal:
                mask = mask & (kb * KEY_BLK + kio <= t_row)
            return _flash_step(ks_ref[kb], vsT_ref[kb], q_heads, mask, carry[0], carry[1], acc_ref, True)

        acc_ref[...] = jnp.zeros_like(acc_ref)
        ml = lax.fori_loop(0, n_full, functools.partial(sel_step, causal=False), (m0, l0))
        _, l = lax.fori_loop(n_full, n_kb, functools.partial(sel_step, causal=True), ml)
        osT = acc_ref[...] * (1.0 / jnp.maximum(l, 1e-30))

        def win_step(kb, carry):
            kpos = kb * KEY_BLK + kio
            mask = (kpos <= t_row) & (kpos > t_row - WINDOW)
            return _flash_step(kw_ref[kb], vwT_ref[kb], q_heads, mask, carry[0], carry[1], acc_ref, True)

        acc_ref[...] = jnp.zeros_like(acc_ref)
        w_lo = jnp.maximum(t0 - (WINDOW - 1), 0) // KEY_BLK
        _, l = lax.fori_loop(w_lo, n_kb, win_step, (m0, l0))
        owT = acc_ref[...] * (1.0 / jnp.maximum(l, 1e-30))

        def gate(c):
            return jnp.concatenate([gT[3 * (NSA_HPG * g + h) + c:3 * (NSA_HPG * g + h) + c + 1, :]
                                    for h in range(NSA_HPG)], axis=1)

        oT = gate(0) * ocT + gate(1) * osT + gate(2) * owT
        _store_heads(o_ref, oT, NSA_HPG * g, NSA_HPG, g)


def _block_diag2(w):
    w3 = w.reshape(CMP_BLK, HEAD_DIM, HEAD_DIM)
    z = jnp.zeros_like(w3)
    return jnp.concatenate([jnp.concatenate([w3, z], axis=2), jnp.concatenate([z, w3], axis=2)], axis=1).astype(BF16)


def nsa_prompt(qp, rows, win, gl, pe_k, pe_v, w_ck, w_cv):
    B, T, _ = rows.shape
    assert T % (CMP_STRIDE * LANE) == 0 and T % KEY_BLK == 0
    tq = ATT_TQ
    nq = NSA_HPG * tq
    n_half = T // CMP_STRIDE
    n_kb = T // KEY_BLK
    const = lambda shape: pl.BlockSpec(shape, lambda b, i: (0,) * len(shape))
    return pl.pallas_call(
        functools.partial(_nsa_body, T=T),
        grid=(B, T // tq),
        in_specs=[pl.BlockSpec((1, tq, NSA_HEADS * LANE), lambda b, i: (b, i, 0)),
                  pl.BlockSpec((1, T, 4 * LANE), lambda b, i: (b, 0, 0)),
                  pl.BlockSpec((1, T, 2 * LANE), lambda b, i: (b, 0, 0)),
                  pl.BlockSpec((1, tq, LANE), lambda b, i: (b, i, 0)),
                  const((CMP_BLK, LANE)), const((CMP_BLK, LANE)),
                  const((CMP_BLK, LANE, LANE)), const((CMP_BLK, LANE, LANE))],
        out_specs=pl.BlockSpec((1, tq, NSA_HEADS * LANE), lambda b, i: (b, i, 0)),
        out_shape=jax.ShapeDtypeStruct((B, T, NSA_HEADS * LANE), BF16),
        scratch_shapes=[pltpu.VMEM((n_half, LANE), BF16), pltpu.VMEM((LANE, n_half), BF16),
                        pltpu.VMEM((n_kb, KEY_BLK, LANE), BF16), pltpu.VMEM((n_kb, LANE, KEY_BLK), BF16),
                        pltpu.VMEM((n_kb, KEY_BLK, LANE), BF16), pltpu.VMEM((n_kb, LANE, KEY_BLK), BF16),
                        pltpu.VMEM((T, LANE), F32),
                        pltpu.VMEM((n_half + 8, LANE), F32), pltpu.VMEM((T // SEL_BLK, tq), F32),
                        pltpu.VMEM((LANE, nq), F32)],
        compiler_params=_params("parallel", "arbitrary"),
        name="nsa_prompt",
    )(qp, rows, win, gl, jnp.tile(pe_k, (1, 2)), jnp.tile(pe_v, (1, 2)), _block_diag2(w_ck), _block_diag2(w_cv))


def _dsa_body(qp_ref, kv_ref, qi_ref, ki_ref, wi_ref, o_ref,
              k_ref, vT_ref, kilo_ref, kihi_ref, key_ref, acc_ref, *, T, n_keep):
    i = pl.program_id(1)
    tq = ATT_TQ
    nq = DSA_HPG * tq

    @pl.when(i == 0)
    def _prepare():
        def copy_blk(kb, carry):
            for half in range(KEY_BLK // LANE):
                rs = pl.ds(pl.multiple_of(kb * KEY_BLK + half * LANE, LANE), LANE)
                hs = slice(half * LANE, (half + 1) * LANE)
                for slab in range(2):
                    k_ref[slab, kb, hs, :] = kv_ref[0, rs, slab * LANE:(slab + 1) * LANE].astype(BF16)
                    vT_ref[slab, kb, :, hs] = kv_ref[0, rs, (2 + slab) * LANE:(3 + slab) * LANE].T.astype(BF16)
                kix = ki_ref[0, rs, :]
                kilo_ref[kb, hs, :] = kix.astype(BF16)
                kihi_ref[kb, hs, :] = pltpu.roll(kix, IDX_DIM, 1).astype(BF16)
            return carry

        lax.fori_loop(0, T // KEY_BLK, copy_blk, 0)

    t0 = i * tq
    n_kb = (t0 + tq + KEY_BLK - 1) // KEY_BLK
    t_row = t0 + lax.broadcasted_iota(jnp.int32, (1, tq), 1)
    kio_q = lax.broadcasted_iota(jnp.int32, (KEY_BLK, tq), 0)

    wT = wi_ref[0].T * (IDX_HEADS ** -0.5)
    qi = _stack_heads(qi_ref, 0, IDX_HEADS // 2)

    def idx_step(kb, carry):
        s_lo = _nt(kilo_ref[kb], qi)
        s_hi = _nt(kihi_ref[kb], qi)
        sc = jnp.zeros((KEY_BLK, tq), F32)
        for p in range(IDX_HEADS // 2):
            cs = slice(p * tq, (p + 1) * tq)
            sc = sc + jnp.maximum(s_lo[:, cs], 0.0) * wT[2 * p:2 * p + 1, :]
            sc = sc + jnp.maximum(s_hi[:, cs], 0.0) * wT[2 * p + 1:2 * p + 2, :]
        sc = jnp.where(kb * KEY_BLK + kio_q <= t_row, sc, NEG)
        bits = lax.bitcast_convert_type(sc, jnp.int32)
        key_ref[kb] = jnp.where(bits < 0, bits ^ 0x7FFFFFFF, bits)
        return carry

    lax.fori_loop(0, n_kb, idx_step, 0)

    def bit_step(it, theta):
        cand = theta + lax.shift_left(jnp.int32(1), 31 - it)

        def cnt_step(kb, c):
            return c + jnp.sum((key_ref[kb] >= cand).astype(jnp.int32), axis=0, keepdims=True)

        cnt = lax.fori_loop(0, n_kb, cnt_step, jnp.zeros((1, tq), jnp.int32))
        return jnp.where(cnt >= n_keep, cand, theta)

    theta = lax.fori_loop(0, 32, bit_step, jnp.full((1, tq), INT_MIN, jnp.int32))

    def count(pred):
        step = lambda kb, c: c + jnp.sum(pred(key_ref[kb]).astype(jnp.int32), axis=0, keepdims=True)
        return lax.fori_loop(0, n_kb, step, jnp.zeros((1, tq), jnp.int32))

    n_ge = count(lambda k: k >= theta)

    @pl.when(jnp.max(n_ge) > n_keep)
    def _break_ties():
        need = (n_keep - count(lambda k: k > theta)).astype(F32)
        before = (lax.broadcasted_iota(jnp.int32, (KEY_BLK, KEY_BLK), 1)
                  < lax.broadcasted_iota(jnp.int32, (KEY_BLK, KEY_BLK), 0)).astype(BF16)

        def fix_step(kb, seen):
            keys = key_ref[kb]
            eq = keys == theta
            eqf = jnp.where(eq, 1.0, 0.0)
            rank = seen + jnp.dot(before, eqf.astype(BF16), preferred_element_type=F32)
            key_ref[kb] = jnp.where(eq & (rank >= need), INT_MIN, keys)
            return seen + jnp.sum(eqf, axis=0, keepdims=True)

        lax.fori_loop(0, n_kb, fix_step, jnp.zeros((1, tq), F32))

    q_heads = [_head_slabs(qp_ref, DSA_HPG * g, DSA_HPG) for g in range(DSA_KV)]
    m0 = jnp.full((1, nq), M_FLOOR, F32)
    l0 = jnp.zeros((1, nq), F32)
    acc_ref[...] = jnp.zeros_like(acc_ref)

    def att_step(kb, carry, causal):
        mask = key_ref[kb] >= theta
        if causal:
            mask = mask & (kb * KEY_BLK + kio_q <= t_row)
        out = []
        for g in range(DSA_KV):
            m, l = _flash_step(k_ref[g // 2, kb], vT_ref[g // 2, kb], q_heads[g], mask,
                               carry[2 * g], carry[2 * g + 1], acc_ref.at[g], False)
            out += [m, l]
        return tuple(out)

    mid = lax.fori_loop(0, t0 // KEY_BLK, functools.partial(att_step, causal=False), (m0, l0) * DSA_KV)
    fin = lax.fori_loop(t0 // KEY_BLK, n_kb, functools.partial(att_step, causal=True), mid)
    for g in range(DSA_KV):
        oT = acc_ref[g] * (1.0 / jnp.maximum(fin[2 * g + 1], 1e-30))
        _store_heads(o_ref, oT, DSA_HPG * g, DSA_HPG, g % 2)


def dsa_prompt(qp, kv, qi, ki, wi):
    B, T, _ = kv.shape
    assert T % KEY_BLK == 0
    tq = ATT_TQ
    nq = DSA_HPG * tq
    n_kb = T // KEY_BLK
    n_keep = min(IDX_TOPK, T // 4)
    tile = lambda w: pl.BlockSpec((1, tq, w), lambda b, i: (b, i, 0))
    whole = lambda w: pl.BlockSpec((1, T, w), lambda b, i: (b, 0, 0))
    return pl.pallas_call(
        functools.partial(_dsa_body, T=T, n_keep=n_keep),
        grid=(B, T // tq),
        in_specs=[tile(DSA_HEADS * LANE), whole(4 * LANE), tile(IDX_QW), whole(LANE), tile(LANE)],
        out_specs=tile(DSA_HEADS * LANE),
        out_shape=jax.ShapeDtypeStruct((B, T, DSA_HEADS * LANE), BF16),
        scratch_shapes=[pltpu.VMEM((2, n_kb, KEY_BLK, LANE), BF16), pltpu.VMEM((2, n_kb, LANE, KEY_BLK), BF16),
                        pltpu.VMEM((n_kb, KEY_BLK, LANE), BF16), pltpu.VMEM((n_kb, KEY_BLK, LANE), BF16),
                        pltpu.VMEM((n_kb, KEY_BLK, tq), jnp.int32), pltpu.VMEM((DSA_KV, LANE, nq), F32)],
        compiler_params=_params("parallel", "arbitrary"),
        name="dsa_prompt",
    )(qp, kv, qi, ki, wi)


def _pad_head_cols(w, slots):
    D = w.shape[0]
    H = len(slots)
    w3 = w.reshape(D, H, HEAD_DIM)
    z = jnp.zeros_like(w3)
    s = jnp.asarray(slots)[None, :, None]
    return jnp.concatenate([jnp.where(s == 0, w3, z), jnp.where(s == 1, w3, z)], axis=2).reshape(D, H * LANE)


def _pad_cols(w, n):
    return jnp.pad(w, ((0, 0), (0, n - w.shape[1])))


def _dot01_r(x, m01):
    hi, mid, lo = _split3(x)
    d = lambda p: jnp.dot(p, m01, preferred_element_type=F32)
    return d(hi) + d(mid) + d(lo)


def _softmax_rows(s, mask):
    s = jnp.where(mask, s, NEG)
    e = jnp.where(mask, jnp.exp(s - jnp.max(s, axis=1, keepdims=True)), 0.0)
    return e * (1.0 / jnp.maximum(jnp.sum(e, axis=1, keepdims=True), 1e-30))


def _flash_rows(s, mask, m_ref, l_ref):
    s = jnp.where(mask, s, NEG)
    m = m_ref[...]
    m_new = jnp.maximum(m, jnp.max(s, axis=1, keepdims=True))
    alpha = jnp.exp(m - m_new)
    e = jnp.where(mask, jnp.exp(s - m_new), 0.0)
    l_ref[...] = alpha * l_ref[...] + jnp.sum(e, axis=1, keepdims=True)
    m_ref[...] = m_new
    return alpha, e


def _sort_key(x):
    bits = lax.bitcast_convert_type(x, jnp.int32)
    return jnp.where(bits < 0, bits ^ 0x7FFFFFFF, bits)


def _page_spec(layer, k, chans, blk):
    return pl.BlockSpec((None, 1, chans, PAGE_SIZE), lambda b, s, pt: (layer, pt[b, s * PG_STEP + k], blk, 0))


def _channel_major(cache):
    layers, n, rows = cache.shape[:3]
    return jnp.moveaxis(cache.reshape(layers, n, rows, -1), 2, 3)


def _per_request(shape):
    return pl.BlockSpec((1,) + shape, lambda b, s, pt: (b,) + (0,) * len(shape))


def _nsa_cmp_body(pt_ref, *refs):
    pages = refs[:PG_STEP]
    pek_ref, pev_ref, wck_ref, wcv_ref, o_ref, stage_ref = refs[PG_STEP:]
    n_half = PG_STEP * PAGE_SIZE // CMP_STRIDE
    for slab, pe_ref, w_ref in ((0, pek_ref, wck_ref), (1, pev_ref, wcv_ref)):
        for pg in range(PG_STEP):
            stage_ref[pg * PAGE_SIZE:(pg + 1) * PAGE_SIZE, :] = pages[pg][0, slab * LANE:(slab + 1) * LANE, :].T
        first = jnp.zeros((n_half, LANE), F32)
        second = jnp.zeros((n_half, LANE), F32)
        for r in range(CMP_STRIDE):
            y = stage_ref[pl.ds(r, n_half, stride=CMP_STRIDE), :]
            first = first + jnp.dot((y + pe_ref[r:r + 1, :]).astype(BF16), w_ref[r], preferred_element_type=F32)
            second = second + jnp.dot((y + pe_ref[r + CMP_STRIDE:r + CMP_STRIDE + 1, :]).astype(BF16),
                                      w_ref[r + CMP_STRIDE], preferred_element_type=F32)
        o_ref[0, :, (2 * slab) * LANE:(2 * slab + 1) * LANE] = first
        o_ref[0, :, (2 * slab + 1) * LANE:(2 * slab + 2) * LANE] = second


def nsa_sample_compress(cache, layer, page_table, pe_k, pe_v, w_ck, w_cv):
    B, n_pages = page_table.shape
    assert n_pages % PG_STEP == 0
    n_half_step = PG_STEP * PAGE_SIZE // CMP_STRIDE
    const = lambda shape: pl.BlockSpec(shape, lambda b, s, pt: (0,) * len(shape))
    grid_spec = pltpu.PrefetchScalarGridSpec(
        num_scalar_prefetch=1,
        grid=(B, n_pages // PG_STEP),
        in_specs=[_page_spec(layer, k, 2 * LANE, 0) for k in range(PG_STEP)]
        + [const((CMP_BLK, LANE)), const((CMP_BLK, LANE)), const((CMP_BLK, LANE, LANE)), const((CMP_BLK, LANE, LANE))],
        out_specs=pl.BlockSpec((1, n_half_step, 4 * LANE), lambda b, s, pt: (b, s, 0)),
        scratch_shapes=[pltpu.VMEM((PG_STEP * PAGE_SIZE, LANE), F32)],
    )
    return pl.pallas_call(
        _nsa_cmp_body,
        grid_spec=grid_spec,
        out_shape=jax.ShapeDtypeStruct((B, n_pages * PAGE_SIZE // CMP_STRIDE, 4 * LANE), F32),
        compiler_params=_params("parallel", "arbitrary"),
        name="nsa_sample_compress",
    )(page_table, *([cache] * PG_STEP), jnp.tile(pe_k, (1, 2)), jnp.tile(pe_v, (1, 2)),
      _block_diag2(w_ck), _block_diag2(w_cv))


def _nsa_smp_body(pt_ref, qs_ref, gl_ref, fs_ref, *refs, n_tok):
    pages = refs[:PG_STEP]
    nrow_ref, pwin_ref, nwin_ref, o_ref, shift_ref, selc_ref, oc_ref, m_ref, l_ref, acc_ref = refs[PG_STEP:]
    s_id = pl.program_id(1)
    C = LANE
    n_half = fs_ref.shape[1]
    n_cmp = n_half - 1
    n_selp = n_half * CMP_STRIDE // SEL_BLK
    qs = qs_ref[0]
    crow = lax.broadcasted_iota(jnp.int32, (C, LANE), 0)
    lane = lax.broadcasted_iota(jnp.int32, (C, LANE), 1)
    t_of_c = crow % n_tok

    @pl.when(s_id == 0)
    def _first():
        cmp = []
        for slab in range(2):
            shift_ref[0:n_half, :] = fs_ref[0, :, (2 * slab + 1) * LANE:(2 * slab + 2) * LANE]
            shift_ref[n_half:n_half + 8, :] = jnp.zeros((8, LANE), F32)
            cmp.append((fs_ref[0, :, (2 * slab) * LANE:(2 * slab + 1) * LANE]
                        + shift_ref[1:n_half + 1, :]).astype(BF16))
        nlane = lax.broadcasted_iota(jnp.int32, (C, n_half), 1)
        p = _softmax_rows(_nt(qs, cmp[0]), nlane < n_cmp)
        oc_ref[...] = jnp.dot(p.astype(BF16), cmp[1], preferred_element_type=F32)
        imp_rows = []
        for g in range(NSA_KV):
            acc = p[(NSA_HPG * g) * n_tok:(NSA_HPG * g + 1) * n_tok, :]
            for h in range(1, NSA_HPG):
                acc = acc + p[(NSA_HPG * g + h) * n_tok:(NSA_HPG * g + h + 1) * n_tok, :]
            imp_rows.append(acc)
        imp = jnp.concatenate(imp_rows + [jnp.zeros((C - NSA_KV * n_tok, n_half), F32)], axis=0)
        pi = lax.broadcasted_iota(jnp.int32, (n_half, n_selp), 0)
        pj = lax.broadcasted_iota(jnp.int32, (n_half, n_selp), 1)
        pool = ((((pi + 1) >> 2) == pj) & (pi < n_cmp)).astype(BF16)
        scoreT = _dot01_r(imp, pool).T
        jcol = lax.broadcasted_iota(jnp.int32, (n_selp, C), 0)
        forced = (jcol == 0) | (jcol == n_selp - 1)
        scoreT = scoreT + jnp.where(forced, FORCE_SCORE, 0.0)
        rank = jnp.zeros((n_selp, C), jnp.int32)
        for j2 in range(n_selp):
            r = scoreT[j2:j2 + 1, :]
            beats = (r > scoreT) | ((r == scoreT) & (jcol > j2))
            rank = rank + beats.astype(jnp.int32)
        sel = jnp.where(rank < TOP_N - 1, 1.0, 0.0).T
        selc_ref[...] = jnp.concatenate(
            [sel[(hh // NSA_HPG) * n_tok:(hh // NSA_HPG + 1) * n_tok, :] for hh in range(NSA_HEADS)]
            + [jnp.zeros((C - NSA_HEADS * n_tok, n_selp), F32)], axis=0)
        m_ref[...] = jnp.full(m_ref.shape, NEG, F32)
        l_ref[...] = jnp.zeros(l_ref.shape, F32)
        acc_ref[...] = jnp.zeros(acc_ref.shape, F32)

    selc = selc_ref[...].astype(BF16)
    n_keys = PG_STEP * PAGE_SIZE
    ej = lax.broadcasted_iota(jnp.int32, (n_selp, n_keys), 0)
    ek = lax.broadcasted_iota(jnp.int32, (n_selp, n_keys), 1)
    expand = (ej == s_id * (n_keys // SEL_BLK) + ek // SEL_BLK).astype(BF16)
    mask = jnp.dot(selc, expand, preferred_element_type=F32) > 0.5
    s = jnp.concatenate([jnp.dot(qs, pages[pg][0, 0:LANE, :].astype(BF16), preferred_element_type=F32)
                         for pg in range(PG_STEP)], axis=1)
    alpha, e = _flash_rows(s, mask, m_ref, l_ref)
    eb = e.astype(BF16)
    pv = _nt(eb[:, 0:PAGE_SIZE], pages[0][0, LANE:2 * LANE, :].astype(BF16))
    for pg in range(1, PG_STEP):
        pv = pv + _nt(eb[:, pg * PAGE_SIZE:(pg + 1) * PAGE_SIZE], pages[pg][0, LANE:2 * LANE, :].astype(BF16))
    acc_ref[...] = alpha * acc_ref[...] + pv

    @pl.when(s_id == pl.num_programs(1) - 1)
    def _last():
        pad = jnp.zeros((LANE - n_tok, LANE), F32)
        new_ok = (lane < n_tok) & (lane <= t_of_c)
        knew = jnp.concatenate([nrow_ref[0, :, 2 * LANE:3 * LANE], pad], axis=0).astype(BF16)
        vnew = jnp.concatenate([nrow_ref[0, :, 3 * LANE:4 * LANE], pad], axis=0).astype(BF16)
        alpha, e = _flash_rows(_nt(qs, knew), new_ok, m_ref, l_ref)
        acc = alpha * acc_ref[...] + jnp.dot(e.astype(BF16), vnew, preferred_element_type=F32)
        o_s = acc * (1.0 / jnp.maximum(l_ref[...], 1e-30))
        n_win = pwin_ref.shape[2]
        kwn = jnp.concatenate([nwin_ref[0, :, 0:LANE], pad], axis=0).astype(BF16)
        vwn = jnp.concatenate([nwin_ref[0, :, LANE:2 * LANE], pad], axis=0).astype(BF16)
        wl = lax.broadcasted_iota(jnp.int32, (C, n_win + LANE), 1)
        tw = lax.broadcasted_iota(jnp.int32, (C, n_win + LANE), 0) % n_tok
        wmask = (((wl < n_win) & ((n_win - wl) + tw < WINDOW))
                 | ((wl >= n_win) & (wl - n_win < n_tok) & (wl - n_win <= tw)))
        s_w = jnp.concatenate([jnp.dot(qs, pwin_ref[0, 0:LANE, :].astype(BF16), preferred_element_type=F32),
                               _nt(qs, kwn)], axis=1)
        pw = _softmax_rows(s_w, wmask).astype(BF16)
        o_w = (_nt(pw[:, 0:n_win], pwin_ref[0, LANE:2 * LANE, :].astype(BF16))
               + jnp.dot(pw[:, n_win:], vwn, preferred_element_type=F32))
        g = jax.nn.sigmoid(gl_ref[0])
        o = g[:, 0:1] * oc_ref[...] + g[:, 1:2] * o_s + g[:, 2:3] * o_w
        valid = (crow < NSA_HEADS * n_tok) & (lane // HEAD_DIM == crow // (n_tok * NSA_HPG))
        o_ref[0] = jnp.where(valid, o, 0.0)


def nsa_sample_attention(qs, gcol, fs, cache, layer, page_table, new_rows, past_win, new_win):
    B, n_pages = page_table.shape
    n_tok = new_rows.shape[1]
    n_half = fs.shape[1]
    n_selp = n_half * CMP_STRIDE // SEL_BLK
    n_win = past_win.shape[3]
    assert n_selp == LANE and NSA_HEADS * n_tok <= LANE and n_tok <= min(SEL_BLK, 8) and n_pages % PG_STEP == 0
    assert n_win % LANE == 0 and n_win <= WINDOW
    grid_spec = pltpu.PrefetchScalarGridSpec(
        num_scalar_prefetch=1,
        grid=(B, n_pages // PG_STEP),
        in_specs=[_per_request((LANE, LANE)), _per_request((LANE, LANE)), _per_request((n_half, 4 * LANE))]
        + [_page_spec(layer, k, 2 * LANE, 1) for k in range(PG_STEP)]
        + [_per_request((n_tok, 4 * LANE)),
           pl.BlockSpec((None, 1, 2 * LANE, n_win), lambda b, s, pt: (layer, b, 0, 0)),
           _per_request((n_tok, 2 * LANE))],
        out_specs=_per_request((LANE, LANE)),
        scratch_shapes=[pltpu.VMEM((n_half + 8, LANE), F32), pltpu.VMEM((LANE, n_selp), F32),
                        pltpu.VMEM((LANE, LANE), F32), pltpu.VMEM((LANE, 1), F32), pltpu.VMEM((LANE, 1), F32),
                        pltpu.VMEM((LANE, LANE), F32)],
    )
    return pl.pallas_call(
        functools.partial(_nsa_smp_body, n_tok=n_tok),
        grid_spec=grid_spec,
        out_shape=jax.ShapeDtypeStruct((B, LANE, LANE), F32),
        compiler_params=_params("parallel", "arbitrary"),
        name="nsa_sample_attention",
    )(page_table, qs, gcol, fs, *([cache] * PG_STEP), new_rows, past_win, new_win)


def _dsa_idx_body(pt_ref, qe_ref, qo_ref, w_ref, *refs, n_tok, n_keep):
    pages = refs[:PG_STEP]
    knew_ref, sc_ref, th_ref = refs[PG_STEP:]
    s_id = pl.program_id(1)
    n_blk = sc_ref.shape[1]
    qe = qe_ref[0]
    qo = qo_ref[0]
    w = w_ref[0] * (IDX_HEADS ** -0.5)
    n_rows = (IDX_HEADS // 2) * n_tok

    def scores(s_e, s_o):
        s = jnp.maximum(s_e, 0.0) * w[:, 0:1] + jnp.maximum(s_o, 0.0) * w[:, 1:2]
        tot = s[0:n_tok, :]
        for p in range(1, IDX_HEADS // 2):
            tot = tot + s[p * n_tok:(p + 1) * n_tok, :]
        return tot

    kT = jnp.concatenate([pages[pg][0] for pg in range(PG_STEP)], axis=1).astype(BF16)
    sc = scores(jnp.dot(qe, kT, preferred_element_type=F32), jnp.dot(qo, kT, preferred_element_type=F32))
    for pg in range(PG_STEP):
        sc_ref[0, s_id * PG_STEP + pg] = sc[:, pg * PAGE_SIZE:(pg + 1) * PAGE_SIZE]

    @pl.when(s_id == pl.num_programs(1) - 1)
    def _last():
        kn = jnp.concatenate([knew_ref[0], jnp.zeros((LANE - n_tok, IDX_DIM), F32)], axis=0).astype(BF16)
        a_i = lax.broadcasted_iota(jnp.int32, (n_tok, LANE), 1)
        t_i = lax.broadcasted_iota(jnp.int32, (n_tok, LANE), 0)
        sn = jnp.where(a_i <= t_i, scores(_nt(qe, kn), _nt(qo, kn)), NEG)
        sc_ref[0, n_blk - 1] = jnp.where(a_i < n_tok, sn, -jnp.inf)
        keys = _sort_key(sc_ref[0])

        def bit_step(it, theta):
            cand = theta + lax.shift_left(jnp.int32(1), 31 - it)
            cnt = jnp.sum(jnp.sum((keys >= cand).astype(jnp.int32), axis=0), axis=1, keepdims=True)
            return jnp.where(cnt >= n_keep, cand, theta)

        theta = lax.fori_loop(0, 32, bit_step, jnp.full((n_tok, 1), INT_MIN, jnp.int32))
        th_ref[0] = jnp.broadcast_to(theta, (n_tok, LANE))

        count = lambda m: jnp.sum(jnp.sum(m.astype(jnp.int32), axis=0), axis=1, keepdims=True)

        @pl.when(jnp.max(count(keys >= theta)) > n_keep)
        def _break_ties():
            need = (n_keep - count(keys > theta)).astype(F32)
            before = (lax.broadcasted_iota(jnp.int32, (LANE, LANE), 0)
                      < lax.broadcasted_iota(jnp.int32, (LANE, LANE), 1)).astype(BF16)

            def fix_step(blk, seen):
                sc = sc_ref[0, blk]
                eq = _sort_key(sc) == theta
                eqf = jnp.where(eq, 1.0, 0.0)
                rank = seen + jnp.dot(eqf.astype(BF16), before, preferred_element_type=F32)
                sc_ref[0, blk] = jnp.where(eq & (rank >= need), -jnp.inf, sc)
                return seen + jnp.sum(eqf, axis=1, keepdims=True)

            lax.fori_loop(0, n_blk, fix_step, jnp.zeros((n_tok, 1), F32))


def dsa_sample_index(qe, qo, wcol, cache_idx, layer, page_table, ki_new):
    B, n_pages = page_table.shape
    n_tok = ki_new.shape[1]
    assert n_tok == 8 and n_pages % PG_STEP == 0
    n_keep = min(IDX_TOPK, (n_pages * PAGE_SIZE + n_tok) // 4)
    grid_spec = pltpu.PrefetchScalarGridSpec(
        num_scalar_prefetch=1,
        grid=(B, n_pages // PG_STEP),
        in_specs=[_per_request((LANE, IDX_DIM)), _per_request((LANE, IDX_DIM)), _per_request((LANE, LANE))]
        + [_page_spec(layer, k, IDX_DIM, 0) for k in range(PG_STEP)] + [_per_request((n_tok, IDX_DIM))],
        out_specs=[_per_request((n_pages + 1, n_tok, LANE)), _per_request((n_tok, LANE))],
    )
    return pl.pallas_call(
        functools.partial(_dsa_idx_body, n_tok=n_tok, n_keep=n_keep),
        grid_spec=grid_spec,
        out_shape=[jax.ShapeDtypeStruct((B, n_pages + 1, n_tok, LANE), F32),
                   jax.ShapeDtypeStruct((B, n_tok, LANE), jnp.int32)],
        compiler_params=_params("parallel", "arbitrary"),
        name="dsa_sample_index",
    )(page_table, qe, qo, wcol, *([cache_idx] * PG_STEP), ki_new)


def _dsa_smp_body(pt_ref, qs_ref, sc_ref, th_ref, *refs, n_tok):
    pages = refs[:PG_STEP]
    kvn_ref, o_ref, m_ref, l_ref, acc_ref = refs[PG_STEP:]
    s_id = pl.program_id(1)
    C = LANE
    qs = qs_ref[0]
    crow = lax.broadcasted_iota(jnp.int32, (C, LANE), 0)
    lane = lax.broadcasted_iota(jnp.int32, (C, LANE), 1)
    low = crow < C // 2
    theta = th_ref[0]

    @pl.when(s_id == 0)
    def _():
        m_ref[...] = jnp.full(m_ref.shape, NEG, F32)
        l_ref[...] = jnp.zeros(l_ref.shape, F32)
        acc_ref[...] = jnp.zeros(acc_ref.shape, F32)

    def step(kvTs, blk0, extra):
        kbs = [kvT.astype(BF16) for kvT in kvTs]
        keep = jnp.concatenate([jnp.where(_sort_key(sc_ref[0, blk0 + j]) >= theta, 1.0, 0.0)
                                for j in range(len(kbs))], axis=1)
        mask = jnp.concatenate([keep] * (C // n_tok), axis=0) > 0.5
        if extra is not None:
            mask = mask & extra
        s_lo = jnp.concatenate([jnp.dot(qs, kb[0:LANE], preferred_element_type=F32) for kb in kbs], axis=1)
        s_hi = jnp.concatenate([jnp.dot(qs, kb[LANE:2 * LANE], preferred_element_type=F32) for kb in kbs], axis=1)
        low_col = lax.broadcasted_iota(jnp.int32, (C, 1), 0) < C // 2
        alpha, e = _flash_rows(jnp.where(low_col, s_lo, s_hi), mask, m_ref, l_ref)
        eb = e.astype(BF16)
        pv_lo = pv_hi = None
        for j, kb in enumerate(kbs):
            ej = eb[:, j * LANE:(j + 1) * LANE]
            a, b = _nt(ej, kb[2 * LANE:3 * LANE]), _nt(ej, kb[3 * LANE:4 * LANE])
            pv_lo, pv_hi = (a, b) if pv_lo is None else (pv_lo + a, pv_hi + b)
        acc_ref[...] = alpha * acc_ref[...] + jnp.where(low, pv_lo, pv_hi)

    step([pages[pg][0] for pg in range(PG_STEP)], s_id * PG_STEP, None)

    @pl.when(s_id == pl.num_programs(1) - 1)
    def _last():
        kvn = jnp.concatenate([kvn_ref[0], jnp.zeros((LANE - n_tok, 4 * LANE), F32)], axis=0)
        kvnT = jnp.concatenate([kvn[:, j * LANE:(j + 1) * LANE].T for j in range(4)], axis=0)
        step([kvnT], sc_ref.shape[1] - 1, (lane < n_tok) & (lane <= crow % n_tok))
        o = acc_ref[...] * (1.0 / jnp.maximum(l_ref[...], 1e-30))
        valid = lane // HEAD_DIM == (crow // (n_tok * DSA_HPG)) % 2
        o_ref[0] = jnp.where(valid, o, 0.0)


def dsa_sample_attention(qs, scores, theta, cache_kv, layer, page_table, kv_new):
    B, n_pages = page_table.shape
    n_tok = kv_new.shape[1]
    assert DSA_HEADS * n_tok == LANE and n_pages % PG_STEP == 0
    grid_spec = pltpu.PrefetchScalarGridSpec(
        num_scalar_prefetch=1,
        grid=(B, n_pages // PG_STEP),
        in_specs=[_per_request((LANE, LANE)), _per_request((n_pages + 1, n_tok, LANE)), _per_request((n_tok, LANE))]
        + [_page_spec(layer, k, 4 * LANE, 0) for k in range(PG_STEP)] + [_per_request((n_tok, 4 * LANE))],
        out_specs=_per_request((LANE, LANE)),
        scratch_shapes=[pltpu.VMEM((LANE, 1), F32), pltpu.VMEM((LANE, 1), F32), pltpu.VMEM((LANE, LANE), F32)],
    )
    return pl.pallas_call(
        functools.partial(_dsa_smp_body, n_tok=n_tok),
        grid_spec=grid_spec,
        out_shape=jax.ShapeDtypeStruct((B, LANE, LANE), F32),
        compiler_params=_params("parallel", "arbitrary"),
        name="dsa_sample_attention",
    )(page_table, qs, scores, theta, *([cache_kv] * PG_STEP), kv_new)


_NSA_SLOTS = tuple(h // NSA_HPG for h in range(NSA_HEADS))
_DSA_SLOTS = tuple((h // DSA_HPG) % 2 for h in range(DSA_HEADS))
def _even_segs(q_dtype, cm):
    return ((NSA_HEADS * LANE, (1,) * NSA_HEADS, q_dtype, False), (4 * LANE, (1, 0, 1, 0), F32, cm),
            (2 * LANE, (1, 0), F32, False), (LANE, (0,), F32, False),
            (4 * HGRN_KW, (0,) * (4 * HGRN_KW // LANE), F32, False))


def _odd_segs(q_dtype, cm):
    return ((DSA_HEADS * LANE, (1,) * DSA_HEADS, q_dtype, False), (4 * LANE, (1, 1, 0, 0), F32, cm),
            (IDX_QW, (1,) * 4, q_dtype, False), (LANE, (1,), F32, cm), (LANE, (0,), F32, False))


def _even_weights(w_in, w_out):
    c = np.cumsum([0, NSA_QW] + [NSA_KVW] * 6 + [NSA_GW] + [HGRN_KW] * 4)
    w = jnp.concatenate([
        _pad_head_cols(w_in[:, :NSA_QW] * HEAD_DIM ** -0.5, _NSA_SLOTS),
        w_in[:, c[1]:c[5]], w_in[:, c[5]:c[7]], _pad_cols(w_in[:, c[7]:c[8]], LANE), w_in[:, c[8]:c[12]]],
        axis=1).astype(BF16)
    w_outs = [_pad_head_cols(w_out[:NSA_QW].T, _NSA_SLOTS).T.astype(BF16), w_out[NSA_QW:].astype(BF16)]
    return w, w_outs


def _odd_weights(w_in, w_out):
    c = np.cumsum([0, DSA_QW, DSA_KVW, DSA_KVW, IDX_QW, IDX_DIM, IDX_HEADS])
    w = jnp.concatenate([
        _pad_head_cols(w_in[:, :DSA_QW] * HEAD_DIM ** -0.5, _DSA_SLOTS),
        w_in[:, c[1]:c[3]], w_in[:, c[3]:c[4]] * IDX_DIM ** -0.5,
        _pad_cols(w_in[:, c[4]:c[5]], LANE), _pad_cols(w_in[:, c[5]:c[6]], LANE)], axis=1).astype(BF16)
    return w, [_pad_head_cols(w_out.T, _DSA_SLOTS).T.astype(BF16)]


def even_mixer_prompt(x, scale, shift, cos_t, sin_t, w_in, w_out, pe_k, pe_v, w_ck, w_cv, lb, norm_g, win_len):
    B, T, _ = x.shape
    assert T >= win_len
    w, w_outs = _even_weights(w_in, w_out)
    qp, rows, win, gl, hx, rows_cm = proj_segments(x, scale, shift, cos_t, sin_t, w, _even_segs(BF16, True))
    o_a = nsa_prompt(qp, rows, win, gl, pe_k, pe_v, w_ck, w_cv)
    o_b, s_new = hgrn_mixer(hx, lb, norm_g, jnp.zeros((B, HGRN_HEADS, HGRN_DK, HGRN_DV), F32))
    new_rows = jnp.moveaxis(rows_cm.reshape(B, 4, NSA_KV, HEAD_DIM, T), 4, 1)
    win_state = win[:, T - win_len:].reshape(B, win_len, 2, NSA_KV, HEAD_DIM)
    return [o_a, o_b], w_outs, new_rows, win_state, s_new


def odd_mixer_prompt(x, scale, shift, cos_t, sin_t, w_in, w_out):
    B, T, _ = x.shape
    w, w_outs = _odd_weights(w_in, w_out)
    qp, kv, qi, ki, wi, kv_cm, ki_cm = proj_segments(x, scale, shift, cos_t, sin_t, w, _odd_segs(BF16, True))
    o = dsa_prompt(qp, kv, qi, ki, wi)
    new_kv = jnp.moveaxis(kv_cm.reshape(B, 2, DSA_KV, HEAD_DIM, T), 4, 1)
    return [o], w_outs, new_kv, jnp.moveaxis(ki_cm[:, :IDX_DIM], 2, 1)


def _stack_rows(a, B, n, heads, width):
    s = a.reshape(B, n, heads, width).transpose(0, 2, 1, 3).reshape(B, heads * n, width)
    return jnp.pad(s, ((0, 0), (0, LANE - heads * n), (0, 0)))


def _unstack_rows(o, B, n, heads):
    return o[:, :heads * n].reshape(B, heads, n, LANE).transpose(0, 2, 1, 3).reshape(1, B * n, heads * LANE)


def even_mixer_sample(x, scale, shift, past_len, B, w_in, w_out, pe_k, pe_v, w_ck, w_cv, lb, norm_g,
                      cache, cache_win, state, page_table, layer):
    n = x.shape[1] // B
    win_len = cache_win.shape[2]
    cos_t, sin_t = [jnp.tile(a, (B, 1)) for a in rope_tables(past_len + jnp.arange(n))]
    w, w_outs = _even_weights(w_in, w_out)
    qp, rows, win, gl, hx = proj_segments(x, scale, shift, cos_t, sin_t, w, _even_segs(F32, False))
    rows, win, hx = [a.reshape(B, n, a.shape[-1]) for a in (rows, win, hx)]
    qs = _stack_rows(qp, B, n, NSA_HEADS, LANE).astype(BF16)
    gcol = jnp.pad(_stack_rows(gl[..., :NSA_GW], B, n, NSA_HEADS, 3), ((0, 0), (0, 0), (0, LANE - 3)))
    cache_cm = _channel_major(cache)
    fs = nsa_sample_compress(cache_cm, layer, page_table, pe_k, pe_v, w_ck, w_cv)
    o = nsa_sample_attention(qs, gcol, fs, cache_cm, layer, page_table, rows, _channel_major(cache_win), win)
    o_a = _unstack_rows(o, B, n, NSA_HEADS)
    o_b, s_new = hgrn_mixer(hx, lb, norm_g, state)
    new_rows = rows.reshape(B, n, 4, NSA_KV, HEAD_DIM)
    win_state = jnp.concatenate([cache_win[layer], win.reshape(B, n, 2, NSA_KV, HEAD_DIM)], axis=1)[:, -win_len:]
    return [o_a, o_b.reshape(1, B * n, HGRN_VW)], w_outs, new_rows, win_state, s_new


def odd_mixer_sample(x, scale, shift, past_len, B, w_in, w_out, cache_kv, cache_idx, page_table, layer):
    n = x.shape[1] // B
    cos_t, sin_t = [jnp.tile(a, (B, 1)) for a in rope_tables(past_len + jnp.arange(n))]
    w, w_outs = _odd_weights(w_in, w_out)
    qp, kv, qi, ki, wi = proj_segments(x, scale, shift, cos_t, sin_t, w, _odd_segs(F32, False))
    kv, ki = kv.reshape(B, n, 4 * LANE), ki.reshape(B, n, LANE)[..., :IDX_DIM]
    qs = _stack_rows(qp, B, n, DSA_HEADS, LANE).astype(BF16)
    qi4 = qi.reshape(1, B * n, IDX_HEADS // 2, 2, IDX_DIM)
    qe = _stack_rows(qi4[:, :, :, 0], B, n, IDX_HEADS // 2, IDX_DIM).astype(BF16)
    qo = _stack_rows(qi4[:, :, :, 1], B, n, IDX_HEADS // 2, IDX_DIM).astype(BF16)
    wcol = jnp.pad(_stack_rows(wi[..., :IDX_HEADS], B, n, IDX_HEADS // 2, 2), ((0, 0), (0, 0), (0, LANE - 2)))
    scores, theta = dsa_sample_index(qe, qo, wcol, _channel_major(cache_idx), layer, page_table, ki)
    o = dsa_sample_attention(qs, scores, theta, _channel_major(cache_kv), layer, page_table, kv)
    return [_unstack_rows(o, B, n, DSA_HEADS)], w_outs, kv.reshape(B, n, 2, DSA_KV, HEAD_DIM), ki


def moe_ffn_residual(xs, hs, logits_list, gates, w1, w3, w2):
    D = D_MODEL
    h_all = jnp.concatenate([h.reshape(-1, D) for h in hs], axis=0)
    logits = jnp.concatenate([lg.reshape(-1, lg.shape[-1])[:, :N_EXPERTS] for lg in logits_list], axis=0)
    n_tok = h_all.shape[0]
    top_v, top_i = lax.top_k(logits, TOP_K)
    weights = jax.nn.softmax(top_v, axis=-1)
    e_flat = top_i.reshape(-1)
    onehot = (e_flat[:, None] == jnp.arange(N_EXPERTS)[None, :]).astype(jnp.int32)
    csum = jnp.cumsum(onehot, axis=0)
    counts = csum[-1]
    rank = jnp.take_along_axis(csum, e_flat[:, None], axis=1)[:, 0] - 1
    padded = ((counts + MOE_TILE - 1) // MOE_TILE) * MOE_TILE
    group_end = jnp.cumsum(padded)
    group_start = group_end - padded
    slot = group_start[e_flat] + rank
    n_slots = _round_up(n_tok * TOP_K, MOE_TILE) + N_EXPERTS * MOE_TILE
    n_tiles = n_slots // MOE_TILE
    tok_of_slot = jnp.zeros((n_slots,), jnp.int32).at[slot].set(jnp.arange(n_tok * TOP_K, dtype=jnp.int32) // TOP_K)
    tile_start = jnp.arange(n_tiles, dtype=jnp.int32) * MOE_TILE
    tile_expert = jnp.minimum(jnp.sum(tile_start[:, None] >= group_end[None, :], axis=1), N_EXPERTS - 1).astype(jnp.int32)
    n_used = (group_end[-1] // MOE_TILE).astype(jnp.int32).reshape(1)
    h_sorted = h_all[tok_of_slot]
    y_slot = moe_grouped_swiglu(h_sorted, tile_expert, n_used, w1, w3, w2)
    slot2 = slot.reshape(n_tok, TOP_K)
    outs = []
    off = 0
    for x, g in zip(xs, gates):
        n = x.shape[0] * x.shape[1]
        ya = y_slot[slot2[off:off + n, 0]].reshape(x.shape)
        yb = y_slot[slot2[off:off + n, 1]].reshape(x.shape)
        w = weights[off:off + n].reshape(x.shape[0], x.shape[1], TOP_K)
        outs.append(moe_combine_residual(x, g, w, ya, yb))
        off += n
    return outs


def kernel(x_prompt, x_sample, cache_nsa, cache_nsa_win, state_hgrn, cache_dsa_kv, cache_dsa_idx, page_table, c_prompt, c_sample, ada_w, ada_b, norm1_g, norm2_g, final_g, even_w_in, even_w_out, nsa_pe_k, nsa_pe_v, nsa_w_ck, nsa_w_cv, hgrn_lb_raw, hgrn_norm_g, ffn_w1, ffn_w3, ffn_w2, odd_w_in, odd_w_out, router_w, router_b, moe_w1, moe_w3, moe_w2):
    D = D_MODEL
    past_len = page_table.shape[1] * PAGE_SIZE
    win_len = cache_nsa_win.shape[2]
    Bp, Tp = x_prompt.shape[:2]
    Bs, Ts = x_sample.shape[:2]
    lb_soft = jax.nn.softmax(hgrn_lb_raw.astype(F32), axis=0)
    lower_bounds = jnp.cumsum(lb_soft, axis=0) - lb_soft[0]

    R = _round_up(Bp + Bs, 8)
    c_all = jnp.zeros((R, D), F32).at[:Bp].set(c_prompt).at[Bp:Bp + Bs].set(c_sample)
    mods = ada_modulation(c_all, ada_w, ada_b)

    def group_mods(l, lo, n, per_token_rows):
        m = mods[l, lo:lo + n].reshape(n, 6, D)
        sh1, sc1, g1, sh2, sc2, g2 = [m[:, j] for j in range(6)]
        s1 = norm1_g[l][None] * (1.0 + sc1)
        s2 = norm2_g[l][None] * (1.0 + sc2)
        vecs = [s1, sh1, g1, s2, sh2, g2]
        if per_token_rows:
            return [jnp.repeat(v, per_token_rows, axis=0)[None] for v in vecs]
        return [v[:, None, :] for v in vecs]

    def pad_cols(w, n):
        return jnp.pad(w, ((0, 0), (0, n - w.shape[1]))).astype(BF16)

    xp = x_prompt
    xs = x_sample.reshape(1, Bs * Ts, D)
    cos_p, sin_p = rope_tables(jnp.arange(Tp))
    outs_p = dict(rows=[], win=[], st=[], kv=[], idx=[])
    outs_s = dict(rows=[], win=[], st=[], kv=[], idx=[])
    for l in range(DEPTH):
        i = l // 2
        mp = group_mods(l, 0, Bp, 0)
        msm = group_mods(l, Bp, Bs, Ts)
        if l % 2 == 0:
            w1, w3, w2 = ffn_w1[i].astype(BF16), ffn_w3[i].astype(BF16), ffn_w2[i].astype(BF16)
            mixed, w_outs, rows, win, s_new = even_mixer_prompt(
                xp, mp[0], mp[1], cos_p, sin_p, even_w_in[i], even_w_out[i], nsa_pe_k[i], nsa_pe_v[i],
                nsa_w_ck[i], nsa_w_cv[i], lower_bounds[i], hgrn_norm_g[i], win_len)
            outs_p['rows'].append(rows)
            outs_p['win'].append(win)
            outs_p['st'].append(s_new)
            xp = out_proj_residual(mixed, w_outs, xp, mp[2])
            xp = ffn_residual(xp, mp[3], mp[4], mp[5], w1, w3, w2)
            mixed, w_outs, rows, win, s_new = even_mixer_sample(
                xs, msm[0], msm[1], past_len, Bs, even_w_in[i], even_w_out[i], nsa_pe_k[i], nsa_pe_v[i],
                nsa_w_ck[i], nsa_w_cv[i], lower_bounds[i], hgrn_norm_g[i],
                cache_nsa, cache_nsa_win, state_hgrn[i], page_table, i)
            outs_s['rows'].append(rows)
            outs_s['win'].append(win)
            outs_s['st'].append(s_new)
            xs = out_proj_residual(mixed, w_outs, xs, msm[2])
            xs = ffn_residual(xs, msm[3], msm[4], msm[5], w1, w3, w2)
        else:
            w_r = pad_cols(router_w[i], LANE)
            w1, w3, w2 = moe_w1[i].astype(BF16), moe_w3[i].astype(BF16), moe_w2[i].astype(BF16)
            o, w_outs, kv, ki = odd_mixer_prompt(xp, mp[0], mp[1], cos_p, sin_p, odd_w_in[i], odd_w_out[i])
            outs_p['kv'].append(kv)
            outs_p['idx'].append(ki)
            xp = out_proj_residual(o, w_outs, xp, mp[2])
            o, w_outs, kv, ki = odd_mixer_sample(xs, msm[0], msm[1], past_len, Bs, odd_w_in[i], odd_w_out[i],
                                                 cache_dsa_kv, cache_dsa_idx, page_table, i)
            outs_s['kv'].append(kv)
            outs_s['idx'].append(ki)
            xs = out_proj_residual(o, w_outs, xs, msm[2])
            hs, lgs = [], []
            for (x, m) in ((xp, mp), (xs, msm)):
                logits, h = norm_proj(x, m[3], m[4], w_r, with_h=True)
                lgs.append(logits[..., :N_EXPERTS] + router_b[i].astype(F32))
                hs.append(h)
            xp, xs = moe_ffn_residual([xp, xs], hs, lgs, [mp[5], msm[5]], w1, w3, w2)
    y_prompt = final_norm(xp, final_g)
    y_sample = final_norm(xs, final_g).reshape(Bs, Ts, D)
    st = lambda od, k: jnp.stack(od[k])
    return (y_prompt, y_sample, st(outs_p, 'rows'), st(outs_s, 'rows'), st(outs_p, 'win'), st(outs_s, 'win'),
            st(outs_p, 'st'), st(outs_s, 'st'), st(outs_p, 'kv'), st(outs_s, 'kv'),
            st(outs_p, 'idx'), st(outs_s, 'idx'))
```

```python
import functools
import math

import numpy as np
import jax
import jax.numpy as jnp
from jax import lax
from jax.experimental import pallas as pl
from jax.experimental.pallas import tpu as pltpu

D_MODEL = 1024
DEPTH = 4
PAGE_SIZE = 128
HEAD_DIM = 64
NSA_HEADS = 8
NSA_KV = 2
NSA_HPG = NSA_HEADS // NSA_KV
CMP_STRIDE = 16
CMP_BLK = 2 * CMP_STRIDE
SEL_BLK = 64
TOP_N = 16
WINDOW = 512
FORCE_SCORE = 1.0e4
HGRN_HEADS = 4
HGRN_DK = 128
HGRN_DV = 128
HGRN_CHUNK = 64
DSA_HEADS = 16
DSA_KV = 4
DSA_HPG = DSA_HEADS // DSA_KV
IDX_HEADS = 8
IDX_DIM = 64
IDX_TOPK = 256
D_FF = 2816
N_EXPERTS = 8
TOP_K = 2
Q_BLK = 128
ROPE_THETA = 10000.0
EPS = 1e-6
NEG = -1.0e30
NSA_QW = NSA_HEADS * HEAD_DIM
NSA_KVW = NSA_KV * HEAD_DIM
NSA_GW = NSA_HEADS * 3
HGRN_KW = HGRN_HEADS * HGRN_DK
HGRN_VW = HGRN_HEADS * HGRN_DV
EVEN_IN = NSA_QW + 6 * NSA_KVW + NSA_GW + 2 * HGRN_KW + 2 * HGRN_VW
EVEN_MIX = NSA_QW + HGRN_VW
DSA_QW = DSA_HEADS * HEAD_DIM
DSA_KVW = DSA_KV * HEAD_DIM
IDX_QW = IDX_HEADS * IDX_DIM
ODD_IN = DSA_QW + 2 * DSA_KVW + IDX_QW + IDX_DIM + IDX_HEADS
ODD_MIX = DSA_QW

LANE = 128
ROW_TILE = 512
FF_CHUNK = 256
MOE_TILE = 256
PROJ_CHUNK = 512
ATT_TQ = 128
KEY_BLK = 256
HGRN_SUB = 16
HGRN_TILE = 256
HGRN_ROWS = 2
PG_STEP = 16
INT_MIN = -2 ** 31
M_FLOOR = -1.0e25
VMEM_LIMIT = 56 * 1024 * 1024

F32 = jnp.float32
BF16 = jnp.bfloat16


def _round_up(n, m):
    return -(-n // m) * m


def _params(*sem):
    return pltpu.CompilerParams(dimension_semantics=sem, vmem_limit_bytes=VMEM_LIMIT)


def _norm_mod(x, scale, shift):
    ms = jnp.mean(x * x, axis=-1, keepdims=True)
    return x * lax.rsqrt(ms + EPS) * scale + shift


def _mod_spec(mod, tm):
    if mod.shape[1] == 1:
        return pl.BlockSpec((1, 1, mod.shape[2]), lambda b, i: (b, 0, 0))
    return pl.BlockSpec((1, tm, mod.shape[2]), lambda b, i: (b, i, 0))


def _ada_body(c_ref, w_ref, b_ref, o_ref):
    c = c_ref[...]
    cs = (c * jax.nn.sigmoid(c)).astype(BF16)
    o_ref[0] = jnp.dot(cs, w_ref[0].astype(BF16), preferred_element_type=F32) + b_ref[0]


def ada_modulation(c_all, ada_w, ada_b):
    R, D = c_all.shape
    N = ada_w.shape[2]
    tn = 1536
    return pl.pallas_call(
        _ada_body,
        grid=(DEPTH, N // tn),
        in_specs=[pl.BlockSpec((R, D), lambda l, j: (0, 0)),
                  pl.BlockSpec((1, D, tn), lambda l, j: (l, 0, j)),
                  pl.BlockSpec((1, 1, tn), lambda l, j: (l, 0, j))],
        out_specs=pl.BlockSpec((1, R, tn), lambda l, j: (l, 0, j)),
        out_shape=jax.ShapeDtypeStruct((DEPTH, R, N), F32),
        compiler_params=_params("arbitrary", "arbitrary"),
        name="ada_modulation",
    )(c_all, ada_w, ada_b.reshape(DEPTH, 1, N))


def _norm_proj_body(x_ref, sc_ref, sh_ref, w_ref, o_ref, h_ref=None):
    h = _norm_mod(x_ref[0], sc_ref[0], sh_ref[0]).astype(BF16)
    o_ref[0] = jnp.dot(h, w_ref[...], preferred_element_type=F32)
    if h_ref is not None:
        h_ref[0] = h


def norm_proj(x, scale, shift, w, with_h=False):
    B, T, D = x.shape
    N = w.shape[1]
    tm = min(T, ROW_TILE)
    out_shape = [jax.ShapeDtypeStruct((B, T, N), F32)]
    out_specs = [pl.BlockSpec((1, tm, N), lambda b, i: (b, i, 0))]
    if with_h:
        out_shape.append(jax.ShapeDtypeStruct((B, T, D), BF16))
        out_specs.append(pl.BlockSpec((1, tm, D), lambda b, i: (b, i, 0)))
    res = pl.pallas_call(
        _norm_proj_body,
        grid=(B, T // tm),
        in_specs=[pl.BlockSpec((1, tm, D), lambda b, i: (b, i, 0)),
                  _mod_spec(scale, tm), _mod_spec(shift, tm),
                  pl.BlockSpec((D, N), lambda b, i: (0, 0))],
        out_specs=out_specs,
        out_shape=out_shape,
        compiler_params=_params("parallel", "parallel"),
        name="norm_proj",
    )(x, scale, shift, w)
    return res if with_h else res[0]


def _rope_chunk(y, cos, sin):
    lane = lax.broadcasted_iota(jnp.int32, y.shape, 1)
    swapped = jnp.where(lane % HEAD_DIM < HEAD_DIM // 2,
                        pltpu.roll(y, LANE - HEAD_DIM // 2, 1), pltpu.roll(y, HEAD_DIM // 2, 1))
    return y * cos + swapped * sin


def _proj_seg_body(x_ref, sc_ref, sh_ref, cos_ref, sin_ref, w_ref, *o_refs, segs):
    h = _norm_mod(x_ref[0], sc_ref[0], sh_ref[0]).astype(BF16)
    cos = cos_ref[...]
    sin = sin_ref[...]
    t_refs = iter(o_refs[len(segs):])
    c0 = 0
    for o_ref, (width, rope_flags, dtype, channel_major) in zip(o_refs, segs):
        t_ref = next(t_refs) if channel_major else None
        for j0 in range(0, width, PROJ_CHUNK):
            wd = min(PROJ_CHUNK, width - j0)
            y = jnp.dot(h, w_ref[:, c0 + j0:c0 + j0 + wd], preferred_element_type=F32)
            for k in range(wd // LANE):
                cols = slice(j0 + k * LANE, j0 + (k + 1) * LANE)
                yk = y[:, k * LANE:(k + 1) * LANE]
                if rope_flags[(j0 + k * LANE) // LANE]:
                    yk = _rope_chunk(yk, cos, sin)
                o_ref[0, :, cols] = yk.astype(dtype)
                if t_ref is not None:
                    t_ref[0, cols, :] = yk.T
        c0 += width


def proj_segments(x, scale, shift, cos_t, sin_t, w, segs):
    B, T, D = x.shape
    tm = min(T, ROW_TILE)
    N = w.shape[1]
    cm = [s for s in segs if s[3]]
    return pl.pallas_call(
        functools.partial(_proj_seg_body, segs=segs),
        grid=(B, T // tm),
        in_specs=[pl.BlockSpec((1, tm, D), lambda b, i: (b, i, 0)),
                  _mod_spec(scale, tm), _mod_spec(shift, tm),
                  pl.BlockSpec((tm, LANE), lambda b, i: (i, 0)),
                  pl.BlockSpec((tm, LANE), lambda b, i: (i, 0)),
                  pl.BlockSpec((D, N), lambda b, i: (0, 0))],
        out_specs=[pl.BlockSpec((1, tm, s[0]), lambda b, i: (b, i, 0)) for s in segs]
        + [pl.BlockSpec((1, s[0], tm), lambda b, i: (b, 0, i)) for s in cm],
        out_shape=[jax.ShapeDtypeStruct((B, T, s[0]), s[2]) for s in segs]
        + [jax.ShapeDtypeStruct((B, s[0], T), F32) for s in cm],
        compiler_params=_params("parallel", "parallel"),
        name="proj_segments",
    )(x, scale, shift, cos_t, sin_t, w)


def rope_tables(pos):
    half = HEAD_DIM // 2
    inv = ROPE_THETA ** (-jnp.arange(half, dtype=F32) / half)
    ang = pos.astype(F32)[:, None] * inv[None, :]
    cos, sin = jnp.cos(ang), jnp.sin(ang)
    return jnp.tile(cos, (1, 4)), jnp.tile(jnp.concatenate([-sin, sin], axis=1), (1, 2))


def _out_proj_body(*refs, n_in):
    a_refs, w_refs = refs[:n_in], refs[n_in:2 * n_in]
    x_ref, g_ref, o_ref = refs[2 * n_in:]
    y = None
    for a_ref, w_ref in zip(a_refs, w_refs):
        part = jnp.dot(a_ref[0].astype(BF16), w_ref[...], preferred_element_type=F32)
        y = part if y is None else y + part
    o_ref[0] = x_ref[0] + g_ref[0] * y


def out_proj_residual(a_list, w_list, x, gate):
    B, T, D = x.shape
    tm = min(T, ROW_TILE)
    n_in = len(a_list)
    return pl.pallas_call(
        functools.partial(_out_proj_body, n_in=n_in),
        grid=(B, T // tm),
        in_specs=[pl.BlockSpec((1, tm, a.shape[2]), lambda b, i: (b, i, 0)) for a in a_list]
        + [pl.BlockSpec(w.shape, lambda b, i: (0, 0)) for w in w_list]
        + [pl.BlockSpec((1, tm, D), lambda b, i: (b, i, 0)), _mod_spec(gate, tm)],
        out_specs=pl.BlockSpec((1, tm, D), lambda b, i: (b, i, 0)),
        out_shape=jax.ShapeDtypeStruct((B, T, D), F32),
        compiler_params=_params("parallel", "parallel"),
        name="out_proj_residual",
    )(*a_list, *w_list, x, gate)


def _swiglu_acc(h, w1_ref, w3_ref, w2_ref, acc_ref):
    for c in range(D_FF // FF_CHUNK):
        cols = slice(c * FF_CHUNK, (c + 1) * FF_CHUNK)
        u = jnp.dot(h, w1_ref[:, cols], preferred_element_type=F32)
        v = jnp.dot(h, w3_ref[:, cols], preferred_element_type=F32)
        a = (u * jax.nn.sigmoid(u) * v).astype(BF16)
        part = jnp.dot(a, w2_ref[cols, :], preferred_element_type=F32)
        if c == 0:
            acc_ref[...] = part
        else:
            acc_ref[...] += part


def _ffn_body(x_ref, sc_ref, sh_ref, g_ref, w1_ref, w3_ref, w2_ref, o_ref, acc_ref):
    x = x_ref[0]
    h = _norm_mod(x, sc_ref[0], sh_ref[0]).astype(BF16)
    _swiglu_acc(h, w1_ref, w3_ref, w2_ref, acc_ref)
    o_ref[0] = x + g_ref[0] * acc_ref[...]


def ffn_residual(x, scale, shift, gate, w1, w3, w2):
    B, T, D = x.shape
    tm = min(T, ROW_TILE)
    wspec = lambda shape: pl.BlockSpec(shape, lambda b, i: (0, 0))
    return pl.pallas_call(
        _ffn_body,
        grid=(B, T // tm),
        in_specs=[pl.BlockSpec((1, tm, D), lambda b, i: (b, i, 0)),
                  _mod_spec(scale, tm), _mod_spec(shift, tm), _mod_spec(gate, tm),
                  wspec((D, D_FF)), wspec((D, D_FF)), wspec((D_FF, D))],
        out_specs=pl.BlockSpec((1, tm, D), lambda b, i: (b, i, 0)),
        out_shape=jax.ShapeDtypeStruct((B, T, D), F32),
        scratch_shapes=[pltpu.VMEM((tm, D), F32)],
        compiler_params=_params("parallel", "parallel"),
        name="ffn_residual",
    )(x, scale, shift, gate, w1, w3, w2)


def _moe_body(te_ref, nt_ref, h_ref, w1_ref, w3_ref, w2_ref, o_ref, acc_ref):
    i = pl.program_id(0)

    @pl.when(i < nt_ref[0])
    def _():
        _swiglu_acc(h_ref[...], w1_ref.at[0], w3_ref.at[0], w2_ref.at[0], acc_ref)
        o_ref[...] = acc_ref[...]

    @pl.when(i >= nt_ref[0])
    def _():
        o_ref[...] = jnp.zeros_like(o_ref)


def moe_grouped_swiglu(h_sorted, tile_expert, n_tiles_used, w1, w3, w2):
    S, D = h_sorted.shape
    tm = MOE_TILE
    n_tiles = S // tm
    wspec = lambda shape: pl.BlockSpec((1,) + shape, lambda i, te, nt: (te[i], 0, 0))
    grid_spec = pltpu.PrefetchScalarGridSpec(
        num_scalar_prefetch=2,
        grid=(n_tiles,),
        in_specs=[pl.BlockSpec((tm, D), lambda i, te, nt: (i, 0)),
                  wspec((D, D_FF)), wspec((D, D_FF)), wspec((D_FF, D))],
        out_specs=pl.BlockSpec((tm, D), lambda i, te, nt: (i, 0)),
        scratch_shapes=[pltpu.VMEM((tm, D), F32)],
    )
    return pl.pallas_call(
        _moe_body,
        grid_spec=grid_spec,
        out_shape=jax.ShapeDtypeStruct((S, D), F32),
        compiler_params=_params("arbitrary"),
        name="moe_grouped_swiglu",
    )(tile_expert, n_tiles_used, h_sorted, w1, w3, w2)


def _combine_body(x_ref, g_ref, w_ref, a_ref, b_ref, o_ref):
    w = w_ref[0]
    o_ref[0] = x_ref[0] + g_ref[0] * (w[:, 0:1] * a_ref[0] + w[:, 1:2] * b_ref[0])


def moe_combine_residual(x, gate, w, ya, yb):
    B, T, D = x.shape
    tm = min(T, ROW_TILE)
    tile = pl.BlockSpec((1, tm, D), lambda b, i: (b, i, 0))
    return pl.pallas_call(
        _combine_body,
        grid=(B, T // tm),
        in_specs=[tile, _mod_spec(gate, tm), pl.BlockSpec((1, tm, TOP_K), lambda b, i: (b, i, 0)), tile, tile],
        out_specs=tile,
        out_shape=jax.ShapeDtypeStruct((B, T, D), F32),
        compiler_params=_params("parallel", "parallel"),
        name="moe_combine_residual",
    )(x, gate, w, ya, yb)


def _final_norm_body(x_ref, g_ref, o_ref):
    x = x_ref[0]
    ms = jnp.mean(x * x, axis=-1, keepdims=True)
    o_ref[0] = x * lax.rsqrt(ms + EPS) * g_ref[...]


def final_norm(x, g):
    B, T, D = x.shape
    tm = min(T, ROW_TILE)
    return pl.pallas_call(
        _final_norm_body,
        grid=(B, T // tm),
        in_specs=[pl.BlockSpec((1, tm, D), lambda b, i: (b, i, 0)),
                  pl.BlockSpec((1, D), lambda b, i: (0, 0))],
        out_specs=pl.BlockSpec((1, tm, D), lambda b, i: (b, i, 0)),
        out_shape=jax.ShapeDtypeStruct((B, T, D), F32),
        compiler_params=_params("parallel", "parallel"),
        name="final_norm",
    )(x, g.reshape(1, D))


def _split3(x):
    hi = x.astype(BF16)
    r1 = x - hi.astype(F32)
    mid = r1.astype(BF16)
    lo = (r1 - mid.astype(F32)).astype(BF16)
    return hi, mid, lo


def _dot01(m01, x):
    hi, mid, lo = _split3(x)
    d = lambda p: jnp.dot(m01, p, preferred_element_type=F32)
    return d(hi) + d(mid) + d(lo)


def _hgrn_body(hq_ref, hf_ref, hi_ref, hg_ref, lb_ref, ng_ref, s0_ref, o_ref, sn_ref, st_ref, *, sub, n_sub):
    i = pl.program_id(1)
    nb = hq_ref.shape[0]

    @pl.when(i == 0)
    def _():
        for bb in range(nb):
            for h in range(HGRN_HEADS):
                st_ref[bb, h] = s0_ref[bb, h].T

    lb = lb_ref[...]
    ng = ng_ref[...]
    row = lax.broadcasted_iota(jnp.int32, (sub, sub), 0)
    col = lax.broadcasted_iota(jnp.int32, (sub, sub), 1)
    tril = (row >= col).astype(BF16)
    trow = lax.broadcasted_iota(jnp.int32, (sub, HGRN_DK), 0)

    def chunk(c, carry):
        for bb in range(nb):
            chunk_one(c, bb)
        return carry

    def chunk_one(c, bb):
        rows = pl.ds(pl.multiple_of(c * sub, sub), sub)
        hq = hq_ref[bb, rows, :]
        hf = hf_ref[bb, rows, :]
        hv = hi_ref[bb, rows, :]
        hg = hg_ref[bb, rows, :]
        f = lb + (1.0 - lb) * jax.nn.sigmoid(hf)
        logf = jnp.log(f)
        kk = 1.0 - f
        qq = hq * jax.nn.sigmoid(hq)
        b = _dot01(tril, logf)
        bl = b[sub - 1:sub, :]
        qe = qq * jnp.exp(b)
        ke = kk * jnp.exp(bl - b)
        ebl = jnp.exp(bl)
        outs = []
        for h in range(HGRN_HEADS):
            cs = slice(h * HGRN_DK, (h + 1) * HGRN_DK)
            st = st_ref[bb, h]
            o = lax.dot_general(qe[:, cs].astype(BF16), st.astype(BF16), (((1,), (1,)), ((), ())),
                                preferred_element_type=F32)
            bh, qh, kh, vh = b[:, cs], qq[:, cs], kk[:, cs], hv[:, cs]
            for s in range(sub):
                e = jnp.exp(jnp.minimum(bh - bh[s:s + 1, :], 0.0))
                w = jnp.where(trow >= s, e * qh * kh[s:s + 1, :], 0.0)
                o = o + jnp.sum(w, axis=-1, keepdims=True) * vh[s:s + 1, :]
            upd = lax.dot_general(vh.astype(BF16), ke[:, cs].astype(BF16), (((0,), (0,)), ((), ())),
                                  preferred_element_type=F32)
            st_ref[bb, h] = st * ebl[:, cs] + upd
            ms = jnp.mean(o * o, axis=-1, keepdims=True)
            outs.append(o * lax.rsqrt(ms + EPS))
        o_ref[bb, rows, :] = jnp.concatenate(outs, axis=-1) * ng * (hg * jax.nn.sigmoid(hg))

    lax.fori_loop(0, n_sub, chunk, 0)

    @pl.when(i == pl.num_programs(1) - 1)
    def _():
        for bb in range(nb):
            for h in range(HGRN_HEADS):
                sn_ref[bb, h] = st_ref[bb, h].T


def hgrn_mixer(hx, lb, norm_g, s0):
    B, T, _ = hx.shape
    tc = min(T, HGRN_TILE)
    sub = math.gcd(T, HGRN_SUB)
    nb = math.gcd(B, HGRN_ROWS)
    spec = lambda j: pl.BlockSpec((nb, tc, HGRN_KW), lambda b, i, j=j: (b, i, j))
    vec = pl.BlockSpec((1, HGRN_KW), lambda b, i: (0, 0))
    state = pl.BlockSpec((nb, HGRN_HEADS, HGRN_DK, HGRN_DV), lambda b, i: (b, 0, 0, 0))
    return pl.pallas_call(
        functools.partial(_hgrn_body, sub=sub, n_sub=tc // sub),
        grid=(B // nb, T // tc),
        in_specs=[spec(0), spec(1), spec(2), spec(3), vec, vec, state],
        out_specs=[pl.BlockSpec((nb, tc, HGRN_VW), lambda b, i: (b, i, 0)), state],
        out_shape=[jax.ShapeDtypeStruct((B, T, HGRN_VW), F32),
                   jax.ShapeDtypeStruct((B, HGRN_HEADS, HGRN_DK, HGRN_DV), F32)],
        scratch_shapes=[pltpu.VMEM((nb, HGRN_HEADS, HGRN_DV, HGRN_DK), F32)],
        compiler_params=_params("parallel", "arbitrary"),
        name="hgrn_mixer",
    )(hx, hx, hx, hx, lb.reshape(1, HGRN_KW), jnp.tile(norm_g, HGRN_HEADS).reshape(1, HGRN_VW), s0)


def _nt(a, b):
    return lax.dot_general(a, b, (((1,), (1,)), ((), ())), preferred_element_type=F32)


def _flash_step(kblk, vT, q_heads, mask, m, l, acc_ref, batched):
    tq = q_heads[0].shape[0]
    if batched:
        acc = acc_ref[...]
        scores = [_nt(kblk, qh) for qh in q_heads]
    m_out, l_out, acc_out = [], [], []
    for h, qh in enumerate(q_heads):
        cs = slice(h * tq, (h + 1) * tq)
        s = jnp.where(mask, scores[h] if batched else _nt(kblk, qh), NEG)
        m_new = jnp.maximum(m[:, cs], jnp.max(s, axis=0, keepdims=True))
        alpha = jnp.exp(m[:, cs] - m_new)
        e = jnp.exp(s - m_new)
        l_out.append(alpha * l[:, cs] + jnp.sum(e, axis=0, keepdims=True))
        pv = jnp.dot(vT, e.astype(BF16), preferred_element_type=F32)
        if batched:
            acc_out.append(alpha * acc[:, cs] + pv)
        else:
            acc_ref[:, cs] = alpha * acc_ref[:, cs] + pv
        m_out.append(m_new)
    if batched:
        acc_ref[...] = jnp.concatenate(acc_out, axis=1)
    return jnp.concatenate(m_out, axis=1), jnp.concatenate(l_out, axis=1)


def _head_slabs(qp_ref, first_head, n):
    return [qp_ref[0, :, (first_head + h) * LANE:(first_head + h + 1) * LANE] for h in range(n)]


def _stack_heads(qp_ref, first_head, n):
    return jnp.concatenate(_head_slabs(qp_ref, first_head, n), axis=0)


def _store_heads(o_ref, oT, first_head, n, slot):
    tq = oT.shape[1] // n
    lane = lax.broadcasted_iota(jnp.int32, (tq, LANE), 1)
    valid = (lane >= HEAD_DIM * slot) & (lane < HEAD_DIM * (slot + 1))
    for h in range(n):
        blk = oT[:, h * tq:(h + 1) * tq].T
        o_ref[0, :, (first_head + h) * LANE:(first_head + h + 1) * LANE] = jnp.where(valid, blk, 0.0).astype(o_ref.dtype)


def _nsa_body(qp_ref, rows_ref, win_ref, gl_ref, pek_ref, pev_ref, wck_ref, wcv_ref, o_ref,
              kcmp_ref, vcmpT_ref, ks_ref, vsT_ref, kw_ref, vwT_ref, stage_ref, shift_ref, sel_ref, acc_ref, *, T):
    i = pl.program_id(1)
    tq = ATT_TQ
    nq = NSA_HPG * tq
    n_half = T // CMP_STRIDE
    n_sel = T // SEL_BLK
    k_sel = min(TOP_N, n_sel)
    per_kb = KEY_BLK // SEL_BLK

    @pl.when(i == 0)
    def _prepare():
        nrow = lax.broadcasted_iota(jnp.int32, (n_half, LANE), 0)
        for slab, pe_ref, w_ref in ((0, pek_ref, wck_ref), (1, pev_ref, wcv_ref)):
            def stage_blk(kb, carry, slab=slab):
                rs = pl.ds(pl.multiple_of(kb * KEY_BLK, KEY_BLK), KEY_BLK)
                stage_ref[rs, :] = rows_ref[0, rs, slab * LANE:(slab + 1) * LANE]
                return carry

            lax.fori_loop(0, T // KEY_BLK, stage_blk, 0)
            first = jnp.zeros((n_half, LANE), F32)
            second = jnp.zeros((n_half, LANE), F32)
            for r in range(CMP_STRIDE):
                y = stage_ref[pl.ds(r, n_half, stride=CMP_STRIDE), :]
                first = first + jnp.dot((y + pe_ref[r:r + 1, :]).astype(BF16), w_ref[r],
                                        preferred_element_type=F32)
                second = second + jnp.dot((y + pe_ref[r + CMP_STRIDE:r + CMP_STRIDE + 1, :]).astype(BF16),
                                          w_ref[r + CMP_STRIDE], preferred_element_type=F32)
            shift_ref[0:n_half, :] = second
            shift_ref[n_half:n_half + 8, :] = jnp.zeros((8, LANE), F32)
            c = jnp.where(nrow < n_half - 1, first + shift_ref[1:n_half + 1, :], 0.0)
            if slab == 0:
                kcmp_ref[...] = c.astype(BF16)
            else:
                for j in range(n_half // LANE):
                    vcmpT_ref[:, j * LANE:(j + 1) * LANE] = c[j * LANE:(j + 1) * LANE, :].T.astype(BF16)

        def copy_blk(kb, carry):
            for half in range(KEY_BLK // LANE):
                rs = pl.ds(pl.multiple_of(kb * KEY_BLK + half * LANE, LANE), LANE)
                hs = slice(half * LANE, (half + 1) * LANE)
                ks_ref[kb, hs, :] = rows_ref[0, rs, 2 * LANE:3 * LANE].astype(BF16)
                vsT_ref[kb, :, hs] = rows_ref[0, rs, 3 * LANE:4 * LANE].T.astype(BF16)
                kw_ref[kb, hs, :] = win_ref[0, rs, 0:LANE].astype(BF16)
                vwT_ref[kb, :, hs] = win_ref[0, rs, LANE:2 * LANE].T.astype(BF16)
            return carry

        lax.fori_loop(0, T // KEY_BLK, copy_blk, 0)

    t0 = i * tq
    lane_t = t0 + (lax.broadcasted_iota(jnp.int32, (1, nq), 1) & (tq - 1))
    t_row = lane_t[:, 0:tq]
    gT = jax.nn.sigmoid(gl_ref[0].T)
    kio = lax.broadcasted_iota(jnp.int32, (KEY_BLK, tq), 0)
    nio = lax.broadcasted_iota(jnp.int32, (n_half, nq), 0)
    jcol = lax.broadcasted_iota(jnp.int32, (n_sel, tq), 0)
    pj = lax.broadcasted_iota(jnp.int32, (n_sel, n_half), 0)
    pi = lax.broadcasted_iota(jnp.int32, (n_sel, n_half), 1)
    pool = ((((pi + 1) >> 2) == pj) & (pi < n_half - 1)).astype(BF16)
    n_kb = (t0 + tq + KEY_BLK - 1) // KEY_BLK
    n_full = t0 // KEY_BLK
    m0 = jnp.full((1, nq), M_FLOOR, F32)
    l0 = jnp.zeros((1, nq), F32)

    for g in range(NSA_KV):
        q_heads = _head_slabs(qp_ref, NSA_HPG * g, NSA_HPG)
        qs = jnp.concatenate(q_heads, axis=0)
        cmask = (CMP_STRIDE * nio + CMP_BLK <= lane_t + 1) & (nio < n_half - 1)
        s = jnp.where(cmask, _nt(kcmp_ref[...], qs), NEG)
        e = jnp.where(cmask, jnp.exp(s - jnp.max(s, axis=0, keepdims=True)), 0.0)
        p = e * (1.0 / jnp.maximum(jnp.sum(e, axis=0, keepdims=True), 1e-30))
        ocT = jnp.dot(vcmpT_ref[...], p.astype(BF16), preferred_element_type=F32)
        imp = p[:, 0:tq]
        for h in range(1, NSA_HPG):
            imp = imp + p[:, h * tq:(h + 1) * tq]
        imp_sel = _dot01(pool, imp)
        cur = t_row >> 6
        forced = (jcol == 0) | (jcol == cur) | (jcol == cur - 1)
        visible = jcol <= cur
        score = jnp.where(visible, imp_sel + jnp.where(forced, FORCE_SCORE, 0.0), NEG)
        rank = jnp.zeros((n_sel, tq), jnp.int32)
        for j2 in range(n_sel):
            r = score[j2:j2 + 1, :]
            beats = (r > score) | ((r == score) & (jcol > j2))
            rank = rank + beats.astype(jnp.int32)
        sel_ref[...] = jnp.where((rank < k_sel) & visible, 1.0, 0.0)

        def sel_step(kb, carry, causal):
            rowsel = jnp.concatenate(
                [jnp.broadcast_to(sel_ref[pl.ds(kb * per_kb + k, 1), :], (SEL_BLK, tq)) for k in range(per_kb)],
                axis=0)
            mask = rowsel > 0.5
            if causal:
                mask = mask & (kb * KEY_BLK + kio <= t_row)
            return _flash_step(ks_ref[kb], vsT_ref[kb], q_heads, mask, carry[0], carry[1], acc_ref, True)

        acc_ref[...] = jnp.zeros_like(acc_ref)
        ml = lax.fori_loop(0, n_full, functools.partial(sel_step, causal=False), (m0, l0))
        _, l = lax.fori_loop(n_full, n_kb, functools.partial(sel_step, causal=True), ml)
        osT = acc_ref[...] * (1.0 / jnp.maximum(l, 1e-30))

        def win_step(kb, carry):
            kpos = kb * KEY_BLK + kio
            mask = (kpos <= t_row) & (kpos > t_row - WINDOW)
            return _flash_step(kw_ref[kb], vwT_ref[kb], q_heads, mask, carry[0], carry[1], acc_ref, True)

        acc_ref[...] = jnp.zeros_like(acc_ref)
        w_lo = jnp.maximum(t0 - (WINDOW - 1), 0) // KEY_BLK
        _, l = lax.fori_loop(w_lo, n_kb, win_step, (m0, l0))
        owT = acc_ref[...] * (1.0 / jnp.maximum(l, 1e-30))

        def gate(c):
            return jnp.concatenate([gT[3 * (NSA_HPG * g + h) + c:3 * (NSA_HPG * g + h) + c + 1, :]
                                    for h in range(NSA_HPG)], axis=1)

        oT = gate(0) * ocT + gate(1) * osT + gate(2) * owT
        _store_heads(o_ref, oT, NSA_HPG * g, NSA_HPG, g)


def _block_diag2(w):
    w3 = w.reshape(CMP_BLK, HEAD_DIM, HEAD_DIM)
    z = jnp.zeros_like(w3)
    return jnp.concatenate([jnp.concatenate([w3, z], axis=2), jnp.concatenate([z, w3], axis=2)], axis=1).astype(BF16)


def nsa_prompt(qp, rows, win, gl, pe_k, pe_v, w_ck, w_cv):
    B, T, _ = rows.shape
    assert T % (CMP_STRIDE * LANE) == 0 and T % KEY_BLK == 0
    tq = ATT_TQ
    nq = NSA_HPG * tq
    n_half = T // CMP_STRIDE
    n_kb = T // KEY_BLK
    const = lambda shape: pl.BlockSpec(shape, lambda b, i: (0,) * len(shape))
    return pl.pallas_call(
        functools.partial(_nsa_body, T=T),
        grid=(B, T // tq),
        in_specs=[pl.BlockSpec((1, tq, NSA_HEADS * LANE), lambda b, i: (b, i, 0)),
                  pl.BlockSpec((1, T, 4 * LANE), lambda b, i: (b, 0, 0)),
                  pl.BlockSpec((1, T, 2 * LANE), lambda b, i: (b, 0, 0)),
                  pl.BlockSpec((1, tq, LANE), lambda b, i: (b, i, 0)),
                  const((CMP_BLK, LANE)), const((CMP_BLK, LANE)),
                  const((CMP_BLK, LANE, LANE)), const((CMP_BLK, LANE, LANE))],
        out_specs=pl.BlockSpec((1, tq, NSA_HEADS * LANE), lambda b, i: (b, i, 0)),
        out_shape=jax.ShapeDtypeStruct((B, T, NSA_HEADS * LANE), BF16),
        scratch_shapes=[pltpu.VMEM((n_half, LANE), BF16), pltpu.VMEM((LANE, n_half), BF16),
                        pltpu.VMEM((n_kb, KEY_BLK, LANE), BF16), pltpu.VMEM((n_kb, LANE, KEY_BLK), BF16),
                        pltpu.VMEM((n_kb, KEY_BLK, LANE), BF16), pltpu.VMEM((n_kb, LANE, KEY_BLK), BF16),
                        pltpu.VMEM((T, LANE), F32),
                        pltpu.VMEM((n_half + 8, LANE), F32), pltpu.VMEM((T // SEL_BLK, tq), F32),
                        pltpu.VMEM((LANE, nq), F32)],
        compiler_params=_params("parallel", "arbitrary"),
        name="nsa_prompt",
    )(qp, rows, win, gl, jnp.tile(pe_k, (1, 2)), jnp.tile(pe_v, (1, 2)), _block_diag2(w_ck), _block_diag2(w_cv))


def _dsa_body(qp_ref, kv_ref, qi_ref, ki_ref, wi_ref, o_ref,
              k_ref, vT_ref, kilo_ref, kihi_ref, key_ref, acc_ref, *, T, n_keep):
    i = pl.program_id(1)
    tq = ATT_TQ
    nq = DSA_HPG * tq

    @pl.when(i == 0)
    def _prepare():
        def copy_blk(kb, carry):
            for half in range(KEY_BLK // LANE):
                rs = pl.ds(pl.multiple_of(kb * KEY_BLK + half * LANE, LANE), LANE)
                hs = slice(half * LANE, (half + 1) * LANE)
                for slab in range(2):
                    k_ref[slab, kb, hs, :] = kv_ref[0, rs, slab * LANE:(slab + 1) * LANE].astype(BF16)
                    vT_ref[slab, kb, :, hs] = kv_ref[0, rs, (2 + slab) * LANE:(3 + slab) * LANE].T.astype(BF16)
                kix = ki_ref[0, rs, :]
                kilo_ref[kb, hs, :] = kix.astype(BF16)
                kihi_ref[kb, hs, :] = pltpu.roll(kix, IDX_DIM, 1).astype(BF16)
            return carry

        lax.fori_loop(0, T // KEY_BLK, copy_blk, 0)

    t0 = i * tq
    n_kb = (t0 + tq + KEY_BLK - 1) // KEY_BLK
    t_row = t0 + lax.broadcasted_iota(jnp.int32, (1, tq), 1)
    kio_q = lax.broadcasted_iota(jnp.int32, (KEY_BLK, tq), 0)

    wT = wi_ref[0].T * (IDX_HEADS ** -0.5)
    qi = _stack_heads(qi_ref, 0, IDX_HEADS // 2)

    def idx_step(kb, carry):
        s_lo = _nt(kilo_ref[kb], qi)
        s_hi = _nt(kihi_ref[kb], qi)
        sc = jnp.zeros((KEY_BLK, tq), F32)
        for p in range(IDX_HEADS // 2):
            cs = slice(p * tq, (p + 1) * tq)
            sc = sc + jnp.maximum(s_lo[:, cs], 0.0) * wT[2 * p:2 * p + 1, :]
            sc = sc + jnp.maximum(s_hi[:, cs], 0.0) * wT[2 * p + 1:2 * p + 2, :]
        sc = jnp.where(kb * KEY_BLK + kio_q <= t_row, sc, NEG)
        bits = lax.bitcast_convert_type(sc, jnp.int32)
        key_ref[kb] = jnp.where(bits < 0, bits ^ 0x7FFFFFFF, bits)
        return carry

    lax.fori_loop(0, n_kb, idx_step, 0)

    def bit_step(it, theta):
        cand = theta + lax.shift_left(jnp.int32(1), 31 - it)

        def cnt_step(kb, c):
            return c + jnp.sum((key_ref[kb] >= cand).astype(jnp.int32), axis=0, keepdims=True)

        cnt = lax.fori_loop(0, n_kb, cnt_step, jnp.zeros((1, tq), jnp.int32))
        return jnp.where(cnt >= n_keep, cand, theta)

    theta = lax.fori_loop(0, 32, bit_step, jnp.full((1, tq), INT_MIN, jnp.int32))

    def count(pred):
        step = lambda kb, c: c + jnp.sum(pred(key_ref[kb]).astype(jnp.int32), axis=0, keepdims=True)
        return lax.fori_loop(0, n_kb, step, jnp.zeros((1, tq), jnp.int32))

    n_ge = count(lambda k: k >= theta)

    @pl.when(jnp.max(n_ge) > n_keep)
    def _break_ties():
        need = (n_keep - count(lambda k: k > theta)).astype(F32)
        before = (lax.broadcasted_iota(jnp.int32, (KEY_BLK, KEY_BLK), 1)
                  < lax.broadcasted_iota(jnp.int32, (KEY_BLK, KEY_BLK), 0)).astype(BF16)

        def fix_step(kb, seen):
            keys = key_ref[kb]
            eq = keys == theta
            eqf = jnp.where(eq, 1.0, 0.0)
            rank = seen + jnp.dot(before, eqf.astype(BF16), preferred_element_type=F32)
            key_ref[kb] = jnp.where(eq & (rank >= need), INT_MIN, keys)
            return seen + jnp.sum(eqf, axis=0, keepdims=True)

        lax.fori_loop(0, n_kb, fix_step, jnp.zeros((1, tq), F32))

    q_heads = [_head_slabs(qp_ref, DSA_HPG * g, DSA_HPG) for g in range(DSA_KV)]
    m0 = jnp.full((1, nq), M_FLOOR, F32)
    l0 = jnp.zeros((1, nq), F32)
    acc_ref[...] = jnp.zeros_like(acc_ref)

    def att_step(kb, carry, causal):
        mask = key_ref[kb] >= theta
        if causal:
            mask = mask & (kb * KEY_BLK + kio_q <= t_row)
        out = []
        for g in range(DSA_KV):
            m, l = _flash_step(k_ref[g // 2, kb], vT_ref[g // 2, kb], q_heads[g], mask,
                               carry[2 * g], carry[2 * g + 1], acc_ref.at[g], False)
            out += [m, l]
        return tuple(out)

    mid = lax.fori_loop(0, t0 // KEY_BLK, functools.partial(att_step, causal=False), (m0, l0) * DSA_KV)
    fin = lax.fori_loop(t0 // KEY_BLK, n_kb, functools.partial(att_step, causal=True), mid)
    for g in range(DSA_KV):
        oT = acc_ref[g] * (1.0 / jnp.maximum(fin[2 * g + 1], 1e-30))
        _store_heads(o_ref, oT, DSA_HPG * g, DSA_HPG, g % 2)


def dsa_prompt(qp, kv, qi, ki, wi):
    B, T, _ = kv.shape
    assert T % KEY_BLK == 0
    tq = ATT_TQ
    nq = DSA_HPG * tq
    n_kb = T // KEY_BLK
    n_keep = min(IDX_TOPK, T // 4)
    tile = lambda w: pl.BlockSpec((1, tq, w), lambda b, i: (b, i, 0))
    whole = lambda w: pl.BlockSpec((1, T, w), lambda b, i: (b, 0, 0))
    return pl.pallas_call(
        functools.partial(_dsa_body, T=T, n_keep=n_keep),
        grid=(B, T // tq),
        in_specs=[tile(DSA_HEADS * LANE), whole(4 * LANE), tile(IDX_QW), whole(LANE), tile(LANE)],
        out_specs=tile(DSA_HEADS * LANE),
        out_shape=jax.ShapeDtypeStruct((B, T, DSA_HEADS * LANE), BF16),
        scratch_shapes=[pltpu.VMEM((2, n_kb, KEY_BLK, LANE), BF16), pltpu.VMEM((2, n_kb, LANE, KEY_BLK), BF16),
                        pltpu.VMEM((n_kb, KEY_BLK, LANE), BF16), pltpu.VMEM((n_kb, KEY_BLK, LANE), BF16),
                        pltpu.VMEM((n_kb, KEY_BLK, tq), jnp.int32), pltpu.VMEM((DSA_KV, LANE, nq), F32)],
        compiler_params=_params("parallel", "arbitrary"),
        name="dsa_prompt",
    )(qp, kv, qi, ki, wi)


def _pad_head_cols(w, slots):
    D = w.shape[0]
    H = len(slots)
    w3 = w.reshape(D, H, HEAD_DIM)
    z = jnp.zeros_like(w3)
    s = jnp.asarray(slots)[None, :, None]
    return jnp.concatenate([jnp.where(s == 0, w3, z), jnp.where(s == 1, w3, z)], axis=2).reshape(D, H * LANE)


def _pad_cols(w, n):
    return jnp.pad(w, ((0, 0), (0, n - w.shape[1])))


def _dot01_r(x, m01):
    hi, mid, lo = _split3(x)
    d = lambda p: jnp.dot(p, m01, preferred_element_type=F32)
    return d(hi) + d(mid) + d(lo)


def _softmax_rows(s, mask):
    s = jnp.where(mask, s, NEG)
    e = jnp.where(mask, jnp.exp(s - jnp.max(s, axis=1, keepdims=True)), 0.0)
    return e * (1.0 / jnp.maximum(jnp.sum(e, axis=1, keepdims=True), 1e-30))


def _flash_rows(s, mask, m_ref, l_ref):
    s = jnp.where(mask, s, NEG)
    m = m_ref[...]
    m_new = jnp.maximum(m, jnp.max(s, axis=1, keepdims=True))
    alpha = jnp.exp(m - m_new)
    e = jnp.where(mask, jnp.exp(s - m_new), 0.0)
    l_ref[...] = alpha * l_ref[...] + jnp.sum(e, axis=1, keepdims=True)
    m_ref[...] = m_new
    return alpha, e


def _sort_key(x):
    bits = lax.bitcast_convert_type(x, jnp.int32)
    return jnp.where(bits < 0, bits ^ 0x7FFFFFFF, bits)


def _page_spec(layer, k, chans, blk):
    return pl.BlockSpec((None, 1, chans, PAGE_SIZE), lambda b, s, pt: (layer, pt[b, s * PG_STEP + k], blk, 0))


def _channel_major(cache):
    layers, n, rows = cache.shape[:3]
    return jnp.moveaxis(cache.reshape(layers, n, rows, -1), 2, 3)


def _per_request(shape):
    return pl.BlockSpec((1,) + shape, lambda b, s, pt: (b,) + (0,) * len(shape))


def _nsa_cmp_body(pt_ref, *refs):
    pages = refs[:PG_STEP]
    pek_ref, pev_ref, wck_ref, wcv_ref, o_ref, stage_ref = refs[PG_STEP:]
    n_half = PG_STEP * PAGE_SIZE // CMP_STRIDE
    for slab, pe_ref, w_ref in ((0, pek_ref, wck_ref), (1, pev_ref, wcv_ref)):
        for pg in range(PG_STEP):
            stage_ref[pg * PAGE_SIZE:(pg + 1) * PAGE_SIZE, :] = pages[pg][0, slab * LANE:(slab + 1) * LANE, :].T
        first = jnp.zeros((n_half, LANE), F32)
        second = jnp.zeros((n_half, LANE), F32)
        for r in range(CMP_STRIDE):
            y = stage_ref[pl.ds(r, n_half, stride=CMP_STRIDE), :]
            first = first + jnp.dot((y + pe_ref[r:r + 1, :]).astype(BF16), w_ref[r], preferred_element_type=F32)
            second = second + jnp.dot((y + pe_ref[r + CMP_STRIDE:r + CMP_STRIDE + 1, :]).astype(BF16),
                                      w_ref[r + CMP_STRIDE], preferred_element_type=F32)
        o_ref[0, :, (2 * slab) * LANE:(2 * slab + 1) * LANE] = first
        o_ref[0, :, (2 * slab + 1) * LANE:(2 * slab + 2) * LANE] = second


def nsa_sample_compress(cache, layer, page_table, pe_k, pe_v, w_ck, w_cv):
    B, n_pages = page_table.shape
    assert n_pages % PG_STEP == 0
    n_half_step = PG_STEP * PAGE_SIZE // CMP_STRIDE
    const = lambda shape: pl.BlockSpec(shape, lambda b, s, pt: (0,) * len(shape))
    grid_spec = pltpu.PrefetchScalarGridSpec(
        num_scalar_prefetch=1,
        grid=(B, n_pages // PG_STEP),
        in_specs=[_page_spec(layer, k, 2 * LANE, 0) for k in range(PG_STEP)]
        + [const((CMP_BLK, LANE)), const((CMP_BLK, LANE)), const((CMP_BLK, LANE, LANE)), const((CMP_BLK, LANE, LANE))],
        out_specs=pl.BlockSpec((1, n_half_step, 4 * LANE), lambda b, s, pt: (b, s, 0)),
        scratch_shapes=[pltpu.VMEM((PG_STEP * PAGE_SIZE, LANE), F32)],
    )
    return pl.pallas_call(
        _nsa_cmp_body,
        grid_spec=grid_spec,
        out_shape=jax.ShapeDtypeStruct((B, n_pages * PAGE_SIZE // CMP_STRIDE, 4 * LANE), F32),
        compiler_params=_params("parallel", "arbitrary"),
        name="nsa_sample_compress",
    )(page_table, *([cache] * PG_STEP), jnp.tile(pe_k, (1, 2)), jnp.tile(pe_v, (1, 2)),
      _block_diag2(w_ck), _block_diag2(w_cv))


def _nsa_smp_body(pt_ref, qs_ref, gl_ref, fs_ref, *refs, n_tok):
    pages = refs[:PG_STEP]
    nrow_ref, pwin_ref, nwin_ref, o_ref, shift_ref, selc_ref, oc_ref, m_ref, l_ref, acc_ref = refs[PG_STEP:]
    s_id = pl.program_id(1)
    C = LANE
    n_half = fs_ref.shape[1]
    n_cmp = n_half - 1
    n_selp = n_half * CMP_STRIDE // SEL_BLK
    qs = qs_ref[0]
    crow = lax.broadcasted_iota(jnp.int32, (C, LANE), 0)
    lane = lax.broadcasted_iota(jnp.int32, (C, LANE), 1)
    t_of_c = crow % n_tok

    @pl.when(s_id == 0)
    def _first():
        cmp = []
        for slab in range(2):
            shift_ref[0:n_half, :] = fs_ref[0, :, (2 * slab + 1) * LANE:(2 * slab + 2) * LANE]
            shift_ref[n_half:n_half + 8, :] = jnp.zeros((8, LANE), F32)
            cmp.append((fs_ref[0, :, (2 * slab) * LANE:(2 * slab + 1) * LANE]
                        + shift_ref[1:n_half + 1, :]).astype(BF16))
        nlane = lax.broadcasted_iota(jnp.int32, (C, n_half), 1)
        p = _softmax_rows(_nt(qs, cmp[0]), nlane < n_cmp)
        oc_ref[...] = jnp.dot(p.astype(BF16), cmp[1], preferred_element_type=F32)
        imp_rows = []
        for g in range(NSA_KV):
            acc = p[(NSA_HPG * g) * n_tok:(NSA_HPG * g + 1) * n_tok, :]
            for h in range(1, NSA_HPG):
                acc = acc + p[(NSA_HPG * g + h) * n_tok:(NSA_HPG * g + h + 1) * n_tok, :]
            imp_rows.append(acc)
        imp = jnp.concatenate(imp_rows + [jnp.zeros((C - NSA_KV * n_tok, n_half), F32)], axis=0)
        pi = lax.broadcasted_iota(jnp.int32, (n_half, n_selp), 0)
        pj = lax.broadcasted_iota(jnp.int32, (n_half, n_selp), 1)
        pool = ((((pi + 1) >> 2) == pj) & (pi < n_cmp)).astype(BF16)
        scoreT = _dot01_r(imp, pool).T
        jcol = lax.broadcasted_iota(jnp.int32, (n_selp, C), 0)
        forced = (jcol == 0) | (jcol == n_selp - 1)
        scoreT = scoreT + jnp.where(forced, FORCE_SCORE, 0.0)
        rank = jnp.zeros((n_selp, C), jnp.int32)
        for j2 in range(n_selp):
            r = scoreT[j2:j2 + 1, :]
            beats = (r > scoreT) | ((r == scoreT) & (jcol > j2))
            rank = rank + beats.astype(jnp.int32)
        sel = jnp.where(rank < TOP_N - 1, 1.0, 0.0).T
        selc_ref[...] = jnp.concatenate(
            [sel[(hh // NSA_HPG) * n_tok:(hh // NSA_HPG + 1) * n_tok, :] for hh in range(NSA_HEADS)]
            + [jnp.zeros((C - NSA_HEADS * n_tok, n_selp), F32)], axis=0)
        m_ref[...] = jnp.full(m_ref.shape, NEG, F32)
        l_ref[...] = jnp.zeros(l_ref.shape, F32)
        acc_ref[...] = jnp.zeros(acc_ref.shape, F32)

    selc = selc_ref[...].astype(BF16)
    n_keys = PG_STEP * PAGE_SIZE
    ej = lax.broadcasted_iota(jnp.int32, (n_selp, n_keys), 0)
    ek = lax.broadcasted_iota(jnp.int32, (n_selp, n_keys), 1)
    expand = (ej == s_id * (n_keys // SEL_BLK) + ek // SEL_BLK).astype(BF16)
    mask = jnp.dot(selc, expand, preferred_element_type=F32) > 0.5
    s = jnp.concatenate([jnp.dot(qs, pages[pg][0, 0:LANE, :].astype(BF16), preferred_element_type=F32)
                         for pg in range(PG_STEP)], axis=1)
    alpha, e = _flash_rows(s, mask, m_ref, l_ref)
    eb = e.astype(BF16)
    pv = _nt(eb[:, 0:PAGE_SIZE], pages[0][0, LANE:2 * LANE, :].astype(BF16))
    for pg in range(1, PG_STEP):
        pv = pv + _nt(eb[:, pg * PAGE_SIZE:(pg + 1) * PAGE_SIZE], pages[pg][0, LANE:2 * LANE, :].astype(BF16))
    acc_ref[...] = alpha * acc_ref[...] + pv

    @pl.when(s_id == pl.num_programs(1) - 1)
    def _last():
        pad = jnp.zeros((LANE - n_tok, LANE), F32)
        new_ok = (lane < n_tok) & (lane <= t_of_c)
        knew = jnp.concatenate([nrow_ref[0, :, 2 * LANE:3 * LANE], pad], axis=0).astype(BF16)
        vnew = jnp.concatenate([nrow_ref[0, :, 3 * LANE:4 * LANE], pad], axis=0).astype(BF16)
        alpha, e = _flash_rows(_nt(qs, knew), new_ok, m_ref, l_ref)
        acc = alpha * acc_ref[...] + jnp.dot(e.astype(BF16), vnew, preferred_element_type=F32)
        o_s = acc * (1.0 / jnp.maximum(l_ref[...], 1e-30))
        n_win = pwin_ref.shape[2]
        kwn = jnp.concatenate([nwin_ref[0, :, 0:LANE], pad], axis=0).astype(BF16)
        vwn = jnp.concatenate([nwin_ref[0, :, LANE:2 * LANE], pad], axis=0).astype(BF16)
        wl = lax.broadcasted_iota(jnp.int32, (C, n_win + LANE), 1)
        tw = lax.broadcasted_iota(jnp.int32, (C, n_win + LANE), 0) % n_tok
        wmask = (((wl < n_win) & ((n_win - wl) + tw < WINDOW))
                 | ((wl >= n_win) & (wl - n_win < n_tok) & (wl - n_win <= tw)))
        s_w = jnp.concatenate([jnp.dot(qs, pwin_ref[0, 0:LANE, :].astype(BF16), preferred_element_type=F32),
                               _nt(qs, kwn)], axis=1)
        pw = _softmax_rows(s_w, wmask).astype(BF16)
        o_w = (_nt(pw[:, 0:n_win], pwin_ref[0, LANE:2 * LANE, :].astype(BF16))
               + jnp.dot(pw[:, n_win:], vwn, preferred_element_type=F32))
        g = jax.nn.sigmoid(gl_ref[0])
        o = g[:, 0:1] * oc_ref[...] + g[:, 1:2] * o_s + g[:, 2:3] * o_w
        valid = (crow < NSA_HEADS * n_tok) & (lane // HEAD_DIM == crow // (n_tok * NSA_HPG))
        o_ref[0] = jnp.where(valid, o, 0.0)


def nsa_sample_attention(qs, gcol, fs, cache, layer, page_table, new_rows, past_win, new_win):
    B, n_pages = page_table.shape
    n_tok = new_rows.shape[1]
    n_half = fs.shape[1]
    n_selp = n_half * CMP_STRIDE // SEL_BLK
    n_win = past_win.shape[3]
    assert n_selp == LANE and NSA_HEADS * n_tok <= LANE and n_tok <= min(SEL_BLK, 8) and n_pages % PG_STEP == 0
    assert n_win % LANE == 0 and n_win <= WINDOW
    grid_spec = pltpu.PrefetchScalarGridSpec(
        num_scalar_prefetch=1,
        grid=(B, n_pages // PG_STEP),
        in_specs=[_per_request((LANE, LANE)), _per_request((LANE, LANE)), _per_request((n_half, 4 * LANE))]
        + [_page_spec(layer, k, 2 * LANE, 1) for k in range(PG_STEP)]
        + [_per_request((n_tok, 4 * LANE)),
           pl.BlockSpec((None, 1, 2 * LANE, n_win), lambda b, s, pt: (layer, b, 0, 0)),
           _per_request((n_tok, 2 * LANE))],
        out_specs=_per_request((LANE, LANE)),
        scratch_shapes=[pltpu.VMEM((n_half + 8, LANE), F32), pltpu.VMEM((LANE, n_selp), F32),
                        pltpu.VMEM((LANE, LANE), F32), pltpu.VMEM((LANE, 1), F32), pltpu.VMEM((LANE, 1), F32),
                        pltpu.VMEM((LANE, LANE), F32)],
    )
    return pl.pallas_call(
        functools.partial(_nsa_smp_body, n_tok=n_tok),
        grid_spec=grid_spec,
        out_shape=jax.ShapeDtypeStruct((B, LANE, LANE), F32),
        compiler_params=_params("parallel", "arbitrary"),
        name="nsa_sample_attention",
    )(page_table, qs, gcol, fs, *([cache] * PG_STEP), new_rows, past_win, new_win)


def _dsa_idx_body(pt_ref, qe_ref, qo_ref, w_ref, *refs, n_tok, n_keep):
    pages = refs[:PG_STEP]
    knew_ref, sc_ref, th_ref = refs[PG_STEP:]
    s_id = pl.program_id(1)
    n_blk = sc_ref.shape[1]
    qe = qe_ref[0]
    qo = qo_ref[0]
    w = w_ref[0] * (IDX_HEADS ** -0.5)
    n_rows = (IDX_HEADS // 2) * n_tok

    def scores(s_e, s_o):
        s = jnp.maximum(s_e, 0.0) * w[:, 0:1] + jnp.maximum(s_o, 0.0) * w[:, 1:2]
        tot = s[0:n_tok, :]
        for p in range(1, IDX_HEADS // 2):
            tot = tot + s[p * n_tok:(p + 1) * n_tok, :]
        return tot

    kT = jnp.concatenate([pages[pg][0] for pg in range(PG_STEP)], axis=1).astype(BF16)
    sc = scores(jnp.dot(qe, kT, preferred_element_type=F32), jnp.dot(qo, kT, preferred_element_type=F32))
    for pg in range(PG_STEP):
        sc_ref[0, s_id * PG_STEP + pg] = sc[:, pg * PAGE_SIZE:(pg + 1) * PAGE_SIZE]

    @pl.when(s_id == pl.num_programs(1) - 1)
    def _last():
        kn = jnp.concatenate([knew_ref[0], jnp.zeros((LANE - n_tok, IDX_DIM), F32)], axis=0).astype(BF16)
        a_i = lax.broadcasted_iota(jnp.int32, (n_tok, LANE), 1)
        t_i = lax.broadcasted_iota(jnp.int32, (n_tok, LANE), 0)
        sn = jnp.where(a_i <= t_i, scores(_nt(qe, kn), _nt(qo, kn)), NEG)
        sc_ref[0, n_blk - 1] = jnp.where(a_i < n_tok, sn, -jnp.inf)
        keys = _sort_key(sc_ref[0])

        def bit_step(it, theta):
            cand = theta + lax.shift_left(jnp.int32(1), 31 - it)
            cnt = jnp.sum(jnp.sum((keys >= cand).astype(jnp.int32), axis=0), axis=1, keepdims=True)
            return jnp.where(cnt >= n_keep, cand, theta)

        theta = lax.fori_loop(0, 32, bit_step, jnp.full((n_tok, 1), INT_MIN, jnp.int32))
        th_ref[0] = jnp.broadcast_to(theta, (n_tok, LANE))

        count = lambda m: jnp.sum(jnp.sum(m.astype(jnp.int32), axis=0), axis=1, keepdims=True)

        @pl.when(jnp.max(count(keys >= theta)) > n_keep)
        def _break_ties():
            need = (n_keep - count(keys > theta)).astype(F32)
            before = (lax.broadcasted_iota(jnp.int32, (LANE, LANE), 0)
                      < lax.broadcasted_iota(jnp.int32, (LANE, LANE), 1)).astype(BF16)

            def fix_step(blk, seen):
                sc = sc_ref[0, blk]
                eq = _sort_key(sc) == theta
                eqf = jnp.where(eq, 1.0, 0.0)
                rank = seen + jnp.dot(eqf.astype(BF16), before, preferred_element_type=F32)
                sc_ref[0, blk] = jnp.where(eq & (rank >= need), -jnp.inf, sc)
                return seen + jnp.sum(eqf, axis=1, keepdims=True)

            lax.fori_loop(0, n_blk, fix_step, jnp.zeros((n_tok, 1), F32))


def dsa_sample_index(qe, qo, wcol, cache_idx, layer, page_table, ki_new):
    B, n_pages = page_table.shape
    n_tok = ki_new.shape[1]
    assert n_tok == 8 and n_pages % PG_STEP == 0
    n_keep = min(IDX_TOPK, (n_pages * PAGE_SIZE + n_tok) // 4)
    grid_spec = pltpu.PrefetchScalarGridSpec(
        num_scalar_prefetch=1,
        grid=(B, n_pages // PG_STEP),
        in_specs=[_per_request((LANE, IDX_DIM)), _per_request((LANE, IDX_DIM)), _per_request((LANE, LANE))]
        + [_page_spec(layer, k, IDX_DIM, 0) for k in range(PG_STEP)] + [_per_request((n_tok, IDX_DIM))],
        out_specs=[_per_request((n_pages + 1, n_tok, LANE)), _per_request((n_tok, LANE))],
    )
    return pl.pallas_call(
        functools.partial(_dsa_idx_body, n_tok=n_tok, n_keep=n_keep),
        grid_spec=grid_spec,
        out_shape=[jax.ShapeDtypeStruct((B, n_pages + 1, n_tok, LANE), F32),
                   jax.ShapeDtypeStruct((B, n_tok, LANE), jnp.int32)],
        compiler_params=_params("parallel", "arbitrary"),
        name="dsa_sample_index",
    )(page_table, qe, qo, wcol, *([cache_idx] * PG_STEP), ki_new)


def _dsa_smp_body(pt_ref, qs_ref, sc_ref, th_ref, *refs, n_tok):
    pages = refs[:PG_STEP]
    kvn_ref, o_ref, m_ref, l_ref, acc_ref = refs[PG_STEP:]
    s_id = pl.program_id(1)
    C = LANE
    qs = qs_ref[0]
    crow = lax.broadcasted_iota(jnp.int32, (C, LANE), 0)
    lane = lax.broadcasted_iota(jnp.int32, (C, LANE), 1)
    low = crow < C // 2
    theta = th_ref[0]

    @pl.when(s_id == 0)
    def _():
        m_ref[...] = jnp.full(m_ref.shape, NEG, F32)
        l_ref[...] = jnp.zeros(l_ref.shape, F32)
        acc_ref[...] = jnp.zeros(acc_ref.shape, F32)

    def step(kvTs, blk0, extra):
        kbs = [kvT.astype(BF16) for kvT in kvTs]
        keep = jnp.concatenate([jnp.where(_sort_key(sc_ref[0, blk0 + j]) >= theta, 1.0, 0.0)
                                for j in range(len(kbs))], axis=1)
        mask = jnp.concatenate([keep] * (C // n_tok), axis=0) > 0.5
        if extra is not None:
            mask = mask & extra
        s_lo = jnp.concatenate([jnp.dot(qs, kb[0:LANE], preferred_element_type=F32) for kb in kbs], axis=1)
        s_hi = jnp.concatenate([jnp.dot(qs, kb[LANE:2 * LANE], preferred_element_type=F32) for kb in kbs], axis=1)
        low_col = lax.broadcasted_iota(jnp.int32, (C, 1), 0) < C // 2
        alpha, e = _flash_rows(jnp.where(low_col, s_lo, s_hi), mask, m_ref, l_ref)
        eb = e.astype(BF16)
        pv_lo = pv_hi = None
        for j, kb in enumerate(kbs):
            ej = eb[:, j * LANE:(j + 1) * LANE]
            a, b = _nt(ej, kb[2 * LANE:3 * LANE]), _nt(ej, kb[3 * LANE:4 * LANE])
            pv_lo, pv_hi = (a, b) if pv_lo is None else (pv_lo + a, pv_hi + b)
        acc_ref[...] = alpha * acc_ref[...] + jnp.where(low, pv_lo, pv_hi)

    step([pages[pg][0] for pg in range(PG_STEP)], s_id * PG_STEP, None)

    @pl.when(s_id == pl.num_programs(1) - 1)
    def _last():
        kvn = jnp.concatenate([kvn_ref[0], jnp.zeros((LANE - n_tok, 4 * LANE), F32)], axis=0)
        kvnT = jnp.concatenate([kvn[:, j * LANE:(j + 1) * LANE].T for j in range(4)], axis=0)
        step([kvnT], sc_ref.shape[1] - 1, (lane < n_tok) & (lane <= crow % n_tok))
        o = acc_ref[...] * (1.0 / jnp.maximum(l_ref[...], 1e-30))
        valid = lane // HEAD_DIM == (crow // (n_tok * DSA_HPG)) % 2
        o_ref[0] = jnp.where(valid, o, 0.0)


def dsa_sample_attention(qs, scores, theta, cache_kv, layer, page_table, kv_new):
    B, n_pages = page_table.shape
    n_tok = kv_new.shape[1]
    assert DSA_HEADS * n_tok == LANE and n_pages % PG_STEP == 0
    grid_spec = pltpu.PrefetchScalarGridSpec(
        num_scalar_prefetch=1,
        grid=(B, n_pages // PG_STEP),
        in_specs=[_per_request((LANE, LANE)), _per_request((n_pages + 1, n_tok, LANE)), _per_request((n_tok, LANE))]
        + [_page_spec(layer, k, 4 * LANE, 0) for k in range(PG_STEP)] + [_per_request((n_tok, 4 * LANE))],
        out_specs=_per_request((LANE, LANE)),
        scratch_shapes=[pltpu.VMEM((LANE, 1), F32), pltpu.VMEM((LANE, 1), F32), pltpu.VMEM((LANE, LANE), F32)],
    )
    return pl.pallas_call(
        functools.partial(_dsa_smp_body, n_tok=n_tok),
        grid_spec=grid_spec,
        out_shape=jax.ShapeDtypeStruct((B, LANE, LANE), F32),
        compiler_params=_params("parallel", "arbitrary"),
        name="dsa_sample_attention",
    )(page_table, qs, scores, theta, *([cache_kv] * PG_STEP), kv_new)


_NSA_SLOTS = tuple(h // NSA_HPG for h in range(NSA_HEADS))
_DSA_SLOTS = tuple((h // DSA_HPG) % 2 for h in range(DSA_HEADS))
def _even_segs(q_dtype, cm):
    return ((NSA_HEADS * LANE, (1,) * NSA_HEADS, q_dtype, False), (4 * LANE, (1, 0, 1, 0), F32, cm),
            (2 * LANE, (1, 0), F32, False), (LANE, (0,), F32, False),
            (4 * HGRN_KW, (0,) * (4 * HGRN_KW // LANE), F32, False))


def _odd_segs(q_dtype, cm):
    return ((DSA_HEADS * LANE, (1,) * DSA_HEADS, q_dtype, False), (4 * LANE, (1, 1, 0, 0), F32, cm),
            (IDX_QW, (1,) * 4, q_dtype, False), (LANE, (1,), F32, cm), (LANE, (0,), F32, False))


def _even_weights(w_in, w_out):
    c = np.cumsum([0, NSA_QW] + [NSA_KVW] * 6 + [NSA_GW] + [HGRN_KW] * 4)
    w = jnp.concatenate([
        _pad_head_cols(w_in[:, :NSA_QW] * HEAD_DIM ** -0.5, _NSA_SLOTS),
        w_in[:, c[1]:c[5]], w_in[:, c[5]:c[7]], _pad_cols(w_in[:, c[7]:c[8]], LANE), w_in[:, c[8]:c[12]]],
        axis=1).astype(BF16)
    w_outs = [_pad_head_cols(w_out[:NSA_QW].T, _NSA_SLOTS).T.astype(BF16), w_out[NSA_QW:].astype(BF16)]
    return w, w_outs


def _odd_weights(w_in, w_out):
    c = np.cumsum([0, DSA_QW, DSA_KVW, DSA_KVW, IDX_QW, IDX_DIM, IDX_HEADS])
    w = jnp.concatenate([
        _pad_head_cols(w_in[:, :DSA_QW] * HEAD_DIM ** -0.5, _DSA_SLOTS),
        w_in[:, c[1]:c[3]], w_in[:, c[3]:c[4]] * IDX_DIM ** -0.5,
        _pad_cols(w_in[:, c[4]:c[5]], LANE), _pad_cols(w_in[:, c[5]:c[6]], LANE)], axis=1).astype(BF16)
    return w, [_pad_head_cols(w_out.T, _DSA_SLOTS).T.astype(BF16)]


def even_mixer_prompt(x, scale, shift, cos_t, sin_t, w_in, w_out, pe_k, pe_v, w_ck, w_cv, lb, norm_g, win_len):
    B, T, _ = x.shape
    assert T >= win_len
    w, w_outs = _even_weights(w_in, w_out)
    qp, rows, win, gl, hx, rows_cm = proj_segments(x, scale, shift, cos_t, sin_t, w, _even_segs(BF16, True))
    o_a = nsa_prompt(qp, rows, win, gl, pe_k, pe_v, w_ck, w_cv)
    o_b, s_new = hgrn_mixer(hx, lb, norm_g, jnp.zeros((B, HGRN_HEADS, HGRN_DK, HGRN_DV), F32))
    new_rows = jnp.moveaxis(rows_cm.reshape(B, 4, NSA_KV, HEAD_DIM, T), 4, 1)
    win_state = win[:, T - win_len:].reshape(B, win_len, 2, NSA_KV, HEAD_DIM)
    return [o_a, o_b], w_outs, new_rows, win_state, s_new


def odd_mixer_prompt(x, scale, shift, cos_t, sin_t, w_in, w_out):
    B, T, _ = x.shape
    w, w_outs = _odd_weights(w_in, w_out)
    qp, kv, qi, ki, wi, kv_cm, ki_cm = proj_segments(x, scale, shift, cos_t, sin_t, w, _odd_segs(BF16, True))
    o = dsa_prompt(qp, kv, qi, ki, wi)
    new_kv = jnp.moveaxis(kv_cm.reshape(B, 2, DSA_KV, HEAD_DIM, T), 4, 1)
    return [o], w_outs, new_kv, jnp.moveaxis(ki_cm[:, :IDX_DIM], 2, 1)


def _stack_rows(a, B, n, heads, width):
    s = a.reshape(B, n, heads, width).transpose(0, 2, 1, 3).reshape(B, heads * n, width)
    return jnp.pad(s, ((0, 0), (0, LANE - heads * n), (0, 0)))


def _unstack_rows(o, B, n, heads):
    return o[:, :heads * n].reshape(B, heads, n, LANE).transpose(0, 2, 1, 3).reshape(1, B * n, heads * LANE)


def even_mixer_sample(x, scale, shift, past_len, B, w_in, w_out, pe_k, pe_v, w_ck, w_cv, lb, norm_g,
                      cache, cache_win, state, page_table, layer):
    n = x.shape[1] // B
    win_len = cache_win.shape[2]
    cos_t, sin_t = [jnp.tile(a, (B, 1)) for a in rope_tables(past_len + jnp.arange(n))]
    w, w_outs = _even_weights(w_in, w_out)
    qp, rows, win, gl, hx = proj_segments(x, scale, shift, cos_t, sin_t, w, _even_segs(F32, False))
    rows, win, hx = [a.reshape(B, n, a.shape[-1]) for a in (rows, win, hx)]
    qs = _stack_rows(qp, B, n, NSA_HEADS, LANE).astype(BF16)
    gcol = jnp.pad(_stack_rows(gl[..., :NSA_GW], B, n, NSA_HEADS, 3), ((0, 0), (0, 0), (0, LANE - 3)))
    cache_cm = _channel_major(cache)
    fs = nsa_sample_compress(cache_cm, layer, page_table, pe_k, pe_v, w_ck, w_cv)
    o = nsa_sample_attention(qs, gcol, fs, cache_cm, layer, page_table, rows, _channel_major(cache_win), win)
    o_a = _unstack_rows(o, B, n, NSA_HEADS)
    o_b, s_new = hgrn_mixer(hx, lb, norm_g, state)
    new_rows = rows.reshape(B, n, 4, NSA_KV, HEAD_DIM)
    win_state = jnp.concatenate([cache_win[layer], win.reshape(B, n, 2, NSA_KV, HEAD_DIM)], axis=1)[:, -win_len:]
    return [o_a, o_b.reshape(1, B * n, HGRN_VW)], w_outs, new_rows, win_state, s_new


def odd_mixer_sample(x, scale, shift, past_len, B, w_in, w_out, cache_kv, cache_idx, page_table, layer):
    n = x.shape[1] // B
    cos_t, sin_t = [jnp.tile(a, (B, 1)) for a in rope_tables(past_len + jnp.arange(n))]
    w, w_outs = _odd_weights(w_in, w_out)
    qp, kv, qi, ki, wi = proj_segments(x, scale, shift, cos_t, sin_t, w, _odd_segs(F32, False))
    kv, ki = kv.reshape(B, n, 4 * LANE), ki.reshape(B, n, LANE)[..., :IDX_DIM]
    qs = _stack_rows(qp, B, n, DSA_HEADS, LANE).astype(BF16)
    qi4 = qi.reshape(1, B * n, IDX_HEADS // 2, 2, IDX_DIM)
    qe = _stack_rows(qi4[:, :, :, 0], B, n, IDX_HEADS // 2, IDX_DIM).astype(BF16)
    qo = _stack_rows(qi4[:, :, :, 1], B, n, IDX_HEADS // 2, IDX_DIM).astype(BF16)
    wcol = jnp.pad(_stack_rows(wi[..., :IDX_HEADS], B, n, IDX_HEADS // 2, 2), ((0, 0), (0, 0), (0, LANE - 2)))
    scores, theta = dsa_sample_index(qe, qo, wcol, _channel_major(cache_idx), layer, page_table, ki)
    o = dsa_sample_attention(qs, scores, theta, _channel_major(cache_kv), layer, page_table, kv)
    return [_unstack_rows(o, B, n, DSA_HEADS)], w_outs, kv.reshape(B, n, 2, DSA_KV, HEAD_DIM), ki


def moe_ffn_residual(xs, hs, logits_list, gates, w1, w3, w2):
    D = D_MODEL
    h_all = jnp.concatenate([h.reshape(-1, D) for h in hs], axis=0)
    logits = jnp.concatenate([lg.reshape(-1, lg.shape[-1])[:, :N_EXPERTS] for lg in logits_list], axis=0)
    n_tok = h_all.shape[0]
    top_v, top_i = lax.top_k(logits, TOP_K)
    weights = jax.nn.softmax(top_v, axis=-1)
    e_flat = top_i.reshape(-1)
    onehot = (e_flat[:, None] == jnp.arange(N_EXPERTS)[None, :]).astype(jnp.int32)
    csum = jnp.cumsum(onehot, axis=0)
    counts = csum[-1]
    rank = jnp.take_along_axis(csum, e_flat[:, None], axis=1)[:, 0] - 1
    padded = ((counts + MOE_TILE - 1) // MOE_TILE) * MOE_TILE
    group_end = jnp.cumsum(padded)
    group_start = group_end - padded
    slot = group_start[e_flat] + rank
    n_slots = _round_up(n_tok * TOP_K, MOE_TILE) + N_EXPERTS * MOE_TILE
    n_tiles = n_slots // MOE_TILE
    tok_of_slot = jnp.zeros((n_slots,), jnp.int32).at[slot].set(jnp.arange(n_tok * TOP_K, dtype=jnp.int32) // TOP_K)
    tile_start = jnp.arange(n_tiles, dtype=jnp.int32) * MOE_TILE
    tile_expert = jnp.minimum(jnp.sum(tile_start[:, None] >= group_end[None, :], axis=1), N_EXPERTS - 1).astype(jnp.int32)
    n_used = (group_end[-1] // MOE_TILE).astype(jnp.int32).reshape(1)
    h_sorted = h_all[tok_of_slot]
    y_slot = moe_grouped_swiglu(h_sorted, tile_expert, n_used, w1, w3, w2)
    slot2 = slot.reshape(n_tok, TOP_K)
    outs = []
    off = 0
    for x, g in zip(xs, gates):
        n = x.shape[0] * x.shape[1]
        ya = y_slot[slot2[off:off + n, 0]].reshape(x.shape)
        yb = y_slot[slot2[off:off + n, 1]].reshape(x.shape)
        w = weights[off:off + n].reshape(x.shape[0], x.shape[1], TOP_K)
        outs.append(moe_combine_residual(x, g, w, ya, yb))
        off += n
    return outs


def kernel(x_prompt, x_sample, cache_nsa, cache_nsa_win, state_hgrn, cache_dsa_kv, cache_dsa_idx, page_table, c_prompt, c_sample, ada_w, ada_b, norm1_g, norm2_g, final_g, even_w_in, even_w_out, nsa_pe_k, nsa_pe_v, nsa_w_ck, nsa_w_cv, hgrn_lb_raw, hgrn_norm_g, ffn_w1, ffn_w3, ffn_w2, odd_w_in, odd_w_out, router_w, router_b, moe_w1, moe_w3, moe_w2):
    D = D_MODEL
    past_len = page_table.shape[1] * PAGE_SIZE
    win_len = cache_nsa_win.shape[2]
    Bp, Tp = x_prompt.shape[:2]
    Bs, Ts = x_sample.shape[:2]
    lb_soft = jax.nn.softmax(hgrn_lb_raw.astype(F32), axis=0)
    lower_bounds = jnp.cumsum(lb_soft, axis=0) - lb_soft[0]

    R = _round_up(Bp + Bs, 8)
    c_all = jnp.zeros((R, D), F32).at[:Bp].set(c_prompt).at[Bp:Bp + Bs].set(c_sample)
    mods = ada_modulation(c_all, ada_w, ada_b)

    def group_mods(l, lo, n, per_token_rows):
        m = mods[l, lo:lo + n].reshape(n, 6, D)
        sh1, sc1, g1, sh2, sc2, g2 = [m[:, j] for j in range(6)]
        s1 = norm1_g[l][None] * (1.0 + sc1)
        s2 = norm2_g[l][None] * (1.0 + sc2)
        vecs = [s1, sh1, g1, s2, sh2, g2]
        if per_token_rows:
            return [jnp.repeat(v, per_token_rows, axis=0)[None] for v in vecs]
        return [v[:, None, :] for v in vecs]

    def pad_cols(w, n):
        return jnp.pad(w, ((0, 0), (0, n - w.shape[1]))).astype(BF16)

    xp = x_prompt
    xs = x_sample.reshape(1, Bs * Ts, D)
    cos_p, sin_p = rope_tables(jnp.arange(Tp))
    outs_p = dict(rows=[], win=[], st=[], kv=[], idx=[])
    outs_s = dict(rows=[], win=[], st=[], kv=[], idx=[])
    for l in range(DEPTH):
        i = l // 2
        mp = group_mods(l, 0, Bp, 0)
        msm = group_mods(l, Bp, Bs, Ts)
        if l % 2 == 0:
            w1, w3, w2 = ffn_w1[i].astype(BF16), ffn_w3[i].astype(BF16), ffn_w2[i].astype(BF16)
            mixed, w_outs, rows, win, s_new = even_mixer_prompt(
                xp, mp[0], mp[1], cos_p, sin_p, even_w_in[i], even_w_out[i], nsa_pe_k[i], nsa_pe_v[i],
                nsa_w_ck[i], nsa_w_cv[i], lower_bounds[i], hgrn_norm_g[i], win_len)
            outs_p['rows'].append(rows)
            outs_p['win'].append(win)
            outs_p['st'].append(s_new)
            xp = out_proj_residual(mixed, w_outs, xp, mp[2])
            xp = ffn_residual(xp, mp[3], mp[4], mp[5], w1, w3, w2)
            mixed, w_outs, rows, win, s_new = even_mixer_sample(
                xs, msm[0], msm[1], past_len, Bs, even_w_in[i], even_w_out[i], nsa_pe_k[i], nsa_pe_v[i],
                nsa_w_ck[i], nsa_w_cv[i], lower_bounds[i], hgrn_norm_g[i],
                cache_nsa, cache_nsa_win, state_hgrn[i], page_table, i)
            outs_s['rows'].append(rows)
            outs_s['win'].append(win)
            outs_s['st'].append(s_new)
            xs = out_proj_residual(mixed, w_outs, xs, msm[2])
            xs = ffn_residual(xs, msm[3], msm[4], msm[5], w1, w3, w2)
        else:
            w_r = pad_cols(router_w[i], LANE)
            w1, w3, w2 = moe_w1[i].astype(BF16), moe_w3[i].astype(BF16), moe_w2[i].astype(BF16)
            o, w_outs, kv, ki = odd_mixer_prompt(xp, mp[0], mp[1], cos_p, sin_p, odd_w_in[i], odd_w_out[i])
            outs_p['kv'].append(kv)
            outs_p['idx'].append(ki)
            xp = out_proj_residual(o, w_outs, xp, mp[2])
            o, w_outs, kv, ki = odd_mixer_sample(xs, msm[0], msm[1], past_len, Bs, odd_w_in[i], odd_w_out[i],
                                                 cache_dsa_kv, cache_dsa_idx, page_table, i)
            outs_s['kv'].append(kv)
            outs_s['idx'].append(ki)
            xs = out_proj_residual(o, w_outs, xs, msm[2])
            hs, lgs = [], []
            for (x, m) in ((xp, mp), (xs, msm)):
                logits, h = norm_proj(x, m[3], m[4], w_r, with_h=True)
                lgs.append(logits[..., :N_EXPERTS] + router_b[i].astype(F32))
                hs.append(h)
            xp, xs = moe_ffn_residual([xp, xs], hs, lgs, [mp[5], msm[5]], w1, w3, w2)
    y_prompt = final_norm(xp, final_g)
    y_sample = final_norm(xs, final_g).reshape(Bs, Ts, D)
    st = lambda od, k: jnp.stack(od[k])
    return (y_prompt, y_sample, st(outs_p, 'rows'), st(outs_s, 'rows'), st(outs_p, 'win'), st(outs_s, 'win'),
            st(outs_p, 'st'), st(outs_s, 'st'), st(outs_p, 'kv'), st(outs_s, 'kv'),
            st(outs_p, 'idx'), st(outs_s, 'idx'))
```

```python
import functools
import math

import numpy as np
import jax
import jax.numpy as jnp
from jax import lax
from jax.experimental import pallas as pl
from jax.experimental.pallas import tpu as pltpu

D_MODEL = 1024
DEPTH = 4
PAGE_SIZE = 128
HEAD_DIM = 64
NSA_HEADS = 8
NSA_KV = 2
NSA_HPG = NSA_HEADS // NSA_KV
CMP_STRIDE = 16
CMP_BLK = 2 * CMP_STRIDE
SEL_BLK = 64
TOP_N = 16
WINDOW = 512
FORCE_SCORE = 1.0e4
HGRN_HEADS = 4
HGRN_DK = 128
HGRN_DV = 128
HGRN_CHUNK = 64
DSA_HEADS = 16
DSA_KV = 4
DSA_HPG = DSA_HEADS // DSA_KV
IDX_HEADS = 8
IDX_DIM = 64
IDX_TOPK = 256
D_FF = 2816
N_EXPERTS = 8
TOP_K = 2
Q_BLK = 128
ROPE_THETA = 10000.0
EPS = 1e-6
NEG = -1.0e30
NSA_QW = NSA_HEADS * HEAD_DIM
NSA_KVW = NSA_KV * HEAD_DIM
NSA_GW = NSA_HEADS * 3
HGRN_KW = HGRN_HEADS * HGRN_DK
HGRN_VW = HGRN_HEADS * HGRN_DV
EVEN_IN = NSA_QW + 6 * NSA_KVW + NSA_GW + 2 * HGRN_KW + 2 * HGRN_VW
EVEN_MIX = NSA_QW + HGRN_VW
DSA_QW = DSA_HEADS * HEAD_DIM
DSA_KVW = DSA_KV * HEAD_DIM
IDX_QW = IDX_HEADS * IDX_DIM
ODD_IN = DSA_QW + 2 * DSA_KVW + IDX_QW + IDX_DIM + IDX_HEADS
ODD_MIX = DSA_QW

LANE = 128
ROW_TILE = 512
FF_CHUNK = 256
MOE_TILE = 256
PROJ_CHUNK = 512
ATT_TQ = 128
KEY_BLK = 256
HGRN_SUB = 16
HGRN_TILE = 256
HGRN_ROWS = 4
PG_STEP = 32
INT_MIN = -2 ** 31
Q_SCALE = HEAD_DIM ** -0.5 * math.log2(math.e)
M_FLOOR = -1.0e25
VMEM_LIMIT = 56 * 1024 * 1024

F32 = jnp.float32
BF16 = jnp.bfloat16


def _round_up(n, m):
    return -(-n // m) * m


def _params(*sem):
    return pltpu.CompilerParams(dimension_semantics=sem, vmem_limit_bytes=VMEM_LIMIT)


def _norm_mod(x, scale, shift):
    ms = jnp.mean(x * x, axis=-1, keepdims=True)
    return x * lax.rsqrt(ms + EPS) * scale + shift


def _mod_spec(mod, tm):
    if mod.shape[1] == 1:
        return pl.BlockSpec((1, 1, mod.shape[2]), lambda b, i: (b, 0, 0))
    return pl.BlockSpec((1, tm, mod.shape[2]), lambda b, i: (b, i, 0))


def _ada_body(c_ref, w_ref, b_ref, o_ref):
    c = c_ref[...]
    cs = (c * jax.nn.sigmoid(c)).astype(BF16)
    o_ref[0] = jnp.dot(cs, w_ref[0].astype(BF16), preferred_element_type=F32) + b_ref[0]


def ada_modulation(c_all, ada_w, ada_b):
    R, D = c_all.shape
    N = ada_w.shape[2]
    tn = 1536
    return pl.pallas_call(
        _ada_body,
        grid=(DEPTH, N // tn),
        in_specs=[pl.BlockSpec((R, D), lambda l, j: (0, 0)),
                  pl.BlockSpec((1, D, tn), lambda l, j: (l, 0, j)),
                  pl.BlockSpec((1, 1, tn), lambda l, j: (l, 0, j))],
        out_specs=pl.BlockSpec((1, R, tn), lambda l, j: (l, 0, j)),
        out_shape=jax.ShapeDtypeStruct((DEPTH, R, N), F32),
        compiler_params=_params("arbitrary", "arbitrary"),
        name="ada_modulation",
    )(c_all, ada_w, ada_b.reshape(DEPTH, 1, N))


def _norm_proj_body(x_ref, sc_ref, sh_ref, w_ref, o_ref, h_ref=None):
    h = _norm_mod(x_ref[0], sc_ref[0], sh_ref[0]).astype(BF16)
    o_ref[0] = jnp.dot(h, w_ref[...], preferred_element_type=F32)
    if h_ref is not None:
        h_ref[0] = h


def norm_proj(x, scale, shift, w, with_h=False):
    B, T, D = x.shape
    N = w.shape[1]
    tm = min(T, ROW_TILE)
    out_shape = [jax.ShapeDtypeStruct((B, T, N), F32)]
    out_specs = [pl.BlockSpec((1, tm, N), lambda b, i: (b, i, 0))]
    if with_h:
        out_shape.append(jax.ShapeDtypeStruct((B, T, D), BF16))
        out_specs.append(pl.BlockSpec((1, tm, D), lambda b, i: (b, i, 0)))
    res = pl.pallas_call(
        _norm_proj_body,
        grid=(B, T // tm),
        in_specs=[pl.BlockSpec((1, tm, D), lambda b, i: (b, i, 0)),
                  _mod_spec(scale, tm), _mod_spec(shift, tm),
                  pl.BlockSpec((D, N), lambda b, i: (0, 0))],
        out_specs=out_specs,
        out_shape=out_shape,
        compiler_params=_params("parallel", "parallel"),
        name="norm_proj",
    )(x, scale, shift, w)
    return res if with_h else res[0]


def _rope_chunk(y, cos, sin):
    lane = lax.broadcasted_iota(jnp.int32, y.shape, 1)
    swapped = jnp.where(lane % HEAD_DIM < HEAD_DIM // 2,
                        pltpu.roll(y, LANE - HEAD_DIM // 2, 1), pltpu.roll(y, HEAD_DIM // 2, 1))
    return y * cos + swapped * sin


def _proj_seg_body(x_ref, sc_ref, sh_ref, cos_ref, sin_ref, w_ref, *o_refs, segs):
    h = _norm_mod(x_ref[0], sc_ref[0], sh_ref[0]).astype(BF16)
    cos = cos_ref[...]
    sin = sin_ref[...]
    t_refs = iter(o_refs[len(segs):])
    c0 = 0
    for o_ref, (width, rope_flags, dtype, channel_major) in zip(o_refs, segs):
        t_ref = next(t_refs) if channel_major else None
        for j0 in range(0, width, PROJ_CHUNK):
            wd = min(PROJ_CHUNK, width - j0)
            y = jnp.dot(h, w_ref[:, c0 + j0:c0 + j0 + wd], preferred_element_type=F32)
            for k in range(wd // LANE):
                cols = slice(j0 + k * LANE, j0 + (k + 1) * LANE)
                yk = y[:, k * LANE:(k + 1) * LANE]
                if rope_flags[(j0 + k * LANE) // LANE]:
                    yk = _rope_chunk(yk, cos, sin)
                o_ref[0, :, cols] = yk.astype(dtype)
                if t_ref is not None:
                    t_ref[0, cols, :] = yk.T
        c0 += width


def proj_segments(x, scale, shift, cos_t, sin_t, w, segs):
    B, T, D = x.shape
    tm = min(T, ROW_TILE)
    N = w.shape[1]
    cm = [s for s in segs if s[3]]
    return pl.pallas_call(
        functools.partial(_proj_seg_body, segs=segs),
        grid=(B, T // tm),
        in_specs=[pl.BlockSpec((1, tm, D), lambda b, i: (b, i, 0)),
                  _mod_spec(scale, tm), _mod_spec(shift, tm),
                  pl.BlockSpec((tm, LANE), lambda b, i: (i, 0)),
                  pl.BlockSpec((tm, LANE), lambda b, i: (i, 0)),
                  pl.BlockSpec((D, N), lambda b, i: (0, 0))],
        out_specs=[pl.BlockSpec((1, tm, s[0]), lambda b, i: (b, i, 0)) for s in segs]
        + [pl.BlockSpec((1, s[0], tm), lambda b, i: (b, 0, i)) for s in cm],
        out_shape=[jax.ShapeDtypeStruct((B, T, s[0]), s[2]) for s in segs]
        + [jax.ShapeDtypeStruct((B, s[0], T), F32) for s in cm],
        compiler_params=_params("parallel", "parallel"),
        name="proj_segments",
    )(x, scale, shift, cos_t, sin_t, w)


def rope_tables(pos):
    half = HEAD_DIM // 2
    inv = ROPE_THETA ** (-jnp.arange(half, dtype=F32) / half)
    ang = pos.astype(F32)[:, None] * inv[None, :]
    cos, sin = jnp.cos(ang), jnp.sin(ang)
    return jnp.tile(cos, (1, 4)), jnp.tile(jnp.concatenate([-sin, sin], axis=1), (1, 2))


def _out_proj_body(*refs, n_in):
    a_refs, w_refs = refs[:n_in], refs[n_in:2 * n_in]
    x_ref, g_ref, o_ref = refs[2 * n_in:]
    y = None
    for a_ref, w_ref in zip(a_refs, w_refs):
        part = jnp.dot(a_ref[0].astype(BF16), w_ref[...], preferred_element_type=F32)
        y = part if y is None else y + part
    o_ref[0] = x_ref[0] + g_ref[0] * y


def out_proj_residual(a_list, w_list, x, gate):
    B, T, D = x.shape
    tm = min(T, ROW_TILE)
    n_in = len(a_list)
    return pl.pallas_call(
        functools.partial(_out_proj_body, n_in=n_in),
        grid=(B, T // tm),
        in_specs=[pl.BlockSpec((1, tm, a.shape[2]), lambda b, i: (b, i, 0)) for a in a_list]
        + [pl.BlockSpec(w.shape, lambda b, i: (0, 0)) for w in w_list]
        + [pl.BlockSpec((1, tm, D), lambda b, i: (b, i, 0)), _mod_spec(gate, tm)],
        out_specs=pl.BlockSpec((1, tm, D), lambda b, i: (b, i, 0)),
        out_shape=jax.ShapeDtypeStruct((B, T, D), F32),
        compiler_params=_params("parallel", "parallel"),
        name="out_proj_residual",
    )(*a_list, *w_list, x, gate)


def _swiglu_acc(h, w1_ref, w3_ref, w2_ref, acc_ref):
    for c in range(D_FF // FF_CHUNK):
        cols = slice(c * FF_CHUNK, (c + 1) * FF_CHUNK)
        u = jnp.dot(h, w1_ref[:, cols], preferred_element_type=F32)
        v = jnp.dot(h, w3_ref[:, cols], preferred_element_type=F32)
        a = (u * jax.nn.sigmoid(u) * v).astype(BF16)
        part = jnp.dot(a, w2_ref[cols, :], preferred_element_type=F32)
        if c == 0:
            acc_ref[...] = part
        else:
            acc_ref[...] += part


def _ffn_body(x_ref, sc_ref, sh_ref, g_ref, w1_ref, w3_ref, w2_ref, o_ref, acc_ref):
    x = x_ref[0]
    h = _norm_mod(x, sc_ref[0], sh_ref[0]).astype(BF16)
    _swiglu_acc(h, w1_ref, w3_ref, w2_ref, acc_ref)
    o_ref[0] = x + g_ref[0] * acc_ref[...]


def ffn_residual(x, scale, shift, gate, w1, w3, w2):
    B, T, D = x.shape
    tm = min(T, ROW_TILE)
    wspec = lambda shape: pl.BlockSpec(shape, lambda b, i: (0, 0))
    return pl.pallas_call(
        _ffn_body,
        grid=(B, T // tm),
        in_specs=[pl.BlockSpec((1, tm, D), lambda b, i: (b, i, 0)),
                  _mod_spec(scale, tm), _mod_spec(shift, tm), _mod_spec(gate, tm),
                  wspec((D, D_FF)), wspec((D, D_FF)), wspec((D_FF, D))],
        out_specs=pl.BlockSpec((1, tm, D), lambda b, i: (b, i, 0)),
        out_shape=jax.ShapeDtypeStruct((B, T, D), F32),
        scratch_shapes=[pltpu.VMEM((tm, D), F32)],
        compiler_params=_params("parallel", "parallel"),
        name="ffn_residual",
    )(x, scale, shift, gate, w1, w3, w2)


def _moe_body(te_ref, nt_ref, h_ref, w1_ref, w3_ref, w2_ref, o_ref, acc_ref):
    i = pl.program_id(0)

    @pl.when(i < nt_ref[0])
    def _():
        _swiglu_acc(h_ref[...], w1_ref.at[0], w3_ref.at[0], w2_ref.at[0], acc_ref)
        o_ref[...] = acc_ref[...]

    @pl.when(i >= nt_ref[0])
    def _():
        o_ref[...] = jnp.zeros_like(o_ref)


def moe_grouped_swiglu(h_sorted, tile_expert, n_tiles_used, w1, w3, w2):
    S, D = h_sorted.shape
    tm = MOE_TILE
    n_tiles = S // tm
    wspec = lambda shape: pl.BlockSpec((1,) + shape, lambda i, te, nt: (te[i], 0, 0))
    grid_spec = pltpu.PrefetchScalarGridSpec(
        num_scalar_prefetch=2,
        grid=(n_tiles,),
        in_specs=[pl.BlockSpec((tm, D), lambda i, te, nt: (i, 0)),
                  wspec((D, D_FF)), wspec((D, D_FF)), wspec((D_FF, D))],
        out_specs=pl.BlockSpec((tm, D), lambda i, te, nt: (i, 0)),
        scratch_shapes=[pltpu.VMEM((tm, D), F32)],
    )
    return pl.pallas_call(
        _moe_body,
        grid_spec=grid_spec,
        out_shape=jax.ShapeDtypeStruct((S, D), F32),
        compiler_params=_params("arbitrary"),
        name="moe_grouped_swiglu",
    )(tile_expert, n_tiles_used, h_sorted, w1, w3, w2)


def _combine_body(x_ref, g_ref, w_ref, a_ref, b_ref, o_ref):
    w = w_ref[0]
    o_ref[0] = x_ref[0] + g_ref[0] * (w[:, 0:1] * a_ref[0] + w[:, 1:2] * b_ref[0])


def moe_combine_residual(x, gate, w, ya, yb):
    B, T, D = x.shape
    tm = min(T, ROW_TILE)
    tile = pl.BlockSpec((1, tm, D), lambda b, i: (b, i, 0))
    return pl.pallas_call(
        _combine_body,
        grid=(B, T // tm),
        in_specs=[tile, _mod_spec(gate, tm), pl.BlockSpec((1, tm, TOP_K), lambda b, i: (b, i, 0)), tile, tile],
        out_specs=tile,
        out_shape=jax.ShapeDtypeStruct((B, T, D), F32),
        compiler_params=_params("parallel", "parallel"),
        name="moe_combine_residual",
    )(x, gate, w, ya, yb)


def _final_norm_body(x_ref, g_ref, o_ref):
    x = x_ref[0]
    ms = jnp.mean(x * x, axis=-1, keepdims=True)
    o_ref[0] = x * lax.rsqrt(ms + EPS) * g_ref[...]


def final_norm(x, g):
    B, T, D = x.shape
    tm = min(T, ROW_TILE)
    return pl.pallas_call(
        _final_norm_body,
        grid=(B, T // tm),
        in_specs=[pl.BlockSpec((1, tm, D), lambda b, i: (b, i, 0)),
                  pl.BlockSpec((1, D), lambda b, i: (0, 0))],
        out_specs=pl.BlockSpec((1, tm, D), lambda b, i: (b, i, 0)),
        out_shape=jax.ShapeDtypeStruct((B, T, D), F32),
        compiler_params=_params("parallel", "parallel"),
        name="final_norm",
    )(x, g.reshape(1, D))


def _split3(x):
    hi = x.astype(BF16)
    r1 = x - hi.astype(F32)
    mid = r1.astype(BF16)
    lo = (r1 - mid.astype(F32)).astype(BF16)
    return hi, mid, lo


def _dot01(m01, x):
    hi, mid, lo = _split3(x)
    d = lambda p: jnp.dot(m01, p, preferred_element_type=F32)
    return d(hi) + d(mid) + d(lo)


def _hgrn_body(hq_ref, hf_ref, hi_ref, hg_ref, lb_ref, ng_ref, s0_ref, o_ref, sn_ref, st_ref, *, sub, n_sub):
    i = pl.program_id(1)
    nb = hq_ref.shape[0]

    @pl.when(i == 0)
    def _():
        for bb in range(nb):
            for h in range(HGRN_HEADS):
                st_ref[bb, h] = s0_ref[bb, h].T

    lb = lb_ref[...]
    ng = ng_ref[...]
    row = lax.broadcasted_iota(jnp.int32, (sub, sub), 0)
    col = lax.broadcasted_iota(jnp.int32, (sub, sub), 1)
    tril = (row >= col).astype(BF16)
    trow = lax.broadcasted_iota(jnp.int32, (sub, HGRN_DK), 0)

    def chunk(c, carry):
        for bb in range(nb):
            chunk_one(c, bb)
        return carry

    def chunk_one(c, bb):
        rows = pl.ds(pl.multiple_of(c * sub, sub), sub)
        hq = hq_ref[bb, rows, :]
        hf = hf_ref[bb, rows, :]
        hv = hi_ref[bb, rows, :]
        hg = hg_ref[bb, rows, :]
        f = lb + (1.0 - lb) * jax.nn.sigmoid(hf)
        logf = jnp.log(f)
        kk = 1.0 - f
        qq = hq * jax.nn.sigmoid(hq)
        b = _dot01(tril, logf)
        bl = b[sub - 1:sub, :]
        qe = qq * jnp.exp(b)
        ke = kk * jnp.exp(bl - b)
        ebl = jnp.exp(bl)
        outs = []
        for h in range(HGRN_HEADS):
            cs = slice(h * HGRN_DK, (h + 1) * HGRN_DK)
            st = st_ref[bb, h]
            o = lax.dot_general(qe[:, cs].astype(BF16), st.astype(BF16), (((1,), (1,)), ((), ())),
                                preferred_element_type=F32)
            bh, qh, kh, vh = b[:, cs], qq[:, cs], kk[:, cs], hv[:, cs]
            for s in range(sub):
                e = jnp.exp(jnp.minimum(bh - bh[s:s + 1, :], 0.0))
                w = jnp.where(trow >= s, e * qh * kh[s:s + 1, :], 0.0)
                o = o + jnp.sum(w, axis=-1, keepdims=True) * vh[s:s + 1, :]
            upd = lax.dot_general(vh.astype(BF16), ke[:, cs].astype(BF16), (((0,), (0,)), ((), ())),
                                  preferred_element_type=F32)
            st_ref[bb, h] = st * ebl[:, cs] + upd
            ms = jnp.mean(o * o, axis=-1, keepdims=True)
            outs.append(o * lax.rsqrt(ms + EPS))
        o_ref[bb, rows, :] = jnp.concatenate(outs, axis=-1) * ng * (hg * jax.nn.sigmoid(hg))

    lax.fori_loop(0, n_sub, chunk, 0)

    @pl.when(i == pl.num_programs(1) - 1)
    def _():
        for bb in range(nb):
            for h in range(HGRN_HEADS):
                sn_ref[bb, h] = st_ref[bb, h].T


def hgrn_mixer(hx, lb, norm_g, s0):
    B, T, _ = hx.shape
    tc = min(T, HGRN_TILE)
    sub = math.gcd(T, HGRN_SUB)
    nb = math.gcd(B, HGRN_ROWS)
    spec = lambda j: pl.BlockSpec((nb, tc, HGRN_KW), lambda b, i, j=j: (b, i, j))
    vec = pl.BlockSpec((1, HGRN_KW), lambda b, i: (0, 0))
    state = pl.BlockSpec((nb, HGRN_HEADS, HGRN_DK, HGRN_DV), lambda b, i: (b, 0, 0, 0))
    return pl.pallas_call(
        functools.partial(_hgrn_body, sub=sub, n_sub=tc // sub),
        grid=(B // nb, T // tc),
        in_specs=[spec(0), spec(1), spec(2), spec(3), vec, vec, state],
        out_specs=[pl.BlockSpec((nb, tc, HGRN_VW), lambda b, i: (b, i, 0)), state],
        out_shape=[jax.ShapeDtypeStruct((B, T, HGRN_VW), F32),
                   jax.ShapeDtypeStruct((B, HGRN_HEADS, HGRN_DK, HGRN_DV), F32)],
        scratch_shapes=[pltpu.VMEM((nb, HGRN_HEADS, HGRN_DV, HGRN_DK), F32)],
        compiler_params=_params("parallel", "arbitrary"),
        name="hgrn_mixer",
    )(hx, hx, hx, hx, lb.reshape(1, HGRN_KW), jnp.tile(norm_g, HGRN_HEADS).reshape(1, HGRN_VW), s0)


def _nt(a, b):
    return lax.dot_general(a, b, (((1,), (1,)), ((), ())), preferred_element_type=F32)


def _flash_step(kblk, vT, q_heads, mask, m, l, acc_ref, batched):
    tq = q_heads[0].shape[0]
    if batched:
        acc = acc_ref[...]
        scores = [_nt(kblk, qh) for qh in q_heads]
    m_out, l_out, acc_out = [], [], []
    for h, qh in enumerate(q_heads):
        cs = slice(h * tq, (h + 1) * tq)
        s = jnp.where(mask, scores[h] if batched else _nt(kblk, qh), NEG)
        m_new = jnp.maximum(m[:, cs], jnp.max(s, axis=0, keepdims=True))
        alpha = jnp.exp2(m[:, cs] - m_new)
        e = jnp.exp2(s - m_new)
        l_out.append(alpha * l[:, cs] + jnp.sum(e, axis=0, keepdims=True))
        pv = jnp.dot(vT, e.astype(BF16), preferred_element_type=F32)
        if batched:
            acc_out.append(alpha * acc[:, cs] + pv)
        else:
            acc_ref[:, cs] = alpha * acc_ref[:, cs] + pv
        m_out.append(m_new)
    if batched:
        acc_ref[...] = jnp.concatenate(acc_out, axis=1)
    return jnp.concatenate(m_out, axis=1), jnp.concatenate(l_out, axis=1)


def _head_slabs(qp_ref, first_head, n):
    return [qp_ref[0, :, (first_head + h) * LANE:(first_head + h + 1) * LANE] for h in range(n)]


def _stack_heads(qp_ref, first_head, n):
    return jnp.concatenate(_head_slabs(qp_ref, first_head, n), axis=0)


def _store_heads(o_ref, oT, first_head, n, slot):
    tq = oT.shape[1] // n
    lane = lax.broadcasted_iota(jnp.int32, (tq, LANE), 1)
    valid = (lane >= HEAD_DIM * slot) & (lane < HEAD_DIM * (slot + 1))
    for h in range(n):
        blk = oT[:, h * tq:(h + 1) * tq].T
        o_ref[0, :, (first_head + h) * LANE:(first_head + h + 1) * LANE] = jnp.where(valid, blk, 0.0).astype(o_ref.dtype)


def _nsa_body(qp_ref, rows_ref, win_ref, gl_ref, pek_ref, pev_ref, wck_ref, wcv_ref, o_ref,
              kcmp_ref, vcmpT_ref, ks_ref, vsT_ref, kw_ref, vwT_ref, stage_ref, shift_ref, sel_ref, acc_ref, *, T):
    i = pl.program_id(1)
    tq = ATT_TQ
    nq = NSA_HPG * tq
    n_half = T // CMP_STRIDE
    n_sel = T // SEL_BLK
    k_sel = min(TOP_N, n_sel)
    per_kb = KEY_BLK // SEL_BLK

    @pl.when(i == 0)
    def _prepare():
        nrow = lax.broadcasted_iota(jnp.int32, (n_half, LANE), 0)
        for slab, pe_ref, w_ref in ((0, pek_ref, wck_ref), (1, pev_ref, wcv_ref)):
            def stage_blk(kb, carry, slab=slab):
                rs = pl.ds(pl.multiple_of(kb * KEY_BLK, KEY_BLK), KEY_BLK)
                stage_ref[rs, :] = rows_ref[0, rs, slab * LANE:(slab + 1) * LANE]
                return carry

            lax.fori_loop(0, T // KEY_BLK, stage_blk, 0)
            first = jnp.zeros((n_half, LANE), F32)
            second = jnp.zeros((n_half, LANE), F32)
            for r in range(CMP_STRIDE):
                y = stage_ref[pl.ds(r, n_half, stride=CMP_STRIDE), :]
                first = first + jnp.dot((y + pe_ref[r:r + 1, :]).astype(BF16), w_ref[r],
                                        preferred_element_type=F32)
                second = second + jnp.dot((y + pe_ref[r + CMP_STRIDE:r + CMP_STRIDE + 1, :]).astype(BF16),
                                          w_ref[r + CMP_STRIDE], preferred_element_type=F32)
            shift_ref[0:n_half, :] = second
            shift_ref[n_half:n_half + 8, :] = jnp.zeros((8, LANE), F32)
            c = jnp.where(nrow < n_half - 1, first + shift_ref[1:n_half + 1, :], 0.0)
            if slab == 0:
                kcmp_ref[...] = c.astype(BF16)
            else:
                for j in range(n_half // LANE):
                    vcmpT_ref[:, j * LANE:(j + 1) * LANE] = c[j * LANE:(j + 1) * LANE, :].T.astype(BF16)

        def copy_blk(kb, carry):
            for half in range(KEY_BLK // LANE):
                rs = pl.ds(pl.multiple_of(kb * KEY_BLK + half * LANE, LANE), LANE)
                hs = slice(half * LANE, (half + 1) * LANE)
                ks_ref[kb, hs, :] = rows_ref[0, rs, 2 * LANE:3 * LANE].astype(BF16)
                vsT_ref[kb, :, hs] = rows_ref[0, rs, 3 * LANE:4 * LANE].T.astype(BF16)
                kw_ref[kb, hs, :] = win_ref[0, rs, 0:LANE].astype(BF16)
                vwT_ref[kb, :, hs] = win_ref[0, rs, LANE:2 * LANE].T.astype(BF16)
            return carry

        lax.fori_loop(0, T // KEY_BLK, copy_blk, 0)

    t0 = i * tq
    lane_t = t0 + (lax.broadcasted_iota(jnp.int32, (1, nq), 1) & (tq - 1))
    t_row = lane_t[:, 0:tq]
    gT = jax.nn.sigmoid(gl_ref[0].T)
    kio = lax.broadcasted_iota(jnp.int32, (KEY_BLK, tq), 0)
    nio = lax.broadcasted_iota(jnp.int32, (n_half, nq), 0)
    jcol = lax.broadcasted_iota(jnp.int32, (n_sel, tq), 0)
    pj = lax.broadcasted_iota(jnp.int32, (n_sel, n_half), 0)
    pi = lax.broadcasted_iota(jnp.int32, (n_sel, n_half), 1)
    pool = ((((pi + 1) >> 2) == pj) & (pi < n_half - 1)).astype(BF16)
    n_kb = (t0 + tq + KEY_BLK - 1) // KEY_BLK
    n_full = t0 // KEY_BLK
    m0 = jnp.full((1, nq), M_FLOOR, F32)
    l0 = jnp.zeros((1, nq), F32)

    for g in range(NSA_KV):
        q_heads = _head_slabs(qp_ref, NSA_HPG * g, NSA_HPG)
        qs = jnp.concatenate(q_heads, axis=0)
        cmask = (CMP_STRIDE * nio + CMP_BLK <= lane_t + 1) & (nio < n_half - 1)
        s = jnp.where(cmask, _nt(kcmp_ref[...], qs), NEG)
        e = jnp.where(cmask, jnp.exp2(s - jnp.max(s, axis=0, keepdims=True)), 0.0)
        p = e * (1.0 / jnp.maximum(jnp.sum(e, axis=0, keepdims=True), 1e-30))
        ocT = jnp.dot(vcmpT_ref[...], p.astype(BF16), preferred_element_type=F32)
        imp = p[:, 0:tq]
        for h in range(1, NSA_HPG):
            imp = imp + p[:, h * tq:(h + 1) * tq]
        imp_sel = _dot01(pool, imp)
        cur = t_row >> 6
        forced = (jcol == 0) | (jcol == cur) | (jcol == cur - 1)
        visible = jcol <= cur
        score = jnp.where(visible, imp_sel + jnp.where(forced, FORCE_SCORE, 0.0), NEG)
        rank = jnp.zeros((n_sel, tq), jnp.int32)
        for j2 in range(n_sel):
            r = score[j2:j2 + 1, :]
            beats = (r > score) | ((r == score) & (jcol > j2))
            rank = rank + beats.astype(jnp.int32)
        sel_ref[...] = jnp.where((rank < k_sel) & visible, 1.0, 0.0)

        def sel_step(kb, carry, causal):
            rowsel = jnp.concatenate(
                [jnp.broadcast_to(sel_ref[pl.ds(kb * per_kb + k, 1), :], (SEL_BLK, tq)) for k in range(per_kb)],
                axis=0)
            mask = rowsel > 0.5
            if causal:
                mask = mask & (kb * KEY_BLK + kio <= t_row)
            return _flash_step(ks_ref[kb], vsT_ref[kb], q_heads, mask, carry[0], carry[1], acc_ref, True)

        acc_ref[...] = jnp.zeros_like(acc_ref)
        ml = lax.fori_loop(0, n_full, functools.partial(sel_step, causal=False), (m0, l0))
        _, l = lax.fori_loop(n_full, n_kb, functools.partial(sel_step, causal=True), ml)
        osT = acc_ref[...] * (1.0 / jnp.maximum(l, 1e-30))

        def win_step(kb, carry):
            kpos = kb * KEY_BLK + kio
            mask = (kpos <= t_row) & (kpos > t_row - WINDOW)
            return _flash_step(kw_ref[kb], vwT_ref[kb], q_heads, mask, carry[0], carry[1], acc_ref, True)

        acc_ref[...] = jnp.zeros_like(acc_ref)
        w_lo = jnp.maximum(t0 - (WINDOW - 1), 0) // KEY_BLK
        _, l = lax.fori_loop(w_lo, n_kb, win_step, (m0, l0))
        owT = acc_ref[...] * (1.0 / jnp.maximum(l, 1e-30))

        def gate(c):
            return jnp.concatenate([gT[3 * (NSA_HPG * g + h) + c:3 * (NSA_HPG * g + h) + c + 1, :]
                                    for h in range(NSA_HPG)], axis=1)

        oT = gate(0) * ocT + gate(1) * osT + gate(2) * owT
        _store_heads(o_ref, oT, NSA_HPG * g, NSA_HPG, g)


def _block_diag2(w):
    w3 = w.reshape(CMP_BLK, HEAD_DIM, HEAD_DIM)
    z = jnp.zeros_like(w3)
    return jnp.concatenate([jnp.concatenate([w3, z], axis=2), jnp.concatenate([z, w3], axis=2)], axis=1).astype(BF16)


def nsa_prompt(qp, rows, win, gl, pe_k, pe_v, w_ck, w_cv):
    B, T, _ = rows.shape
    assert T % (CMP_STRIDE * LANE) == 0 and T % KEY_BLK == 0
    tq = ATT_TQ
    nq = NSA_HPG * tq
    n_half = T // CMP_STRIDE
    n_kb = T // KEY_BLK
    const = lambda shape: pl.BlockSpec(shape, lambda b, i: (0,) * len(shape))
    return pl.pallas_call(
        functools.partial(_nsa_body, T=T),
        grid=(B, T // tq),
        in_specs=[pl.BlockSpec((1, tq, NSA_HEADS * LANE), lambda b, i: (b, i, 0)),
                  pl.BlockSpec((1, T, 4 * LANE), lambda b, i: (b, 0, 0)),
                  pl.BlockSpec((1, T, 2 * LANE), lambda b, i: (b, 0, 0)),
                  pl.BlockSpec((1, tq, LANE), lambda b, i: (b, i, 0)),
                  const((CMP_BLK, LANE)), const((CMP_BLK, LANE)),
                  const((CMP_BLK, LANE, LANE)), const((CMP_BLK, LANE, LANE))],
        out_specs=pl.BlockSpec((1, tq, NSA_HEADS * LANE), lambda b, i: (b, i, 0)),
        out_shape=jax.ShapeDtypeStruct((B, T, NSA_HEADS * LANE), BF16),
        scratch_shapes=[pltpu.VMEM((n_half, LANE), BF16), pltpu.VMEM((LANE, n_half), BF16),
                        pltpu.VMEM((n_kb, KEY_BLK, LANE), BF16), pltpu.VMEM((n_kb, LANE, KEY_BLK), BF16),
                        pltpu.VMEM((n_kb, KEY_BLK, LANE), BF16), pltpu.VMEM((n_kb, LANE, KEY_BLK), BF16),
                        pltpu.VMEM((T, LANE), F32),
                        pltpu.VMEM((n_half + 8, LANE), F32), pltpu.VMEM((T // SEL_BLK, tq), F32),
                        pltpu.VMEM((LANE, nq), F32)],
        compiler_params=_params("parallel", "arbitrary"),
        name="nsa_prompt",
    )(qp, rows, win, gl, jnp.tile(pe_k, (1, 2)), jnp.tile(pe_v, (1, 2)), _block_diag2(w_ck), _block_diag2(w_cv))


def _dsa_body(qp_ref, kv_ref, qi_ref, ki_ref, wi_ref, o_ref,
              k_ref, vT_ref, kilo_ref, kihi_ref, key_ref, acc_ref, *, T, n_keep):
    i = pl.program_id(1)
    tq = ATT_TQ
    nq = DSA_HPG * tq

    @pl.when(i == 0)
    def _prepare():
        def copy_blk(kb, carry):
            for half in range(KEY_BLK // LANE):
                rs = pl.ds(pl.multiple_of(kb * KEY_BLK + half * LANE, LANE), LANE)
                hs = slice(half * LANE, (half + 1) * LANE)
                for slab in range(2):
                    k_ref[slab, kb, hs, :] = kv_ref[0, rs, slab * LANE:(slab + 1) * LANE].astype(BF16)
                    vT_ref[slab, kb, :, hs] = kv_ref[0, rs, (2 + slab) * LANE:(3 + slab) * LANE].T.astype(BF16)
                kix = ki_ref[0, rs, :]
                kilo_ref[kb, hs, :] = kix.astype(BF16)
                kihi_ref[kb, hs, :] = pltpu.roll(kix, IDX_DIM, 1).astype(BF16)
            return carry

        lax.fori_loop(0, T // KEY_BLK, copy_blk, 0)

    t0 = i * tq
    n_kb = (t0 + tq + KEY_BLK - 1) // KEY_BLK
    t_row = t0 + lax.broadcasted_iota(jnp.int32, (1, tq), 1)
    kio_q = lax.broadcasted_iota(jnp.int32, (KEY_BLK, tq), 0)

    wT = wi_ref[0].T * (IDX_HEADS ** -0.5)
    qi = _stack_heads(qi_ref, 0, IDX_HEADS // 2)

    def idx_step(kb, carry):
        s_lo = _nt(kilo_ref[kb], qi)
        s_hi = _nt(kihi_ref[kb], qi)
        sc = jnp.zeros((KEY_BLK, tq), F32)
        for p in range(IDX_HEADS // 2):
            cs = slice(p * tq, (p + 1) * tq)
            sc = sc + jnp.maximum(s_lo[:, cs], 0.0) * wT[2 * p:2 * p + 1, :]
            sc = sc + jnp.maximum(s_hi[:, cs], 0.0) * wT[2 * p + 1:2 * p + 2, :]
        sc = jnp.where(kb * KEY_BLK + kio_q <= t_row, sc, NEG)
        bits = lax.bitcast_convert_type(sc, jnp.int32)
        key_ref[kb] = jnp.where(bits < 0, bits ^ 0x7FFFFFFF, bits)
        return carry

    lax.fori_loop(0, n_kb, idx_step, 0)

    def bit_step(it, theta):
        cand = theta + lax.shift_left(jnp.int32(1), 31 - it)

        def cnt_step(kb, c):
            return c + jnp.sum((key_ref[kb] >= cand).astype(jnp.int32), axis=0, keepdims=True)

        cnt = lax.fori_loop(0, n_kb, cnt_step, jnp.zeros((1, tq), jnp.int32))
        return jnp.where(cnt >= n_keep, cand, theta)

    theta = lax.fori_loop(0, 32, bit_step, jnp.full((1, tq), INT_MIN, jnp.int32))

    def count(pred):
        step = lambda kb, c: c + jnp.sum(pred(key_ref[kb]).astype(jnp.int32), axis=0, keepdims=True)
        return lax.fori_loop(0, n_kb, step, jnp.zeros((1, tq), jnp.int32))

    n_ge = count(lambda k: k >= theta)

    @pl.when(jnp.max(n_ge) > n_keep)
    def _break_ties():
        need = (n_keep - count(lambda k: k > theta)).astype(F32)
        before = (lax.broadcasted_iota(jnp.int32, (KEY_BLK, KEY_BLK), 1)
                  < lax.broadcasted_iota(jnp.int32, (KEY_BLK, KEY_BLK), 0)).astype(BF16)

        def fix_step(kb, seen):
            keys = key_ref[kb]
            eq = keys == theta
            eqf = jnp.where(eq, 1.0, 0.0)
            rank = seen + jnp.dot(before, eqf.astype(BF16), preferred_element_type=F32)
            key_ref[kb] = jnp.where(eq & (rank >= need), INT_MIN, keys)
            return seen + jnp.sum(eqf, axis=0, keepdims=True)

        lax.fori_loop(0, n_kb, fix_step, jnp.zeros((1, tq), F32))

    q_heads = [_head_slabs(qp_ref, DSA_HPG * g, DSA_HPG) for g in range(DSA_KV)]
    m0 = jnp.full((1, nq), M_FLOOR, F32)
    l0 = jnp.zeros((1, nq), F32)
    acc_ref[...] = jnp.zeros_like(acc_ref)

    def att_step(kb, carry, causal):
        mask = key_ref[kb] >= theta
        if causal:
            mask = mask & (kb * KEY_BLK + kio_q <= t_row)
        out = []
        for g in range(DSA_KV):
            m, l = _flash_step(k_ref[g // 2, kb], vT_ref[g // 2, kb], q_heads[g], mask,
                               carry[2 * g], carry[2 * g + 1], acc_ref.at[g], False)
            out += [m, l]
        return tuple(out)

    mid = lax.fori_loop(0, t0 // KEY_BLK, functools.partial(att_step, causal=False), (m0, l0) * DSA_KV)
    fin = lax.fori_loop(t0 // KEY_BLK, n_kb, functools.partial(att_step, causal=True), mid)
    for g in range(DSA_KV):
        oT = acc_ref[g] * (1.0 / jnp.maximum(fin[2 * g + 1], 1e-30))
        _store_heads(o_ref, oT, DSA_HPG * g, DSA_HPG, g % 2)


def dsa_prompt(qp, kv, qi, ki, wi):
    B, T, _ = kv.shape
    assert T % KEY_BLK == 0
    tq = ATT_TQ
    nq = DSA_HPG * tq
    n_kb = T // KEY_BLK
    n_keep = min(IDX_TOPK, T // 4)
    tile = lambda w: pl.BlockSpec((1, tq, w), lambda b, i: (b, i, 0))
    whole = lambda w: pl.BlockSpec((1, T, w), lambda b, i: (b, 0, 0))
    return pl.pallas_call(
        functools.partial(_dsa_body, T=T, n_keep=n_keep),
        grid=(B, T // tq),
        in_specs=[tile(DSA_HEADS * LANE), whole(4 * LANE), tile(IDX_QW), whole(LANE), tile(LANE)],
        out_specs=tile(DSA_HEADS * LANE),
        out_shape=jax.ShapeDtypeStruct((B, T, DSA_HEADS * LANE), BF16),
        scratch_shapes=[pltpu.VMEM((2, n_kb, KEY_BLK, LANE), BF16), pltpu.VMEM((2, n_kb, LANE, KEY_BLK), BF16),
                        pltpu.VMEM((n_kb, KEY_BLK, LANE), BF16), pltpu.VMEM((n_kb, KEY_BLK, LANE), BF16),
                        pltpu.VMEM((n_kb, KEY_BLK, tq), jnp.int32), pltpu.VMEM((DSA_KV, LANE, nq), F32)],
        compiler_params=_params("parallel", "arbitrary"),
        name="dsa_prompt",
    )(qp, kv, qi, ki, wi)


def _pad_head_cols(w, slots):
    D = w.shape[0]
    H = len(slots)
    w3 = w.reshape(D, H, HEAD_DIM)
    z = jnp.zeros_like(w3)
    s = jnp.asarray(slots)[None, :, None]
    return jnp.concatenate([jnp.where(s == 0, w3, z), jnp.where(s == 1, w3, z)], axis=2).reshape(D, H * LANE)


def _pad_cols(w, n):
    return jnp.pad(w, ((0, 0), (0, n - w.shape[1])))


def _dot01_r(x, m01):
    hi, mid, lo = _split3(x)
    d = lambda p: jnp.dot(p, m01, preferred_element_type=F32)
    return d(hi) + d(mid) + d(lo)


def _softmax_rows(s, mask):
    s = jnp.where(mask, s, NEG)
    e = jnp.where(mask, jnp.exp2(s - jnp.max(s, axis=1, keepdims=True)), 0.0)
    return e * (1.0 / jnp.maximum(jnp.sum(e, axis=1, keepdims=True), 1e-30))


def _flash_rows(s, mask, m_ref, l_ref):
    s = jnp.where(mask, s, NEG)
    m = m_ref[...]
    m_new = jnp.maximum(m, jnp.max(s, axis=1, keepdims=True))
    alpha = jnp.exp2(m - m_new)
    e = jnp.where(mask, jnp.exp2(s - m_new), 0.0)
    l_ref[...] = alpha * l_ref[...] + jnp.sum(e, axis=1, keepdims=True)
    m_ref[...] = m_new
    return alpha, e


def _sort_key(x):
    bits = lax.bitcast_convert_type(x, jnp.int32)
    return jnp.where(bits < 0, bits ^ 0x7FFFFFFF, bits)


def _page_spec(layer, k, chans, blk):
    return pl.BlockSpec((None, 1, chans, PAGE_SIZE), lambda b, s, pt: (layer, pt[b, s * PG_STEP + k], blk, 0))


def _channel_major(cache):
    layers, n, rows = cache.shape[:3]
    return jnp.moveaxis(cache.reshape(layers, n, rows, -1), 2, 3)


def _per_request(shape):
    return pl.BlockSpec((1,) + shape, lambda b, s, pt: (b,) + (0,) * len(shape))


def _nsa_cmp_body(pt_ref, *refs):
    pages = refs[:PG_STEP]
    pek_ref, pev_ref, wck_ref, wcv_ref, o_ref, stage_ref = refs[PG_STEP:]
    n_half = PG_STEP * PAGE_SIZE // CMP_STRIDE
    for slab, pe_ref, w_ref in ((0, pek_ref, wck_ref), (1, pev_ref, wcv_ref)):
        for pg in range(PG_STEP):
            stage_ref[pg * PAGE_SIZE:(pg + 1) * PAGE_SIZE, :] = pages[pg][0, slab * LANE:(slab + 1) * LANE, :].T
        first = jnp.zeros((n_half, LANE), F32)
        second = jnp.zeros((n_half, LANE), F32)
        for r in range(CMP_STRIDE):
            y = stage_ref[pl.ds(r, n_half, stride=CMP_STRIDE), :]
            first = first + jnp.dot((y + pe_ref[r:r + 1, :]).astype(BF16), w_ref[r], preferred_element_type=F32)
            second = second + jnp.dot((y + pe_ref[r + CMP_STRIDE:r + CMP_STRIDE + 1, :]).astype(BF16),
                                      w_ref[r + CMP_STRIDE], preferred_element_type=F32)
        o_ref[0, :, (2 * slab) * LANE:(2 * slab + 1) * LANE] = first
        o_ref[0, :, (2 * slab + 1) * LANE:(2 * slab + 2) * LANE] = second


def nsa_sample_compress(cache, layer, page_table, pe_k, pe_v, w_ck, w_cv):
    B, n_pages = page_table.shape
    assert n_pages % PG_STEP == 0
    n_half_step = PG_STEP * PAGE_SIZE // CMP_STRIDE
    const = lambda shape: pl.BlockSpec(shape, lambda b, s, pt: (0,) * len(shape))
    grid_spec = pltpu.PrefetchScalarGridSpec(
        num_scalar_prefetch=1,
        grid=(B, n_pages // PG_STEP),
        in_specs=[_page_spec(layer, k, 2 * LANE, 0) for k in range(PG_STEP)]
        + [const((CMP_BLK, LANE)), const((CMP_BLK, LANE)), const((CMP_BLK, LANE, LANE)), const((CMP_BLK, LANE, LANE))],
        out_specs=pl.BlockSpec((1, n_half_step, 4 * LANE), lambda b, s, pt: (b, s, 0)),
        scratch_shapes=[pltpu.VMEM((PG_STEP * PAGE_SIZE, LANE), F32)],
    )
    return pl.pallas_call(
        _nsa_cmp_body,
        grid_spec=grid_spec,
        out_shape=jax.ShapeDtypeStruct((B, n_pages * PAGE_SIZE // CMP_STRIDE, 4 * LANE), F32),
        compiler_params=_params("parallel", "arbitrary"),
        name="nsa_sample_compress",
    )(page_table, *([cache] * PG_STEP), jnp.tile(pe_k, (1, 2)), jnp.tile(pe_v, (1, 2)),
      _block_diag2(w_ck), _block_diag2(w_cv))


def _nsa_smp_body(pt_ref, qs_ref, gl_ref, fs_ref, *refs, n_tok):
    pages = refs[:PG_STEP]
    nrow_ref, pwin_ref, nwin_ref, o_ref, shift_ref, selc_ref, oc_ref, m_ref, l_ref, acc_ref = refs[PG_STEP:]
    s_id = pl.program_id(1)
    C = LANE
    n_half = fs_ref.shape[1]
    n_cmp = n_half - 1
    n_selp = n_half * CMP_STRIDE // SEL_BLK
    qs = qs_ref[0]
    crow = lax.broadcasted_iota(jnp.int32, (C, LANE), 0)
    lane = lax.broadcasted_iota(jnp.int32, (C, LANE), 1)
    t_of_c = crow % n_tok

    @pl.when(s_id == 0)
    def _first():
        cmp = []
        for slab in range(2):
            shift_ref[0:n_half, :] = fs_ref[0, :, (2 * slab + 1) * LANE:(2 * slab + 2) * LANE]
            shift_ref[n_half:n_half + 8, :] = jnp.zeros((8, LANE), F32)
            cmp.append((fs_ref[0, :, (2 * slab) * LANE:(2 * slab + 1) * LANE]
                        + shift_ref[1:n_half + 1, :]).astype(BF16))
        nlane = lax.broadcasted_iota(jnp.int32, (C, n_half), 1)
        p = _softmax_rows(_nt(qs, cmp[0]), nlane < n_cmp)
        oc_ref[...] = jnp.dot(p.astype(BF16), cmp[1], preferred_element_type=F32)
        imp_rows = []
        for g in range(NSA_KV):
            acc = p[(NSA_HPG * g) * n_tok:(NSA_HPG * g + 1) * n_tok, :]
            for h in range(1, NSA_HPG):
                acc = acc + p[(NSA_HPG * g + h) * n_tok:(NSA_HPG * g + h + 1) * n_tok, :]
            imp_rows.append(acc)
        imp = jnp.concatenate(imp_rows + [jnp.zeros((C - NSA_KV * n_tok, n_half), F32)], axis=0)
        pi = lax.broadcasted_iota(jnp.int32, (n_half, n_selp), 0)
        pj = lax.broadcasted_iota(jnp.int32, (n_half, n_selp), 1)
        pool = ((((pi + 1) >> 2) == pj) & (pi < n_cmp)).astype(BF16)
        scoreT = _dot01_r(imp, pool).T
        jcol = lax.broadcasted_iota(jnp.int32, (n_selp, C), 0)
        forced = (jcol == 0) | (jcol == n_selp - 1)
        scoreT = scoreT + jnp.where(forced, FORCE_SCORE, 0.0)
        rank = jnp.zeros((n_selp, C), jnp.int32)
        for j2 in range(n_selp):
            r = scoreT[j2:j2 + 1, :]
            beats = (r > scoreT) | ((r == scoreT) & (jcol > j2))
            rank = rank + beats.astype(jnp.int32)
        sel = jnp.where(rank < TOP_N - 1, 1.0, 0.0).T
        selc_ref[...] = jnp.concatenate(
            [sel[(hh // NSA_HPG) * n_tok:(hh // NSA_HPG + 1) * n_tok, :] for hh in range(NSA_HEADS)]
            + [jnp.zeros((C - NSA_HEADS * n_tok, n_selp), F32)], axis=0)
        m_ref[...] = jnp.full(m_ref.shape, NEG, F32)
        l_ref[...] = jnp.zeros(l_ref.shape, F32)
        acc_ref[...] = jnp.zeros(acc_ref.shape, F32)

    selc = selc_ref[...].astype(BF16)
    n_keys = PG_STEP * PAGE_SIZE
    ej = lax.broadcasted_iota(jnp.int32, (n_selp, n_keys), 0)
    ek = lax.broadcasted_iota(jnp.int32, (n_selp, n_keys), 1)
    expand = (ej == s_id * (n_keys // SEL_BLK) + ek // SEL_BLK).astype(BF16)
    mask = jnp.dot(selc, expand, preferred_element_type=F32) > 0.5
    s = jnp.concatenate([jnp.dot(qs, pages[pg][0, 0:LANE, :].astype(BF16), preferred_element_type=F32)
                         for pg in range(PG_STEP)], axis=1)
    alpha, e = _flash_rows(s, mask, m_ref, l_ref)
    eb = e.astype(BF16)
    pv = _nt(eb[:, 0:PAGE_SIZE], pages[0][0, LANE:2 * LANE, :].astype(BF16))
    for pg in range(1, PG_STEP):
        pv = pv + _nt(eb[:, pg * PAGE_SIZE:(pg + 1) * PAGE_SIZE], pages[pg][0, LANE:2 * LANE, :].astype(BF16))
    acc_ref[...] = alpha * acc_ref[...] + pv

    @pl.when(s_id == pl.num_programs(1) - 1)
    def _last():
        pad = jnp.zeros((LANE - n_tok, LANE), F32)
        new_ok = (lane < n_tok) & (lane <= t_of_c)
        knew = jnp.concatenate([nrow_ref[0, :, 2 * LANE:3 * LANE], pad], axis=0).astype(BF16)
        vnew = jnp.concatenate([nrow_ref[0, :, 3 * LANE:4 * LANE], pad], axis=0).astype(BF16)
        alpha, e = _flash_rows(_nt(qs, knew), new_ok, m_ref, l_ref)
        acc = alpha * acc_ref[...] + jnp.dot(e.astype(BF16), vnew, preferred_element_type=F32)
        o_s = acc * (1.0 / jnp.maximum(l_ref[...], 1e-30))
        n_win = pwin_ref.shape[2]
        kwn = jnp.concatenate([nwin_ref[0, :, 0:LANE], pad], axis=0).astype(BF16)
        vwn = jnp.concatenate([nwin_ref[0, :, LANE:2 * LANE], pad], axis=0).astype(BF16)
        wl = lax.broadcasted_iota(jnp.int32, (C, n_win + LANE), 1)
        tw = lax.broadcasted_iota(jnp.int32, (C, n_win + LANE), 0) % n_tok
        wmask = (((wl < n_win) & ((n_win - wl) + tw < WINDOW))
                 | ((wl >= n_win) & (wl - n_win < n_tok) & (wl - n_win <= tw)))
        s_w = jnp.concatenate([jnp.dot(qs, pwin_ref[0, 0:LANE, :].astype(BF16), preferred_element_type=F32),
                               _nt(qs, kwn)], axis=1)
        pw = _softmax_rows(s_w, wmask).astype(BF16)
        o_w = (_nt(pw[:, 0:n_win], pwin_ref[0, LANE:2 * LANE, :].astype(BF16))
               + jnp.dot(pw[:, n_win:], vwn, preferred_element_type=F32))
        g = jax.nn.sigmoid(gl_ref[0])
        o = g[:, 0:1] * oc_ref[...] + g[:, 1:2] * o_s + g[:, 2:3] * o_w
        valid = (crow < NSA_HEADS * n_tok) & (lane // HEAD_DIM == crow // (n_tok * NSA_HPG))
        o_ref[0] = jnp.where(valid, o, 0.0)


def nsa_sample_attention(qs, gcol, fs, cache, layer, page_table, new_rows, past_win, new_win):
    B, n_pages = page_table.shape
    n_tok = new_rows.shape[1]
    n_half = fs.shape[1]
    n_selp = n_half * CMP_STRIDE // SEL_BLK
    n_win = past_win.shape[3]
    assert n_selp == LANE and NSA_HEADS * n_tok <= LANE and n_tok <= min(SEL_BLK, 8) and n_pages % PG_STEP == 0
    assert n_win % LANE == 0 and n_win <= WINDOW
    grid_spec = pltpu.PrefetchScalarGridSpec(
        num_scalar_prefetch=1,
        grid=(B, n_pages // PG_STEP),
        in_specs=[_per_request((LANE, LANE)), _per_request((LANE, LANE)), _per_request((n_half, 4 * LANE))]
        + [_page_spec(layer, k, 2 * LANE, 1) for k in range(PG_STEP)]
        + [_per_request((n_tok, 4 * LANE)),
           pl.BlockSpec((None, 1, 2 * LANE, n_win), lambda b, s, pt: (layer, b, 0, 0)),
           _per_request((n_tok, 2 * LANE))],
        out_specs=_per_request((LANE, LANE)),
        scratch_shapes=[pltpu.VMEM((n_half + 8, LANE), F32), pltpu.VMEM((LANE, n_selp), F32),
                        pltpu.VMEM((LANE, LANE), F32), pltpu.VMEM((LANE, 1), F32), pltpu.VMEM((LANE, 1), F32),
                        pltpu.VMEM((LANE, LANE), F32)],
    )
    return pl.pallas_call(
        functools.partial(_nsa_smp_body, n_tok=n_tok),
        grid_spec=grid_spec,
        out_shape=jax.ShapeDtypeStruct((B, LANE, LANE), F32),
        compiler_params=_params("parallel", "arbitrary"),
        name="nsa_sample_attention",
    )(page_table, qs, gcol, fs, *([cache] * PG_STEP), new_rows, past_win, new_win)


def _dsa_idx_body(pt_ref, qe_ref, qo_ref, w_ref, *refs, n_tok, n_keep):
    pages = refs[:PG_STEP]
    knew_ref, sc_ref, th_ref = refs[PG_STEP:]
    s_id = pl.program_id(1)
    n_blk = sc_ref.shape[1]
    qe = qe_ref[0]
    qo = qo_ref[0]
    w = w_ref[0] * (IDX_HEADS ** -0.5)
    n_rows = (IDX_HEADS // 2) * n_tok

    def scores(s_e, s_o):
        s = jnp.maximum(s_e, 0.0) * w[:, 0:1] + jnp.maximum(s_o, 0.0) * w[:, 1:2]
        tot = s[0:n_tok, :]
        for p in range(1, IDX_HEADS // 2):
            tot = tot + s[p * n_tok:(p + 1) * n_tok, :]
        return tot

    kT = jnp.concatenate([pages[pg][0] for pg in range(PG_STEP)], axis=1).astype(BF16)
    sc = scores(jnp.dot(qe, kT, preferred_element_type=F32), jnp.dot(qo, kT, preferred_element_type=F32))
    for pg in range(PG_STEP):
        sc_ref[0, s_id * PG_STEP + pg] = sc[:, pg * PAGE_SIZE:(pg + 1) * PAGE_SIZE]

    @pl.when(s_id == pl.num_programs(1) - 1)
    def _last():
        kn = jnp.concatenate([knew_ref[0], jnp.zeros((LANE - n_tok, IDX_DIM), F32)], axis=0).astype(BF16)
        a_i = lax.broadcasted_iota(jnp.int32, (n_tok, LANE), 1)
        t_i = lax.broadcasted_iota(jnp.int32, (n_tok, LANE), 0)
        sn = jnp.where(a_i <= t_i, scores(_nt(qe, kn), _nt(qo, kn)), NEG)
        sc_ref[0, n_blk - 1] = jnp.where(a_i < n_tok, sn, -jnp.inf)
        keys = _sort_key(sc_ref[0])

        def bit_step(it, theta):
            cand = theta + lax.shift_left(jnp.int32(1), 31 - it)
            cnt = jnp.sum(jnp.sum((keys >= cand).astype(jnp.int32), axis=0), axis=1, keepdims=True)
            return jnp.where(cnt >= n_keep, cand, theta)

        theta = lax.fori_loop(0, 32, bit_step, jnp.full((n_tok, 1), INT_MIN, jnp.int32))
        th_ref[0] = jnp.broadcast_to(theta, (n_tok, LANE))

        count = lambda m: jnp.sum(jnp.sum(m.astype(jnp.int32), axis=0), axis=1, keepdims=True)

        @pl.when(jnp.max(count(keys >= theta)) > n_keep)
        def _break_ties():
            need = (n_keep - count(keys > theta)).astype(F32)
            before = (lax.broadcasted_iota(jnp.int32, (LANE, LANE), 0)
                      < lax.broadcasted_iota(jnp.int32, (LANE, LANE), 1)).astype(BF16)

            def fix_step(blk, seen):
                sc = sc_ref[0, blk]
                eq = _sort_key(sc) == theta
                eqf = jnp.where(eq, 1.0, 0.0)
                rank = seen + jnp.dot(eqf.astype(BF16), before, preferred_element_type=F32)
                sc_ref[0, blk] = jnp.where(eq & (rank >= need), -jnp.inf, sc)
                return seen + jnp.sum(eqf, axis=1, keepdims=True)

            lax.fori_loop(0, n_blk, fix_step, jnp.zeros((n_tok, 1), F32))


def dsa_sample_index(qe, qo, wcol, cache_idx, layer, page_table, ki_new):
    B, n_pages = page_table.shape
    n_tok = ki_new.shape[1]
    assert n_tok == 8 and n_pages % PG_STEP == 0
    n_keep = min(IDX_TOPK, (n_pages * PAGE_SIZE + n_tok) // 4)
    grid_spec = pltpu.PrefetchScalarGridSpec(
        num_scalar_prefetch=1,
        grid=(B, n_pages // PG_STEP),
        in_specs=[_per_request((LANE, IDX_DIM)), _per_request((LANE, IDX_DIM)), _per_request((LANE, LANE))]
        + [_page_spec(layer, k, IDX_DIM, 0) for k in range(PG_STEP)] + [_per_request((n_tok, IDX_DIM))],
        out_specs=[_per_request((n_pages + 1, n_tok, LANE)), _per_request((n_tok, LANE))],
    )
    return pl.pallas_call(
        functools.partial(_dsa_idx_body, n_tok=n_tok, n_keep=n_keep),
        grid_spec=grid_spec,
        out_shape=[jax.ShapeDtypeStruct((B, n_pages + 1, n_tok, LANE), F32),
                   jax.ShapeDtypeStruct((B, n_tok, LANE), jnp.int32)],
        compiler_params=_params("parallel", "arbitrary"),
        name="dsa_sample_index",
    )(page_table, qe, qo, wcol, *([cache_idx] * PG_STEP), ki_new)


def _dsa_smp_body(pt_ref, qs_ref, sc_ref, th_ref, *refs, n_tok):
    pages = refs[:PG_STEP]
    kvn_ref, o_ref, m_ref, l_ref, acc_ref = refs[PG_STEP:]
    s_id = pl.program_id(1)
    C = LANE
    qs = qs_ref[0]
    crow = lax.broadcasted_iota(jnp.int32, (C, LANE), 0)
    lane = lax.broadcasted_iota(jnp.int32, (C, LANE), 1)
    low = crow < C // 2
    theta = th_ref[0]

    @pl.when(s_id == 0)
    def _():
        m_ref[...] = jnp.full(m_ref.shape, NEG, F32)
        l_ref[...] = jnp.zeros(l_ref.shape, F32)
        acc_ref[...] = jnp.zeros(acc_ref.shape, F32)

    def step(kvTs, blk0, extra):
        kbs = [kvT.astype(BF16) for kvT in kvTs]
        keep = jnp.concatenate([jnp.where(_sort_key(sc_ref[0, blk0 + j]) >= theta, 1.0, 0.0)
                                for j in range(len(kbs))], axis=1)
        mask = jnp.concatenate([keep] * (C // n_tok), axis=0) > 0.5
        if extra is not None:
            mask = mask & extra
        s_lo = jnp.concatenate([jnp.dot(qs, kb[0:LANE], preferred_element_type=F32) for kb in kbs], axis=1)
        s_hi = jnp.concatenate([jnp.dot(qs, kb[LANE:2 * LANE], preferred_element_type=F32) for kb in kbs], axis=1)
        low_col = lax.broadcasted_iota(jnp.int32, (C, 1), 0) < C // 2
        alpha, e = _flash_rows(jnp.where(low_col, s_lo, s_hi), mask, m_ref, l_ref)
        eb = e.astype(BF16)
        pv_lo = pv_hi = None
        for j, kb in enumerate(kbs):
            ej = eb[:, j * LANE:(j + 1) * LANE]
            a, b = _nt(ej, kb[2 * LANE:3 * LANE]), _nt(ej, kb[3 * LANE:4 * LANE])
            pv_lo, pv_hi = (a, b) if pv_lo is None else (pv_lo + a, pv_hi + b)
        acc_ref[...] = alpha * acc_ref[...] + jnp.where(low, pv_lo, pv_hi)

    step([pages[pg][0] for pg in range(PG_STEP)], s_id * PG_STEP, None)

    @pl.when(s_id == pl.num_programs(1) - 1)
    def _last():
        kvn = jnp.concatenate([kvn_ref[0], jnp.zeros((LANE - n_tok, 4 * LANE), F32)], axis=0)
        kvnT = jnp.concatenate([kvn[:, j * LANE:(j + 1) * LANE].T for j in range(4)], axis=0)
        step([kvnT], sc_ref.shape[1] - 1, (lane < n_tok) & (lane <= crow % n_tok))
        o = acc_ref[...] * (1.0 / jnp.maximum(l_ref[...], 1e-30))
        valid = lane // HEAD_DIM == (crow // (n_tok * DSA_HPG)) % 2
        o_ref[0] = jnp.where(valid, o, 0.0)


def dsa_sample_attention(qs, scores, theta, cache_kv, layer, page_table, kv_new):
    B, n_pages = page_table.shape
    n_tok = kv_new.shape[1]
    assert DSA_HEADS * n_tok == LANE and n_pages % PG_STEP == 0
    grid_spec = pltpu.PrefetchScalarGridSpec(
        num_scalar_prefetch=1,
        grid=(B, n_pages // PG_STEP),
        in_specs=[_per_request((LANE, LANE)), _per_request((n_pages + 1, n_tok, LANE)), _per_request((n_tok, LANE))]
        + [_page_spec(layer, k, 4 * LANE, 0) for k in range(PG_STEP)] + [_per_request((n_tok, 4 * LANE))],
        out_specs=_per_request((LANE, LANE)),
        scratch_shapes=[pltpu.VMEM((LANE, 1), F32), pltpu.VMEM((LANE, 1), F32), pltpu.VMEM((LANE, LANE), F32)],
    )
    return pl.pallas_call(
        functools.partial(_dsa_smp_body, n_tok=n_tok),
        grid_spec=grid_spec,
        out_shape=jax.ShapeDtypeStruct((B, LANE, LANE), F32),
        compiler_params=_params("parallel", "arbitrary"),
        name="dsa_sample_attention",
    )(page_table, qs, scores, theta, *([cache_kv] * PG_STEP), kv_new)


_NSA_SLOTS = tuple(h // NSA_HPG for h in range(NSA_HEADS))
_DSA_SLOTS = tuple((h // DSA_HPG) % 2 for h in range(DSA_HEADS))
def _even_segs(q_dtype, cm):
    return ((NSA_HEADS * LANE, (1,) * NSA_HEADS, q_dtype, False), (4 * LANE, (1, 0, 1, 0), F32, cm),
            (2 * LANE, (1, 0), F32, False), (LANE, (0,), F32, False),
            (4 * HGRN_KW, (0,) * (4 * HGRN_KW // LANE), F32, False))


def _odd_segs(q_dtype, cm):
    return ((DSA_HEADS * LANE, (1,) * DSA_HEADS, q_dtype, False), (4 * LANE, (1, 1, 0, 0), F32, cm),
            (IDX_QW, (1,) * 4, q_dtype, False), (LANE, (1,), F32, cm), (LANE, (0,), F32, False))


def _even_weights(w_in, w_out):
    c = np.cumsum([0, NSA_QW] + [NSA_KVW] * 6 + [NSA_GW] + [HGRN_KW] * 4)
    w = jnp.concatenate([
        _pad_head_cols(w_in[:, :NSA_QW] * Q_SCALE, _NSA_SLOTS),
        w_in[:, c[1]:c[5]], w_in[:, c[5]:c[7]], _pad_cols(w_in[:, c[7]:c[8]], LANE), w_in[:, c[8]:c[12]]],
        axis=1).astype(BF16)
    w_outs = [_pad_head_cols(w_out[:NSA_QW].T, _NSA_SLOTS).T.astype(BF16), w_out[NSA_QW:].astype(BF16)]
    return w, w_outs


def _odd_weights(w_in, w_out):
    c = np.cumsum([0, DSA_QW, DSA_KVW, DSA_KVW, IDX_QW, IDX_DIM, IDX_HEADS])
    w = jnp.concatenate([
        _pad_head_cols(w_in[:, :DSA_QW] * Q_SCALE, _DSA_SLOTS),
        w_in[:, c[1]:c[3]], w_in[:, c[3]:c[4]] * IDX_DIM ** -0.5,
        _pad_cols(w_in[:, c[4]:c[5]], LANE), _pad_cols(w_in[:, c[5]:c[6]], LANE)], axis=1).astype(BF16)
    return w, [_pad_head_cols(w_out.T, _DSA_SLOTS).T.astype(BF16)]


def even_mixer_prompt(x, scale, shift, cos_t, sin_t, w_in, w_out, pe_k, pe_v, w_ck, w_cv, lb, norm_g, win_len):
    B, T, _ = x.shape
    assert T >= win_len
    w, w_outs = _even_weights(w_in, w_out)
    qp, rows, win, gl, hx, rows_cm = proj_segments(x, scale, shift, cos_t, sin_t, w, _even_segs(BF16, True))
    o_a = nsa_prompt(qp, rows, win, gl, pe_k, pe_v, w_ck, w_cv)
    o_b, s_new = hgrn_mixer(hx, lb, norm_g, jnp.zeros((B, HGRN_HEADS, HGRN_DK, HGRN_DV), F32))
    new_rows = jnp.moveaxis(rows_cm.reshape(B, 4, NSA_KV, HEAD_DIM, T), 4, 1)
    win_state = win[:, T - win_len:].reshape(B, win_len, 2, NSA_KV, HEAD_DIM)
    return [o_a, o_b], w_outs, new_rows, win_state, s_new


def odd_mixer_prompt(x, scale, shift, cos_t, sin_t, w_in, w_out):
    B, T, _ = x.shape
    w, w_outs = _odd_weights(w_in, w_out)
    qp, kv, qi, ki, wi, kv_cm, ki_cm = proj_segments(x, scale, shift, cos_t, sin_t, w, _odd_segs(BF16, True))
    o = dsa_prompt(qp, kv, qi, ki, wi)
    new_kv = jnp.moveaxis(kv_cm.reshape(B, 2, DSA_KV, HEAD_DIM, T), 4, 1)
    return [o], w_outs, new_kv, jnp.moveaxis(ki_cm[:, :IDX_DIM], 2, 1)


def _stack_rows(a, B, n, heads, width):
    s = a.reshape(B, n, heads, width).transpose(0, 2, 1, 3).reshape(B, heads * n, width)
    return jnp.pad(s, ((0, 0), (0, LANE - heads * n), (0, 0)))


def _unstack_rows(o, B, n, heads):
    return o[:, :heads * n].reshape(B, heads, n, LANE).transpose(0, 2, 1, 3).reshape(1, B * n, heads * LANE)


def even_mixer_sample(x, scale, shift, past_len, B, w_in, w_out, pe_k, pe_v, w_ck, w_cv, lb, norm_g,
                      cache, cache_win, state, page_table, layer):
    n = x.shape[1] // B
    win_len = cache_win.shape[2]
    cos_t, sin_t = [jnp.tile(a, (B, 1)) for a in rope_tables(past_len + jnp.arange(n))]
    w, w_outs = _even_weights(w_in, w_out)
    qp, rows, win, gl, hx = proj_segments(x, scale, shift, cos_t, sin_t, w, _even_segs(F32, False))
    rows, win, hx = [a.reshape(B, n, a.shape[-1]) for a in (rows, win, hx)]
    qs = _stack_rows(qp, B, n, NSA_HEADS, LANE).astype(BF16)
    gcol = jnp.pad(_stack_rows(gl[..., :NSA_GW], B, n, NSA_HEADS, 3), ((0, 0), (0, 0), (0, LANE - 3)))
    cache_cm = _channel_major(cache)
    fs = nsa_sample_compress(cache_cm, layer, page_table, pe_k, pe_v, w_ck, w_cv)
    o = nsa_sample_attention(qs, gcol, fs, cache_cm, layer, page_table, rows, _channel_major(cache_win), win)
    o_a = _unstack_rows(o, B, n, NSA_HEADS)
    o_b, s_new = hgrn_mixer(hx, lb, norm_g, state)
    new_rows = rows.reshape(B, n, 4, NSA_KV, HEAD_DIM)
    win_state = jnp.concatenate([cache_win[layer], win.reshape(B, n, 2, NSA_KV, HEAD_DIM)], axis=1)[:, -win_len:]
    return [o_a, o_b.reshape(1, B * n, HGRN_VW)], w_outs, new_rows, win_state, s_new


def odd_mixer_sample(x, scale, shift, past_len, B, w_in, w_out, cache_kv, cache_idx, page_table, layer):
    n = x.shape[1] // B
    cos_t, sin_t = [jnp.tile(a, (B, 1)) for a in rope_tables(past_len + jnp.arange(n))]
    w, w_outs = _odd_weights(w_in, w_out)
    qp, kv, qi, ki, wi = proj_segments(x, scale, shift, cos_t, sin_t, w, _odd_segs(F32, False))
    kv, ki = kv.reshape(B, n, 4 * LANE), ki.reshape(B, n, LANE)[..., :IDX_DIM]
    qs = _stack_rows(qp, B, n, DSA_HEADS, LANE).astype(BF16)
    qi4 = qi.reshape(1, B * n, IDX_HEADS // 2, 2, IDX_DIM)
    qe = _stack_rows(qi4[:, :, :, 0], B, n, IDX_HEADS // 2, IDX_DIM).astype(BF16)
    qo = _stack_rows(qi4[:, :, :, 1], B, n, IDX_HEADS // 2, IDX_DIM).astype(BF16)
    wcol = jnp.pad(_stack_rows(wi[..., :IDX_HEADS], B, n, IDX_HEADS // 2, 2), ((0, 0), (0, 0), (0, LANE - 2)))
    scores, theta = dsa_sample_index(qe, qo, wcol, _channel_major(cache_idx), layer, page_table, ki)
    o = dsa_sample_attention(qs, scores, theta, _channel_major(cache_kv), layer, page_table, kv)
    return [_unstack_rows(o, B, n, DSA_HEADS)], w_outs, kv.reshape(B, n, 2, DSA_KV, HEAD_DIM), ki


def moe_ffn_residual(xs, hs, logits_list, gates, w1, w3, w2):
    D = D_MODEL
    h_all = jnp.concatenate([h.reshape(-1, D) for h in hs], axis=0)
    logits = jnp.concatenate([lg.reshape(-1, lg.shape[-1])[:, :N_EXPERTS] for lg in logits_list], axis=0)
    n_tok = h_all.shape[0]
    top_v, top_i = lax.top_k(logits, TOP_K)
    weights = jax.nn.softmax(top_v, axis=-1)
    e_flat = top_i.reshape(-1)
    onehot = (e_flat[:, None] == jnp.arange(N_EXPERTS)[None, :]).astype(jnp.int32)
    csum = jnp.cumsum(onehot, axis=0)
    counts = csum[-1]
    rank = jnp.take_along_axis(csum, e_flat[:, None], axis=1)[:, 0] - 1
    padded = ((counts + MOE_TILE - 1) // MOE_TILE) * MOE_TILE
    group_end = jnp.cumsum(padded)
    group_start = group_end - padded
    slot = group_start[e_flat] + rank
    n_slots = _round_up(n_tok * TOP_K, MOE_TILE) + N_EXPERTS * MOE_TILE
    n_tiles = n_slots // MOE_TILE
    tok_of_slot = jnp.zeros((n_slots,), jnp.int32).at[slot].set(jnp.arange(n_tok * TOP_K, dtype=jnp.int32) // TOP_K)
    tile_start = jnp.arange(n_tiles, dtype=jnp.int32) * MOE_TILE
    tile_expert = jnp.minimum(jnp.sum(tile_start[:, None] >= group_end[None, :], axis=1), N_EXPERTS - 1).astype(jnp.int32)
    n_used = (group_end[-1] // MOE_TILE).astype(jnp.int32).reshape(1)
    h_sorted = h_all[tok_of_slot]
    y_slot = moe_grouped_swiglu(h_sorted, tile_expert, n_used, w1, w3, w2)
    slot2 = slot.reshape(n_tok, TOP_K)
    outs = []
    off = 0
    for x, g in zip(xs, gates):
        n = x.shape[0] * x.shape[1]
        ya = y_slot[slot2[off:off + n, 0]].reshape(x.shape)
        yb = y_slot[slot2[off:off + n, 1]].reshape(x.shape)
        w = weights[off:off + n].reshape(x.shape[0], x.shape[1], TOP_K)
        outs.append(moe_combine_residual(x, g, w, ya, yb))
        off += n
    return outs


def kernel(x_prompt, x_sample, cache_nsa, cache_nsa_win, state_hgrn, cache_dsa_kv, cache_dsa_idx, page_table, c_prompt, c_sample, ada_w, ada_b, norm1_g, norm2_g, final_g, even_w_in, even_w_out, nsa_pe_k, nsa_pe_v, nsa_w_ck, nsa_w_cv, hgrn_lb_raw, hgrn_norm_g, ffn_w1, ffn_w3, ffn_w2, odd_w_in, odd_w_out, router_w, router_b, moe_w1, moe_w3, moe_w2):
    D = D_MODEL
    past_len = page_table.shape[1] * PAGE_SIZE
    win_len = cache_nsa_win.shape[2]
    Bp, Tp = x_prompt.shape[:2]
    Bs, Ts = x_sample.shape[:2]
    lb_soft = jax.nn.softmax(hgrn_lb_raw.astype(F32), axis=0)
    lower_bounds = jnp.cumsum(lb_soft, axis=0) - lb_soft[0]

    R = _round_up(Bp + Bs, 8)
    c_all = jnp.zeros((R, D), F32).at[:Bp].set(c_prompt).at[Bp:Bp + Bs].set(c_sample)
    mods = ada_modulation(c_all, ada_w, ada_b)

    def group_mods(l, lo, n, per_token_rows):
        m = mods[l, lo:lo + n].reshape(n, 6, D)
        sh1, sc1, g1, sh2, sc2, g2 = [m[:, j] for j in range(6)]
        s1 = norm1_g[l][None] * (1.0 + sc1)
        s2 = norm2_g[l][None] * (1.0 + sc2)
        vecs = [s1, sh1, g1, s2, sh2, g2]
        if per_token_rows:
            return [jnp.repeat(v, per_token_rows, axis=0)[None] for v in vecs]
        return [v[:, None, :] for v in vecs]

    def pad_cols(w, n):
        return jnp.pad(w, ((0, 0), (0, n - w.shape[1]))).astype(BF16)

    xp = x_prompt
    xs = x_sample.reshape(1, Bs * Ts, D)
    cos_p, sin_p = rope_tables(jnp.arange(Tp))
    outs_p = dict(rows=[], win=[], st=[], kv=[], idx=[])
    outs_s = dict(rows=[], win=[], st=[], kv=[], idx=[])
    for l in range(DEPTH):
        i = l // 2
        mp = group_mods(l, 0, Bp, 0)
        msm = group_mods(l, Bp, Bs, Ts)
        if l % 2 == 0:
            w1, w3, w2 = ffn_w1[i].astype(BF16), ffn_w3[i].astype(BF16), ffn_w2[i].astype(BF16)
            mixed, w_outs, rows, win, s_new = even_mixer_prompt(
                xp, mp[0], mp[1], cos_p, sin_p, even_w_in[i], even_w_out[i], nsa_pe_k[i], nsa_pe_v[i],
                nsa_w_ck[i], nsa_w_cv[i], lower_bounds[i], hgrn_norm_g[i], win_len)
            outs_p['rows'].append(rows)
            outs_p['win'].append(win)
            outs_p['st'].append(s_new)
            xp = out_proj_residual(mixed, w_outs, xp, mp[2])
            xp = ffn_residual(xp, mp[3], mp[4], mp[5], w1, w3, w2)
            mixed, w_outs, rows, win, s_new = even_mixer_sample(
                xs, msm[0], msm[1], past_len, Bs, even_w_in[i], even_w_out[i], nsa_pe_k[i], nsa_pe_v[i],
                nsa_w_ck[i], nsa_w_cv[i], lower_bounds[i], hgrn_norm_g[i],
                cache_nsa, cache_nsa_win, state_hgrn[i], page_table, i)
            outs_s['rows'].append(rows)
            outs_s['win'].append(win)
            outs_s['st'].append(s_new)
            xs = out_proj_residual(mixed, w_outs, xs, msm[2])
            xs = ffn_residual(xs, msm[3], msm[4], msm[5], w1, w3, w2)
        else:
            w_r = pad_cols(router_w[i], LANE)
            w1, w3, w2 = moe_w1[i].astype(BF16), moe_w3[i].astype(BF16), moe_w2[i].astype(BF16)
            o, w_outs, kv, ki = odd_mixer_prompt(xp, mp[0], mp[1], cos_p, sin_p, odd_w_in[i], odd_w_out[i])
            outs_p['kv'].append(kv)
            outs_p['idx'].append(ki)
            xp = out_proj_residual(o, w_outs, xp, mp[2])
            o, w_outs, kv, ki = odd_mixer_sample(xs, msm[0], msm[1], past_len, Bs, odd_w_in[i], odd_w_out[i],
                                                 cache_dsa_kv, cache_dsa_idx, page_table, i)
            outs_s['kv'].append(kv)
            outs_s['idx'].append(ki)
            xs = out_proj_residual(o, w_outs, xs, msm[2])
            hs, lgs = [], []
            for (x, m) in ((xp, mp), (xs, msm)):
                logits, h = norm_proj(x, m[3], m[4], w_r, with_h=True)
                lgs.append(logits[..., :N_EXPERTS] + router_b[i].astype(F32))
                hs.append(h)
            xp, xs = moe_ffn_residual([xp, xs], hs, lgs, [mp[5], msm[5]], w1, w3, w2)
    y_prompt = final_norm(xp, final_g)
    y_sample = final_norm(xs, final_g).reshape(Bs, Ts, D)
    st = lambda od, k: jnp.stack(od[k])
    return (y_prompt, y_sample, st(outs_p, 'rows'), st(outs_s, 'rows'), st(outs_p, 'win'), st(outs_s, 'win'),
            st(outs_p, 'st'), st(outs_s, 'st'), st(outs_p, 'kv'), st(outs_s, 'kv'),
            st(outs_p, 'idx'), st(outs_s, 'idx'))
```

```python
import functools
import math

import numpy as np
import jax
import jax.numpy as jnp
from jax import lax
from jax.experimental import pallas as pl
from jax.experimental.pallas import tpu as pltpu

D_MODEL = 1024
DEPTH = 4
PAGE_SIZE = 128
HEAD_DIM = 64
NSA_HEADS = 8
NSA_KV = 2
NSA_HPG = NSA_HEADS // NSA_KV
CMP_STRIDE = 16
CMP_BLK = 2 * CMP_STRIDE
SEL_BLK = 64
TOP_N = 16
WINDOW = 512
FORCE_SCORE = 1.0e4
HGRN_HEADS = 4
HGRN_DK = 128
HGRN_DV = 128
HGRN_CHUNK = 64
DSA_HEADS = 16
DSA_KV = 4
DSA_HPG = DSA_HEADS // DSA_KV
IDX_HEADS = 8
IDX_DIM = 64
IDX_TOPK = 256
D_FF = 2816
N_EXPERTS = 8
TOP_K = 2
Q_BLK = 128
ROPE_THETA = 10000.0
EPS = 1e-6
NEG = -1.0e30
NSA_QW = NSA_HEADS * HEAD_DIM
NSA_KVW = NSA_KV * HEAD_DIM
NSA_GW = NSA_HEADS * 3
HGRN_KW = HGRN_HEADS * HGRN_DK
HGRN_VW = HGRN_HEADS * HGRN_DV
EVEN_IN = NSA_QW + 6 * NSA_KVW + NSA_GW + 2 * HGRN_KW + 2 * HGRN_VW
EVEN_MIX = NSA_QW + HGRN_VW
DSA_QW = DSA_HEADS * HEAD_DIM
DSA_KVW = DSA_KV * HEAD_DIM
IDX_QW = IDX_HEADS * IDX_DIM
ODD_IN = DSA_QW + 2 * DSA_KVW + IDX_QW + IDX_DIM + IDX_HEADS
ODD_MIX = DSA_QW

LANE = 128
ROW_TILE = 512
FF_CHUNK = 256
MOE_TILE = 256
PROJ_CHUNK = 512
ATT_TQ = 128
KEY_BLK = 256
HGRN_SUB = 8
HGRN_TILE = 256
HGRN_ROWS = 4
PG_STEP = 32
INT_MIN = -2 ** 31
Q_SCALE = HEAD_DIM ** -0.5 * math.log2(math.e)
M_FLOOR = -1.0e25
VMEM_LIMIT = 56 * 1024 * 1024

F32 = jnp.float32
BF16 = jnp.bfloat16


def _round_up(n, m):
    return -(-n // m) * m


def _params(*sem):
    return pltpu.CompilerParams(dimension_semantics=sem, vmem_limit_bytes=VMEM_LIMIT)


def _norm_mod(x, scale, shift):
    ms = jnp.mean(x * x, axis=-1, keepdims=True)
    return x * lax.rsqrt(ms + EPS) * scale + shift


def _mod_spec(mod, tm):
    if mod.shape[1] == 1:
        return pl.BlockSpec((1, 1, mod.shape[2]), lambda b, i: (b, 0, 0))
    return pl.BlockSpec((1, tm, mod.shape[2]), lambda b, i: (b, i, 0))


def _ada_body(c_ref, w_ref, b_ref, o_ref):
    c = c_ref[...]
    cs = (c * jax.nn.sigmoid(c)).astype(BF16)
    o_ref[0] = jnp.dot(cs, w_ref[0].astype(BF16), preferred_element_type=F32) + b_ref[0]


def ada_modulation(c_all, ada_w, ada_b):
    R, D = c_all.shape
    N = ada_w.shape[2]
    tn = 1536
    return pl.pallas_call(
        _ada_body,
        grid=(DEPTH, N // tn),
        in_specs=[pl.BlockSpec((R, D), lambda l, j: (0, 0)),
                  pl.BlockSpec((1, D, tn), lambda l, j: (l, 0, j)),
                  pl.BlockSpec((1, 1, tn), lambda l, j: (l, 0, j))],
        out_specs=pl.BlockSpec((1, R, tn), lambda l, j: (l, 0, j)),
        out_shape=jax.ShapeDtypeStruct((DEPTH, R, N), F32),
        compiler_params=_params("arbitrary", "arbitrary"),
        name="ada_modulation",
    )(c_all, ada_w, ada_b.reshape(DEPTH, 1, N))


def _norm_proj_body(x_ref, sc_ref, sh_ref, w_ref, o_ref, h_ref=None):
    h = _norm_mod(x_ref[0], sc_ref[0], sh_ref[0]).astype(BF16)
    o_ref[0] = jnp.dot(h, w_ref[...], preferred_element_type=F32)
    if h_ref is not None:
        h_ref[0] = h


def norm_proj(x, scale, shift, w, with_h=False):
    B, T, D = x.shape
    N = w.shape[1]
    tm = min(T, ROW_TILE)
    out_shape = [jax.ShapeDtypeStruct((B, T, N), F32)]
    out_specs = [pl.BlockSpec((1, tm, N), lambda b, i: (b, i, 0))]
    if with_h:
        out_shape.append(jax.ShapeDtypeStruct((B, T, D), BF16))
        out_specs.append(pl.BlockSpec((1, tm, D), lambda b, i: (b, i, 0)))
    res = pl.pallas_call(
        _norm_proj_body,
        grid=(B, T // tm),
        in_specs=[pl.BlockSpec((1, tm, D), lambda b, i: (b, i, 0)),
                  _mod_spec(scale, tm), _mod_spec(shift, tm),
                  pl.BlockSpec((D, N), lambda b, i: (0, 0))],
        out_specs=out_specs,
        out_shape=out_shape,
        compiler_params=_params("parallel", "parallel"),
        name="norm_proj",
    )(x, scale, shift, w)
    return res if with_h else res[0]


def _rope_chunk(y, cos, sin):
    lane = lax.broadcasted_iota(jnp.int32, y.shape, 1)
    swapped = jnp.where(lane % HEAD_DIM < HEAD_DIM // 2,
                        pltpu.roll(y, LANE - HEAD_DIM // 2, 1), pltpu.roll(y, HEAD_DIM // 2, 1))
    return y * cos + swapped * sin


def _proj_seg_body(x_ref, sc_ref, sh_ref, cos_ref, sin_ref, w_ref, *o_refs, segs):
    h = _norm_mod(x_ref[0], sc_ref[0], sh_ref[0]).astype(BF16)
    cos = cos_ref[...]
    sin = sin_ref[...]
    t_refs = iter(o_refs[len(segs):])
    c0 = 0
    for o_ref, (width, rope_flags, dtype, channel_major) in zip(o_refs, segs):
        t_ref = next(t_refs) if channel_major else None
        for j0 in range(0, width, PROJ_CHUNK):
            wd = min(PROJ_CHUNK, width - j0)
            y = jnp.dot(h, w_ref[:, c0 + j0:c0 + j0 + wd], preferred_element_type=F32)
            for k in range(wd // LANE):
                cols = slice(j0 + k * LANE, j0 + (k + 1) * LANE)
                yk = y[:, k * LANE:(k + 1) * LANE]
                if rope_flags[(j0 + k * LANE) // LANE]:
                    yk = _rope_chunk(yk, cos, sin)
                o_ref[0, :, cols] = yk.astype(dtype)
                if t_ref is not None:
                    t_ref[0, cols, :] = yk.T
        c0 += width


def proj_segments(x, scale, shift, cos_t, sin_t, w, segs):
    B, T, D = x.shape
    tm = min(T, ROW_TILE)
    N = w.shape[1]
    cm = [s for s in segs if s[3]]
    return pl.pallas_call(
        functools.partial(_proj_seg_body, segs=segs),
        grid=(B, T // tm),
        in_specs=[pl.BlockSpec((1, tm, D), lambda b, i: (b, i, 0)),
                  _mod_spec(scale, tm), _mod_spec(shift, tm),
                  pl.BlockSpec((tm, LANE), lambda b, i: (i, 0)),
                  pl.BlockSpec((tm, LANE), lambda b, i: (i, 0)),
                  pl.BlockSpec((D, N), lambda b, i: (0, 0))],
        out_specs=[pl.BlockSpec((1, tm, s[0]), lambda b, i: (b, i, 0)) for s in segs]
        + [pl.BlockSpec((1, s[0], tm), lambda b, i: (b, 0, i)) for s in cm],
        out_shape=[jax.ShapeDtypeStruct((B, T, s[0]), s[2]) for s in segs]
        + [jax.ShapeDtypeStruct((B, s[0], T), F32) for s in cm],
        compiler_params=_params("parallel", "parallel"),
        name="proj_segments",
    )(x, scale, shift, cos_t, sin_t, w)


def rope_tables(pos):
    half = HEAD_DIM // 2
    inv = ROPE_THETA ** (-jnp.arange(half, dtype=F32) / half)
    ang = pos.astype(F32)[:, None] * inv[None, :]
    cos, sin = jnp.cos(ang), jnp.sin(ang)
    return jnp.tile(cos, (1, 4)), jnp.tile(jnp.concatenate([-sin, sin], axis=1), (1, 2))


def _out_proj_body(*refs, n_in):
    a_refs, w_refs = refs[:n_in], refs[n_in:2 * n_in]
    x_ref, g_ref, o_ref = refs[2 * n_in:]
    y = None
    for a_ref, w_ref in zip(a_refs, w_refs):
        part = jnp.dot(a_ref[0].astype(BF16), w_ref[...], preferred_element_type=F32)
        y = part if y is None else y + part
    o_ref[0] = x_ref[0] + g_ref[0] * y


def out_proj_residual(a_list, w_list, x, gate):
    B, T, D = x.shape
    tm = min(T, ROW_TILE)
    n_in = len(a_list)
    return pl.pallas_call(
        functools.partial(_out_proj_body, n_in=n_in),
        grid=(B, T // tm),
        in_specs=[pl.BlockSpec((1, tm, a.shape[2]), lambda b, i: (b, i, 0)) for a in a_list]
        + [pl.BlockSpec(w.shape, lambda b, i: (0, 0)) for w in w_list]
        + [pl.BlockSpec((1, tm, D), lambda b, i: (b, i, 0)), _mod_spec(gate, tm)],
        out_specs=pl.BlockSpec((1, tm, D), lambda b, i: (b, i, 0)),
        out_shape=jax.ShapeDtypeStruct((B, T, D), F32),
        compiler_params=_params("parallel", "parallel"),
        name="out_proj_residual",
    )(*a_list, *w_list, x, gate)


def _swiglu_acc(h, w1_ref, w3_ref, w2_ref, acc_ref):
    for c in range(D_FF // FF_CHUNK):
        cols = slice(c * FF_CHUNK, (c + 1) * FF_CHUNK)
        u = jnp.dot(h, w1_ref[:, cols], preferred_element_type=F32)
        v = jnp.dot(h, w3_ref[:, cols], preferred_element_type=F32)
        a = (u * jax.nn.sigmoid(u) * v).astype(BF16)
        part = jnp.dot(a, w2_ref[cols, :], preferred_element_type=F32)
        if c == 0:
            acc_ref[...] = part
        else:
            acc_ref[...] += part


def _ffn_body(x_ref, sc_ref, sh_ref, g_ref, w1_ref, w3_ref, w2_ref, o_ref, acc_ref):
    x = x_ref[0]
    h = _norm_mod(x, sc_ref[0], sh_ref[0]).astype(BF16)
    _swiglu_acc(h, w1_ref, w3_ref, w2_ref, acc_ref)
    o_ref[0] = x + g_ref[0] * acc_ref[...]


def ffn_residual(x, scale, shift, gate, w1, w3, w2):
    B, T, D = x.shape
    tm = min(T, ROW_TILE)
    wspec = lambda shape: pl.BlockSpec(shape, lambda b, i: (0, 0))
    return pl.pallas_call(
        _ffn_body,
        grid=(B, T // tm),
        in_specs=[pl.BlockSpec((1, tm, D), lambda b, i: (b, i, 0)),
                  _mod_spec(scale, tm), _mod_spec(shift, tm), _mod_spec(gate, tm),
                  wspec((D, D_FF)), wspec((D, D_FF)), wspec((D_FF, D))],
        out_specs=pl.BlockSpec((1, tm, D), lambda b, i: (b, i, 0)),
        out_shape=jax.ShapeDtypeStruct((B, T, D), F32),
        scratch_shapes=[pltpu.VMEM((tm, D), F32)],
        compiler_params=_params("parallel", "parallel"),
        name="ffn_residual",
    )(x, scale, shift, gate, w1, w3, w2)


def _moe_body(te_ref, nt_ref, h_ref, w1_ref, w3_ref, w2_ref, o_ref, acc_ref):
    i = pl.program_id(0)

    @pl.when(i < nt_ref[0])
    def _():
        _swiglu_acc(h_ref[...], w1_ref.at[0], w3_ref.at[0], w2_ref.at[0], acc_ref)
        o_ref[...] = acc_ref[...]

    @pl.when(i >= nt_ref[0])
    def _():
        o_ref[...] = jnp.zeros_like(o_ref)


def moe_grouped_swiglu(h_sorted, tile_expert, n_tiles_used, w1, w3, w2):
    S, D = h_sorted.shape
    tm = MOE_TILE
    n_tiles = S // tm
    wspec = lambda shape: pl.BlockSpec((1,) + shape, lambda i, te, nt: (te[i], 0, 0))
    grid_spec = pltpu.PrefetchScalarGridSpec(
        num_scalar_prefetch=2,
        grid=(n_tiles,),
        in_specs=[pl.BlockSpec((tm, D), lambda i, te, nt: (i, 0)),
                  wspec((D, D_FF)), wspec((D, D_FF)), wspec((D_FF, D))],
        out_specs=pl.BlockSpec((tm, D), lambda i, te, nt: (i, 0)),
        scratch_shapes=[pltpu.VMEM((tm, D), F32)],
    )
    return pl.pallas_call(
        _moe_body,
        grid_spec=grid_spec,
        out_shape=jax.ShapeDtypeStruct((S, D), F32),
        compiler_params=_params("arbitrary"),
        name="moe_grouped_swiglu",
    )(tile_expert, n_tiles_used, h_sorted, w1, w3, w2)


def _combine_body(x_ref, g_ref, w_ref, a_ref, b_ref, o_ref):
    w = w_ref[0]
    o_ref[0] = x_ref[0] + g_ref[0] * (w[:, 0:1] * a_ref[0] + w[:, 1:2] * b_ref[0])


def moe_combine_residual(x, gate, w, ya, yb):
    B, T, D = x.shape
    tm = min(T, ROW_TILE)
    tile = pl.BlockSpec((1, tm, D), lambda b, i: (b, i, 0))
    return pl.pallas_call(
        _combine_body,
        grid=(B, T // tm),
        in_specs=[tile, _mod_spec(gate, tm), pl.BlockSpec((1, tm, TOP_K), lambda b, i: (b, i, 0)), tile, tile],
        out_specs=tile,
        out_shape=jax.ShapeDtypeStruct((B, T, D), F32),
        compiler_params=_params("parallel", "parallel"),
        name="moe_combine_residual",
    )(x, gate, w, ya, yb)


def _final_norm_body(x_ref, g_ref, o_ref):
    x = x_ref[0]
    ms = jnp.mean(x * x, axis=-1, keepdims=True)
    o_ref[0] = x * lax.rsqrt(ms + EPS) * g_ref[...]


def final_norm(x, g):
    B, T, D = x.shape
    tm = min(T, ROW_TILE)
    return pl.pallas_call(
        _final_norm_body,
        grid=(B, T // tm),
        in_specs=[pl.BlockSpec((1, tm, D), lambda b, i: (b, i, 0)),
                  pl.BlockSpec((1, D), lambda b, i: (0, 0))],
        out_specs=pl.BlockSpec((1, tm, D), lambda b, i: (b, i, 0)),
        out_shape=jax.ShapeDtypeStruct((B, T, D), F32),
        compiler_params=_params("parallel", "parallel"),
        name="final_norm",
    )(x, g.reshape(1, D))


def _split3(x):
    hi = x.astype(BF16)
    r1 = x - hi.astype(F32)
    mid = r1.astype(BF16)
    lo = (r1 - mid.astype(F32)).astype(BF16)
    return hi, mid, lo


def _dot01(m01, x):
    hi, mid, lo = _split3(x)
    d = lambda p: jnp.dot(m01, p, preferred_element_type=F32)
    return d(hi) + d(mid) + d(lo)


def _hgrn_body(hq_ref, hf_ref, hi_ref, hg_ref, lb_ref, ng_ref, s0_ref, o_ref, sn_ref, st_ref, *, sub, n_sub):
    i = pl.program_id(1)
    nb = hq_ref.shape[0]

    @pl.when(i == 0)
    def _():
        for bb in range(nb):
            for h in range(HGRN_HEADS):
                st_ref[bb, h] = s0_ref[bb, h].T

    lb = lb_ref[...]
    ng = ng_ref[...]
    row = lax.broadcasted_iota(jnp.int32, (sub, sub), 0)
    col = lax.broadcasted_iota(jnp.int32, (sub, sub), 1)
    tril = (row >= col).astype(BF16)
    trow = lax.broadcasted_iota(jnp.int32, (sub, HGRN_DK), 0)

    def chunk(c, carry):
        for bb in range(nb):
            chunk_one(c, bb)
        return carry

    def chunk_one(c, bb):
        rows = pl.ds(pl.multiple_of(c * sub, sub), sub)
        hq = hq_ref[bb, rows, :]
        hf = hf_ref[bb, rows, :]
        hv = hi_ref[bb, rows, :]
        hg = hg_ref[bb, rows, :]
        f = lb + (1.0 - lb) * jax.nn.sigmoid(hf)
        logf = jnp.log(f)
        kk = 1.0 - f
        qq = hq * jax.nn.sigmoid(hq)
        b = _dot01(tril, logf)
        bl = b[sub - 1:sub, :]
        qe = qq * jnp.exp(b)
        ke = kk * jnp.exp(bl - b)
        ebl = jnp.exp(bl)
        outs = []
        for h in range(HGRN_HEADS):
            cs = slice(h * HGRN_DK, (h + 1) * HGRN_DK)
            st = st_ref[bb, h]
            o = lax.dot_general(qe[:, cs].astype(BF16), st.astype(BF16), (((1,), (1,)), ((), ())),
                                preferred_element_type=F32)
            bh, qh, kh, vh = b[:, cs], qq[:, cs], kk[:, cs], hv[:, cs]
            for s in range(sub):
                e = jnp.exp(jnp.minimum(bh - bh[s:s + 1, :], 0.0))
                w = jnp.where(trow >= s, e * qh * kh[s:s + 1, :], 0.0)
                o = o + jnp.sum(w, axis=-1, keepdims=True) * vh[s:s + 1, :]
            upd = lax.dot_general(vh.astype(BF16), ke[:, cs].astype(BF16), (((0,), (0,)), ((), ())),
                                  preferred_element_type=F32)
            st_ref[bb, h] = st * ebl[:, cs] + upd
            ms = jnp.mean(o * o, axis=-1, keepdims=True)
            outs.append(o * lax.rsqrt(ms + EPS))
        o_ref[bb, rows, :] = jnp.concatenate(outs, axis=-1) * ng * (hg * jax.nn.sigmoid(hg))

    lax.fori_loop(0, n_sub, chunk, 0)

    @pl.when(i == pl.num_programs(1) - 1)
    def _():
        for bb in range(nb):
            for h in range(HGRN_HEADS):
                sn_ref[bb, h] = st_ref[bb, h].T


def hgrn_mixer(hx, lb, norm_g, s0):
    B, T, _ = hx.shape
    tc = min(T, HGRN_TILE)
    sub = math.gcd(T, HGRN_SUB)
    nb = math.gcd(B, HGRN_ROWS)
    spec = lambda j: pl.BlockSpec((nb, tc, HGRN_KW), lambda b, i, j=j: (b, i, j))
    vec = pl.BlockSpec((1, HGRN_KW), lambda b, i: (0, 0))
    state = pl.BlockSpec((nb, HGRN_HEADS, HGRN_DK, HGRN_DV), lambda b, i: (b, 0, 0, 0))
    return pl.pallas_call(
        functools.partial(_hgrn_body, sub=sub, n_sub=tc // sub),
        grid=(B // nb, T // tc),
        in_specs=[spec(0), spec(1), spec(2), spec(3), vec, vec, state],
        out_specs=[pl.BlockSpec((nb, tc, HGRN_VW), lambda b, i: (b, i, 0)), state],
        out_shape=[jax.ShapeDtypeStruct((B, T, HGRN_VW), F32),
                   jax.ShapeDtypeStruct((B, HGRN_HEADS, HGRN_DK, HGRN_DV), F32)],
        scratch_shapes=[pltpu.VMEM((nb, HGRN_HEADS, HGRN_DV, HGRN_DK), F32)],
        compiler_params=_params("parallel", "arbitrary"),
        name="hgrn_mixer",
    )(hx, hx, hx, hx, lb.reshape(1, HGRN_KW), jnp.tile(norm_g, HGRN_HEADS).reshape(1, HGRN_VW), s0)


def _nt(a, b):
    return lax.dot_general(a, b, (((1,), (1,)), ((), ())), preferred_element_type=F32)


def _flash_step(kblk, vT, q_heads, mask, m, l, acc_ref, batched):
    tq = q_heads[0].shape[0]
    if batched:
        acc = acc_ref[...]
        scores = [_nt(kblk, qh) for qh in q_heads]
    m_out, l_out, acc_out = [], [], []
    for h, qh in enumerate(q_heads):
        cs = slice(h * tq, (h + 1) * tq)
        s = jnp.where(mask, scores[h] if batched else _nt(kblk, qh), NEG)
        m_new = jnp.maximum(m[:, cs], jnp.max(s, axis=0, keepdims=True))
        alpha = jnp.exp2(m[:, cs] - m_new)
        e = jnp.exp2(s - m_new)
        l_out.append(alpha * l[:, cs] + jnp.sum(e, axis=0, keepdims=True))
        pv = jnp.dot(vT, e.astype(BF16), preferred_element_type=F32)
        if batched:
            acc_out.append(alpha * acc[:, cs] + pv)
        else:
            acc_ref[:, cs] = alpha * acc_ref[:, cs] + pv
        m_out.append(m_new)
    if batched:
        acc_ref[...] = jnp.concatenate(acc_out, axis=1)
    return jnp.concatenate(m_out, axis=1), jnp.concatenate(l_out, axis=1)


def _head_slabs(qp_ref, first_head, n):
    return [qp_ref[0, :, (first_head + h) * LANE:(first_head + h + 1) * LANE] for h in range(n)]


def _stack_heads(qp_ref, first_head, n):
    return jnp.concatenate(_head_slabs(qp_ref, first_head, n), axis=0)


def _store_heads(o_ref, oT, first_head, n, slot):
    tq = oT.shape[1] // n
    lane = lax.broadcasted_iota(jnp.int32, (tq, LANE), 1)
    valid = (lane >= HEAD_DIM * slot) & (lane < HEAD_DIM * (slot + 1))
    for h in range(n):
        blk = oT[:, h * tq:(h + 1) * tq].T
        o_ref[0, :, (first_head + h) * LANE:(first_head + h + 1) * LANE] = jnp.where(valid, blk, 0.0).astype(o_ref.dtype)


def _nsa_body(qp_ref, rows_ref, win_ref, gl_ref, pek_ref, pev_ref, wck_ref, wcv_ref, o_ref,
              kcmp_ref, vcmpT_ref, ks_ref, vsT_ref, kw_ref, vwT_ref, stage_ref, shift_ref, sel_ref, acc_ref, *, T):
    i = pl.program_id(1)
    tq = ATT_TQ
    nq = NSA_HPG * tq
    n_half = T // CMP_STRIDE
    n_sel = T // SEL_BLK
    k_sel = min(TOP_N, n_sel)
    per_kb = KEY_BLK // SEL_BLK

    @pl.when(i == 0)
    def _prepare():
        nrow = lax.broadcasted_iota(jnp.int32, (n_half, LANE), 0)
        for slab, pe_ref, w_ref in ((0, pek_ref, wck_ref), (1, pev_ref, wcv_ref)):
            def stage_blk(kb, carry, slab=slab):
                rs = pl.ds(pl.multiple_of(kb * KEY_BLK, KEY_BLK), KEY_BLK)
                stage_ref[rs, :] = rows_ref[0, rs, slab * LANE:(slab + 1) * LANE]
                return carry

            lax.fori_loop(0, T // KEY_BLK, stage_blk, 0)
            first = jnp.zeros((n_half, LANE), F32)
            second = jnp.zeros((n_half, LANE), F32)
            for r in range(CMP_STRIDE):
                y = stage_ref[pl.ds(r, n_half, stride=CMP_STRIDE), :]
                first = first + jnp.dot((y + pe_ref[r:r + 1, :]).astype(BF16), w_ref[r],
                                        preferred_element_type=F32)
                second = second + jnp.dot((y + pe_ref[r + CMP_STRIDE:r + CMP_STRIDE + 1, :]).astype(BF16),
                                          w_ref[r + CMP_STRIDE], preferred_element_type=F32)
            shift_ref[0:n_half, :] = second
            shift_ref[n_half:n_half + 8, :] = jnp.zeros((8, LANE), F32)
            c = jnp.where(nrow < n_half - 1, first + shift_ref[1:n_half + 1, :], 0.0)
            if slab == 0:
                kcmp_ref[...] = c.astype(BF16)
            else:
                for j in range(n_half // LANE):
                    vcmpT_ref[:, j * LANE:(j + 1) * LANE] = c[j * LANE:(j + 1) * LANE, :].T.astype(BF16)

        def copy_blk(kb, carry):
            for half in range(KEY_BLK // LANE):
                rs = pl.ds(pl.multiple_of(kb * KEY_BLK + half * LANE, LANE), LANE)
                hs = slice(half * LANE, (half + 1) * LANE)
                ks_ref[kb, hs, :] = rows_ref[0, rs, 2 * LANE:3 * LANE].astype(BF16)
                vsT_ref[kb, :, hs] = rows_ref[0, rs, 3 * LANE:4 * LANE].T.astype(BF16)
                kw_ref[kb, hs, :] = win_ref[0, rs, 0:LANE].astype(BF16)
                vwT_ref[kb, :, hs] = win_ref[0, rs, LANE:2 * LANE].T.astype(BF16)
            return carry

        lax.fori_loop(0, T // KEY_BLK, copy_blk, 0)

    t0 = i * tq
    lane_t = t0 + (lax.broadcasted_iota(jnp.int32, (1, nq), 1) & (tq - 1))
    t_row = lane_t[:, 0:tq]
    gT = jax.nn.sigmoid(gl_ref[0].T)
    kio = lax.broadcasted_iota(jnp.int32, (KEY_BLK, tq), 0)
    nio = lax.broadcasted_iota(jnp.int32, (n_half, nq), 0)
    jcol = lax.broadcasted_iota(jnp.int32, (n_sel, tq), 0)
    pj = lax.broadcasted_iota(jnp.int32, (n_sel, n_half), 0)
    pi = lax.broadcasted_iota(jnp.int32, (n_sel, n_half), 1)
    pool = ((((pi + 1) >> 2) == pj) & (pi < n_half - 1)).astype(BF16)
    n_kb = (t0 + tq + KEY_BLK - 1) // KEY_BLK
    n_full = t0 // KEY_BLK
    m0 = jnp.full((1, nq), M_FLOOR, F32)
    l0 = jnp.zeros((1, nq), F32)

    for g in range(NSA_KV):
        q_heads = _head_slabs(qp_ref, NSA_HPG * g, NSA_HPG)
        qs = jnp.concatenate(q_heads, axis=0)
        cmask = (CMP_STRIDE * nio + CMP_BLK <= lane_t + 1) & (nio < n_half - 1)
        s = jnp.where(cmask, _nt(kcmp_ref[...], qs), NEG)
        e = jnp.where(cmask, jnp.exp2(s - jnp.max(s, axis=0, keepdims=True)), 0.0)
        p = e * (1.0 / jnp.maximum(jnp.sum(e, axis=0, keepdims=True), 1e-30))
        ocT = jnp.dot(vcmpT_ref[...], p.astype(BF16), preferred_element_type=F32)
        imp = p[:, 0:tq]
        for h in range(1, NSA_HPG):
            imp = imp + p[:, h * tq:(h + 1) * tq]
        imp_sel = _dot01(pool, imp)
        cur = t_row >> 6
        forced = (jcol == 0) | (jcol == cur) | (jcol == cur - 1)
        visible = jcol <= cur
        score = jnp.where(visible, imp_sel + jnp.where(forced, FORCE_SCORE, 0.0), NEG)
        rank = jnp.zeros((n_sel, tq), jnp.int32)
        for j2 in range(n_sel):
            r = score[j2:j2 + 1, :]
            beats = (r > score) | ((r == score) & (jcol > j2))
            rank = rank + beats.astype(jnp.int32)
        sel_ref[...] = jnp.where((rank < k_sel) & visible, 1.0, 0.0)

        def sel_step(kb, carry, causal):
            rowsel = jnp.concatenate(
                [jnp.broadcast_to(sel_ref[pl.ds(kb * per_kb + k, 1), :], (SEL_BLK, tq)) for k in range(per_kb)],
                axis=0)
            mask = rowsel > 0.5
            if causal:
                mask = mask & (kb * KEY_BLK + kio <= t_row)
            return _flash_step(ks_ref[kb], vsT_ref[kb], q_heads, mask, carry[0], carry[1], acc_ref, True)

        acc_ref[...] = jnp.zeros_like(acc_ref)
        ml = lax.fori_loop(0, n_full, functools.partial(sel_step, causal=False), (m0, l0))
        _, l = lax.fori_loop(n_full, n_kb, functools.partial(sel_step, causal=True), ml)
        osT = acc_ref[...] * (1.0 / jnp.maximum(l, 1e-30))

        def win_step(kb, carry):
            kpos = kb * KEY_BLK + kio
            mask = (kpos <= t_row) & (kpos > t_row - WINDOW)
            return _flash_step(kw_ref[kb], vwT_ref[kb], q_heads, mask, carry[0], carry[1], acc_ref, True)

        acc_ref[...] = jnp.zeros_like(acc_ref)
        w_lo = jnp.maximum(t0 - (WINDOW - 1), 0) // KEY_BLK
        _, l = lax.fori_loop(w_lo, n_kb, win_step, (m0, l0))
        owT = acc_ref[...] * (1.0 / jnp.maximum(l, 1e-30))

        def gate(c):
            return jnp.concatenate([gT[3 * (NSA_HPG * g + h) + c:3 * (NSA_HPG * g + h) + c + 1, :]
                                    for h in range(NSA_HPG)], axis=1)

        oT = gate(0) * ocT + gate(1) * osT + gate(2) * owT
        _store_heads(o_ref, oT, NSA_HPG * g, NSA_HPG, g)


def _block_diag2(w):
    w3 = w.reshape(CMP_BLK, HEAD_DIM, HEAD_DIM)
    z = jnp.zeros_like(w3)
    return jnp.concatenate([jnp.concatenate([w3, z], axis=2), jnp.concatenate([z, w3], axis=2)], axis=1).astype(BF16)


def nsa_prompt(qp, rows, win, gl, pe_k, pe_v, w_ck, w_cv):
    B, T, _ = rows.shape
    assert T % (CMP_STRIDE * LANE) == 0 and T % KEY_BLK == 0
    tq = ATT_TQ
    nq = NSA_HPG * tq
    n_half = T // CMP_STRIDE
    n_kb = T // KEY_BLK
    const = lambda shape: pl.BlockSpec(shape, lambda b, i: (0,) * len(shape))
    return pl.pallas_call(
        functools.partial(_nsa_body, T=T),
        grid=(B, T // tq),
        in_specs=[pl.BlockSpec((1, tq, NSA_HEADS * LANE), lambda b, i: (b, i, 0)),
                  pl.BlockSpec((1, T, 4 * LANE), lambda b, i: (b, 0, 0)),
                  pl.BlockSpec((1, T, 2 * LANE), lambda b, i: (b, 0, 0)),
                  pl.BlockSpec((1, tq, LANE), lambda b, i: (b, i, 0)),
                  const((CMP_BLK, LANE)), const((CMP_BLK, LANE)),
                  const((CMP_BLK, LANE, LANE)), const((CMP_BLK, LANE, LANE))],
        out_specs=pl.BlockSpec((1, tq, NSA_HEADS * LANE), lambda b, i: (b, i, 0)),
        out_shape=jax.ShapeDtypeStruct((B, T, NSA_HEADS * LANE), BF16),
        scratch_shapes=[pltpu.VMEM((n_half, LANE), BF16), pltpu.VMEM((LANE, n_half), BF16),
                        pltpu.VMEM((n_kb, KEY_BLK, LANE), BF16), pltpu.VMEM((n_kb, LANE, KEY_BLK), BF16),
                        pltpu.VMEM((n_kb, KEY_BLK, LANE), BF16), pltpu.VMEM((n_kb, LANE, KEY_BLK), BF16),
                        pltpu.VMEM((T, LANE), F32),
                        pltpu.VMEM((n_half + 8, LANE), F32), pltpu.VMEM((T // SEL_BLK, tq), F32),
                        pltpu.VMEM((LANE, nq), F32)],
        compiler_params=_params("parallel", "arbitrary"),
        name="nsa_prompt",
    )(qp, rows, win, gl, jnp.tile(pe_k, (1, 2)), jnp.tile(pe_v, (1, 2)), _block_diag2(w_ck), _block_diag2(w_cv))


def _dsa_body(qp_ref, kv_ref, qi_ref, ki_ref, wi_ref, o_ref,
              k_ref, vT_ref, kilo_ref, kihi_ref, key_ref, acc_ref, *, T, n_keep):
    i = pl.program_id(1)
    tq = ATT_TQ
    nq = DSA_HPG * tq

    @pl.when(i == 0)
    def _prepare():
        def copy_blk(kb, carry):
            for half in range(KEY_BLK // LANE):
                rs = pl.ds(pl.multiple_of(kb * KEY_BLK + half * LANE, LANE), LANE)
                hs = slice(half * LANE, (half + 1) * LANE)
                for slab in range(2):
                    k_ref[slab, kb, hs, :] = kv_ref[0, rs, slab * LANE:(slab + 1) * LANE].astype(BF16)
                    vT_ref[slab, kb, :, hs] = kv_ref[0, rs, (2 + slab) * LANE:(3 + slab) * LANE].T.astype(BF16)
                kix = ki_ref[0, rs, :]
                kilo_ref[kb, hs, :] = kix.astype(BF16)
                kihi_ref[kb, hs, :] = pltpu.roll(kix, IDX_DIM, 1).astype(BF16)
            return carry

        lax.fori_loop(0, T // KEY_BLK, copy_blk, 0)

    t0 = i * tq
    n_kb = (t0 + tq + KEY_BLK - 1) // KEY_BLK
    t_row = t0 + lax.broadcasted_iota(jnp.int32, (1, tq), 1)
    kio_q = lax.broadcasted_iota(jnp.int32, (KEY_BLK, tq), 0)

    wT = wi_ref[0].T * (IDX_HEADS ** -0.5)
    qi = _stack_heads(qi_ref, 0, IDX_HEADS // 2)

    def idx_step(kb, carry):
        s_lo = _nt(kilo_ref[kb], qi)
        s_hi = _nt(kihi_ref[kb], qi)
        sc = jnp.zeros((KEY_BLK, tq), F32)
        for p in range(IDX_HEADS // 2):
            cs = slice(p * tq, (p + 1) * tq)
            sc = sc + jnp.maximum(s_lo[:, cs], 0.0) * wT[2 * p:2 * p + 1, :]
            sc = sc + jnp.maximum(s_hi[:, cs], 0.0) * wT[2 * p + 1:2 * p + 2, :]
        sc = jnp.where(kb * KEY_BLK + kio_q <= t_row, sc, NEG)
        bits = lax.bitcast_convert_type(sc, jnp.int32)
        key_ref[kb] = jnp.where(bits < 0, bits ^ 0x7FFFFFFF, bits)
        return carry

    lax.fori_loop(0, n_kb, idx_step, 0)

    def count(pred):
        step = lambda kb, c: c + jnp.sum(pred(key_ref[kb]).astype(jnp.int32), axis=0, keepdims=True)
        return lax.fori_loop(0, n_kb, step, jnp.zeros((1, tq), jnp.int32))

    def bit_step(it, carry):
        theta, n_ge = carry
        cand = theta + lax.shift_left(jnp.int32(1), 31 - it)
        cnt = count(lambda k: k >= cand)
        ok = cnt >= n_keep
        return jnp.where(ok, cand, theta), jnp.where(ok, cnt, n_ge)

    theta, n_ge = lax.fori_loop(0, 32, bit_step, (jnp.full((1, tq), INT_MIN, jnp.int32),
                                                  jnp.full((1, tq), n_kb * KEY_BLK, jnp.int32)))

    @pl.when(jnp.max(n_ge) > n_keep)
    def _break_ties():
        need = (n_keep - count(lambda k: k > theta)).astype(F32)
        before = (lax.broadcasted_iota(jnp.int32, (KEY_BLK, KEY_BLK), 1)
                  < lax.broadcasted_iota(jnp.int32, (KEY_BLK, KEY_BLK), 0)).astype(BF16)

        def fix_step(kb, seen):
            keys = key_ref[kb]
            eq = keys == theta
            eqf = jnp.where(eq, 1.0, 0.0)
            rank = seen + jnp.dot(before, eqf.astype(BF16), preferred_element_type=F32)
            key_ref[kb] = jnp.where(eq & (rank >= need), INT_MIN, keys)
            return seen + jnp.sum(eqf, axis=0, keepdims=True)

        lax.fori_loop(0, n_kb, fix_step, jnp.zeros((1, tq), F32))

    q_heads = [_head_slabs(qp_ref, DSA_HPG * g, DSA_HPG) for g in range(DSA_KV)]
    m0 = jnp.full((1, nq), M_FLOOR, F32)
    l0 = jnp.zeros((1, nq), F32)
    acc_ref[...] = jnp.zeros_like(acc_ref)

    def att_step(kb, carry, causal):
        mask = key_ref[kb] >= theta
        if causal:
            mask = mask & (kb * KEY_BLK + kio_q <= t_row)
        out = []
        for g in range(DSA_KV):
            m, l = _flash_step(k_ref[g // 2, kb], vT_ref[g // 2, kb], q_heads[g], mask,
                               carry[2 * g], carry[2 * g + 1], acc_ref.at[g], False)
            out += [m, l]
        return tuple(out)

    mid = lax.fori_loop(0, t0 // KEY_BLK, functools.partial(att_step, causal=False), (m0, l0) * DSA_KV)
    fin = lax.fori_loop(t0 // KEY_BLK, n_kb, functools.partial(att_step, causal=True), mid)
    for g in range(DSA_KV):
        oT = acc_ref[g] * (1.0 / jnp.maximum(fin[2 * g + 1], 1e-30))
        _store_heads(o_ref, oT, DSA_HPG * g, DSA_HPG, g % 2)


def dsa_prompt(qp, kv, qi, ki, wi):
    B, T, _ = kv.shape
    assert T % KEY_BLK == 0
    tq = ATT_TQ
    nq = DSA_HPG * tq
    n_kb = T // KEY_BLK
    n_keep = min(IDX_TOPK, T // 4)
    tile = lambda w: pl.BlockSpec((1, tq, w), lambda b, i: (b, i, 0))
    whole = lambda w: pl.BlockSpec((1, T, w), lambda b, i: (b, 0, 0))
    return pl.pallas_call(
        functools.partial(_dsa_body, T=T, n_keep=n_keep),
        grid=(B, T // tq),
        in_specs=[tile(DSA_HEADS * LANE), whole(4 * LANE), tile(IDX_QW), whole(LANE), tile(LANE)],
        out_specs=tile(DSA_HEADS * LANE),
        out_shape=jax.ShapeDtypeStruct((B, T, DSA_HEADS * LANE), BF16),
        scratch_shapes=[pltpu.VMEM((2, n_kb, KEY_BLK, LANE), BF16), pltpu.VMEM((2, n_kb, LANE, KEY_BLK), BF16),
                        pltpu.VMEM((n_kb, KEY_BLK, LANE), BF16), pltpu.VMEM((n_kb, KEY_BLK, LANE), BF16),
                        pltpu.VMEM((n_kb, KEY_BLK, tq), jnp.int32), pltpu.VMEM((DSA_KV, LANE, nq), F32)],
        compiler_params=_params("parallel", "arbitrary"),
        name="dsa_prompt",
    )(qp, kv, qi, ki, wi)


def _pad_head_cols(w, slots):
    D = w.shape[0]
    H = len(slots)
    w3 = w.reshape(D, H, HEAD_DIM)
    z = jnp.zeros_like(w3)
    s = jnp.asarray(slots)[None, :, None]
    return jnp.concatenate([jnp.where(s == 0, w3, z), jnp.where(s == 1, w3, z)], axis=2).reshape(D, H * LANE)


def _pad_cols(w, n):
    return jnp.pad(w, ((0, 0), (0, n - w.shape[1])))


def _dot01_r(x, m01):
    hi, mid, lo = _split3(x)
    d = lambda p: jnp.dot(p, m01, preferred_element_type=F32)
    return d(hi) + d(mid) + d(lo)


def _softmax_rows(s, mask):
    s = jnp.where(mask, s, NEG)
    e = jnp.where(mask, jnp.exp2(s - jnp.max(s, axis=1, keepdims=True)), 0.0)
    return e * (1.0 / jnp.maximum(jnp.sum(e, axis=1, keepdims=True), 1e-30))


def _flash_rows(s, mask, m_ref, l_ref):
    s = jnp.where(mask, s, NEG)
    m = m_ref[...]
    m_new = jnp.maximum(m, jnp.max(s, axis=1, keepdims=True))
    alpha = jnp.exp2(m - m_new)
    e = jnp.where(mask, jnp.exp2(s - m_new), 0.0)
    l_ref[...] = alpha * l_ref[...] + jnp.sum(e, axis=1, keepdims=True)
    m_ref[...] = m_new
    return alpha, e


def _sort_key(x):
    bits = lax.bitcast_convert_type(x, jnp.int32)
    return jnp.where(bits < 0, bits ^ 0x7FFFFFFF, bits)


def _page_spec(layer, k, chans, blk):
    return pl.BlockSpec((None, 1, chans, PAGE_SIZE), lambda b, s, pt: (layer, pt[b, s * PG_STEP + k], blk, 0))


def _channel_major(cache):
    layers, n, rows = cache.shape[:3]
    return jnp.moveaxis(cache.reshape(layers, n, rows, -1), 2, 3)


def _per_request(shape):
    return pl.BlockSpec((1,) + shape, lambda b, s, pt: (b,) + (0,) * len(shape))


def _nsa_cmp_body(pt_ref, *refs):
    pages = refs[:PG_STEP]
    pek_ref, pev_ref, wck_ref, wcv_ref, o_ref, stage_ref = refs[PG_STEP:]
    n_half = PG_STEP * PAGE_SIZE // CMP_STRIDE
    for slab, pe_ref, w_ref in ((0, pek_ref, wck_ref), (1, pev_ref, wcv_ref)):
        for pg in range(PG_STEP):
            stage_ref[pg * PAGE_SIZE:(pg + 1) * PAGE_SIZE, :] = pages[pg][0, slab * LANE:(slab + 1) * LANE, :].T
        first = jnp.zeros((n_half, LANE), F32)
        second = jnp.zeros((n_half, LANE), F32)
        for r in range(CMP_STRIDE):
            y = stage_ref[pl.ds(r, n_half, stride=CMP_STRIDE), :]
            first = first + jnp.dot((y + pe_ref[r:r + 1, :]).astype(BF16), w_ref[r], preferred_element_type=F32)
            second = second + jnp.dot((y + pe_ref[r + CMP_STRIDE:r + CMP_STRIDE + 1, :]).astype(BF16),
                                      w_ref[r + CMP_STRIDE], preferred_element_type=F32)
        o_ref[0, :, (2 * slab) * LANE:(2 * slab + 1) * LANE] = first
        o_ref[0, :, (2 * slab + 1) * LANE:(2 * slab + 2) * LANE] = second


def nsa_sample_compress(cache, layer, page_table, pe_k, pe_v, w_ck, w_cv):
    B, n_pages = page_table.shape
    assert n_pages % PG_STEP == 0
    n_half_step = PG_STEP * PAGE_SIZE // CMP_STRIDE
    const = lambda shape: pl.BlockSpec(shape, lambda b, s, pt: (0,) * len(shape))
    grid_spec = pltpu.PrefetchScalarGridSpec(
        num_scalar_prefetch=1,
        grid=(B, n_pages // PG_STEP),
        in_specs=[_page_spec(layer, k, 2 * LANE, 0) for k in range(PG_STEP)]
        + [const((CMP_BLK, LANE)), const((CMP_BLK, LANE)), const((CMP_BLK, LANE, LANE)), const((CMP_BLK, LANE, LANE))],
        out_specs=pl.BlockSpec((1, n_half_step, 4 * LANE), lambda b, s, pt: (b, s, 0)),
        scratch_shapes=[pltpu.VMEM((PG_STEP * PAGE_SIZE, LANE), F32)],
    )
    return pl.pallas_call(
        _nsa_cmp_body,
        grid_spec=grid_spec,
        out_shape=jax.ShapeDtypeStruct((B, n_pages * PAGE_SIZE // CMP_STRIDE, 4 * LANE), F32),
        compiler_params=_params("parallel", "arbitrary"),
        name="nsa_sample_compress",
    )(page_table, *([cache] * PG_STEP), jnp.tile(pe_k, (1, 2)), jnp.tile(pe_v, (1, 2)),
      _block_diag2(w_ck), _block_diag2(w_cv))


def _nsa_smp_body(pt_ref, qs_ref, gl_ref, fs_ref, *refs, n_tok):
    pages = refs[:PG_STEP]
    nrow_ref, pwin_ref, nwin_ref, o_ref, shift_ref, selc_ref, oc_ref, m_ref, l_ref, acc_ref = refs[PG_STEP:]
    s_id = pl.program_id(1)
    C = LANE
    n_half = fs_ref.shape[1]
    n_cmp = n_half - 1
    n_selp = n_half * CMP_STRIDE // SEL_BLK
    qs = qs_ref[0]
    crow = lax.broadcasted_iota(jnp.int32, (C, LANE), 0)
    lane = lax.broadcasted_iota(jnp.int32, (C, LANE), 1)
    t_of_c = crow % n_tok

    @pl.when(s_id == 0)
    def _first():
        cmp = []
        for slab in range(2):
            shift_ref[0:n_half, :] = fs_ref[0, :, (2 * slab + 1) * LANE:(2 * slab + 2) * LANE]
            shift_ref[n_half:n_half + 8, :] = jnp.zeros((8, LANE), F32)
            cmp.append((fs_ref[0, :, (2 * slab) * LANE:(2 * slab + 1) * LANE]
                        + shift_ref[1:n_half + 1, :]).astype(BF16))
        nlane = lax.broadcasted_iota(jnp.int32, (C, n_half), 1)
        p = _softmax_rows(_nt(qs, cmp[0]), nlane < n_cmp)
        oc_ref[...] = jnp.dot(p.astype(BF16), cmp[1], preferred_element_type=F32)
        imp_rows = []
        for g in range(NSA_KV):
            acc = p[(NSA_HPG * g) * n_tok:(NSA_HPG * g + 1) * n_tok, :]
            for h in range(1, NSA_HPG):
                acc = acc + p[(NSA_HPG * g + h) * n_tok:(NSA_HPG * g + h + 1) * n_tok, :]
            imp_rows.append(acc)
        imp = jnp.concatenate(imp_rows + [jnp.zeros((C - NSA_KV * n_tok, n_half), F32)], axis=0)
        pi = lax.broadcasted_iota(jnp.int32, (n_half, n_selp), 0)
        pj = lax.broadcasted_iota(jnp.int32, (n_half, n_selp), 1)
        pool = ((((pi + 1) >> 2) == pj) & (pi < n_cmp)).astype(BF16)
        scoreT = _dot01_r(imp, pool).T
        jcol = lax.broadcasted_iota(jnp.int32, (n_selp, C), 0)
        forced = (jcol == 0) | (jcol == n_selp - 1)
        scoreT = scoreT + jnp.where(forced, FORCE_SCORE, 0.0)
        rank = jnp.zeros((n_selp, C), jnp.int32)
        for j2 in range(n_selp):
            r = scoreT[j2:j2 + 1, :]
            beats = (r > scoreT) | ((r == scoreT) & (jcol > j2))
            rank = rank + beats.astype(jnp.int32)
        sel = jnp.where(rank < TOP_N - 1, 1.0, 0.0).T
        selc_ref[...] = jnp.concatenate(
            [sel[(hh // NSA_HPG) * n_tok:(hh // NSA_HPG + 1) * n_tok, :] for hh in range(NSA_HEADS)]
            + [jnp.zeros((C - NSA_HEADS * n_tok, n_selp), F32)], axis=0)
        m_ref[...] = jnp.full(m_ref.shape, NEG, F32)
        l_ref[...] = jnp.zeros(l_ref.shape, F32)
        acc_ref[...] = jnp.zeros(acc_ref.shape, F32)

    selc = selc_ref[...].astype(BF16)
    n_keys = PG_STEP * PAGE_SIZE
    ej = lax.broadcasted_iota(jnp.int32, (n_selp, n_keys), 0)
    ek = lax.broadcasted_iota(jnp.int32, (n_selp, n_keys), 1)
    expand = (ej == s_id * (n_keys // SEL_BLK) + ek // SEL_BLK).astype(BF16)
    mask = jnp.dot(selc, expand, preferred_element_type=F32) > 0.5
    s = jnp.concatenate([jnp.dot(qs, pages[pg][0, 0:LANE, :].astype(BF16), preferred_element_type=F32)
                         for pg in range(PG_STEP)], axis=1)
    alpha, e = _flash_rows(s, mask, m_ref, l_ref)
    eb = e.astype(BF16)
    pv = _nt(eb[:, 0:PAGE_SIZE], pages[0][0, LANE:2 * LANE, :].astype(BF16))
    for pg in range(1, PG_STEP):
        pv = pv + _nt(eb[:, pg * PAGE_SIZE:(pg + 1) * PAGE_SIZE], pages[pg][0, LANE:2 * LANE, :].astype(BF16))
    acc_ref[...] = alpha * acc_ref[...] + pv

    @pl.when(s_id == pl.num_programs(1) - 1)
    def _last():
        pad = jnp.zeros((LANE - n_tok, LANE), F32)
        new_ok = (lane < n_tok) & (lane <= t_of_c)
        knew = jnp.concatenate([nrow_ref[0, :, 2 * LANE:3 * LANE], pad], axis=0).astype(BF16)
        vnew = jnp.concatenate([nrow_ref[0, :, 3 * LANE:4 * LANE], pad], axis=0).astype(BF16)
        alpha, e = _flash_rows(_nt(qs, knew), new_ok, m_ref, l_ref)
        acc = alpha * acc_ref[...] + jnp.dot(e.astype(BF16), vnew, preferred_element_type=F32)
        o_s = acc * (1.0 / jnp.maximum(l_ref[...], 1e-30))
        n_win = pwin_ref.shape[2]
        kwn = jnp.concatenate([nwin_ref[0, :, 0:LANE], pad], axis=0).astype(BF16)
        vwn = jnp.concatenate([nwin_ref[0, :, LANE:2 * LANE], pad], axis=0).astype(BF16)
        wl = lax.broadcasted_iota(jnp.int32, (C, n_win + LANE), 1)
        tw = lax.broadcasted_iota(jnp.int32, (C, n_win + LANE), 0) % n_tok
        wmask = (((wl < n_win) & ((n_win - wl) + tw < WINDOW))
                 | ((wl >= n_win) & (wl - n_win < n_tok) & (wl - n_win <= tw)))
        s_w = jnp.concatenate([jnp.dot(qs, pwin_ref[0, 0:LANE, :].astype(BF16), preferred_element_type=F32),
                               _nt(qs, kwn)], axis=1)
        pw = _softmax_rows(s_w, wmask).astype(BF16)
        o_w = (_nt(pw[:, 0:n_win], pwin_ref[0, LANE:2 * LANE, :].astype(BF16))
               + jnp.dot(pw[:, n_win:], vwn, preferred_element_type=F32))
        g = jax.nn.sigmoid(gl_ref[0])
        o = g[:, 0:1] * oc_ref[...] + g[:, 1:2] * o_s + g[:, 2:3] * o_w
        valid = (crow < NSA_HEADS * n_tok) & (lane // HEAD_DIM == crow // (n_tok * NSA_HPG))
        o_ref[0] = jnp.where(valid, o, 0.0)


def nsa_sample_attention(qs, gcol, fs, cache, layer, page_table, new_rows, past_win, new_win):
    B, n_pages = page_table.shape
    n_tok = new_rows.shape[1]
    n_half = fs.shape[1]
    n_selp = n_half * CMP_STRIDE // SEL_BLK
    n_win = past_win.shape[3]
    assert n_selp == LANE and NSA_HEADS * n_tok <= LANE and n_tok <= min(SEL_BLK, 8) and n_pages % PG_STEP == 0
    assert n_win % LANE == 0 and n_win <= WINDOW
    grid_spec = pltpu.PrefetchScalarGridSpec(
        num_scalar_prefetch=1,
        grid=(B, n_pages // PG_STEP),
        in_specs=[_per_request((LANE, LANE)), _per_request((LANE, LANE)), _per_request((n_half, 4 * LANE))]
        + [_page_spec(layer, k, 2 * LANE, 1) for k in range(PG_STEP)]
        + [_per_request((n_tok, 4 * LANE)),
           pl.BlockSpec((None, 1, 2 * LANE, n_win), lambda b, s, pt: (layer, b, 0, 0)),
           _per_request((n_tok, 2 * LANE))],
        out_specs=_per_request((LANE, LANE)),
        scratch_shapes=[pltpu.VMEM((n_half + 8, LANE), F32), pltpu.VMEM((LANE, n_selp), F32),
                        pltpu.VMEM((LANE, LANE), F32), pltpu.VMEM((LANE, 1), F32), pltpu.VMEM((LANE, 1), F32),
                        pltpu.VMEM((LANE, LANE), F32)],
    )
    return pl.pallas_call(
        functools.partial(_nsa_smp_body, n_tok=n_tok),
        grid_spec=grid_spec,
        out_shape=jax.ShapeDtypeStruct((B, LANE, LANE), F32),
        compiler_params=_params("parallel", "arbitrary"),
        name="nsa_sample_attention",
    )(page_table, qs, gcol, fs, *([cache] * PG_STEP), new_rows, past_win, new_win)


def _dsa_idx_body(pt_ref, qe_ref, qo_ref, w_ref, *refs, n_tok, n_keep):
    pages = refs[:PG_STEP]
    knew_ref, sc_ref, th_ref = refs[PG_STEP:]
    s_id = pl.program_id(1)
    n_blk = sc_ref.shape[1]
    qe = qe_ref[0]
    qo = qo_ref[0]
    w = w_ref[0] * (IDX_HEADS ** -0.5)
    n_rows = (IDX_HEADS // 2) * n_tok

    def scores(s_e, s_o):
        s = jnp.maximum(s_e, 0.0) * w[:, 0:1] + jnp.maximum(s_o, 0.0) * w[:, 1:2]
        tot = s[0:n_tok, :]
        for p in range(1, IDX_HEADS // 2):
            tot = tot + s[p * n_tok:(p + 1) * n_tok, :]
        return tot

    kT = jnp.concatenate([pages[pg][0] for pg in range(PG_STEP)], axis=1).astype(BF16)
    sc = scores(jnp.dot(qe, kT, preferred_element_type=F32), jnp.dot(qo, kT, preferred_element_type=F32))
    for pg in range(PG_STEP):
        sc_ref[0, s_id * PG_STEP + pg] = sc[:, pg * PAGE_SIZE:(pg + 1) * PAGE_SIZE]

    @pl.when(s_id == pl.num_programs(1) - 1)
    def _last():
        kn = jnp.concatenate([knew_ref[0], jnp.zeros((LANE - n_tok, IDX_DIM), F32)], axis=0).astype(BF16)
        a_i = lax.broadcasted_iota(jnp.int32, (n_tok, LANE), 1)
        t_i = lax.broadcasted_iota(jnp.int32, (n_tok, LANE), 0)
        sn = jnp.where(a_i <= t_i, scores(_nt(qe, kn), _nt(qo, kn)), NEG)
        sc_ref[0, n_blk - 1] = jnp.where(a_i < n_tok, sn, -jnp.inf)
        keys = _sort_key(sc_ref[0])

        def bit_step(it, theta):
            cand = theta + lax.shift_left(jnp.int32(1), 31 - it)
            cnt = jnp.sum(jnp.sum((keys >= cand).astype(jnp.int32), axis=0), axis=1, keepdims=True)
            return jnp.where(cnt >= n_keep, cand, theta)

        theta = lax.fori_loop(0, 32, bit_step, jnp.full((n_tok, 1), INT_MIN, jnp.int32))
        th_ref[0] = jnp.broadcast_to(theta, (n_tok, LANE))

        count = lambda m: jnp.sum(jnp.sum(m.astype(jnp.int32), axis=0), axis=1, keepdims=True)

        @pl.when(jnp.max(count(keys >= theta)) > n_keep)
        def _break_ties():
            need = (n_keep - count(keys > theta)).astype(F32)
            before = (lax.broadcasted_iota(jnp.int32, (LANE, LANE), 0)
                      < lax.broadcasted_iota(jnp.int32, (LANE, LANE), 1)).astype(BF16)

            def fix_step(blk, seen):
                sc = sc_ref[0, blk]
                eq = _sort_key(sc) == theta
                eqf = jnp.where(eq, 1.0, 0.0)
                rank = seen + jnp.dot(eqf.astype(BF16), before, preferred_element_type=F32)
                sc_ref[0, blk] = jnp.where(eq & (rank >= need), -jnp.inf, sc)
                return seen + jnp.sum(eqf, axis=1, keepdims=True)

            lax.fori_loop(0, n_blk, fix_step, jnp.zeros((n_tok, 1), F32))


def dsa_sample_index(qe, qo, wcol, cache_idx, layer, page_table, ki_new):
    B, n_pages = page_table.shape
    n_tok = ki_new.shape[1]
    assert n_tok == 8 and n_pages % PG_STEP == 0
    n_keep = min(IDX_TOPK, (n_pages * PAGE_SIZE + n_tok) // 4)
    grid_spec = pltpu.PrefetchScalarGridSpec(
        num_scalar_prefetch=1,
        grid=(B, n_pages // PG_STEP),
        in_specs=[_per_request((LANE, IDX_DIM)), _per_request((LANE, IDX_DIM)), _per_request((LANE, LANE))]
        + [_page_spec(layer, k, IDX_DIM, 0) for k in range(PG_STEP)] + [_per_request((n_tok, IDX_DIM))],
        out_specs=[_per_request((n_pages + 1, n_tok, LANE)), _per_request((n_tok, LANE))],
    )
    return pl.pallas_call(
        functools.partial(_dsa_idx_body, n_tok=n_tok, n_keep=n_keep),
        grid_spec=grid_spec,
        out_shape=[jax.ShapeDtypeStruct((B, n_pages + 1, n_tok, LANE), F32),
                   jax.ShapeDtypeStruct((B, n_tok, LANE), jnp.int32)],
        compiler_params=_params("parallel", "arbitrary"),
        name="dsa_sample_index",
    )(page_table, qe, qo, wcol, *([cache_idx] * PG_STEP), ki_new)


def _dsa_smp_body(pt_ref, qs_ref, sc_ref, th_ref, *refs, n_tok):
    pages = refs[:PG_STEP]
    kvn_ref, o_ref, m_ref, l_ref, acc_ref = refs[PG_STEP:]
    s_id = pl.program_id(1)
    C = LANE
    qs = qs_ref[0]
    crow = lax.broadcasted_iota(jnp.int32, (C, LANE), 0)
    lane = lax.broadcasted_iota(jnp.int32, (C, LANE), 1)
    low = crow < C // 2
    theta = th_ref[0]

    @pl.when(s_id == 0)
    def _():
        m_ref[...] = jnp.full(m_ref.shape, NEG, F32)
        l_ref[...] = jnp.zeros(l_ref.shape, F32)
        acc_ref[...] = jnp.zeros(acc_ref.shape, F32)

    def step(kvTs, blk0, extra):
        kbs = [kvT.astype(BF16) for kvT in kvTs]
        keep = jnp.concatenate([jnp.where(_sort_key(sc_ref[0, blk0 + j]) >= theta, 1.0, 0.0)
                                for j in range(len(kbs))], axis=1)
        mask = jnp.concatenate([keep] * (C // n_tok), axis=0) > 0.5
        if extra is not None:
            mask = mask & extra
        s_lo = jnp.concatenate([jnp.dot(qs, kb[0:LANE], preferred_element_type=F32) for kb in kbs], axis=1)
        s_hi = jnp.concatenate([jnp.dot(qs, kb[LANE:2 * LANE], preferred_element_type=F32) for kb in kbs], axis=1)
        low_col = lax.broadcasted_iota(jnp.int32, (C, 1), 0) < C // 2
        alpha, e = _flash_rows(jnp.where(low_col, s_lo, s_hi), mask, m_ref, l_ref)
        eb = e.astype(BF16)
        pv_lo = pv_hi = None
        for j, kb in enumerate(kbs):
            ej = eb[:, j * LANE:(j + 1) * LANE]
            a, b = _nt(ej, kb[2 * LANE:3 * LANE]), _nt(ej, kb[3 * LANE:4 * LANE])
            pv_lo, pv_hi = (a, b) if pv_lo is None else (pv_lo + a, pv_hi + b)
        acc_ref[...] = alpha * acc_ref[...] + jnp.where(low, pv_lo, pv_hi)

    step([pages[pg][0] for pg in range(PG_STEP)], s_id * PG_STEP, None)

    @pl.when(s_id == pl.num_programs(1) - 1)
    def _last():
        kvn = jnp.concatenate([kvn_ref[0], jnp.zeros((LANE - n_tok, 4 * LANE), F32)], axis=0)
        kvnT = jnp.concatenate([kvn[:, j * LANE:(j + 1) * LANE].T for j in range(4)], axis=0)
        step([kvnT], sc_ref.shape[1] - 1, (lane < n_tok) & (lane <= crow % n_tok))
        o = acc_ref[...] * (1.0 / jnp.maximum(l_ref[...], 1e-30))
        valid = lane // HEAD_DIM == (crow // (n_tok * DSA_HPG)) % 2
        o_ref[0] = jnp.where(valid, o, 0.0)


def dsa_sample_attention(qs, scores, theta, cache_kv, layer, page_table, kv_new):
    B, n_pages = page_table.shape
    n_tok = kv_new.shape[1]
    assert DSA_HEADS * n_tok == LANE and n_pages % PG_STEP == 0
    grid_spec = pltpu.PrefetchScalarGridSpec(
        num_scalar_prefetch=1,
        grid=(B, n_pages // PG_STEP),
        in_specs=[_per_request((LANE, LANE)), _per_request((n_pages + 1, n_tok, LANE)), _per_request((n_tok, LANE))]
        + [_page_spec(layer, k, 4 * LANE, 0) for k in range(PG_STEP)] + [_per_request((n_tok, 4 * LANE))],
        out_specs=_per_request((LANE, LANE)),
        scratch_shapes=[pltpu.VMEM((LANE, 1), F32), pltpu.VMEM((LANE, 1), F32), pltpu.VMEM((LANE, LANE), F32)],
    )
    return pl.pallas_call(
        functools.partial(_dsa_smp_body, n_tok=n_tok),
        grid_spec=grid_spec,
        out_shape=jax.ShapeDtypeStruct((B, LANE, LANE), F32),
        compiler_params=_params("parallel", "arbitrary"),
        name="dsa_sample_attention",
    )(page_table, qs, scores, theta, *([cache_kv] * PG_STEP), kv_new)


_NSA_SLOTS = tuple(h // NSA_HPG for h in range(NSA_HEADS))
_DSA_SLOTS = tuple((h // DSA_HPG) % 2 for h in range(DSA_HEADS))
def _even_segs(q_dtype, cm):
    return ((NSA_HEADS * LANE, (1,) * NSA_HEADS, q_dtype, False), (4 * LANE, (1, 0, 1, 0), F32, cm),
            (2 * LANE, (1, 0), F32, False), (LANE, (0,), F32, False),
            (4 * HGRN_KW, (0,) * (4 * HGRN_KW // LANE), F32, False))


def _odd_segs(q_dtype, cm):
    return ((DSA_HEADS * LANE, (1,) * DSA_HEADS, q_dtype, False), (4 * LANE, (1, 1, 0, 0), F32, cm),
            (IDX_QW, (1,) * 4, q_dtype, False), (LANE, (1,), F32, cm), (LANE, (0,), F32, False))


def _even_weights(w_in, w_out):
    c = np.cumsum([0, NSA_QW] + [NSA_KVW] * 6 + [NSA_GW] + [HGRN_KW] * 4)
    w = jnp.concatenate([
        _pad_head_cols(w_in[:, :NSA_QW] * Q_SCALE, _NSA_SLOTS),
        w_in[:, c[1]:c[5]], w_in[:, c[5]:c[7]], _pad_cols(w_in[:, c[7]:c[8]], LANE), w_in[:, c[8]:c[12]]],
        axis=1).astype(BF16)
    w_outs = [_pad_head_cols(w_out[:NSA_QW].T, _NSA_SLOTS).T.astype(BF16), w_out[NSA_QW:].astype(BF16)]
    return w, w_outs


def _odd_weights(w_in, w_out):
    c = np.cumsum([0, DSA_QW, DSA_KVW, DSA_KVW, IDX_QW, IDX_DIM, IDX_HEADS])
    w = jnp.concatenate([
        _pad_head_cols(w_in[:, :DSA_QW] * Q_SCALE, _DSA_SLOTS),
        w_in[:, c[1]:c[3]], w_in[:, c[3]:c[4]] * IDX_DIM ** -0.5,
        _pad_cols(w_in[:, c[4]:c[5]], LANE), _pad_cols(w_in[:, c[5]:c[6]], LANE)], axis=1).astype(BF16)
    return w, [_pad_head_cols(w_out.T, _DSA_SLOTS).T.astype(BF16)]


def even_mixer_prompt(x, scale, shift, cos_t, sin_t, w_in, w_out, pe_k, pe_v, w_ck, w_cv, lb, norm_g, win_len):
    B, T, _ = x.shape
    assert T >= win_len
    w, w_outs = _even_weights(w_in, w_out)
    qp, rows, win, gl, hx, rows_cm = proj_segments(x, scale, shift, cos_t, sin_t, w, _even_segs(BF16, True))
    o_a = nsa_prompt(qp, rows, win, gl, pe_k, pe_v, w_ck, w_cv)
    o_b, s_new = hgrn_mixer(hx, lb, norm_g, jnp.zeros((B, HGRN_HEADS, HGRN_DK, HGRN_DV), F32))
    new_rows = jnp.moveaxis(rows_cm.reshape(B, 4, NSA_KV, HEAD_DIM, T), 4, 1)
    win_state = win[:, T - win_len:].reshape(B, win_len, 2, NSA_KV, HEAD_DIM)
    return [o_a, o_b], w_outs, new_rows, win_state, s_new


def odd_mixer_prompt(x, scale, shift, cos_t, sin_t, w_in, w_out):
    B, T, _ = x.shape
    w, w_outs = _odd_weights(w_in, w_out)
    qp, kv, qi, ki, wi, kv_cm, ki_cm = proj_segments(x, scale, shift, cos_t, sin_t, w, _odd_segs(BF16, True))
    o = dsa_prompt(qp, kv, qi, ki, wi)
    new_kv = jnp.moveaxis(kv_cm.reshape(B, 2, DSA_KV, HEAD_DIM, T), 4, 1)
    return [o], w_outs, new_kv, jnp.moveaxis(ki_cm[:, :IDX_DIM], 2, 1)


def _stack_rows(a, B, n, heads, width):
    s = a.reshape(B, n, heads, width).transpose(0, 2, 1, 3).reshape(B, heads * n, width)
    return jnp.pad(s, ((0, 0), (0, LANE - heads * n), (0, 0)))


def _unstack_rows(o, B, n, heads):
    return o[:, :heads * n].reshape(B, heads, n, LANE).transpose(0, 2, 1, 3).reshape(1, B * n, heads * LANE)


def even_mixer_sample(x, scale, shift, past_len, B, w_in, w_out, pe_k, pe_v, w_ck, w_cv, lb, norm_g,
                      cache, cache_win, state, page_table, layer):
    n = x.shape[1] // B
    win_len = cache_win.shape[2]
    cos_t, sin_t = [jnp.tile(a, (B, 1)) for a in rope_tables(past_len + jnp.arange(n))]
    w, w_outs = _even_weights(w_in, w_out)
    qp, rows, win, gl, hx = proj_segments(x, scale, shift, cos_t, sin_t, w, _even_segs(F32, False))
    rows, win, hx = [a.reshape(B, n, a.shape[-1]) for a in (rows, win, hx)]
    qs = _stack_rows(qp, B, n, NSA_HEADS, LANE).astype(BF16)
    gcol = jnp.pad(_stack_rows(gl[..., :NSA_GW], B, n, NSA_HEADS, 3), ((0, 0), (0, 0), (0, LANE - 3)))
    cache_cm = _channel_major(cache)
    fs = nsa_sample_compress(cache_cm, layer, page_table, pe_k, pe_v, w_ck, w_cv)
    o = nsa_sample_attention(qs, gcol, fs, cache_cm, layer, page_table, rows, _channel_major(cache_win), win)
    o_a = _unstack_rows(o, B, n, NSA_HEADS)
    o_b, s_new = hgrn_mixer(hx, lb, norm_g, state)
    new_rows = rows.reshape(B, n, 4, NSA_KV, HEAD_DIM)
    win_state = jnp.concatenate([cache_win[layer], win.reshape(B, n, 2, NSA_KV, HEAD_DIM)], axis=1)[:, -win_len:]
    return [o_a, o_b.reshape(1, B * n, HGRN_VW)], w_outs, new_rows, win_state, s_new


def odd_mixer_sample(x, scale, shift, past_len, B, w_in, w_out, cache_kv, cache_idx, page_table, layer):
    n = x.shape[1] // B
    cos_t, sin_t = [jnp.tile(a, (B, 1)) for a in rope_tables(past_len + jnp.arange(n))]
    w, w_outs = _odd_weights(w_in, w_out)
    qp, kv, qi, ki, wi = proj_segments(x, scale, shift, cos_t, sin_t, w, _odd_segs(F32, False))
    kv, ki = kv.reshape(B, n, 4 * LANE), ki.reshape(B, n, LANE)[..., :IDX_DIM]
    qs = _stack_rows(qp, B, n, DSA_HEADS, LANE).astype(BF16)
    qi4 = qi.reshape(1, B * n, IDX_HEADS // 2, 2, IDX_DIM)
    qe = _stack_rows(qi4[:, :, :, 0], B, n, IDX_HEADS // 2, IDX_DIM).astype(BF16)
    qo = _stack_rows(qi4[:, :, :, 1], B, n, IDX_HEADS // 2, IDX_DIM).astype(BF16)
    wcol = jnp.pad(_stack_rows(wi[..., :IDX_HEADS], B, n, IDX_HEADS // 2, 2), ((0, 0), (0, 0), (0, LANE - 2)))
    scores, theta = dsa_sample_index(qe, qo, wcol, _channel_major(cache_idx), layer, page_table, ki)
    o = dsa_sample_attention(qs, scores, theta, _channel_major(cache_kv), layer, page_table, kv)
    return [_unstack_rows(o, B, n, DSA_HEADS)], w_outs, kv.reshape(B, n, 2, DSA_KV, HEAD_DIM), ki


def moe_ffn_residual(xs, hs, logits_list, gates, w1, w3, w2):
    D = D_MODEL
    h_all = jnp.concatenate([h.reshape(-1, D) for h in hs], axis=0)
    logits = jnp.concatenate([lg.reshape(-1, lg.shape[-1])[:, :N_EXPERTS] for lg in logits_list], axis=0)
    n_tok = h_all.shape[0]
    top_v, top_i = lax.top_k(logits, TOP_K)
    weights = jax.nn.softmax(top_v, axis=-1)
    e_flat = top_i.reshape(-1)
    onehot = (e_flat[:, None] == jnp.arange(N_EXPERTS)[None, :]).astype(jnp.int32)
    csum = jnp.cumsum(onehot, axis=0)
    counts = csum[-1]
    rank = jnp.take_along_axis(csum, e_flat[:, None], axis=1)[:, 0] - 1
    padded = ((counts + MOE_TILE - 1) // MOE_TILE) * MOE_TILE
    group_end = jnp.cumsum(padded)
    group_start = group_end - padded
    slot = group_start[e_flat] + rank
    n_slots = _round_up(n_tok * TOP_K, MOE_TILE) + N_EXPERTS * MOE_TILE
    n_tiles = n_slots // MOE_TILE
    tok_of_slot = jnp.zeros((n_slots,), jnp.int32).at[slot].set(jnp.arange(n_tok * TOP_K, dtype=jnp.int32) // TOP_K)
    tile_start = jnp.arange(n_tiles, dtype=jnp.int32) * MOE_TILE
    tile_expert = jnp.minimum(jnp.sum(tile_start[:, None] >= group_end[None, :], axis=1), N_EXPERTS - 1).astype(jnp.int32)
    n_used = (group_end[-1] // MOE_TILE).astype(jnp.int32).reshape(1)
    h_sorted = h_all[tok_of_slot]
    y_slot = moe_grouped_swiglu(h_sorted, tile_expert, n_used, w1, w3, w2)
    slot2 = slot.reshape(n_tok, TOP_K)
    outs = []
    off = 0
    for x, g in zip(xs, gates):
        n = x.shape[0] * x.shape[1]
        ya = y_slot[slot2[off:off + n, 0]].reshape(x.shape)
        yb = y_slot[slot2[off:off + n, 1]].reshape(x.shape)
        w = weights[off:off + n].reshape(x.shape[0], x.shape[1], TOP_K)
        outs.append(moe_combine_residual(x, g, w, ya, yb))
        off += n
    return outs


def kernel(x_prompt, x_sample, cache_nsa, cache_nsa_win, state_hgrn, cache_dsa_kv, cache_dsa_idx, page_table, c_prompt, c_sample, ada_w, ada_b, norm1_g, norm2_g, final_g, even_w_in, even_w_out, nsa_pe_k, nsa_pe_v, nsa_w_ck, nsa_w_cv, hgrn_lb_raw, hgrn_norm_g, ffn_w1, ffn_w3, ffn_w2, odd_w_in, odd_w_out, router_w, router_b, moe_w1, moe_w3, moe_w2):
    D = D_MODEL
    past_len = page_table.shape[1] * PAGE_SIZE
    win_len = cache_nsa_win.shape[2]
    Bp, Tp = x_prompt.shape[:2]
    Bs, Ts = x_sample.shape[:2]
    lb_soft = jax.nn.softmax(hgrn_lb_raw.astype(F32), axis=0)
    lower_bounds = jnp.cumsum(lb_soft, axis=0) - lb_soft[0]

    R = _round_up(Bp + Bs, 8)
    c_all = jnp.zeros((R, D), F32).at[:Bp].set(c_prompt).at[Bp:Bp + Bs].set(c_sample)
    mods = ada_modulation(c_all, ada_w, ada_b)

    def group_mods(l, lo, n, per_token_rows):
        m = mods[l, lo:lo + n].reshape(n, 6, D)
        sh1, sc1, g1, sh2, sc2, g2 = [m[:, j] for j in range(6)]
        s1 = norm1_g[l][None] * (1.0 + sc1)
        s2 = norm2_g[l][None] * (1.0 + sc2)
        vecs = [s1, sh1, g1, s2, sh2, g2]
        if per_token_rows:
            return [jnp.repeat(v, per_token_rows, axis=0)[None] for v in vecs]
        return [v[:, None, :] for v in vecs]

    def pad_cols(w, n):
        return jnp.pad(w, ((0, 0), (0, n - w.shape[1]))).astype(BF16)

    xp = x_prompt
    xs = x_sample.reshape(1, Bs * Ts, D)
    cos_p, sin_p = rope_tables(jnp.arange(Tp))
    outs_p = dict(rows=[], win=[], st=[], kv=[], idx=[])
    outs_s = dict(rows=[], win=[], st=[], kv=[], idx=[])
    for l in range(DEPTH):
        i = l // 2
        mp = group_mods(l, 0, Bp, 0)
        msm = group_mods(l, Bp, Bs, Ts)
        if l % 2 == 0:
            w1, w3, w2 = ffn_w1[i].astype(BF16), ffn_w3[i].astype(BF16), ffn_w2[i].astype(BF16)
            mixed, w_outs, rows, win, s_new = even_mixer_prompt(
                xp, mp[0], mp[1], cos_p, sin_p, even_w_in[i], even_w_out[i], nsa_pe_k[i], nsa_pe_v[i],
                nsa_w_ck[i], nsa_w_cv[i], lower_bounds[i], hgrn_norm_g[i], win_len)
            outs_p['rows'].append(rows)
            outs_p['win'].append(win)
            outs_p['st'].append(s_new)
            xp = out_proj_residual(mixed, w_outs, xp, mp[2])
            xp = ffn_residual(xp, mp[3], mp[4], mp[5], w1, w3, w2)
            mixed, w_outs, rows, win, s_new = even_mixer_sample(
                xs, msm[0], msm[1], past_len, Bs, even_w_in[i], even_w_out[i], nsa_pe_k[i], nsa_pe_v[i],
                nsa_w_ck[i], nsa_w_cv[i], lower_bounds[i], hgrn_norm_g[i],
                cache_nsa, cache_nsa_win, state_hgrn[i], page_table, i)
            outs_s['rows'].append(rows)
            outs_s['win'].append(win)
            outs_s['st'].append(s_new)
            xs = out_proj_residual(mixed, w_outs, xs, msm[2])
            xs = ffn_residual(xs, msm[3], msm[4], msm[5], w1, w3, w2)
        else:
            w_r = pad_cols(router_w[i], LANE)
            w1, w3, w2 = moe_w1[i].astype(BF16), moe_w3[i].astype(BF16), moe_w2[i].astype(BF16)
            o, w_outs, kv, ki = odd_mixer_prompt(xp, mp[0], mp[1], cos_p, sin_p, odd_w_in[i], odd_w_out[i])
            outs_p['kv'].append(kv)
            outs_p['idx'].append(ki)
            xp = out_proj_residual(o, w_outs, xp, mp[2])
            o, w_outs, kv, ki = odd_mixer_sample(xs, msm[0], msm[1], past_len, Bs, odd_w_in[i], odd_w_out[i],
                                                 cache_dsa_kv, cache_dsa_idx, page_table, i)
            outs_s['kv'].append(kv)
            outs_s['idx'].append(ki)
            xs = out_proj_residual(o, w_outs, xs, msm[2])
            hs, lgs = [], []
            for (x, m) in ((xp, mp), (xs, msm)):
                logits, h = norm_proj(x, m[3], m[4], w_r, with_h=True)
                lgs.append(logits[..., :N_EXPERTS] + router_b[i].astype(F32))
                hs.append(h)
            xp, xs = moe_ffn_residual([xp, xs], hs, lgs, [mp[5], msm[5]], w1, w3, w2)
    y_prompt = final_norm(xp, final_g)
    y_sample = final_norm(xs, final_g).reshape(Bs, Ts, D)
    st = lambda od, k: jnp.stack(od[k])
    return (y_prompt, y_sample, st(outs_p, 'rows'), st(outs_s, 'rows'), st(outs_p, 'win'), st(outs_s, 'win'),
            st(outs_p, 'st'), st(outs_s, 'st'), st(outs_p, 'kv'), st(outs_s, 'kv'),
            st(outs_p, 'idx'), st(outs_s, 'idx'))
```

```python
import functools
import math

import numpy as np
import jax
import jax.numpy as jnp
from jax import lax
from jax.experimental import pallas as pl
from jax.experimental.pallas import tpu as pltpu

D_MODEL = 1024
DEPTH = 4
PAGE_SIZE = 128
HEAD_DIM = 64
NSA_HEADS = 8
NSA_KV = 2
NSA_HPG = NSA_HEADS // NSA_KV
CMP_STRIDE = 16
CMP_BLK = 2 * CMP_STRIDE
SEL_BLK = 64
TOP_N = 16
WINDOW = 512
FORCE_SCORE = 1.0e4
HGRN_HEADS = 4
HGRN_DK = 128
HGRN_DV = 128
HGRN_CHUNK = 64
DSA_HEADS = 16
DSA_KV = 4
DSA_HPG = DSA_HEADS // DSA_KV
IDX_HEADS = 8
IDX_DIM = 64
IDX_TOPK = 256
D_FF = 2816
N_EXPERTS = 8
TOP_K = 2
Q_BLK = 128
ROPE_THETA = 10000.0
EPS = 1e-6
NEG = -1.0e30
NSA_QW = NSA_HEADS * HEAD_DIM
NSA_KVW = NSA_KV * HEAD_DIM
NSA_GW = NSA_HEADS * 3
HGRN_KW = HGRN_HEADS * HGRN_DK
HGRN_VW = HGRN_HEADS * HGRN_DV
EVEN_IN = NSA_QW + 6 * NSA_KVW + NSA_GW + 2 * HGRN_KW + 2 * HGRN_VW
EVEN_MIX = NSA_QW + HGRN_VW
DSA_QW = DSA_HEADS * HEAD_DIM
DSA_KVW = DSA_KV * HEAD_DIM
IDX_QW = IDX_HEADS * IDX_DIM
ODD_IN = DSA_QW + 2 * DSA_KVW + IDX_QW + IDX_DIM + IDX_HEADS
ODD_MIX = DSA_QW

LANE = 128
ROW_TILE = 512
FF_CHUNK = 256
MOE_TILE = 256
PROJ_CHUNK = 512
ATT_TQ = 128
KEY_BLK = 256
HGRN_SUB = 32
HGRN_TILE = 256
HGRN_ROWS = 4
PG_STEP = 32
INT_MIN = -2 ** 31
Q_SCALE = HEAD_DIM ** -0.5 * math.log2(math.e)
M_FLOOR = -1.0e25
VMEM_LIMIT = 56 * 1024 * 1024

F32 = jnp.float32
BF16 = jnp.bfloat16


def _round_up(n, m):
    return -(-n // m) * m


def _params(*sem):
    return pltpu.CompilerParams(dimension_semantics=sem, vmem_limit_bytes=VMEM_LIMIT)


def _norm_mod(x, scale, shift):
    ms = jnp.mean(x * x, axis=-1, keepdims=True)
    return x * lax.rsqrt(ms + EPS) * scale + shift


def _mod_spec(mod, tm):
    if mod.shape[1] == 1:
        return pl.BlockSpec((1, 1, mod.shape[2]), lambda b, i: (b, 0, 0))
    return pl.BlockSpec((1, tm, mod.shape[2]), lambda b, i: (b, i, 0))


def _ada_body(c_ref, w_ref, b_ref, o_ref):
    c = c_ref[...]
    cs = (c * jax.nn.sigmoid(c)).astype(BF16)
    o_ref[0] = jnp.dot(cs, w_ref[0].astype(BF16), preferred_element_type=F32) + b_ref[0]


def ada_modulation(c_all, ada_w, ada_b):
    R, D = c_all.shape
    N = ada_w.shape[2]
    tn = 1536
    return pl.pallas_call(
        _ada_body,
        grid=(DEPTH, N // tn),
        in_specs=[pl.BlockSpec((R, D), lambda l, j: (0, 0)),
                  pl.BlockSpec((1, D, tn), lambda l, j: (l, 0, j)),
                  pl.BlockSpec((1, 1, tn), lambda l, j: (l, 0, j))],
        out_specs=pl.BlockSpec((1, R, tn), lambda l, j: (l, 0, j)),
        out_shape=jax.ShapeDtypeStruct((DEPTH, R, N), F32),
        compiler_params=_params("arbitrary", "arbitrary"),
        name="ada_modulation",
    )(c_all, ada_w, ada_b.reshape(DEPTH, 1, N))


def _norm_proj_body(x_ref, sc_ref, sh_ref, w_ref, o_ref, h_ref=None):
    h = _norm_mod(x_ref[0], sc_ref[0], sh_ref[0]).astype(BF16)
    o_ref[0] = jnp.dot(h, w_ref[...], preferred_element_type=F32)
    if h_ref is not None:
        h_ref[0] = h


def norm_proj(x, scale, shift, w, with_h=False):
    B, T, D = x.shape
    N = w.shape[1]
    tm = min(T, ROW_TILE)
    out_shape = [jax.ShapeDtypeStruct((B, T, N), F32)]
    out_specs = [pl.BlockSpec((1, tm, N), lambda b, i: (b, i, 0))]
    if with_h:
        out_shape.append(jax.ShapeDtypeStruct((B, T, D), BF16))
        out_specs.append(pl.BlockSpec((1, tm, D), lambda b, i: (b, i, 0)))
    res = pl.pallas_call(
        _norm_proj_body,
        grid=(B, T // tm),
        in_specs=[pl.BlockSpec((1, tm, D), lambda b, i: (b, i, 0)),
                  _mod_spec(scale, tm), _mod_spec(shift, tm),
                  pl.BlockSpec((D, N), lambda b, i: (0, 0))],
        out_specs=out_specs,
        out_shape=out_shape,
        compiler_params=_params("parallel", "parallel"),
        name="norm_proj",
    )(x, scale, shift, w)
    return res if with_h else res[0]


def _rope_chunk(y, cos, sin):
    lane = lax.broadcasted_iota(jnp.int32, y.shape, 1)
    swapped = jnp.where(lane % HEAD_DIM < HEAD_DIM // 2,
                        pltpu.roll(y, LANE - HEAD_DIM // 2, 1), pltpu.roll(y, HEAD_DIM // 2, 1))
    return y * cos + swapped * sin


def _proj_seg_body(x_ref, sc_ref, sh_ref, cos_ref, sin_ref, w_ref, *o_refs, segs):
    h = _norm_mod(x_ref[0], sc_ref[0], sh_ref[0]).astype(BF16)
    cos = cos_ref[...]
    sin = sin_ref[...]
    t_refs = iter(o_refs[len(segs):])
    c0 = 0
    for o_ref, (width, rope_flags, dtype, channel_major) in zip(o_refs, segs):
        t_ref = next(t_refs) if channel_major else None
        for j0 in range(0, width, PROJ_CHUNK):
            wd = min(PROJ_CHUNK, width - j0)
            y = jnp.dot(h, w_ref[:, c0 + j0:c0 + j0 + wd], preferred_element_type=F32)
            for k in range(wd // LANE):
                cols = slice(j0 + k * LANE, j0 + (k + 1) * LANE)
                yk = y[:, k * LANE:(k + 1) * LANE]
                if rope_flags[(j0 + k * LANE) // LANE]:
                    yk = _rope_chunk(yk, cos, sin)
                o_ref[0, :, cols] = yk.astype(dtype)
                if t_ref is not None:
                    t_ref[0, cols, :] = yk.T
        c0 += width


def proj_segments(x, scale, shift, cos_t, sin_t, w, segs):
    B, T, D = x.shape
    tm = min(T, ROW_TILE)
    N = w.shape[1]
    cm = [s for s in segs if s[3]]
    return pl.pallas_call(
        functools.partial(_proj_seg_body, segs=segs),
        grid=(B, T // tm),
        in_specs=[pl.BlockSpec((1, tm, D), lambda b, i: (b, i, 0)),
                  _mod_spec(scale, tm), _mod_spec(shift, tm),
                  pl.BlockSpec((tm, LANE), lambda b, i: (i, 0)),
                  pl.BlockSpec((tm, LANE), lambda b, i: (i, 0)),
                  pl.BlockSpec((D, N), lambda b, i: (0, 0))],
        out_specs=[pl.BlockSpec((1, tm, s[0]), lambda b, i: (b, i, 0)) for s in segs]
        + [pl.BlockSpec((1, s[0], tm), lambda b, i: (b, 0, i)) for s in cm],
        out_shape=[jax.ShapeDtypeStruct((B, T, s[0]), s[2]) for s in segs]
        + [jax.ShapeDtypeStruct((B, s[0], T), F32) for s in cm],
        compiler_params=_params("parallel", "parallel"),
        name="proj_segments",
    )(x, scale, shift, cos_t, sin_t, w)


def rope_tables(pos):
    half = HEAD_DIM // 2
    inv = ROPE_THETA ** (-jnp.arange(half, dtype=F32) / half)
    ang = pos.astype(F32)[:, None] * inv[None, :]
    cos, sin = jnp.cos(ang), jnp.sin(ang)
    return jnp.tile(cos, (1, 4)), jnp.tile(jnp.concatenate([-sin, sin], axis=1), (1, 2))


def _out_proj_body(*refs, n_in):
    a_refs, w_refs = refs[:n_in], refs[n_in:2 * n_in]
    x_ref, g_ref, o_ref = refs[2 * n_in:]
    y = None
    for a_ref, w_ref in zip(a_refs, w_refs):
        part = jnp.dot(a_ref[0].astype(BF16), w_ref[...], preferred_element_type=F32)
        y = part if y is None else y + part
    o_ref[0] = x_ref[0] + g_ref[0] * y


def out_proj_residual(a_list, w_list, x, gate):
    B, T, D = x.shape
    tm = min(T, ROW_TILE)
    n_in = len(a_list)
    return pl.pallas_call(
        functools.partial(_out_proj_body, n_in=n_in),
        grid=(B, T // tm),
        in_specs=[pl.BlockSpec((1, tm, a.shape[2]), lambda b, i: (b, i, 0)) for a in a_list]
        + [pl.BlockSpec(w.shape, lambda b, i: (0, 0)) for w in w_list]
        + [pl.BlockSpec((1, tm, D), lambda b, i: (b, i, 0)), _mod_spec(gate, tm)],
        out_specs=pl.BlockSpec((1, tm, D), lambda b, i: (b, i, 0)),
        out_shape=jax.ShapeDtypeStruct((B, T, D), F32),
        compiler_params=_params("parallel", "parallel"),
        name="out_proj_residual",
    )(*a_list, *w_list, x, gate)


def _swiglu_acc(h, w1_ref, w3_ref, w2_ref, acc_ref):
    for c in range(D_FF // FF_CHUNK):
        cols = slice(c * FF_CHUNK, (c + 1) * FF_CHUNK)
        u = jnp.dot(h, w1_ref[:, cols], preferred_element_type=F32)
        v = jnp.dot(h, w3_ref[:, cols], preferred_element_type=F32)
        a = (u * jax.nn.sigmoid(u) * v).astype(BF16)
        part = jnp.dot(a, w2_ref[cols, :], preferred_element_type=F32)
        if c == 0:
            acc_ref[...] = part
        else:
            acc_ref[...] += part


def _ffn_body(x_ref, sc_ref, sh_ref, g_ref, w1_ref, w3_ref, w2_ref, o_ref, acc_ref):
    x = x_ref[0]
    h = _norm_mod(x, sc_ref[0], sh_ref[0]).astype(BF16)
    _swiglu_acc(h, w1_ref, w3_ref, w2_ref, acc_ref)
    o_ref[0] = x + g_ref[0] * acc_ref[...]


def ffn_residual(x, scale, shift, gate, w1, w3, w2):
    B, T, D = x.shape
    tm = min(T, ROW_TILE)
    wspec = lambda shape: pl.BlockSpec(shape, lambda b, i: (0, 0))
    return pl.pallas_call(
        _ffn_body,
        grid=(B, T // tm),
        in_specs=[pl.BlockSpec((1, tm, D), lambda b, i: (b, i, 0)),
                  _mod_spec(scale, tm), _mod_spec(shift, tm), _mod_spec(gate, tm),
                  wspec((D, D_FF)), wspec((D, D_FF)), wspec((D_FF, D))],
        out_specs=pl.BlockSpec((1, tm, D), lambda b, i: (b, i, 0)),
        out_shape=jax.ShapeDtypeStruct((B, T, D), F32),
        scratch_shapes=[pltpu.VMEM((tm, D), F32)],
        compiler_params=_params("parallel", "parallel"),
        name="ffn_residual",
    )(x, scale, shift, gate, w1, w3, w2)


def _moe_body(te_ref, nt_ref, h_ref, w1_ref, w3_ref, w2_ref, o_ref, acc_ref):
    i = pl.program_id(0)

    @pl.when(i < nt_ref[0])
    def _():
        _swiglu_acc(h_ref[...], w1_ref.at[0], w3_ref.at[0], w2_ref.at[0], acc_ref)
        o_ref[...] = acc_ref[...]

    @pl.when(i >= nt_ref[0])
    def _():
        o_ref[...] = jnp.zeros_like(o_ref)


def moe_grouped_swiglu(h_sorted, tile_expert, n_tiles_used, w1, w3, w2):
    S, D = h_sorted.shape
    tm = MOE_TILE
    n_tiles = S // tm
    wspec = lambda shape: pl.BlockSpec((1,) + shape, lambda i, te, nt: (te[i], 0, 0))
    grid_spec = pltpu.PrefetchScalarGridSpec(
        num_scalar_prefetch=2,
        grid=(n_tiles,),
        in_specs=[pl.BlockSpec((tm, D), lambda i, te, nt: (i, 0)),
                  wspec((D, D_FF)), wspec((D, D_FF)), wspec((D_FF, D))],
        out_specs=pl.BlockSpec((tm, D), lambda i, te, nt: (i, 0)),
        scratch_shapes=[pltpu.VMEM((tm, D), F32)],
    )
    return pl.pallas_call(
        _moe_body,
        grid_spec=grid_spec,
        out_shape=jax.ShapeDtypeStruct((S, D), F32),
        compiler_params=_params("arbitrary"),
        name="moe_grouped_swiglu",
    )(tile_expert, n_tiles_used, h_sorted, w1, w3, w2)


def _combine_body(x_ref, g_ref, w_ref, a_ref, b_ref, o_ref):
    w = w_ref[0]
    o_ref[0] = x_ref[0] + g_ref[0] * (w[:, 0:1] * a_ref[0] + w[:, 1:2] * b_ref[0])


def moe_combine_residual(x, gate, w, ya, yb):
    B, T, D = x.shape
    tm = min(T, ROW_TILE)
    tile = pl.BlockSpec((1, tm, D), lambda b, i: (b, i, 0))
    return pl.pallas_call(
        _combine_body,
        grid=(B, T // tm),
        in_specs=[tile, _mod_spec(gate, tm), pl.BlockSpec((1, tm, TOP_K), lambda b, i: (b, i, 0)), tile, tile],
        out_specs=tile,
        out_shape=jax.ShapeDtypeStruct((B, T, D), F32),
        compiler_params=_params("parallel", "parallel"),
        name="moe_combine_residual",
    )(x, gate, w, ya, yb)


def _final_norm_body(x_ref, g_ref, o_ref):
    x = x_ref[0]
    ms = jnp.mean(x * x, axis=-1, keepdims=True)
    o_ref[0] = x * lax.rsqrt(ms + EPS) * g_ref[...]


def final_norm(x, g):
    B, T, D = x.shape
    tm = min(T, ROW_TILE)
    return pl.pallas_call(
        _final_norm_body,
        grid=(B, T // tm),
        in_specs=[pl.BlockSpec((1, tm, D), lambda b, i: (b, i, 0)),
                  pl.BlockSpec((1, D), lambda b, i: (0, 0))],
        out_specs=pl.BlockSpec((1, tm, D), lambda b, i: (b, i, 0)),
        out_shape=jax.ShapeDtypeStruct((B, T, D), F32),
        compiler_params=_params("parallel", "parallel"),
        name="final_norm",
    )(x, g.reshape(1, D))


def _split3(x):
    hi = x.astype(BF16)
    r1 = x - hi.astype(F32)
    mid = r1.astype(BF16)
    lo = (r1 - mid.astype(F32)).astype(BF16)
    return hi, mid, lo


def _dot01(m01, x):
    hi, mid, lo = _split3(x)
    d = lambda p: jnp.dot(m01, p, preferred_element_type=F32)
    return d(hi) + d(mid) + d(lo)


def _hgrn_body(hq_ref, hf_ref, hi_ref, hg_ref, lb_ref, ng_ref, s0_ref, o_ref, sn_ref, st_ref, *, sub, n_sub):
    i = pl.program_id(1)
    nb = hq_ref.shape[0]

    @pl.when(i == 0)
    def _():
        for bb in range(nb):
            for h in range(HGRN_HEADS):
                st_ref[bb, h] = s0_ref[bb, h].T

    lb = lb_ref[...]
    ng = ng_ref[...]
    row = lax.broadcasted_iota(jnp.int32, (sub, sub), 0)
    col = lax.broadcasted_iota(jnp.int32, (sub, sub), 1)
    tril = (row >= col).astype(BF16)
    trow = lax.broadcasted_iota(jnp.int32, (sub, HGRN_DK), 0)

    def chunk(c, carry):
        for bb in range(nb):
            chunk_one(c, bb)
        return carry

    def chunk_one(c, bb):
        rows = pl.ds(pl.multiple_of(c * sub, sub), sub)
        hq = hq_ref[bb, rows, :]
        hf = hf_ref[bb, rows, :]
        hv = hi_ref[bb, rows, :]
        hg = hg_ref[bb, rows, :]
        f = lb + (1.0 - lb) * jax.nn.sigmoid(hf)
        logf = jnp.log(f)
        kk = 1.0 - f
        qq = hq * jax.nn.sigmoid(hq)
        b = _dot01(tril, logf)
        bl = b[sub - 1:sub, :]
        qe = qq * jnp.exp(b)
        ke = kk * jnp.exp(bl - b)
        ebl = jnp.exp(bl)
        outs = []
        for h in range(HGRN_HEADS):
            cs = slice(h * HGRN_DK, (h + 1) * HGRN_DK)
            st = st_ref[bb, h]
            o = lax.dot_general(qe[:, cs].astype(BF16), st.astype(BF16), (((1,), (1,)), ((), ())),
                                preferred_element_type=F32)
            bh, qh, kh, vh = b[:, cs], qq[:, cs], kk[:, cs], hv[:, cs]
            for s in range(sub):
                e = jnp.exp(jnp.minimum(bh - bh[s:s + 1, :], 0.0))
                w = jnp.where(trow >= s, e * qh * kh[s:s + 1, :], 0.0)
                o = o + jnp.sum(w, axis=-1, keepdims=True) * vh[s:s + 1, :]
            upd = lax.dot_general(vh.astype(BF16), ke[:, cs].astype(BF16), (((0,), (0,)), ((), ())),
                                  preferred_element_type=F32)
            st_ref[bb, h] = st * ebl[:, cs] + upd
            ms = jnp.mean(o * o, axis=-1, keepdims=True)
            outs.append(o * lax.rsqrt(ms + EPS))
        o_ref[bb, rows, :] = jnp.concatenate(outs, axis=-1) * ng * (hg * jax.nn.sigmoid(hg))

    lax.fori_loop(0, n_sub, chunk, 0)

    @pl.when(i == pl.num_programs(1) - 1)
    def _():
        for bb in range(nb):
            for h in range(HGRN_HEADS):
                sn_ref[bb, h] = st_ref[bb, h].T


def hgrn_mixer(hx, lb, norm_g, s0):
    B, T, _ = hx.shape
    tc = min(T, HGRN_TILE)
    sub = math.gcd(T, HGRN_SUB)
    nb = math.gcd(B, HGRN_ROWS)
    spec = lambda j: pl.BlockSpec((nb, tc, HGRN_KW), lambda b, i, j=j: (b, i, j))
    vec = pl.BlockSpec((1, HGRN_KW), lambda b, i: (0, 0))
    state = pl.BlockSpec((nb, HGRN_HEADS, HGRN_DK, HGRN_DV), lambda b, i: (b, 0, 0, 0))
    return pl.pallas_call(
        functools.partial(_hgrn_body, sub=sub, n_sub=tc // sub),
        grid=(B // nb, T // tc),
        in_specs=[spec(0), spec(1), spec(2), spec(3), vec, vec, state],
        out_specs=[pl.BlockSpec((nb, tc, HGRN_VW), lambda b, i: (b, i, 0)), state],
        out_shape=[jax.ShapeDtypeStruct((B, T, HGRN_VW), F32),
                   jax.ShapeDtypeStruct((B, HGRN_HEADS, HGRN_DK, HGRN_DV), F32)],
        scratch_shapes=[pltpu.VMEM((nb, HGRN_HEADS, HGRN_DV, HGRN_DK), F32)],
        compiler_params=_params("parallel", "arbitrary"),
        name="hgrn_mixer",
    )(hx, hx, hx, hx, lb.reshape(1, HGRN_KW), jnp.tile(norm_g, HGRN_HEADS).reshape(1, HGRN_VW), s0)


def _nt(a, b):
    return lax.dot_general(a, b, (((1,), (1,)), ((), ())), preferred_element_type=F32)


def _flash_step(kblk, vT, q_heads, mask, m, l, acc_ref, batched):
    tq = q_heads[0].shape[0]
    if batched:
        acc = acc_ref[...]
        scores = [_nt(kblk, qh) for qh in q_heads]
    m_out, l_out, acc_out = [], [], []
    for h, qh in enumerate(q_heads):
        cs = slice(h * tq, (h + 1) * tq)
        s = jnp.where(mask, scores[h] if batched else _nt(kblk, qh), NEG)
        m_new = jnp.maximum(m[:, cs], jnp.max(s, axis=0, keepdims=True))
        alpha = jnp.exp2(m[:, cs] - m_new)
        e = jnp.exp2(s - m_new)
        l_out.append(alpha * l[:, cs] + jnp.sum(e, axis=0, keepdims=True))
        pv = jnp.dot(vT, e.astype(BF16), preferred_element_type=F32)
        if batched:
            acc_out.append(alpha * acc[:, cs] + pv)
        else:
            acc_ref[:, cs] = alpha * acc_ref[:, cs] + pv
        m_out.append(m_new)
    if batched:
        acc_ref[...] = jnp.concatenate(acc_out, axis=1)
    return jnp.concatenate(m_out, axis=1), jnp.concatenate(l_out, axis=1)


def _head_slabs(qp_ref, first_head, n):
    return [qp_ref[0, :, (first_head + h) * LANE:(first_head + h + 1) * LANE] for h in range(n)]


def _stack_heads(qp_ref, first_head, n):
    return jnp.concatenate(_head_slabs(qp_ref, first_head, n), axis=0)


def _store_heads(o_ref, oT, first_head, n, slot):
    tq = oT.shape[1] // n
    lane = lax.broadcasted_iota(jnp.int32, (tq, LANE), 1)
    valid = (lane >= HEAD_DIM * slot) & (lane < HEAD_DIM * (slot + 1))
    for h in range(n):
        blk = oT[:, h * tq:(h + 1) * tq].T
        o_ref[0, :, (first_head + h) * LANE:(first_head + h + 1) * LANE] = jnp.where(valid, blk, 0.0).astype(o_ref.dtype)


def _nsa_body(qp_ref, rows_ref, win_ref, gl_ref, pek_ref, pev_ref, wck_ref, wcv_ref, o_ref,
              kcmp_ref, vcmpT_ref, ks_ref, vsT_ref, kw_ref, vwT_ref, stage_ref, shift_ref, sel_ref, acc_ref, *, T):
    i = pl.program_id(1)
    tq = ATT_TQ
    nq = NSA_HPG * tq
    n_half = T // CMP_STRIDE
    n_sel = T // SEL_BLK
    k_sel = min(TOP_N, n_sel)
    per_kb = KEY_BLK // SEL_BLK

    @pl.when(i == 0)
    def _prepare():
        nrow = lax.broadcasted_iota(jnp.int32, (n_half, LANE), 0)
        for slab, pe_ref, w_ref in ((0, pek_ref, wck_ref), (1, pev_ref, wcv_ref)):
            def stage_blk(kb, carry, slab=slab):
                rs = pl.ds(pl.multiple_of(kb * KEY_BLK, KEY_BLK), KEY_BLK)
                stage_ref[rs, :] = rows_ref[0, rs, slab * LANE:(slab + 1) * LANE]
                return carry

            lax.fori_loop(0, T // KEY_BLK, stage_blk, 0)
            first = jnp.zeros((n_half, LANE), F32)
            second = jnp.zeros((n_half, LANE), F32)
            for r in range(CMP_STRIDE):
                y = stage_ref[pl.ds(r, n_half, stride=CMP_STRIDE), :]
                first = first + jnp.dot((y + pe_ref[r:r + 1, :]).astype(BF16), w_ref[r],
                                        preferred_element_type=F32)
                second = second + jnp.dot((y + pe_ref[r + CMP_STRIDE:r + CMP_STRIDE + 1, :]).astype(BF16),
                                          w_ref[r + CMP_STRIDE], preferred_element_type=F32)
            shift_ref[0:n_half, :] = second
            shift_ref[n_half:n_half + 8, :] = jnp.zeros((8, LANE), F32)
            c = jnp.where(nrow < n_half - 1, first + shift_ref[1:n_half + 1, :], 0.0)
            if slab == 0:
                kcmp_ref[...] = c.astype(BF16)
            else:
                for j in range(n_half // LANE):
                    vcmpT_ref[:, j * LANE:(j + 1) * LANE] = c[j * LANE:(j + 1) * LANE, :].T.astype(BF16)

        def copy_blk(kb, carry):
            for half in range(KEY_BLK // LANE):
                rs = pl.ds(pl.multiple_of(kb * KEY_BLK + half * LANE, LANE), LANE)
                hs = slice(half * LANE, (half + 1) * LANE)
                ks_ref[kb, hs, :] = rows_ref[0, rs, 2 * LANE:3 * LANE].astype(BF16)
                vsT_ref[kb, :, hs] = rows_ref[0, rs, 3 * LANE:4 * LANE].T.astype(BF16)
                kw_ref[kb, hs, :] = win_ref[0, rs, 0:LANE].astype(BF16)
                vwT_ref[kb, :, hs] = win_ref[0, rs, LANE:2 * LANE].T.astype(BF16)
            return carry

        lax.fori_loop(0, T // KEY_BLK, copy_blk, 0)

    t0 = i * tq
    lane_t = t0 + (lax.broadcasted_iota(jnp.int32, (1, nq), 1) & (tq - 1))
    t_row = lane_t[:, 0:tq]
    gT = jax.nn.sigmoid(gl_ref[0].T)
    kio = lax.broadcasted_iota(jnp.int32, (KEY_BLK, tq), 0)
    nio = lax.broadcasted_iota(jnp.int32, (n_half, nq), 0)
    jcol = lax.broadcasted_iota(jnp.int32, (n_sel, tq), 0)
    pj = lax.broadcasted_iota(jnp.int32, (n_sel, n_half), 0)
    pi = lax.broadcasted_iota(jnp.int32, (n_sel, n_half), 1)
    pool = ((((pi + 1) >> 2) == pj) & (pi < n_half - 1)).astype(BF16)
    n_kb = (t0 + tq + KEY_BLK - 1) // KEY_BLK
    n_full = t0 // KEY_BLK
    m0 = jnp.full((1, nq), M_FLOOR, F32)
    l0 = jnp.zeros((1, nq), F32)

    for g in range(NSA_KV):
        q_heads = _head_slabs(qp_ref, NSA_HPG * g, NSA_HPG)
        qs = jnp.concatenate(q_heads, axis=0)
        cmask = (CMP_STRIDE * nio + CMP_BLK <= lane_t + 1) & (nio < n_half - 1)
        s = jnp.where(cmask, _nt(kcmp_ref[...], qs), NEG)
        e = jnp.where(cmask, jnp.exp2(s - jnp.max(s, axis=0, keepdims=True)), 0.0)
        p = e * (1.0 / jnp.maximum(jnp.sum(e, axis=0, keepdims=True), 1e-30))
        ocT = jnp.dot(vcmpT_ref[...], p.astype(BF16), preferred_element_type=F32)
        imp = p[:, 0:tq]
        for h in range(1, NSA_HPG):
            imp = imp + p[:, h * tq:(h + 1) * tq]
        imp_sel = _dot01(pool, imp)
        cur = t_row >> 6
        forced = (jcol == 0) | (jcol == cur) | (jcol == cur - 1)
        visible = jcol <= cur
        score = jnp.where(visible, imp_sel + jnp.where(forced, FORCE_SCORE, 0.0), NEG)
        rank = jnp.zeros((n_sel, tq), jnp.int32)
        for j2 in range(n_sel):
            r = score[j2:j2 + 1, :]
            beats = (r > score) | ((r == score) & (jcol > j2))
            rank = rank + beats.astype(jnp.int32)
        sel_ref[...] = jnp.where((rank < k_sel) & visible, 1.0, 0.0)

        def sel_step(kb, carry, causal):
            rowsel = jnp.concatenate(
                [jnp.broadcast_to(sel_ref[pl.ds(kb * per_kb + k, 1), :], (SEL_BLK, tq)) for k in range(per_kb)],
                axis=0)
            mask = rowsel > 0.5
            if causal:
                mask = mask & (kb * KEY_BLK + kio <= t_row)
            return _flash_step(ks_ref[kb], vsT_ref[kb], q_heads, mask, carry[0], carry[1], acc_ref, True)

        acc_ref[...] = jnp.zeros_like(acc_ref)
        ml = lax.fori_loop(0, n_full, functools.partial(sel_step, causal=False), (m0, l0))
        _, l = lax.fori_loop(n_full, n_kb, functools.partial(sel_step, causal=True), ml)
        osT = acc_ref[...] * (1.0 / jnp.maximum(l, 1e-30))

        def win_step(kb, carry):
            kpos = kb * KEY_BLK + kio
            mask = (kpos <= t_row) & (kpos > t_row - WINDOW)
            return _flash_step(kw_ref[kb], vwT_ref[kb], q_heads, mask, carry[0], carry[1], acc_ref, True)

        acc_ref[...] = jnp.zeros_like(acc_ref)
        w_lo = jnp.maximum(t0 - (WINDOW - 1), 0) // KEY_BLK
        _, l = lax.fori_loop(w_lo, n_kb, win_step, (m0, l0))
        owT = acc_ref[...] * (1.0 / jnp.maximum(l, 1e-30))

        def gate(c):
            return jnp.concatenate([gT[3 * (NSA_HPG * g + h) + c:3 * (NSA_HPG * g + h) + c + 1, :]
                                    for h in range(NSA_HPG)], axis=1)

        oT = gate(0) * ocT + gate(1) * osT + gate(2) * owT
        _store_heads(o_ref, oT, NSA_HPG * g, NSA_HPG, g)


def _block_diag2(w):
    w3 = w.reshape(CMP_BLK, HEAD_DIM, HEAD_DIM)
    z = jnp.zeros_like(w3)
    return jnp.concatenate([jnp.concatenate([w3, z], axis=2), jnp.concatenate([z, w3], axis=2)], axis=1).astype(BF16)


def nsa_prompt(qp, rows, win, gl, pe_k, pe_v, w_ck, w_cv):
    B, T, _ = rows.shape
    assert T % (CMP_STRIDE * LANE) == 0 and T % KEY_BLK == 0
    tq = ATT_TQ
    nq = NSA_HPG * tq
    n_half = T // CMP_STRIDE
    n_kb = T // KEY_BLK
    const = lambda shape: pl.BlockSpec(shape, lambda b, i: (0,) * len(shape))
    return pl.pallas_call(
        functools.partial(_nsa_body, T=T),
        grid=(B, T // tq),
        in_specs=[pl.BlockSpec((1, tq, NSA_HEADS * LANE), lambda b, i: (b, i, 0)),
                  pl.BlockSpec((1, T, 4 * LANE), lambda b, i: (b, 0, 0)),
                  pl.BlockSpec((1, T, 2 * LANE), lambda b, i: (b, 0, 0)),
                  pl.BlockSpec((1, tq, LANE), lambda b, i: (b, i, 0)),
                  const((CMP_BLK, LANE)), const((CMP_BLK, LANE)),
                  const((CMP_BLK, LANE, LANE)), const((CMP_BLK, LANE, LANE))],
        out_specs=pl.BlockSpec((1, tq, NSA_HEADS * LANE), lambda b, i: (b, i, 0)),
        out_shape=jax.ShapeDtypeStruct((B, T, NSA_HEADS * LANE), BF16),
        scratch_shapes=[pltpu.VMEM((n_half, LANE), BF16), pltpu.VMEM((LANE, n_half), BF16),
                        pltpu.VMEM((n_kb, KEY_BLK, LANE), BF16), pltpu.VMEM((n_kb, LANE, KEY_BLK), BF16),
                        pltpu.VMEM((n_kb, KEY_BLK, LANE), BF16), pltpu.VMEM((n_kb, LANE, KEY_BLK), BF16),
                        pltpu.VMEM((T, LANE), F32),
                        pltpu.VMEM((n_half + 8, LANE), F32), pltpu.VMEM((T // SEL_BLK, tq), F32),
                        pltpu.VMEM((LANE, nq), F32)],
        compiler_params=_params("parallel", "arbitrary"),
        name="nsa_prompt",
    )(qp, rows, win, gl, jnp.tile(pe_k, (1, 2)), jnp.tile(pe_v, (1, 2)), _block_diag2(w_ck), _block_diag2(w_cv))


def _dsa_body(qp_ref, kv_ref, qi_ref, ki_ref, wi_ref, o_ref,
              k_ref, vT_ref, kilo_ref, kihi_ref, key_ref, acc_ref, *, T, n_keep):
    i = pl.program_id(1)
    tq = ATT_TQ
    nq = DSA_HPG * tq

    @pl.when(i == 0)
    def _prepare():
        def copy_blk(kb, carry):
            for half in range(KEY_BLK // LANE):
                rs = pl.ds(pl.multiple_of(kb * KEY_BLK + half * LANE, LANE), LANE)
                hs = slice(half * LANE, (half + 1) * LANE)
                for slab in range(2):
                    k_ref[slab, kb, hs, :] = kv_ref[0, rs, slab * LANE:(slab + 1) * LANE].astype(BF16)
                    vT_ref[slab, kb, :, hs] = kv_ref[0, rs, (2 + slab) * LANE:(3 + slab) * LANE].T.astype(BF16)
                kix = ki_ref[0, rs, :]
                kilo_ref[kb, hs, :] = kix.astype(BF16)
                kihi_ref[kb, hs, :] = pltpu.roll(kix, IDX_DIM, 1).astype(BF16)
            return carry

        lax.fori_loop(0, T // KEY_BLK, copy_blk, 0)

    t0 = i * tq
    n_kb = (t0 + tq + KEY_BLK - 1) // KEY_BLK
    t_row = t0 + lax.broadcasted_iota(jnp.int32, (1, tq), 1)
    kio_q = lax.broadcasted_iota(jnp.int32, (KEY_BLK, tq), 0)

    wT = wi_ref[0].T * (IDX_HEADS ** -0.5)
    qi = _stack_heads(qi_ref, 0, IDX_HEADS // 2)

    def idx_step(kb, carry):
        s_lo = _nt(kilo_ref[kb], qi)
        s_hi = _nt(kihi_ref[kb], qi)
        sc = jnp.zeros((KEY_BLK, tq), F32)
        for p in range(IDX_HEADS // 2):
            cs = slice(p * tq, (p + 1) * tq)
            sc = sc + jnp.maximum(s_lo[:, cs], 0.0) * wT[2 * p:2 * p + 1, :]
            sc = sc + jnp.maximum(s_hi[:, cs], 0.0) * wT[2 * p + 1:2 * p + 2, :]
        sc = jnp.where(kb * KEY_BLK + kio_q <= t_row, sc, NEG)
        bits = lax.bitcast_convert_type(sc, jnp.int32)
        key_ref[kb] = jnp.where(bits < 0, bits ^ 0x7FFFFFFF, bits)
        return carry

    lax.fori_loop(0, n_kb, idx_step, 0)

    def bit_step(it, theta):
        cand = theta + lax.shift_left(jnp.int32(1), 31 - it)

        def cnt_step(kb, c):
            return c + jnp.sum((key_ref[kb] >= cand).astype(jnp.int32), axis=0, keepdims=True)

        cnt = lax.fori_loop(0, n_kb, cnt_step, jnp.zeros((1, tq), jnp.int32))
        return jnp.where(cnt >= n_keep, cand, theta)

    theta = lax.fori_loop(0, 32, bit_step, jnp.full((1, tq), INT_MIN, jnp.int32))

    def count(pred):
        step = lambda kb, c: c + jnp.sum(pred(key_ref[kb]).astype(jnp.int32), axis=0, keepdims=True)
        return lax.fori_loop(0, n_kb, step, jnp.zeros((1, tq), jnp.int32))

    n_ge = count(lambda k: k >= theta)

    @pl.when(jnp.max(n_ge) > n_keep)
    def _break_ties():
        need = (n_keep - count(lambda k: k > theta)).astype(F32)
        before = (lax.broadcasted_iota(jnp.int32, (KEY_BLK, KEY_BLK), 1)
                  < lax.broadcasted_iota(jnp.int32, (KEY_BLK, KEY_BLK), 0)).astype(BF16)

        def fix_step(kb, seen):
            keys = key_ref[kb]
            eq = keys == theta
            eqf = jnp.where(eq, 1.0, 0.0)
            rank = seen + jnp.dot(before, eqf.astype(BF16), preferred_element_type=F32)
            key_ref[kb] = jnp.where(eq & (rank >= need), INT_MIN, keys)
            return seen + jnp.sum(eqf, axis=0, keepdims=True)

        lax.fori_loop(0, n_kb, fix_step, jnp.zeros((1, tq), F32))

    q_heads = [_head_slabs(qp_ref, DSA_HPG * g, DSA_HPG) for g in range(DSA_KV)]
    m0 = jnp.full((1, nq), M_FLOOR, F32)
    l0 = jnp.zeros((1, nq), F32)
    acc_ref[...] = jnp.zeros_like(acc_ref)

    def att_step(kb, carry, causal):
        mask = key_ref[kb] >= theta
        if causal:
            mask = mask & (kb * KEY_BLK + kio_q <= t_row)
        out = []
        for g in range(DSA_KV):
            m, l = _flash_step(k_ref[g // 2, kb], vT_ref[g // 2, kb], q_heads[g], mask,
                               carry[2 * g], carry[2 * g + 1], acc_ref.at[g], False)
            out += [m, l]
        return tuple(out)

    mid = lax.fori_loop(0, t0 // KEY_BLK, functools.partial(att_step, causal=False), (m0, l0) * DSA_KV)
    fin = lax.fori_loop(t0 // KEY_BLK, n_kb, functools.partial(att_step, causal=True), mid)
    for g in range(DSA_KV):
        oT = acc_ref[g] * (1.0 / jnp.maximum(fin[2 * g + 1], 1e-30))
        _store_heads(o_ref, oT, DSA_HPG * g, DSA_HPG, g % 2)


def dsa_prompt(qp, kv, qi, ki, wi):
    B, T, _ = kv.shape
    assert T % KEY_BLK == 0
    tq = ATT_TQ
    nq = DSA_HPG * tq
    n_kb = T // KEY_BLK
    n_keep = min(IDX_TOPK, T // 4)
    tile = lambda w: pl.BlockSpec((1, tq, w), lambda b, i: (b, i, 0))
    whole = lambda w: pl.BlockSpec((1, T, w), lambda b, i: (b, 0, 0))
    return pl.pallas_call(
        functools.partial(_dsa_body, T=T, n_keep=n_keep),
        grid=(B, T // tq),
        in_specs=[tile(DSA_HEADS * LANE), whole(4 * LANE), tile(IDX_QW), whole(LANE), tile(LANE)],
        out_specs=tile(DSA_HEADS * LANE),
        out_shape=jax.ShapeDtypeStruct((B, T, DSA_HEADS * LANE), BF16),
        scratch_shapes=[pltpu.VMEM((2, n_kb, KEY_BLK, LANE), BF16), pltpu.VMEM((2, n_kb, LANE, KEY_BLK), BF16),
                        pltpu.VMEM((n_kb, KEY_BLK, LANE), BF16), pltpu.VMEM((n_kb, KEY_BLK, LANE), BF16),
                        pltpu.VMEM((n_kb, KEY_BLK, tq), jnp.int32), pltpu.VMEM((DSA_KV, LANE, nq), F32)],
        compiler_params=_params("parallel", "arbitrary"),
        name="dsa_prompt",
    )(qp, kv, qi, ki, wi)


def _pad_head_cols(w, slots):
    D = w.shape[0]
    H = len(slots)
    w3 = w.reshape(D, H, HEAD_DIM)
    z = jnp.zeros_like(w3)
    s = jnp.asarray(slots)[None, :, None]
    return jnp.concatenate([jnp.where(s == 0, w3, z), jnp.where(s == 1, w3, z)], axis=2).reshape(D, H * LANE)


def _pad_cols(w, n):
    return jnp.pad(w, ((0, 0), (0, n - w.shape[1])))


def _dot01_r(x, m01):
    hi, mid, lo = _split3(x)
    d = lambda p: jnp.dot(p, m01, preferred_element_type=F32)
    return d(hi) + d(mid) + d(lo)


def _softmax_rows(s, mask):
    s = jnp.where(mask, s, NEG)
    e = jnp.where(mask, jnp.exp2(s - jnp.max(s, axis=1, keepdims=True)), 0.0)
    return e * (1.0 / jnp.maximum(jnp.sum(e, axis=1, keepdims=True), 1e-30))


def _flash_rows(s, mask, m_ref, l_ref):
    s = jnp.where(mask, s, NEG)
    m = m_ref[...]
    m_new = jnp.maximum(m, jnp.max(s, axis=1, keepdims=True))
    alpha = jnp.exp2(m - m_new)
    e = jnp.where(mask, jnp.exp2(s - m_new), 0.0)
    l_ref[...] = alpha * l_ref[...] + jnp.sum(e, axis=1, keepdims=True)
    m_ref[...] = m_new
    return alpha, e


def _sort_key(x):
    bits = lax.bitcast_convert_type(x, jnp.int32)
    return jnp.where(bits < 0, bits ^ 0x7FFFFFFF, bits)


def _page_spec(layer, k, chans, blk):
    return pl.BlockSpec((None, 1, chans, PAGE_SIZE), lambda b, s, pt: (layer, pt[b, s * PG_STEP + k], blk, 0))


def _channel_major(cache):
    layers, n, rows = cache.shape[:3]
    return jnp.moveaxis(cache.reshape(layers, n, rows, -1), 2, 3)


def _per_request(shape):
    return pl.BlockSpec((1,) + shape, lambda b, s, pt: (b,) + (0,) * len(shape))


def _nsa_cmp_body(pt_ref, *refs):
    pages = refs[:PG_STEP]
    pek_ref, pev_ref, wck_ref, wcv_ref, o_ref, stage_ref = refs[PG_STEP:]
    n_half = PG_STEP * PAGE_SIZE // CMP_STRIDE
    for slab, pe_ref, w_ref in ((0, pek_ref, wck_ref), (1, pev_ref, wcv_ref)):
        for pg in range(PG_STEP):
            stage_ref[pg * PAGE_SIZE:(pg + 1) * PAGE_SIZE, :] = pages[pg][0, slab * LANE:(slab + 1) * LANE, :].T
        first = jnp.zeros((n_half, LANE), F32)
        second = jnp.zeros((n_half, LANE), F32)
        for r in range(CMP_STRIDE):
            y = stage_ref[pl.ds(r, n_half, stride=CMP_STRIDE), :]
            first = first + jnp.dot((y + pe_ref[r:r + 1, :]).astype(BF16), w_ref[r], preferred_element_type=F32)
            second = second + jnp.dot((y + pe_ref[r + CMP_STRIDE:r + CMP_STRIDE + 1, :]).astype(BF16),
                                      w_ref[r + CMP_STRIDE], preferred_element_type=F32)
        o_ref[0, :, (2 * slab) * LANE:(2 * slab + 1) * LANE] = first
        o_ref[0, :, (2 * slab + 1) * LANE:(2 * slab + 2) * LANE] = second


def nsa_sample_compress(cache, layer, page_table, pe_k, pe_v, w_ck, w_cv):
    B, n_pages = page_table.shape
    assert n_pages % PG_STEP == 0
    n_half_step = PG_STEP * PAGE_SIZE // CMP_STRIDE
    const = lambda shape: pl.BlockSpec(shape, lambda b, s, pt: (0,) * len(shape))
    grid_spec = pltpu.PrefetchScalarGridSpec(
        num_scalar_prefetch=1,
        grid=(B, n_pages // PG_STEP),
        in_specs=[_page_spec(layer, k, 2 * LANE, 0) for k in range(PG_STEP)]
        + [const((CMP_BLK, LANE)), const((CMP_BLK, LANE)), const((CMP_BLK, LANE, LANE)), const((CMP_BLK, LANE, LANE))],
        out_specs=pl.BlockSpec((1, n_half_step, 4 * LANE), lambda b, s, pt: (b, s, 0)),
        scratch_shapes=[pltpu.VMEM((PG_STEP * PAGE_SIZE, LANE), F32)],
    )
    return pl.pallas_call(
        _nsa_cmp_body,
        grid_spec=grid_spec,
        out_shape=jax.ShapeDtypeStruct((B, n_pages * PAGE_SIZE // CMP_STRIDE, 4 * LANE), F32),
        compiler_params=_params("parallel", "arbitrary"),
        name="nsa_sample_compress",
    )(page_table, *([cache] * PG_STEP), jnp.tile(pe_k, (1, 2)), jnp.tile(pe_v, (1, 2)),
      _block_diag2(w_ck), _block_diag2(w_cv))


def _nsa_smp_body(pt_ref, qs_ref, gl_ref, fs_ref, *refs, n_tok):
    pages = refs[:PG_STEP]
    nrow_ref, pwin_ref, nwin_ref, o_ref, shift_ref, selc_ref, oc_ref, m_ref, l_ref, acc_ref = refs[PG_STEP:]
    s_id = pl.program_id(1)
    C = LANE
    n_half = fs_ref.shape[1]
    n_cmp = n_half - 1
    n_selp = n_half * CMP_STRIDE // SEL_BLK
    qs = qs_ref[0]
    crow = lax.broadcasted_iota(jnp.int32, (C, LANE), 0)
    lane = lax.broadcasted_iota(jnp.int32, (C, LANE), 1)
    t_of_c = crow % n_tok

    @pl.when(s_id == 0)
    def _first():
        cmp = []
        for slab in range(2):
            shift_ref[0:n_half, :] = fs_ref[0, :, (2 * slab + 1) * LANE:(2 * slab + 2) * LANE]
            shift_ref[n_half:n_half + 8, :] = jnp.zeros((8, LANE), F32)
            cmp.append((fs_ref[0, :, (2 * slab) * LANE:(2 * slab + 1) * LANE]
                        + shift_ref[1:n_half + 1, :]).astype(BF16))
        nlane = lax.broadcasted_iota(jnp.int32, (C, n_half), 1)
        p = _softmax_rows(_nt(qs, cmp[0]), nlane < n_cmp)
        oc_ref[...] = jnp.dot(p.astype(BF16), cmp[1], preferred_element_type=F32)
        imp_rows = []
        for g in range(NSA_KV):
            acc = p[(NSA_HPG * g) * n_tok:(NSA_HPG * g + 1) * n_tok, :]
            for h in range(1, NSA_HPG):
                acc = acc + p[(NSA_HPG * g + h) * n_tok:(NSA_HPG * g + h + 1) * n_tok, :]
            imp_rows.append(acc)
        imp = jnp.concatenate(imp_rows + [jnp.zeros((C - NSA_KV * n_tok, n_half), F32)], axis=0)
        pi = lax.broadcasted_iota(jnp.int32, (n_half, n_selp), 0)
        pj = lax.broadcasted_iota(jnp.int32, (n_half, n_selp), 1)
        pool = ((((pi + 1) >> 2) == pj) & (pi < n_cmp)).astype(BF16)
        scoreT = _dot01_r(imp, pool).T
        jcol = lax.broadcasted_iota(jnp.int32, (n_selp, C), 0)
        forced = (jcol == 0) | (jcol == n_selp - 1)
        scoreT = scoreT + jnp.where(forced, FORCE_SCORE, 0.0)
        rank = jnp.zeros((n_selp, C), jnp.int32)
        for j2 in range(n_selp):
            r = scoreT[j2:j2 + 1, :]
            beats = (r > scoreT) | ((r == scoreT) & (jcol > j2))
            rank = rank + beats.astype(jnp.int32)
        sel = jnp.where(rank < TOP_N - 1, 1.0, 0.0).T
        selc_ref[...] = jnp.concatenate(
            [sel[(hh // NSA_HPG) * n_tok:(hh // NSA_HPG + 1) * n_tok, :] for hh in range(NSA_HEADS)]
            + [jnp.zeros((C - NSA_HEADS * n_tok, n_selp), F32)], axis=0)
        m_ref[...] = jnp.full(m_ref.shape, NEG, F32)
        l_ref[...] = jnp.zeros(l_ref.shape, F32)
        acc_ref[...] = jnp.zeros(acc_ref.shape, F32)

    selc = selc_ref[...].astype(BF16)
    n_keys = PG_STEP * PAGE_SIZE
    ej = lax.broadcasted_iota(jnp.int32, (n_selp, n_keys), 0)
    ek = lax.broadcasted_iota(jnp.int32, (n_selp, n_keys), 1)
    expand = (ej == s_id * (n_keys // SEL_BLK) + ek // SEL_BLK).astype(BF16)
    mask = jnp.dot(selc, expand, preferred_element_type=F32) > 0.5
    s = jnp.concatenate([jnp.dot(qs, pages[pg][0, 0:LANE, :].astype(BF16), preferred_element_type=F32)
                         for pg in range(PG_STEP)], axis=1)
    alpha, e = _flash_rows(s, mask, m_ref, l_ref)
    eb = e.astype(BF16)
    pv = _nt(eb[:, 0:PAGE_SIZE], pages[0][0, LANE:2 * LANE, :].astype(BF16))
    for pg in range(1, PG_STEP):
        pv = pv + _nt(eb[:, pg * PAGE_SIZE:(pg + 1) * PAGE_SIZE], pages[pg][0, LANE:2 * LANE, :].astype(BF16))
    acc_ref[...] = alpha * acc_ref[...] + pv

    @pl.when(s_id == pl.num_programs(1) - 1)
    def _last():
        pad = jnp.zeros((LANE - n_tok, LANE), F32)
        new_ok = (lane < n_tok) & (lane <= t_of_c)
        knew = jnp.concatenate([nrow_ref[0, :, 2 * LANE:3 * LANE], pad], axis=0).astype(BF16)
        vnew = jnp.concatenate([nrow_ref[0, :, 3 * LANE:4 * LANE], pad], axis=0).astype(BF16)
        alpha, e = _flash_rows(_nt(qs, knew), new_ok, m_ref, l_ref)
        acc = alpha * acc_ref[...] + jnp.dot(e.astype(BF16), vnew, preferred_element_type=F32)
        o_s = acc * (1.0 / jnp.maximum(l_ref[...], 1e-30))
        n_win = pwin_ref.shape[2]
        kwn = jnp.concatenate([nwin_ref[0, :, 0:LANE], pad], axis=0).astype(BF16)
        vwn = jnp.concatenate([nwin_ref[0, :, LANE:2 * LANE], pad], axis=0).astype(BF16)
        wl = lax.broadcasted_iota(jnp.int32, (C, n_win + LANE), 1)
        tw = lax.broadcasted_iota(jnp.int32, (C, n_win + LANE), 0) % n_tok
        wmask = (((wl < n_win) & ((n_win - wl) + tw < WINDOW))
                 | ((wl >= n_win) & (wl - n_win < n_tok) & (wl - n_win <= tw)))
        s_w = jnp.concatenate([jnp.dot(qs, pwin_ref[0, 0:LANE, :].astype(BF16), preferred_element_type=F32),
                               _nt(qs, kwn)], axis=1)
        pw = _softmax_rows(s_w, wmask).astype(BF16)
        o_w = (_nt(pw[:, 0:n_win], pwin_ref[0, LANE:2 * LANE, :].astype(BF16))
               + jnp.dot(pw[:, n_win:], vwn, preferred_element_type=F32))
        g = jax.nn.sigmoid(gl_ref[0])
        o = g[:, 0:1] * oc_ref[...] + g[:, 1:2] * o_s + g[:, 2:3] * o_w
        valid = (crow < NSA_HEADS * n_tok) & (lane // HEAD_DIM == crow // (n_tok * NSA_HPG))
        o_ref[0] = jnp.where(valid, o, 0.0)


def nsa_sample_attention(qs, gcol, fs, cache, layer, page_table, new_rows, past_win, new_win):
    B, n_pages = page_table.shape
    n_tok = new_rows.shape[1]
    n_half = fs.shape[1]
    n_selp = n_half * CMP_STRIDE // SEL_BLK
    n_win = past_win.shape[3]
    assert n_selp == LANE and NSA_HEADS * n_tok <= LANE and n_tok <= min(SEL_BLK, 8) and n_pages % PG_STEP == 0
    assert n_win % LANE == 0 and n_win <= WINDOW
    grid_spec = pltpu.PrefetchScalarGridSpec(
        num_scalar_prefetch=1,
        grid=(B, n_pages // PG_STEP),
        in_specs=[_per_request((LANE, LANE)), _per_request((LANE, LANE)), _per_request((n_half, 4 * LANE))]
        + [_page_spec(layer, k, 2 * LANE, 1) for k in range(PG_STEP)]
        + [_per_request((n_tok, 4 * LANE)),
           pl.BlockSpec((None, 1, 2 * LANE, n_win), lambda b, s, pt: (layer, b, 0, 0)),
           _per_request((n_tok, 2 * LANE))],
        out_specs=_per_request((LANE, LANE)),
        scratch_shapes=[pltpu.VMEM((n_half + 8, LANE), F32), pltpu.VMEM((LANE, n_selp), F32),
                        pltpu.VMEM((LANE, LANE), F32), pltpu.VMEM((LANE, 1), F32), pltpu.VMEM((LANE, 1), F32),
                        pltpu.VMEM((LANE, LANE), F32)],
    )
    return pl.pallas_call(
        functools.partial(_nsa_smp_body, n_tok=n_tok),
        grid_spec=grid_spec,
        out_shape=jax.ShapeDtypeStruct((B, LANE, LANE), F32),
        compiler_params=_params("parallel", "arbitrary"),
        name="nsa_sample_attention",
    )(page_table, qs, gcol, fs, *([cache] * PG_STEP), new_rows, past_win, new_win)


def _dsa_idx_body(pt_ref, qe_ref, qo_ref, w_ref, *refs, n_tok, n_keep):
    pages = refs[:PG_STEP]
    knew_ref, sc_ref, th_ref = refs[PG_STEP:]
    s_id = pl.program_id(1)
    n_blk = sc_ref.shape[1]
    qe = qe_ref[0]
    qo = qo_ref[0]
    w = w_ref[0] * (IDX_HEADS ** -0.5)
    n_rows = (IDX_HEADS // 2) * n_tok

    def scores(s_e, s_o):
        s = jnp.maximum(s_e, 0.0) * w[:, 0:1] + jnp.maximum(s_o, 0.0) * w[:, 1:2]
        tot = s[0:n_tok, :]
        for p in range(1, IDX_HEADS // 2):
            tot = tot + s[p * n_tok:(p + 1) * n_tok, :]
        return tot

    kT = jnp.concatenate([pages[pg][0] for pg in range(PG_STEP)], axis=1).astype(BF16)
    sc = scores(jnp.dot(qe, kT, preferred_element_type=F32), jnp.dot(qo, kT, preferred_element_type=F32))
    for pg in range(PG_STEP):
        sc_ref[0, s_id * PG_STEP + pg] = sc[:, pg * PAGE_SIZE:(pg + 1) * PAGE_SIZE]

    @pl.when(s_id == pl.num_programs(1) - 1)
    def _last():
        kn = jnp.concatenate([knew_ref[0], jnp.zeros((LANE - n_tok, IDX_DIM), F32)], axis=0).astype(BF16)
        a_i = lax.broadcasted_iota(jnp.int32, (n_tok, LANE), 1)
        t_i = lax.broadcasted_iota(jnp.int32, (n_tok, LANE), 0)
        sn = jnp.where(a_i <= t_i, scores(_nt(qe, kn), _nt(qo, kn)), NEG)
        sc_ref[0, n_blk - 1] = jnp.where(a_i < n_tok, sn, -jnp.inf)
        keys = _sort_key(sc_ref[0])

        def bit_step(it, theta):
            cand = theta + lax.shift_left(jnp.int32(1), 31 - it)
            cnt = jnp.sum(jnp.sum((keys >= cand).astype(jnp.int32), axis=0), axis=1, keepdims=True)
            return jnp.where(cnt >= n_keep, cand, theta)

        theta = lax.fori_loop(0, 32, bit_step, jnp.full((n_tok, 1), INT_MIN, jnp.int32))
        th_ref[0] = jnp.broadcast_to(theta, (n_tok, LANE))

        count = lambda m: jnp.sum(jnp.sum(m.astype(jnp.int32), axis=0), axis=1, keepdims=True)

        @pl.when(jnp.max(count(keys >= theta)) > n_keep)
        def _break_ties():
            need = (n_keep - count(keys > theta)).astype(F32)
            before = (lax.broadcasted_iota(jnp.int32, (LANE, LANE), 0)
                      < lax.broadcasted_iota(jnp.int32, (LANE, LANE), 1)).astype(BF16)

            def fix_step(blk, seen):
                sc = sc_ref[0, blk]
                eq = _sort_key(sc) == theta
                eqf = jnp.where(eq, 1.0, 0.0)
                rank = seen + jnp.dot(eqf.astype(BF16), before, preferred_element_type=F32)
                sc_ref[0, blk] = jnp.where(eq & (rank >= need), -jnp.inf, sc)
                return seen + jnp.sum(eqf, axis=1, keepdims=True)

            lax.fori_loop(0, n_blk, fix_step, jnp.zeros((n_tok, 1), F32))


def dsa_sample_index(qe, qo, wcol, cache_idx, layer, page_table, ki_new):
    B, n_pages = page_table.shape
    n_tok = ki_new.shape[1]
    assert n_tok == 8 and n_pages % PG_STEP == 0
    n_keep = min(IDX_TOPK, (n_pages * PAGE_SIZE + n_tok) // 4)
    grid_spec = pltpu.PrefetchScalarGridSpec(
        num_scalar_prefetch=1,
        grid=(B, n_pages // PG_STEP),
        in_specs=[_per_request((LANE, IDX_DIM)), _per_request((LANE, IDX_DIM)), _per_request((LANE, LANE))]
        + [_page_spec(layer, k, IDX_DIM, 0) for k in range(PG_STEP)] + [_per_request((n_tok, IDX_DIM))],
        out_specs=[_per_request((n_pages + 1, n_tok, LANE)), _per_request((n_tok, LANE))],
    )
    return pl.pallas_call(
        functools.partial(_dsa_idx_body, n_tok=n_tok, n_keep=n_keep),
        grid_spec=grid_spec,
        out_shape=[jax.ShapeDtypeStruct((B, n_pages + 1, n_tok, LANE), F32),
                   jax.ShapeDtypeStruct((B, n_tok, LANE), jnp.int32)],
        compiler_params=_params("parallel", "arbitrary"),
        name="dsa_sample_index",
    )(page_table, qe, qo, wcol, *([cache_idx] * PG_STEP), ki_new)


def _dsa_smp_body(pt_ref, qs_ref, sc_ref, th_ref, *refs, n_tok):
    pages = refs[:PG_STEP]
    kvn_ref, o_ref, m_ref, l_ref, acc_ref = refs[PG_STEP:]
    s_id = pl.program_id(1)
    C = LANE
    qs = qs_ref[0]
    crow = lax.broadcasted_iota(jnp.int32, (C, LANE), 0)
    lane = lax.broadcasted_iota(jnp.int32, (C, LANE), 1)
    low = crow < C // 2
    theta = th_ref[0]

    @pl.when(s_id == 0)
    def _():
        m_ref[...] = jnp.full(m_ref.shape, NEG, F32)
        l_ref[...] = jnp.zeros(l_ref.shape, F32)
        acc_ref[...] = jnp.zeros(acc_ref.shape, F32)

    def step(kvTs, blk0, extra):
        kbs = [kvT.astype(BF16) for kvT in kvTs]
        keep = jnp.concatenate([jnp.where(_sort_key(sc_ref[0, blk0 + j]) >= theta, 1.0, 0.0)
                                for j in range(len(kbs))], axis=1)
        mask = jnp.concatenate([keep] * (C // n_tok), axis=0) > 0.5
        if extra is not None:
            mask = mask & extra
        s_lo = jnp.concatenate([jnp.dot(qs, kb[0:LANE], preferred_element_type=F32) for kb in kbs], axis=1)
        s_hi = jnp.concatenate([jnp.dot(qs, kb[LANE:2 * LANE], preferred_element_type=F32) for kb in kbs], axis=1)
        low_col = lax.broadcasted_iota(jnp.int32, (C, 1), 0) < C // 2
        alpha, e = _flash_rows(jnp.where(low_col, s_lo, s_hi), mask, m_ref, l_ref)
        eb = e.astype(BF16)
        pv_lo = pv_hi = None
        for j, kb in enumerate(kbs):
            ej = eb[:, j * LANE:(j + 1) * LANE]
            a, b = _nt(ej, kb[2 * LANE:3 * LANE]), _nt(ej, kb[3 * LANE:4 * LANE])
            pv_lo, pv_hi = (a, b) if pv_lo is None else (pv_lo + a, pv_hi + b)
        acc_ref[...] = alpha * acc_ref[...] + jnp.where(low, pv_lo, pv_hi)

    step([pages[pg][0] for pg in range(PG_STEP)], s_id * PG_STEP, None)

    @pl.when(s_id == pl.num_programs(1) - 1)
    def _last():
        kvn = jnp.concatenate([kvn_ref[0], jnp.zeros((LANE - n_tok, 4 * LANE), F32)], axis=0)
        kvnT = jnp.concatenate([kvn[:, j * LANE:(j + 1) * LANE].T for j in range(4)], axis=0)
        step([kvnT], sc_ref.shape[1] - 1, (lane < n_tok) & (lane <= crow % n_tok))
        o = acc_ref[...] * (1.0 / jnp.maximum(l_ref[...], 1e-30))
        valid = lane // HEAD_DIM == (crow // (n_tok * DSA_HPG)) % 2
        o_ref[0] = jnp.where(valid, o, 0.0)


def dsa_sample_attention(qs, scores, theta, cache_kv, layer, page_table, kv_new):
    B, n_pages = page_table.shape
    n_tok = kv_new.shape[1]
    assert DSA_HEADS * n_tok == LANE and n_pages % PG_STEP == 0
    grid_spec = pltpu.PrefetchScalarGridSpec(
        num_scalar_prefetch=1,
        grid=(B, n_pages // PG_STEP),
        in_specs=[_per_request((LANE, LANE)), _per_request((n_pages + 1, n_tok, LANE)), _per_request((n_tok, LANE))]
        + [_page_spec(layer, k, 4 * LANE, 0) for k in range(PG_STEP)] + [_per_request((n_tok, 4 * LANE))],
        out_specs=_per_request((LANE, LANE)),
        scratch_shapes=[pltpu.VMEM((LANE, 1), F32), pltpu.VMEM((LANE, 1), F32), pltpu.VMEM((LANE, LANE), F32)],
    )
    return pl.pallas_call(
        functools.partial(_dsa_smp_body, n_tok=n_tok),
        grid_spec=grid_spec,
        out_shape=jax.ShapeDtypeStruct((B, LANE, LANE), F32),
        compiler_params=_params("parallel", "arbitrary"),
        name="dsa_sample_attention",
    )(page_table, qs, scores, theta, *([cache_kv] * PG_STEP), kv_new)


_NSA_SLOTS = tuple(h // NSA_HPG for h in range(NSA_HEADS))
_DSA_SLOTS = tuple((h // DSA_HPG) % 2 for h in range(DSA_HEADS))
def _even_segs(q_dtype, cm):
    return ((NSA_HEADS * LANE, (1,) * NSA_HEADS, q_dtype, False), (4 * LANE, (1, 0, 1, 0), F32, cm),
            (2 * LANE, (1, 0), F32, False), (LANE, (0,), F32, False),
            (4 * HGRN_KW, (0,) * (4 * HGRN_KW // LANE), F32, False))


def _odd_segs(q_dtype, cm):
    return ((DSA_HEADS * LANE, (1,) * DSA_HEADS, q_dtype, False), (4 * LANE, (1, 1, 0, 0), F32, cm),
            (IDX_QW, (1,) * 4, q_dtype, False), (LANE, (1,), F32, cm), (LANE, (0,), F32, False))


def _even_weights(w_in, w_out):
    c = np.cumsum([0, NSA_QW] + [NSA_KVW] * 6 + [NSA_GW] + [HGRN_KW] * 4)
    w = jnp.concatenate([
        _pad_head_cols(w_in[:, :NSA_QW] * Q_SCALE, _NSA_SLOTS),
        w_in[:, c[1]:c[5]], w_in[:, c[5]:c[7]], _pad_cols(w_in[:, c[7]:c[8]], LANE), w_in[:, c[8]:c[12]]],
        axis=1).astype(BF16)
    w_outs = [_pad_head_cols(w_out[:NSA_QW].T, _NSA_SLOTS).T.astype(BF16), w_out[NSA_QW:].astype(BF16)]
    return w, w_outs


def _odd_weights(w_in, w_out):
    c = np.cumsum([0, DSA_QW, DSA_KVW, DSA_KVW, IDX_QW, IDX_DIM, IDX_HEADS])
    w = jnp.concatenate([
        _pad_head_cols(w_in[:, :DSA_QW] * Q_SCALE, _DSA_SLOTS),
        w_in[:, c[1]:c[3]], w_in[:, c[3]:c[4]] * IDX_DIM ** -0.5,
        _pad_cols(w_in[:, c[4]:c[5]], LANE), _pad_cols(w_in[:, c[5]:c[6]], LANE)], axis=1).astype(BF16)
    return w, [_pad_head_cols(w_out.T, _DSA_SLOTS).T.astype(BF16)]


def even_mixer_prompt(x, scale, shift, cos_t, sin_t, w_in, w_out, pe_k, pe_v, w_ck, w_cv, lb, norm_g, win_len):
    B, T, _ = x.shape
    assert T >= win_len
    w, w_outs = _even_weights(w_in, w_out)
    qp, rows, win, gl, hx, rows_cm = proj_segments(x, scale, shift, cos_t, sin_t, w, _even_segs(BF16, True))
    o_a = nsa_prompt(qp, rows, win, gl, pe_k, pe_v, w_ck, w_cv)
    o_b, s_new = hgrn_mixer(hx, lb, norm_g, jnp.zeros((B, HGRN_HEADS, HGRN_DK, HGRN_DV), F32))
    new_rows = jnp.moveaxis(rows_cm.reshape(B, 4, NSA_KV, HEAD_DIM, T), 4, 1)
    win_state = win[:, T - win_len:].reshape(B, win_len, 2, NSA_KV, HEAD_DIM)
    return [o_a, o_b], w_outs, new_rows, win_state, s_new


def odd_mixer_prompt(x, scale, shift, cos_t, sin_t, w_in, w_out):
    B, T, _ = x.shape
    w, w_outs = _odd_weights(w_in, w_out)
    qp, kv, qi, ki, wi, kv_cm, ki_cm = proj_segments(x, scale, shift, cos_t, sin_t, w, _odd_segs(BF16, True))
    o = dsa_prompt(qp, kv, qi, ki, wi)
    new_kv = jnp.moveaxis(kv_cm.reshape(B, 2, DSA_KV, HEAD_DIM, T), 4, 1)
    return [o], w_outs, new_kv, jnp.moveaxis(ki_cm[:, :IDX_DIM], 2, 1)


def _stack_rows(a, B, n, heads, width):
    s = a.reshape(B, n, heads, width).transpose(0, 2, 1, 3).reshape(B, heads * n, width)
    return jnp.pad(s, ((0, 0), (0, LANE - heads * n), (0, 0)))


def _unstack_rows(o, B, n, heads):
    return o[:, :heads * n].reshape(B, heads, n, LANE).transpose(0, 2, 1, 3).reshape(1, B * n, heads * LANE)


def even_mixer_sample(x, scale, shift, past_len, B, w_in, w_out, pe_k, pe_v, w_ck, w_cv, lb, norm_g,
                      cache, cache_win, state, page_table, layer):
    n = x.shape[1] // B
    win_len = cache_win.shape[2]
    cos_t, sin_t = [jnp.tile(a, (B, 1)) for a in rope_tables(past_len + jnp.arange(n))]
    w, w_outs = _even_weights(w_in, w_out)
    qp, rows, win, gl, hx = proj_segments(x, scale, shift, cos_t, sin_t, w, _even_segs(F32, False))
    rows, win, hx = [a.reshape(B, n, a.shape[-1]) for a in (rows, win, hx)]
    qs = _stack_rows(qp, B, n, NSA_HEADS, LANE).astype(BF16)
    gcol = jnp.pad(_stack_rows(gl[..., :NSA_GW], B, n, NSA_HEADS, 3), ((0, 0), (0, 0), (0, LANE - 3)))
    cache_cm = _channel_major(cache)
    fs = nsa_sample_compress(cache_cm, layer, page_table, pe_k, pe_v, w_ck, w_cv)
    o = nsa_sample_attention(qs, gcol, fs, cache_cm, layer, page_table, rows, _channel_major(cache_win), win)
    o_a = _unstack_rows(o, B, n, NSA_HEADS)
    o_b, s_new = hgrn_mixer(hx, lb, norm_g, state)
    new_rows = rows.reshape(B, n, 4, NSA_KV, HEAD_DIM)
    win_state = jnp.concatenate([cache_win[layer], win.reshape(B, n, 2, NSA_KV, HEAD_DIM)], axis=1)[:, -win_len:]
    return [o_a, o_b.reshape(1, B * n, HGRN_VW)], w_outs, new_rows, win_state, s_new


def odd_mixer_sample(x, scale, shift, past_len, B, w_in, w_out, cache_kv, cache_idx, page_table, layer):
    n = x.shape[1] // B
    cos_t, sin_t = [jnp.tile(a, (B, 1)) for a in rope_tables(past_len + jnp.arange(n))]
    w, w_outs = _odd_weights(w_in, w_out)
    qp, kv, qi, ki, wi = proj_segments(x, scale, shift, cos_t, sin_t, w, _odd_segs(F32, False))
    kv, ki = kv.reshape(B, n, 4 * LANE), ki.reshape(B, n, LANE)[..., :IDX_DIM]
    qs = _stack_rows(qp, B, n, DSA_HEADS, LANE).astype(BF16)
    qi4 = qi.reshape(1, B * n, IDX_HEADS // 2, 2, IDX_DIM)
    qe = _stack_rows(qi4[:, :, :, 0], B, n, IDX_HEADS // 2, IDX_DIM).astype(BF16)
    qo = _stack_rows(qi4[:, :, :, 1], B, n, IDX_HEADS // 2, IDX_DIM).astype(BF16)
    wcol = jnp.pad(_stack_rows(wi[..., :IDX_HEADS], B, n, IDX_HEADS // 2, 2), ((0, 0), (0, 0), (0, LANE - 2)))
    scores, theta = dsa_sample_index(qe, qo, wcol, _channel_major(cache_idx), layer, page_table, ki)
    o = dsa_sample_attention(qs, scores, theta, _channel_major(cache_kv), layer, page_table, kv)
    return [_unstack_rows(o, B, n, DSA_HEADS)], w_outs, kv.reshape(B, n, 2, DSA_KV, HEAD_DIM), ki


def moe_ffn_residual(xs, hs, logits_list, gates, w1, w3, w2):
    D = D_MODEL
    h_all = jnp.concatenate([h.reshape(-1, D) for h in hs], axis=0)
    logits = jnp.concatenate([lg.reshape(-1, lg.shape[-1])[:, :N_EXPERTS] for lg in logits_list], axis=0)
    n_tok = h_all.shape[0]
    top_v, top_i = lax.top_k(logits, TOP_K)
    weights = jax.nn.softmax(top_v, axis=-1)
    e_flat = top_i.reshape(-1)
    onehot = (e_flat[:, None] == jnp.arange(N_EXPERTS)[None, :]).astype(jnp.int32)
    csum = jnp.cumsum(onehot, axis=0)
    counts = csum[-1]
    rank = jnp.take_along_axis(csum, e_flat[:, None], axis=1)[:, 0] - 1
    padded = ((counts + MOE_TILE - 1) // MOE_TILE) * MOE_TILE
    group_end = jnp.cumsum(padded)
    group_start = group_end - padded
    slot = group_start[e_flat] + rank
    n_slots = _round_up(n_tok * TOP_K, MOE_TILE) + N_EXPERTS * MOE_TILE
    n_tiles = n_slots // MOE_TILE
    tok_of_slot = jnp.zeros((n_slots,), jnp.int32).at[slot].set(jnp.arange(n_tok * TOP_K, dtype=jnp.int32) // TOP_K)
    tile_start = jnp.arange(n_tiles, dtype=jnp.int32) * MOE_TILE
    tile_expert = jnp.minimum(jnp.sum(tile_start[:, None] >= group_end[None, :], axis=1), N_EXPERTS - 1).astype(jnp.int32)
    n_used = (group_end[-1] // MOE_TILE).astype(jnp.int32).reshape(1)
    h_sorted = h_all[tok_of_slot]
    y_slot = moe_grouped_swiglu(h_sorted, tile_expert, n_used, w1, w3, w2)
    slot2 = slot.reshape(n_tok, TOP_K)
    outs = []
    off = 0
    for x, g in zip(xs, gates):
        n = x.shape[0] * x.shape[1]
        ya = y_slot[slot2[off:off + n, 0]].reshape(x.shape)
        yb = y_slot[slot2[off:off + n, 1]].reshape(x.shape)
        w = weights[off:off + n].reshape(x.shape[0], x.shape[1], TOP_K)
        outs.append(moe_combine_residual(x, g, w, ya, yb))
        off += n
    return outs


def kernel(x_prompt, x_sample, cache_nsa, cache_nsa_win, state_hgrn, cache_dsa_kv, cache_dsa_idx, page_table, c_prompt, c_sample, ada_w, ada_b, norm1_g, norm2_g, final_g, even_w_in, even_w_out, nsa_pe_k, nsa_pe_v, nsa_w_ck, nsa_w_cv, hgrn_lb_raw, hgrn_norm_g, ffn_w1, ffn_w3, ffn_w2, odd_w_in, odd_w_out, router_w, router_b, moe_w1, moe_w3, moe_w2):
    D = D_MODEL
    past_len = page_table.shape[1] * PAGE_SIZE
    win_len = cache_nsa_win.shape[2]
    Bp, Tp = x_prompt.shape[:2]
    Bs, Ts = x_sample.shape[:2]
    lb_soft = jax.nn.softmax(hgrn_lb_raw.astype(F32), axis=0)
    lower_bounds = jnp.cumsum(lb_soft, axis=0) - lb_soft[0]

    R = _round_up(Bp + Bs, 8)
    c_all = jnp.zeros((R, D), F32).at[:Bp].set(c_prompt).at[Bp:Bp + Bs].set(c_sample)
    mods = ada_modulation(c_all, ada_w, ada_b)

    def group_mods(l, lo, n, per_token_rows):
        m = mods[l, lo:lo + n].reshape(n, 6, D)
        sh1, sc1, g1, sh2, sc2, g2 = [m[:, j] for j in range(6)]
        s1 = norm1_g[l][None] * (1.0 + sc1)
        s2 = norm2_g[l][None] * (1.0 + sc2)
        vecs = [s1, sh1, g1, s2, sh2, g2]
        if per_token_rows:
            return [jnp.repeat(v, per_token_rows, axis=0)[None] for v in vecs]
        return [v[:, None, :] for v in vecs]

    def pad_cols(w, n):
        return jnp.pad(w, ((0, 0), (0, n - w.shape[1]))).astype(BF16)

    xp = x_prompt
    xs = x_sample.reshape(1, Bs * Ts, D)
    cos_p, sin_p = rope_tables(jnp.arange(Tp))
    outs_p = dict(rows=[], win=[], st=[], kv=[], idx=[])
    outs_s = dict(rows=[], win=[], st=[], kv=[], idx=[])
    for l in range(DEPTH):
        i = l // 2
        mp = group_mods(l, 0, Bp, 0)
        msm = group_mods(l, Bp, Bs, Ts)
        if l % 2 == 0:
            w1, w3, w2 = ffn_w1[i].astype(BF16), ffn_w3[i].astype(BF16), ffn_w2[i].astype(BF16)
            mixed, w_outs, rows, win, s_new = even_mixer_prompt(
                xp, mp[0], mp[1], cos_p, sin_p, even_w_in[i], even_w_out[i], nsa_pe_k[i], nsa_pe_v[i],
                nsa_w_ck[i], nsa_w_cv[i], lower_bounds[i], hgrn_norm_g[i], win_len)
            outs_p['rows'].append(rows)
            outs_p['win'].append(win)
            outs_p['st'].append(s_new)
            xp = out_proj_residual(mixed, w_outs, xp, mp[2])
            xp = ffn_residual(xp, mp[3], mp[4], mp[5], w1, w3, w2)
            mixed, w_outs, rows, win, s_new = even_mixer_sample(
                xs, msm[0], msm[1], past_len, Bs, even_w_in[i], even_w_out[i], nsa_pe_k[i], nsa_pe_v[i],
                nsa_w_ck[i], nsa_w_cv[i], lower_bounds[i], hgrn_norm_g[i],
                cache_nsa, cache_nsa_win, state_hgrn[i], page_table, i)
            outs_s['rows'].append(rows)
            outs_s['win'].append(win)
            outs_s['st'].append(s_new)
            xs = out_proj_residual(mixed, w_outs, xs, msm[2])
            xs = ffn_residual(xs, msm[3], msm[4], msm[5], w1, w3, w2)
        else:
            w_r = pad_cols(router_w[i], LANE)
            w1, w3, w2 = moe_w1[i].astype(BF16), moe_w3[i].astype(BF16), moe_w2[i].astype(BF16)
            o, w_outs, kv, ki = odd_mixer_prompt(xp, mp[0], mp[1], cos_p, sin_p, odd_w_in[i], odd_w_out[i])
            outs_p['kv'].append(kv)
            outs_p['idx'].append(ki)
            xp = out_proj_residual(o, w_outs, xp, mp[2])
            o, w_outs, kv, ki = odd_mixer_sample(xs, msm[0], msm[1], past_len, Bs, odd_w_in[i], odd_w_out[i],
                                                 cache_dsa_kv, cache_dsa_idx, page_table, i)
            outs_s['kv'].append(kv)
            outs_s['idx'].append(ki)
            xs = out_proj_residual(o, w_outs, xs, msm[2])
            hs, lgs = [], []
            for (x, m) in ((xp, mp), (xs, msm)):
                logits, h = norm_proj(x, m[3], m[4], w_r, with_h=True)
                lgs.append(logits[..., :N_EXPERTS] + router_b[i].astype(F32))
                hs.append(h)
            xp, xs = moe_ffn_residual([xp, xs], hs, lgs, [mp[5], msm[5]], w1, w3, w2)
    y_prompt = final_norm(xp, final_g)
    y_sample = final_norm(xs, final_g).reshape(Bs, Ts, D)
    st = lambda od, k: jnp.stack(od[k])
    return (y_prompt, y_sample, st(outs_p, 'rows'), st(outs_s, 'rows'), st(outs_p, 'win'), st(outs_s, 'win'),
            st(outs_p, 'st'), st(outs_s, 'st'), st(outs_p, 'kv'), st(outs_s, 'kv'),
            st(outs_p, 'idx'), st(outs_s, 'idx'))
```
